```python
import math
import jax, jax.numpy as jnp
from jax import lax
import numpy as np

D_MODEL = 1024
BATCH = 16
SEQ = 2048
DEPTH = 1
DEC_BATCH = 16
DEC_SEQ = 64
PAST_LEN = 2048

CHUNK = 64
Q_BLOCK = 128
S5_WIDTH = D_MODEL // 2
S5_GROUP = 16
S5_GROUPS = S5_WIDTH // S5_GROUP
S5_STATE = 64
DT_MIN = 1e-3
DT_MAX = 1e-1
MLA_HEADS = 8
QK_NOPE = 64
QK_ROPE = 32
QK_HEAD = QK_NOPE + QK_ROPE
V_HEAD = 64
MLA_WIDTH = MLA_HEADS * V_HEAD
Q_LORA = 3 * D_MODEL // 8
KV_LORA = D_MODEL // 4
ROPE_THETA = 10000.0
D_IN = S5_WIDTH + Q_LORA + KV_LORA + QK_ROPE
D_MIX = S5_WIDTH + MLA_WIDTH
N_EXPERTS = 32
TOP_K = 4
D_FF = D_MODEL
SWIGLU_LIMIT = 7.0
SWIGLU_ALPHA = 1.702
MOE_BLOCK = 512
EPS = 1e-6

kernel_name = 'hybrid_s5_mla_moe_stream_step'

F32 = jnp.float32


def rms_norm(x, g):
    xf = x.astype(F32)
    y = xf * lax.rsqrt(jnp.mean(xf * xf, axis=-1, keepdims=True) + EPS)
    return (y * g.astype(F32)).astype(x.dtype)


def head_gain(g_nope, g_rope_freq):
    return jnp.concatenate([g_nope, g_rope_freq, g_rope_freq], axis=-1)


def rope_tables(pos):
    inv = ROPE_THETA ** (-jnp.arange(QK_ROPE // 2, dtype=F32) * (2.0 / QK_ROPE))
    ang = pos[:, None] * inv[None, :]
    return jnp.cos(ang), jnp.sin(ang)


def apply_rope(x, cos, sin):
    xf = x.astype(F32)
    x1, x2 = xf[..., :QK_ROPE // 2], xf[..., QK_ROPE // 2:]
    return jnp.concatenate([x1 * cos - x2 * sin, x1 * sin + x2 * cos], axis=-1).astype(x.dtype)


def s5_mixer(u, h0_re, h0_im, lam_re, lam_im, log_dt, b_re, b_im, c_re, c_im, d_skip, w_glu, b_glu):
    bsz, length, _ = u.shape
    uf = u.astype(F32).reshape(bsz, length, S5_GROUPS, S5_GROUP)
    lam = lax.complex(lam_re.astype(F32), lam_im.astype(F32))
    dt = jnp.exp(log_dt.astype(F32))[:, None]
    lam_bar = jnp.exp(lam * dt)
    b_bar = ((lam_bar - 1.0) / lam)[..., None] * lax.complex(b_re.astype(F32), b_im.astype(F32))
    bu = lax.complex(jnp.einsum('gnp,blgp->blgn', jnp.real(b_bar), uf),
                     jnp.einsum('gnp,blgp->blgn', jnp.imag(b_bar), uf))
    if h0_re is not None:
        h0 = lax.complex(h0_re.astype(F32), h0_im.astype(F32))
        bu = bu.at[:, 0].add(lam_bar * h0)
    a = jnp.broadcast_to(lam_bar, bu.shape)

    def combine(e1, e2):
        a1, b1 = e1
        a2, b2 = e2
        return a1 * a2, a2 * b1 + b2

    _, hs = lax.associative_scan(combine, (a, bu), axis=1)
    y = (jnp.einsum('gpn,blgn->blgp', c_re.astype(F32), jnp.real(hs))
         - jnp.einsum('gpn,blgn->blgp', c_im.astype(F32), jnp.imag(hs))
         + d_skip.astype(F32).reshape(S5_GROUPS, S5_GROUP) * uf)
    y = jax.nn.gelu(y.reshape(bsz, length, S5_WIDTH)).astype(u.dtype)
    y = y * jax.nn.sigmoid(y @ w_glu + b_glu)
    h_last = hs[:, -1]
    return y, jnp.real(h_last), jnp.imag(h_last)


def mla_queries(c_q, cos, sin, g_q_lat, w_uq, g_qn):
    bsz, length, _ = c_q.shape
    q = (rms_norm(c_q, g_q_lat) @ w_uq).reshape(bsz, length, MLA_HEADS, QK_HEAD)
    q = jnp.concatenate([q[..., :QK_NOPE], apply_rope(q[..., QK_NOPE:], cos[:, None], sin[:, None])], axis=-1)
    return rms_norm(q, g_qn)


def mla_keys_values(lat, k_rot, w_ukv, g_kn):
    bsz, length, _ = lat.shape
    kv = (lat @ w_ukv).reshape(bsz, length, MLA_HEADS, QK_NOPE + V_HEAD)
    k = jnp.concatenate([kv[..., :QK_NOPE],
                         jnp.broadcast_to(k_rot[:, :, None, :], (bsz, length, MLA_HEADS, QK_ROPE))], axis=-1)
    return rms_norm(k, g_kn), kv[..., QK_NOPE:]


def chunk_causal_attention(q, k, v):
    bsz, length = q.shape[0], q.shape[1]
    n_blocks = length // Q_BLOCK
    scale = QK_HEAD ** -0.5
    kf = k.astype(F32)
    key_chunk = jnp.arange(length) // CHUNK
    qb = q.reshape(bsz, n_blocks, Q_BLOCK, MLA_HEADS, QK_HEAD).transpose(1, 0, 2, 3, 4)

    def one_block(args):
        q_blk, blk = args
        s = jnp.einsum('bqhd,bkhd->bhqk', q_blk.astype(F32), kf) * scale
        q_chunk = (blk * Q_BLOCK + jnp.arange(Q_BLOCK)) // CHUNK
        s = jnp.where(key_chunk[None, :] <= q_chunk[:, None], s, -jnp.inf)
        p = jax.nn.softmax(s, axis=-1).astype(v.dtype)
        return jnp.einsum('bhqk,bkhd->bqhd', p, v)

    o = lax.map(one_block, (qb, jnp.arange(n_blocks)))
    return o.transpose(1, 0, 2, 3, 4).reshape(bsz, length, MLA_WIDTH)


def full_attention(q, k, v):
    bsz, lq = q.shape[0], q.shape[1]
    s = jnp.einsum('bqhd,bkhd->bhqk', q.astype(F32), k.astype(F32)) * (QK_HEAD ** -0.5)
    p = jax.nn.softmax(s, axis=-1).astype(v.dtype)
    return jnp.einsum('bhqk,bkhd->bqhd', p, v).reshape(bsz, lq, MLA_WIDTH)


def moe_ffn(h, w_router, b_router, w1, b1, w2, b2):
    n_tok = h.shape[0]
    n_assign = n_tok * TOP_K
    logits = h.astype(F32) @ w_router.astype(F32) + b_router.astype(F32)
    top_val, top_idx = lax.top_k(logits, TOP_K)
    gates = jax.nn.softmax(top_val, axis=-1)
    flat_e = top_idx.reshape(-1)
    order = jnp.argsort(flat_e)
    e_sorted = flat_e[order]
    tok_sorted = (order // TOP_K).astype(jnp.int32)
    counts = jnp.bincount(flat_e, length=N_EXPERTS)
    padded = (counts + MOE_BLOCK - 1) // MOE_BLOCK * MOE_BLOCK
    pad_end = jnp.cumsum(padded)
    start = jnp.cumsum(counts) - counts
    dest = ((pad_end - padded)[e_sorted] + jnp.arange(n_assign) - start[e_sorted]).astype(jnp.int32)
    n_blocks = -(-(n_assign + N_EXPERTS * (MOE_BLOCK - 1)) // MOE_BLOCK)
    rows = jnp.full((n_blocks * MOE_BLOCK,), n_tok, jnp.int32).at[dest].set(tok_sorted)
    block_e = jnp.minimum(jnp.searchsorted(pad_end, jnp.arange(n_blocks) * MOE_BLOCK, side='right'), N_EXPERTS - 1)
    h_pad = jnp.concatenate([h, jnp.zeros((1, h.shape[1]), h.dtype)], axis=0)
    xb = h_pad[rows].reshape(n_blocks, MOE_BLOCK, h.shape[1])

    def expert_block(args):
        x_blk, e = args
        hg = x_blk @ w1[e] + b1[e]
        gate = jnp.minimum(hg[:, :D_FF], SWIGLU_LIMIT)
        lin = jnp.clip(hg[:, D_FF:], -SWIGLU_LIMIT, SWIGLU_LIMIT)
        act = gate * jax.nn.sigmoid(SWIGLU_ALPHA * gate) * (lin + 1.0)
        return act @ w2[e] + b2[e]

    yb = lax.map(expert_block, (xb, block_e)).reshape(n_blocks * MOE_BLOCK, h.shape[1])
    slot = jnp.zeros((n_assign,), jnp.int32).at[order].set(dest)
    return jnp.einsum('tkd,tk->td', yb[slot].reshape(n_tok, TOP_K, h.shape[1]), gates.astype(yb.dtype))


def trunk_layer(x, past, g_mix, w_in, lam_re, lam_im, log_dt, b_s5_re, b_s5_im, c_s5_re, c_s5_im,
                d_s5, w_glu, b_glu, g_q_lat, w_uq, g_kv_lat, w_ukv, g_qn_nope, g_qn_rope,
                g_kn_nope, g_kn_rope, g_s5_out, g_mla_out, w_out, g_ffn, w_router, b_router,
                w_mlp1, b_mlp1, w_mlp2, b_mlp2):
    bsz, length, _ = x.shape
    offset = 0 if past is None else past[0].shape[1]
    cos, sin = rope_tables(offset + jnp.arange(length, dtype=F32))
    z = rms_norm(x, g_mix) @ w_in
    u, c_q, c_kv, k_pe = jnp.split(z, [S5_WIDTH, S5_WIDTH + Q_LORA, S5_WIDTH + Q_LORA + KV_LORA], axis=-1)
    h0_re, h0_im = (None, None) if past is None else (past[2], past[3])
    s5_out, h_re, h_im = s5_mixer(u, h0_re, h0_im, lam_re, lam_im, log_dt, b_s5_re, b_s5_im,
                                  c_s5_re, c_s5_im, d_s5, w_glu, b_glu)
    q = mla_queries(c_q, cos, sin, g_q_lat, w_uq, head_gain(g_qn_nope, g_qn_rope))
    lat = rms_norm(c_kv, g_kv_lat)
    k_rot = apply_rope(k_pe, cos, sin)
    g_kn = head_gain(g_kn_nope, g_kn_rope)
    if past is None:
        k, v = mla_keys_values(lat, k_rot, w_ukv, g_kn)
        attn = chunk_causal_attention(q, k, v)
    else:
        k, v = mla_keys_values(jnp.concatenate([past[0].astype(lat.dtype), lat], axis=1),
                               jnp.concatenate([past[1].astype(k_rot.dtype), k_rot], axis=1), w_ukv, g_kn)
        attn = full_attention(q, k, v)
    merged = jnp.concatenate([rms_norm(s5_out, g_s5_out), rms_norm(attn, g_mla_out)], axis=-1)
    x = x + merged @ w_out
    ffn = moe_ffn(rms_norm(x, g_ffn).reshape(bsz * length, D_MODEL), w_router, b_router,
                  w_mlp1, b_mlp1, w_mlp2, b_mlp2)
    x = x + ffn.reshape(bsz, length, D_MODEL)
    return x, lat, k_rot, h_re, h_im


def setup_inputs(seed: int = 0) -> dict:
    key = jax.random.key(seed)
    ks = jax.random.split(key, 40)

    def nrm(k, shape, scale):
        return jax.random.normal(k, shape, F32) * scale

    def gain(k, shape):
        return 1.0 + 0.01 * jax.random.normal(k, shape, F32)

    L = DEPTH
    lam_im = jnp.pi * jnp.arange(S5_STATE, dtype=F32)[None, None, :] + 0.01 * jax.random.normal(ks[7], (L, S5_GROUPS, S5_STATE), F32)
    return {
        'x_prompt': nrm(ks[0], (BATCH, SEQ, D_MODEL), 1.0),
        'x_sample': nrm(ks[1], (DEC_BATCH, DEC_SEQ, D_MODEL), 1.0),
        'cache_kv_latent': nrm(ks[2], (L, DEC_BATCH, PAST_LEN, KV_LORA), 1.0),
        'cache_k_rope': nrm(ks[3], (L, DEC_BATCH, PAST_LEN, QK_ROPE), 1.0),
        'state_s5_re': nrm(ks[4], (L, DEC_BATCH, S5_GROUPS, S5_STATE), 0.1),
        'state_s5_im': nrm(ks[5], (L, DEC_BATCH, S5_GROUPS, S5_STATE), 0.1),
        'g_mix': gain(ks[6], (L, D_MODEL)),
        'w_in': nrm(ks[8], (L, D_MODEL, D_IN), D_MODEL ** -0.5),
        'lam_re': -0.5 + 0.01 * jax.random.normal(ks[9], (L, S5_GROUPS, S5_STATE), F32),
        'lam_im': lam_im,
        'log_dt': jax.random.uniform(ks[10], (L, S5_GROUPS), F32, math.log(DT_MIN), math.log(DT_MAX)),
        'b_s5_re': nrm(ks[11], (L, S5_GROUPS, S5_STATE, S5_GROUP), (2 * S5_GROUP) ** -0.5),
        'b_s5_im': nrm(ks[12], (L, S5_GROUPS, S5_STATE, S5_GROUP), (2 * S5_GROUP) ** -0.5),
        'c_s5_re': nrm(ks[13], (L, S5_GROUPS, S5_GROUP, S5_STATE), (2 * S5_STATE) ** -0.5),
        'c_s5_im': nrm(ks[14], (L, S5_GROUPS, S5_GROUP, S5_STATE), (2 * S5_STATE) ** -0.5),
        'd_s5': nrm(ks[15], (L, S5_WIDTH), 1.0),
        'w_glu': nrm(ks[16], (L, S5_WIDTH, S5_WIDTH), S5_WIDTH ** -0.5),
        'b_glu': nrm(ks[17], (L, S5_WIDTH), 0.01),
        'g_q_lat': gain(ks[18], (L, Q_LORA)),
        'w_uq': nrm(ks[19], (L, Q_LORA, MLA_HEADS * QK_HEAD), Q_LORA ** -0.5),
        'g_kv_lat': gain(ks[20], (L, KV_LORA)),
        'w_ukv': nrm(ks[21], (L, KV_LORA, MLA_HEADS * (QK_NOPE + V_HEAD)), KV_LORA ** -0.5),
        'g_qn_nope': gain(ks[22], (L, QK_NOPE)),
        'g_qn_rope': gain(ks[23], (L, QK_ROPE // 2)),
        'g_kn_nope': gain(ks[24], (L, QK_NOPE)),
        'g_kn_rope': gain(ks[25], (L, QK_ROPE // 2)),
        'g_s5_out': gain(ks[26], (L, S5_WIDTH)),
        'g_mla_out': gain(ks[27], (L, MLA_WIDTH)),
        'w_out': nrm(ks[28], (L, D_MIX, D_MODEL), D_MIX ** -0.5),
        'g_ffn': gain(ks[29], (L, D_MODEL)),
        'w_router': nrm(ks[30], (L, D_MODEL, N_EXPERTS), D_MODEL ** -0.5),
        'b_router': nrm(ks[31], (L, N_EXPERTS), 0.01),
        'w_mlp1': nrm(ks[32], (L, N_EXPERTS, D_MODEL, 2 * D_FF), D_MODEL ** -0.5),
        'b_mlp1': nrm(ks[33], (L, N_EXPERTS, 2 * D_FF), 0.01),
        'w_mlp2': nrm(ks[34], (L, N_EXPERTS, D_FF, D_MODEL), D_FF ** -0.5),
        'b_mlp2': nrm(ks[35], (L, N_EXPERTS, D_MODEL), 0.01),
    }


def reference(x_prompt, x_sample, cache_kv_latent, cache_k_rope, state_s5_re, state_s5_im,
              g_mix, w_in, lam_re, lam_im, log_dt, b_s5_re, b_s5_im, c_s5_re, c_s5_im, d_s5,
              w_glu, b_glu, g_q_lat, w_uq, g_kv_lat, w_ukv, g_qn_nope, g_qn_rope, g_kn_nope,
              g_kn_rope, g_s5_out, g_mla_out, w_out, g_ffn, w_router, b_router, w_mlp1, b_mlp1,
              w_mlp2, b_mlp2):
    yp, ys = x_prompt, x_sample
    p_lat, p_kr, p_re, p_im = [], [], [], []
    s_lat, s_kr, s_re, s_im = [], [], [], []
    for l in range(DEPTH):
        lp = (g_mix[l], w_in[l], lam_re[l], lam_im[l], log_dt[l], b_s5_re[l], b_s5_im[l],
              c_s5_re[l], c_s5_im[l], d_s5[l], w_glu[l], b_glu[l], g_q_lat[l], w_uq[l],
              g_kv_lat[l], w_ukv[l], g_qn_nope[l], g_qn_rope[l], g_kn_nope[l], g_kn_rope[l],
              g_s5_out[l], g_mla_out[l], w_out[l], g_ffn[l], w_router[l], b_router[l],
              w_mlp1[l], b_mlp1[l], w_mlp2[l], b_mlp2[l])
        yp, lat, kr, hre, him = trunk_layer(yp, None, *lp)
        p_lat.append(lat); p_kr.append(kr); p_re.append(hre); p_im.append(him)
        past = (cache_kv_latent[l], cache_k_rope[l], state_s5_re[l], state_s5_im[l])
        ys, lat, kr, hre, him = trunk_layer(ys, past, *lp)
        s_lat.append(lat); s_kr.append(kr); s_re.append(hre); s_im.append(him)
    return (yp, ys, jnp.stack(p_lat), jnp.stack(p_kr), jnp.stack(p_re), jnp.stack(p_im),
            jnp.stack(s_lat), jnp.stack(s_kr), jnp.stack(s_re), jnp.stack(s_im))
```

```python
import functools
import math

import jax
import jax.numpy as jnp
from jax import lax
from jax.experimental import pallas as pl
from jax.experimental.pallas import tpu as pltpu

F32 = jnp.float32
BF16 = jnp.bfloat16

D_MODEL = 1024
S5_WIDTH = 512
S5_GROUP = 16
S5_GROUPS = 32
S5_STATE = 64
MLA_HEADS = 8
QK_NOPE = 64
QK_ROPE = 32
QK_HEAD = QK_NOPE + QK_ROPE
V_HEAD = 64
MLA_WIDTH = MLA_HEADS * V_HEAD
Q_LORA = 384
KV_LORA = 256
ROPE_THETA = 10000.0
CHUNK = 64
N_EXPERTS = 32
TOP_K = 4
D_FF = D_MODEL
SWIGLU_LIMIT = 7.0
SWIGLU_ALPHA = 1.702
EPS = 1e-6

LANES = 128
HEAD_PAD = 128
D_IN_PAD = 1280
S5_CHUNKS = 4
S5_CHUNK_LANES = 1024
S5_STATE_LANES = S5_CHUNKS * S5_CHUNK_LANES
MOE_ROWS = 512
VMEM_LIMIT = 56 * 1024 * 1024


def _cparams(sem, vmem=None):
    return pltpu.CompilerParams(dimension_semantics=sem, vmem_limit_bytes=vmem)


def _rms(x, g, n=None):
    n = x.shape[-1] if n is None else n
    ms = jnp.sum(x * x, axis=-1, keepdims=True) * (1.0 / n)
    return x * lax.rsqrt(ms + EPS) * g


def _prep_kernel(lr_ref, li_ref, ldt_ref, br_ref, bi_ref, ar_ref, ai_ref, bbr_ref, bbi_ref):
    lr = lr_ref[...]
    li = li_ref[...]
    dt = jnp.exp(ldt_ref[...])
    mag = jnp.exp(lr * dt)
    ar = mag * jnp.cos(li * dt)
    ai = mag * jnp.sin(li * dt)
    ar_ref[...] = ar
    ai_ref[...] = ai
    den = lr * lr + li * li
    cr = ((ar - 1.0) * lr + ai * li) / den
    ci = (ai * lr - (ar - 1.0) * li) / den
    br = br_ref[...]
    bi = bi_ref[...]
    bbr_ref[...] = cr[:, None, :] * br - ci[:, None, :] * bi
    bbi_ref[...] = cr[:, None, :] * bi + ci[:, None, :] * br


def _s5_prep(lam_re, lam_im, log_dt, b_re, b_im):
    g, n = lam_re.shape
    p = b_re.shape[-1]
    out = pl.pallas_call(
        _prep_kernel,
        out_shape=(jax.ShapeDtypeStruct((g, n), F32), jax.ShapeDtypeStruct((g, n), F32),
                   jax.ShapeDtypeStruct((g, p, n), F32), jax.ShapeDtypeStruct((g, p, n), F32)),
        name="s5_prep",
    )(lam_re, lam_im, log_dt.reshape(g, 1), jnp.swapaxes(b_re, 1, 2), jnp.swapaxes(b_im, 1, 2))
    return out


def _state_layout(re, im):
    lead = re.shape[:-2]
    re = re.reshape(lead + (S5_CHUNKS, 512))
    im = im.reshape(lead + (S5_CHUNKS, 512))
    return jnp.stack([re, im], axis=-2).reshape(lead + (S5_STATE_LANES,))


def _state_unlayout(h):
    lead = h.shape[:-1]
    h = h.reshape(lead + (S5_CHUNKS, 2, 512))
    re = h[..., 0, :].reshape(lead + (S5_GROUPS, S5_STATE))
    im = h[..., 1, :].reshape(lead + (S5_GROUPS, S5_STATE))
    return re, im


def _s5_weights(bbr, bbi, c_re, c_im):
    eye8 = jnp.eye(8, dtype=F32)

    def blockdiag(m):
        a, b = m.shape[1], m.shape[2]
        return (eye8[:, None, :, None] * m[:, :, None, :]).reshape(8 * a, 8 * b)

    wb, wc = [], []
    for c in range(S5_CHUNKS):
        sl = slice(8 * c, 8 * c + 8)
        wb.append(jnp.concatenate([blockdiag(bbr[sl]), blockdiag(bbi[sl])], axis=1))
        cr_t = jnp.swapaxes(c_re[sl], 1, 2)
        ci_t = jnp.swapaxes(c_im[sl], 1, 2)
        wc.append(jnp.concatenate([blockdiag(cr_t), -blockdiag(ci_t)], axis=0))
    return jnp.stack(wb).astype(BF16), jnp.stack(wc).astype(BF16)


def _rope_swap(x, lane, period):
    return jnp.where(lane, pltpu.roll(x, LANES - 16, axis=1), pltpu.roll(x, 16, axis=1))


def _in_proj_kernel(x_ref, gmix_ref, w_ref, gq_ref, gkv_ref, ck_ref, sk_ref,
                    u_ref, qn_ref, lat_ref, krope_ref, krot_ref):
    x = x_ref[...]
    xn = _rms(x, gmix_ref[...]).astype(BF16)
    z = jnp.dot(xn, w_ref[...], preferred_element_type=F32)
    u_ref[...] = z[:, :S5_WIDTH]
    qn_ref[...] = _rms(z[:, S5_WIDTH:S5_WIDTH + Q_LORA], gq_ref[...]).astype(BF16)
    c0 = S5_WIDTH + Q_LORA
    lat_ref[...] = _rms(z[:, c0:c0 + KV_LORA], gkv_ref[...])
    kp = z[:, c0 + KV_LORA:]
    lane = lax.broadcasted_iota(jnp.int32, kp.shape, 1)
    first_half = (lane % 64) < 16
    sw = jnp.where(first_half, pltpu.roll(kp, LANES - 16, axis=1), pltpu.roll(kp, 16, axis=1))
    kr = kp * ck_ref[...] + sw * sk_ref[...]
    krope_ref[...] = kr[:, :QK_ROPE]
    krot_ref[...] = jnp.where(lane >= 64, kr, 0.0)


def _in_proj(x, g_mix, w_in_p, g_q_lat, g_kv_lat, ck, sk, bsz, length):
    t = bsz * length
    tm = min(512, length)
    n_l = length // tm
    row = lambda i: (i, 0)
    tab = lambda i: (i % n_l, 0)
    full = lambda i: (0, 0)
    return pl.pallas_call(
        _in_proj_kernel,
        grid=(t // tm,),
        in_specs=[pl.BlockSpec((tm, D_MODEL), row),
                  pl.BlockSpec((1, D_MODEL), full),
                  pl.BlockSpec((D_MODEL, D_IN_PAD), full),
                  pl.BlockSpec((1, Q_LORA), full),
                  pl.BlockSpec((1, KV_LORA), full),
                  pl.BlockSpec((tm, LANES), tab),
                  pl.BlockSpec((tm, LANES), tab)],
        out_specs=[pl.BlockSpec((tm, S5_WIDTH), lambda i: (i % n_l, i // n_l)),
                   pl.BlockSpec((tm, Q_LORA), row),
                   pl.BlockSpec((tm, KV_LORA), row),
                   pl.BlockSpec((tm, QK_ROPE), row),
                   pl.BlockSpec((tm, LANES), row)],
        out_shape=(jax.ShapeDtypeStruct((length, bsz * S5_WIDTH), F32),
                   jax.ShapeDtypeStruct((t, Q_LORA), BF16),
                   jax.ShapeDtypeStruct((t, KV_LORA), F32),
                   jax.ShapeDtypeStruct((t, QK_ROPE), F32),
                   jax.ShapeDtypeStruct((t, LANES), F32)),
        compiler_params=_cparams(("parallel",), VMEM_LIMIT),
        name="in_proj",
    )(x, g_mix, w_in_p, g_q_lat, g_kv_lat, ck, sk)


def _s5_kernel(u_ref, h0_ref, lam_ref, wb_ref, wc_ref, dskip_ref, wglu_ref, bglu_ref, gout_ref,
               y_ref, hlast_ref, bu_ref, h_ref, *, bsz, steps):
    c_id = pl.program_id(0)

    @pl.when(c_id == 0)
    def _():
        h_ref[...] = h0_ref[...]

    u = u_ref[...]
    ub = u.astype(BF16)
    for c in range(S5_CHUNKS):
        bu_ref[:, c * S5_CHUNK_LANES:(c + 1) * S5_CHUNK_LANES] = jnp.dot(
            ub[:, c * LANES:(c + 1) * LANES], wb_ref[c], preferred_element_type=F32)

    for c in range(S5_CHUNKS):
        re = slice(c * S5_CHUNK_LANES, c * S5_CHUNK_LANES + 512)
        im = slice(c * S5_CHUNK_LANES + 512, (c + 1) * S5_CHUNK_LANES)
        lam_r = jnp.broadcast_to(lam_ref[:, re], (bsz, 512))
        lam_i = jnp.broadcast_to(lam_ref[:, im], (bsz, 512))

        def step(t, carry, re=re, im=im, lam_r=lam_r, lam_i=lam_i):
            hr, hi = carry
            r0 = pl.multiple_of(t * bsz, bsz)
            nr = lam_r * hr - lam_i * hi + bu_ref[pl.ds(r0, bsz), re]
            ni = lam_r * hi + lam_i * hr + bu_ref[pl.ds(r0, bsz), im]
            bu_ref[pl.ds(r0, bsz), re] = nr
            bu_ref[pl.ds(r0, bsz), im] = ni
            return nr, ni

        hr, hi = lax.fori_loop(0, steps, step, (h_ref[:, re], h_ref[:, im]), unroll=2)
        h_ref[:, re] = hr
        h_ref[:, im] = hi

    ys = []
    for c in range(S5_CHUNKS):
        hs = bu_ref[:, c * S5_CHUNK_LANES:(c + 1) * S5_CHUNK_LANES].astype(BF16)
        ys.append(jnp.dot(hs, wc_ref[c], preferred_element_type=F32))
    y = jnp.concatenate(ys, axis=1) + dskip_ref[...] * u
    y = jax.nn.gelu(y)
    gate = jnp.dot(y.astype(BF16), wglu_ref[...], preferred_element_type=F32) + bglu_ref[...]
    y = y * jax.nn.sigmoid(gate)
    y_ref[...] = _rms(y, gout_ref[...])

    @pl.when(c_id == pl.num_programs(0) - 1)
    def _():
        hlast_ref[...] = h_ref[...]


def _s5(u_tm, h0, lam, wb, wc, d_skip, w_glu, b_glu, g_out, bsz, length):
    steps = min(32, length)
    rows = steps * bsz
    full2 = lambda c: (0, 0)
    full3 = lambda c: (0, 0, 0)
    return pl.pallas_call(
        functools.partial(_s5_kernel, bsz=bsz, steps=steps),
        grid=(length // steps,),
        in_specs=[pl.BlockSpec((rows, S5_WIDTH), lambda c: (c, 0)),
                  pl.BlockSpec((bsz, S5_STATE_LANES), full2),
                  pl.BlockSpec((1, S5_STATE_LANES), full2),
                  pl.BlockSpec((S5_CHUNKS, LANES, S5_CHUNK_LANES), full3),
                  pl.BlockSpec((S5_CHUNKS, S5_CHUNK_LANES, LANES), full3),
                  pl.BlockSpec((1, S5_WIDTH), full2),
                  pl.BlockSpec((S5_WIDTH, S5_WIDTH), full2),
                  pl.BlockSpec((1, S5_WIDTH), full2),
                  pl.BlockSpec((1, S5_WIDTH), full2)],
        out_specs=[pl.BlockSpec((rows, S5_WIDTH), lambda c: (c, 0)),
                   pl.BlockSpec((bsz, S5_STATE_LANES), full2)],
        out_shape=(jax.ShapeDtypeStruct((length * bsz, S5_WIDTH), F32),
                   jax.ShapeDtypeStruct((bsz, S5_STATE_LANES), F32)),
        scratch_shapes=[pltpu.VMEM((rows, S5_STATE_LANES), F32),
                        pltpu.VMEM((bsz, S5_STATE_LANES), F32)],
        compiler_params=_cparams(("arbitrary",), VMEM_LIMIT),
        name="s5_mixer",
    )(u_tm, h0, lam, wb, wc, d_skip, w_glu, b_glu, g_out)


def _q_proj_kernel(qn_ref, w_ref, cq_ref, sq_ref, g_ref, q_ref):
    qf = jnp.dot(qn_ref[...], w_ref[...], preferred_element_type=F32)
    lane = lax.broadcasted_iota(jnp.int32, (qf.shape[0], HEAD_PAD), 1)
    first_half = lane < QK_NOPE + 16
    cq, sq, g = cq_ref[...], sq_ref[...], g_ref[...]
    for h in range(MLA_HEADS):
        x = qf[:, h * HEAD_PAD:(h + 1) * HEAD_PAD]
        sw = jnp.where(first_half, pltpu.roll(x, LANES - 16, axis=1), pltpu.roll(x, 16, axis=1))
        xr = x * cq + sw * sq
        q_ref[0, h] = (_rms(xr, g, QK_HEAD) * (QK_HEAD ** -0.5)).astype(BF16)


def _q_proj(qn, w_uq_p, cq, sq, gq, bsz, length):
    tm = min(512, length)
    n_l = length // tm
    full = lambda b, i: (0, 0)
    return pl.pallas_call(
        _q_proj_kernel,
        grid=(bsz, n_l),
        in_specs=[pl.BlockSpec((tm, Q_LORA), lambda b, i: (b * n_l + i, 0)),
                  pl.BlockSpec((Q_LORA, MLA_HEADS * HEAD_PAD), full),
                  pl.BlockSpec((tm, HEAD_PAD), lambda b, i: (i, 0)),
                  pl.BlockSpec((tm, HEAD_PAD), lambda b, i: (i, 0)),
                  pl.BlockSpec((1, HEAD_PAD), full)],
        out_specs=pl.BlockSpec((1, MLA_HEADS, tm, HEAD_PAD), lambda b, i: (b, 0, i, 0)),
        out_shape=jax.ShapeDtypeStruct((bsz, MLA_HEADS, length, HEAD_PAD), BF16),
        compiler_params=_cparams(("parallel", "parallel"), VMEM_LIMIT),
        name="q_proj",
    )(qn, w_uq_p, cq, sq, gq)


def _kv_proj_kernel(lat_ref, krot_ref, wk_ref, wv_ref, g_ref, k_ref, v_ref):
    lat = lat_ref[...].astype(BF16)
    kf = jnp.dot(lat, wk_ref[...], preferred_element_type=F32)
    vf = jnp.dot(lat, wv_ref[...], preferred_element_type=F32)
    krot = krot_ref[...]
    g = g_ref[...]
    for h in range(MLA_HEADS):
        k = kf[:, h * HEAD_PAD:(h + 1) * HEAD_PAD] + krot
        k_ref[0, h] = _rms(k, g, QK_HEAD).astype(BF16)
    for p in range(MLA_HEADS // 2):
        v_ref[0, p] = vf[:, p * LANES:(p + 1) * LANES].astype(BF16)


def _kv_proj(lat, krot, w_k, w_v, gk, bsz, length):
    tm = min(512, length)
    n_l = length // tm
    full = lambda b, i: (0, 0)
    row = lambda b, i: (b * n_l + i, 0)
    return pl.pallas_call(
        _kv_proj_kernel,
        grid=(bsz, n_l),
        in_specs=[pl.BlockSpec((tm, KV_LORA), row),
                  pl.BlockSpec((tm, HEAD_PAD), row),
                  pl.BlockSpec((KV_LORA, MLA_HEADS * HEAD_PAD), full),
                  pl.BlockSpec((KV_LORA, MLA_WIDTH), full),
                  pl.BlockSpec((1, HEAD_PAD), full)],
        out_specs=[pl.BlockSpec((1, MLA_HEADS, tm, HEAD_PAD), lambda b, i: (b, 0, i, 0)),
                   pl.BlockSpec((1, MLA_HEADS // 2, tm, LANES), lambda b, i: (b, 0, i, 0))],
        out_shape=(jax.ShapeDtypeStruct((bsz, MLA_HEADS, length, HEAD_PAD), BF16),
                   jax.ShapeDtypeStruct((bsz, MLA_HEADS // 2, length, LANES), BF16)),
        compiler_params=_cparams(("parallel", "parallel"), VMEM_LIMIT),
        name="kv_proj",
    )(lat, krot, w_k, w_v, gk)


def _scores(q, k):
    return lax.dot_general(q, k, (((1,), (1,)), ((), ())), preferred_element_type=F32)


def _attn_prompt_kernel(q_ref, k_ref, v_ref, o_ref, *, length, tq):
    n_q = length // tq
    row = lax.broadcasted_iota(jnp.int32, (tq, tq), 0)
    col = lax.broadcasted_iota(jnp.int32, (tq, tq), 1)
    visible = (col // CHUNK) <= (row // CHUNK)
    lane = lax.broadcasted_iota(jnp.int32, (tq, LANES), 1)

    def q_body(qi, _):
        q0 = pl.multiple_of(qi * tq, tq)
        outs = []
        for hh in range(2):
            q = q_ref[0, hh, pl.ds(q0, tq), :]

            def kv_step(kj, carry, masked, hh=hh, q=q):
                m, l, acc = carry
                k0 = pl.multiple_of(kj * tq, tq)
                s = _scores(q, k_ref[0, hh, pl.ds(k0, tq), :])
                if masked:
                    s = jnp.where(visible, s, -jnp.inf)
                m_new = jnp.maximum(m, jnp.max(s, axis=-1, keepdims=True))
                alpha = jnp.exp(m - m_new)
                p = jnp.exp(s - m_new)
                l = alpha * l + jnp.sum(p, axis=-1, keepdims=True)
                acc = alpha * acc + jnp.dot(p.astype(BF16), v_ref[0, 0, pl.ds(k0, tq), :],
                                            preferred_element_type=F32)
                return m_new, l, acc

            init = (jnp.full((tq, 1), -jnp.inf, F32), jnp.zeros((tq, 1), F32), jnp.zeros((tq, LANES), F32))
            carry = lax.fori_loop(0, qi, functools.partial(kv_step, masked=False), init)
            m, l, acc = kv_step(qi, carry, True)
            outs.append(acc / l)
        o_ref[0, pl.ds(q0, tq), :] = jnp.where(lane < V_HEAD, outs[0], outs[1])
        return 0

    lax.fori_loop(0, n_q, q_body, 0)


def _attn_prompt(q, k, v, bsz, length):
    tq = 256
    pairs = MLA_HEADS // 2
    return pl.pallas_call(
        functools.partial(_attn_prompt_kernel, length=length, tq=tq),
        grid=(bsz, pairs),
        in_specs=[pl.BlockSpec((1, 2, length, HEAD_PAD), lambda b, p: (b, p, 0, 0)),
                  pl.BlockSpec((1, 2, length, HEAD_PAD), lambda b, p: (b, p, 0, 0)),
                  pl.BlockSpec((1, 1, length, LANES), lambda b, p: (b, p, 0, 0))],
        out_specs=pl.BlockSpec((1, length, LANES), lambda b, p: (b, 0, p)),
        out_shape=jax.ShapeDtypeStruct((bsz, length, MLA_WIDTH), F32),
        compiler_params=_cparams(("parallel", "parallel"), VMEM_LIMIT),
        name="attn_prompt",
    )(q, k, v)


def _attn_sample_kernel(q_ref, kc_ref, vc_ref, kn_ref, vn_ref, o_ref):
    lq = q_ref.shape[2]
    lane = lax.broadcasted_iota(jnp.int32, (lq, LANES), 1)
    outs = []
    for hh in range(2):
        q = q_ref[0, hh]
        sc = _scores(q, kc_ref[0, hh])
        sn = _scores(q, kn_ref[0, hh])
        m = jnp.maximum(jnp.max(sc, axis=-1, keepdims=True), jnp.max(sn, axis=-1, keepdims=True))
        pc = jnp.exp(sc - m)
        pn = jnp.exp(sn - m)
        l = jnp.sum(pc, axis=-1, keepdims=True) + jnp.sum(pn, axis=-1, keepdims=True)
        acc = (jnp.dot(pc.astype(BF16), vc_ref[0, 0], preferred_element_type=F32)
               + jnp.dot(pn.astype(BF16), vn_ref[0, 0], preferred_element_type=F32))
        outs.append(acc / l)
    o_ref[0] = jnp.where(lane < V_HEAD, outs[0], outs[1])


def _attn_sample(q, kc, vc, kn, vn, bsz, lq, past):
    pairs = MLA_HEADS // 2
    hp = lambda b, p: (b, p, 0, 0)
    return pl.pallas_call(
        _attn_sample_kernel,
        grid=(bsz, pairs),
        in_specs=[pl.BlockSpec((1, 2, lq, HEAD_PAD), hp),
                  pl.BlockSpec((1, 2, past, HEAD_PAD), hp),
                  pl.BlockSpec((1, 1, past, LANES), hp),
                  pl.BlockSpec((1, 2, lq, HEAD_PAD), hp),
                  pl.BlockSpec((1, 1, lq, LANES), hp)],
        out_specs=pl.BlockSpec((1, lq, LANES), lambda b, p: (b, 0, p)),
        out_shape=jax.ShapeDtypeStruct((bsz, lq, MLA_WIDTH), F32),
        compiler_params=_cparams(("parallel", "parallel"), VMEM_LIMIT),
        name="attn_sample",
    )(q, kc, vc, kn, vn)


def _out_proj_kernel(x_ref, s5_ref, at_ref, gmla_ref, wout_ref, gffn_ref, wr_ref, br_ref,
                     x1_ref, hn_ref, idx_ref, gate_ref):
    an = _rms(at_ref[...], gmla_ref[...])
    merged = jnp.concatenate([s5_ref[...].astype(BF16), an.astype(BF16)], axis=1)
    x1 = x_ref[...] + jnp.dot(merged, wout_ref[...], preferred_element_type=F32)
    x1_ref[...] = x1
    hn = _rms(x1, gffn_ref[...])
    hn_ref[...] = hn
    logits = jnp.dot(hn, wr_ref[...], precision=lax.Precision.HIGHEST,
                     preferred_element_type=F32) + br_ref[...]
    lane = lax.broadcasted_iota(jnp.int32, logits.shape, 1)
    lane_f = lane.astype(F32)
    work = jnp.where(lane < N_EXPERTS, logits, -jnp.inf)
    vals, idxs = [], []
    for _ in range(TOP_K):
        m = jnp.max(work, axis=-1, keepdims=True)
        i = jnp.min(jnp.where(work == m, lane_f, float(LANES)), axis=-1, keepdims=True)
        vals.append(m)
        idxs.append(i)
        work = jnp.where(lane_f == i, -jnp.inf, work)
    es = [jnp.exp(v - vals[0]) for v in vals]
    den = es[0] + es[1] + es[2] + es[3]
    idx_out = jnp.zeros(logits.shape, F32)
    gate_out = jnp.zeros(logits.shape, F32)
    for k in range(TOP_K):
        idx_out = jnp.where(lane == k, idxs[k], idx_out)
        gate_out = jnp.where(lane == k, es[k] / den, gate_out)
    idx_ref[...] = idx_out.astype(jnp.int32)
    gate_ref[...] = gate_out


def _out_proj(x, s5n_tm, attn, g_mla, w_out, g_ffn, w_r_p, b_r_p, bsz, length):
    t = bsz * length
    tm = min(512, length)
    n_l = length // tm
    row = lambda i: (i, 0)
    full = lambda i: (0, 0)
    return pl.pallas_call(
        _out_proj_kernel,
        grid=(t // tm,),
        in_specs=[pl.BlockSpec((tm, D_MODEL), row),
                  pl.BlockSpec((tm, S5_WIDTH), lambda i: (i % n_l, i // n_l)),
                  pl.BlockSpec((tm, MLA_WIDTH), row),
                  pl.BlockSpec((1, MLA_WIDTH), full),
                  pl.BlockSpec((D_MODEL, D_MODEL), full),
                  pl.BlockSpec((1, D_MODEL), full),
                  pl.BlockSpec((D_MODEL, LANES), full),
                  pl.BlockSpec((1, LANES), full)],
        out_specs=[pl.BlockSpec((tm, D_MODEL), row),
                   pl.BlockSpec((tm, D_MODEL), row),
                   pl.BlockSpec((tm, LANES), row),
                   pl.BlockSpec((tm, LANES), row)],
        out_shape=(jax.ShapeDtypeStruct((t, D_MODEL), F32),
                   jax.ShapeDtypeStruct((t, D_MODEL), F32),
                   jax.ShapeDtypeStruct((t, LANES), jnp.int32),
                   jax.ShapeDtypeStruct((t, LANES), F32)),
        compiler_params=_cparams(("parallel",), VMEM_LIMIT),
        name="out_proj",
    )(x, s5n_tm, attn, g_mla, w_out, g_ffn, w_r_p, b_r_p)


def _row_copy(src_hbm, dst, sem, src_row, dst_row):
    return pltpu.make_async_copy(src_hbm.at[pl.ds(src_row, 1)], dst.at[pl.ds(dst_row, 1)], sem)


def _gather_rows(idx_ref, src_hbm, buf, sem, n_rows):
    def issue(r, c):
        _row_copy(src_hbm, buf, sem, idx_ref[0, 0, r], r).start()
        return c

    def drain(r, c):
        _row_copy(src_hbm, buf, sem, 0, r).wait()
        return c

    lax.fori_loop(0, n_rows, issue, 0, unroll=8)
    lax.fori_loop(0, n_rows, drain, 0, unroll=8)


def _dispatch_kernel(nb_ref, idx_ref, src_hbm, o_ref, buf, sem):
    @pl.when(pl.program_id(0) < nb_ref[0])
    def _():
        _gather_rows(idx_ref, src_hbm, buf, sem, buf.shape[0])
        o_ref[...] = buf[...].astype(BF16)

    @pl.when(pl.program_id(0) >= nb_ref[0])
    def _():
        o_ref[...] = jnp.zeros(o_ref.shape, o_ref.dtype)


def _dispatch(nb_used, rows, hn, n_blocks):
    return pl.pallas_call(
        _dispatch_kernel,
        grid_spec=pltpu.PrefetchScalarGridSpec(
            num_scalar_prefetch=1,
            grid=(n_blocks,),
            in_specs=[pl.BlockSpec((1, 1, MOE_ROWS), lambda b, nb: (b, 0, 0), memory_space=pltpu.SMEM),
                      pl.BlockSpec(memory_space=pl.ANY)],
            out_specs=pl.BlockSpec((MOE_ROWS, D_MODEL), lambda b, nb: (b, 0)),
            scratch_shapes=[pltpu.VMEM((MOE_ROWS, D_MODEL), F32), pltpu.SemaphoreType.DMA(())]),
        out_shape=jax.ShapeDtypeStruct((n_blocks * MOE_ROWS, D_MODEL), BF16),
        compiler_params=_cparams(("arbitrary",), VMEM_LIMIT),
        name="moe_dispatch",
    )(nb_used, rows.reshape(n_blocks, 1, MOE_ROWS), hn)


def _expert_kernel(be_ref, nb_ref, x_ref, w1_ref, b1_ref, w2_ref, b2_ref, y_ref, w1b_ref, w2b_ref):
    b = pl.program_id(0)
    prev = be_ref[jnp.maximum(b - 1, 0)]

    @pl.when(jnp.logical_or(b == 0, be_ref[b] != prev))
    def _():
        w1b_ref[...] = w1_ref[0].astype(BF16)
        w2b_ref[...] = w2_ref[0].astype(BF16)

    @pl.when(b < nb_ref[0])
    def _():
        hg = jnp.dot(x_ref[...], w1b_ref[...], preferred_element_type=F32) + b1_ref[0]
        gate = jnp.minimum(hg[:, :D_FF], SWIGLU_LIMIT)
        lin = jnp.clip(hg[:, D_FF:], -SWIGLU_LIMIT, SWIGLU_LIMIT)
        act = gate * jax.nn.sigmoid(SWIGLU_ALPHA * gate) * (lin + 1.0)
        y_ref[...] = jnp.dot(act.astype(BF16), w2b_ref[...], preferred_element_type=F32) + b2_ref[0]

    @pl.when(b >= nb_ref[0])
    def _():
        y_ref[...] = jnp.zeros(y_ref.shape, y_ref.dtype)


def _experts(block_e, nb_used, xs, w1, b1, w2, b2, n_blocks):
    return pl.pallas_call(
        _expert_kernel,
        grid_spec=pltpu.PrefetchScalarGridSpec(
            num_scalar_prefetch=2,
            grid=(n_blocks,),
            in_specs=[pl.BlockSpec((MOE_ROWS, D_MODEL), lambda b, be, nb: (b, 0)),
                      pl.BlockSpec((1, D_MODEL, 2 * D_FF), lambda b, be, nb: (be[b], 0, 0)),
                      pl.BlockSpec((1, 1, 2 * D_FF), lambda b, be, nb: (be[b], 0, 0)),
                      pl.BlockSpec((1, D_FF, D_MODEL), lambda b, be, nb: (be[b], 0, 0)),
                      pl.BlockSpec((1, 1, D_MODEL), lambda b, be, nb: (be[b], 0, 0))],
            out_specs=pl.BlockSpec((MOE_ROWS, D_MODEL), lambda b, be, nb: (b, 0)),
            scratch_shapes=[pltpu.VMEM((D_MODEL, 2 * D_FF), BF16), pltpu.VMEM((D_FF, D_MODEL), BF16)]),
        out_shape=jax.ShapeDtypeStruct((n_blocks * MOE_ROWS, D_MODEL), F32),
        compiler_params=_cparams(("arbitrary",), VMEM_LIMIT),
        name="moe_experts",
    )(block_e, nb_used, xs, w1, b1.reshape(N_EXPERTS, 1, 2 * D_FF), w2, b2.reshape(N_EXPERTS, 1, D_MODEL))


def _combine_kernel(slot_ref, gate_ref, x1_ref, yb_hbm, o_ref, buf, sem):
    tmc = o_ref.shape[0]
    _gather_rows(slot_ref, yb_hbm, buf, sem, TOP_K * tmc)
    g = gate_ref[...]
    acc = x1_ref[...]
    for k in range(TOP_K):
        acc = acc + g[:, k:k + 1] * buf[k * tmc:(k + 1) * tmc, :]
    o_ref[...] = acc


def _combine(slot_km, gates, x1, yb, row0, n_tok, tmc):
    blk0 = row0 // tmc
    return pl.pallas_call(
        _combine_kernel,
        grid=(n_tok // tmc,),
        in_specs=[pl.BlockSpec((1, 1, TOP_K * tmc), lambda i: (blk0 + i, 0, 0), memory_space=pltpu.SMEM),
                  pl.BlockSpec((tmc, LANES), lambda i: (blk0 + i, 0)),
                  pl.BlockSpec((tmc, D_MODEL), lambda i: (blk0 + i, 0)),
                  pl.BlockSpec(memory_space=pl.ANY)],
        out_specs=pl.BlockSpec((tmc, D_MODEL), lambda i: (i, 0)),
        out_shape=jax.ShapeDtypeStruct((n_tok, D_MODEL), F32),
        scratch_shapes=[pltpu.VMEM((TOP_K * tmc, D_MODEL), F32), pltpu.SemaphoreType.DMA(())],
        compiler_params=_cparams(("arbitrary",), VMEM_LIMIT),
        name="moe_combine",
    )(slot_km, gates, x1, yb)


def _moe(hn, top_idx, n_tok):
    n_assign = n_tok * TOP_K
    flat_e = top_idx.reshape(-1)
    order = jnp.argsort(flat_e)
    e_sorted = flat_e[order]
    tok_sorted = (order // TOP_K).astype(jnp.int32)
    counts = jnp.bincount(flat_e, length=N_EXPERTS)
    padded = (counts + MOE_ROWS - 1) // MOE_ROWS * MOE_ROWS
    pad_end = jnp.cumsum(padded)
    start = jnp.cumsum(counts) - counts
    dest = ((pad_end - padded)[e_sorted] + jnp.arange(n_assign) - start[e_sorted]).astype(jnp.int32)
    n_blocks = -(-(n_assign + N_EXPERTS * (MOE_ROWS - 1)) // MOE_ROWS)
    rows = jnp.zeros((n_blocks * MOE_ROWS,), jnp.int32).at[dest].set(tok_sorted)
    block_e = jnp.minimum(jnp.searchsorted(pad_end, jnp.arange(n_blocks) * MOE_ROWS, side='right'),
                          N_EXPERTS - 1).astype(jnp.int32)
    slot = jnp.zeros((n_assign,), jnp.int32).at[order].set(dest)
    nb_used = (pad_end[-1] // MOE_ROWS).astype(jnp.int32).reshape(1)
    return rows, block_e, slot, nb_used, n_blocks


def _rope_tables(offset, length):
    pos = offset + jnp.arange(length, dtype=F32)
    inv = ROPE_THETA ** (-jnp.arange(QK_ROPE // 2, dtype=F32) * (2.0 / QK_ROPE))
    ang = pos[:, None] * inv[None, :]
    cos, sin = jnp.cos(ang), jnp.sin(ang)
    z32 = jnp.zeros((length, 32), F32)
    rot_c = jnp.concatenate([cos, cos], axis=1)
    rot_s = jnp.concatenate([-sin, sin], axis=1)
    ck = jnp.concatenate([rot_c, z32, rot_c, z32], axis=1)
    sk = jnp.concatenate([rot_s, z32, rot_s, z32], axis=1)
    cq = jnp.concatenate([jnp.ones((length, QK_NOPE), F32), rot_c, z32], axis=1)
    sq = jnp.concatenate([jnp.zeros((length, QK_NOPE), F32), rot_s, z32], axis=1)
    return ck, sk, cq, sq


def _head_gain(g_nope, g_rope):
    return jnp.concatenate([g_nope, g_rope, g_rope, jnp.zeros((HEAD_PAD - QK_HEAD,), F32)]).reshape(1, HEAD_PAD)


def _mixer_stage(x, past, prm, bsz, length):
    offset = 0 if past is None else past[0].shape[1]
    ck, sk, cq, sq = _rope_tables(offset, length)
    x2 = x.reshape(bsz * length, D_MODEL)
    u_tm, qn, lat, k_rope, krot = _in_proj(x2, prm['g_mix'], prm['w_in_p'], prm['g_q_lat'], prm['g_kv_lat'],
                                           ck, sk, bsz, length)
    if past is None:
        h0 = jnp.zeros((bsz, S5_STATE_LANES), F32)
    else:
        h0 = _state_layout(past[2], past[3])
    s5n_tm, h_last = _s5(u_tm.reshape(length * bsz, S5_WIDTH), h0, prm['lam'], prm['wb'], prm['wc'],
                         prm['d_skip'], prm['w_glu'], prm['b_glu'], prm['g_s5_out'], bsz, length)
    h_re, h_im = _state_unlayout(h_last)
    q = _q_proj(qn, prm['w_uq_p'], cq, sq, prm['gq'], bsz, length)
    k, v = _kv_proj(lat, krot, prm['w_k'], prm['w_v'], prm['gk'], bsz, length)
    if past is None:
        attn = _attn_prompt(q, k, v, bsz, length)
    else:
        n_past = past[0].shape[1]
        c_lat = past[0].reshape(bsz * n_past, KV_LORA)
        c_rot = jnp.pad(past[1].reshape(bsz * n_past, QK_ROPE), ((0, 0), (QK_NOPE, HEAD_PAD - QK_HEAD)))
        kc, vc = _kv_proj(c_lat, c_rot, prm['w_k'], prm['w_v'], prm['gk'], bsz, n_past)
        attn = _attn_sample(q, kc, vc, k, v, bsz, length, n_past)
    x1, hn, idx, gates = _out_proj(x2, s5n_tm.reshape(length, bsz * S5_WIDTH),
                                   attn.reshape(bsz * length, MLA_WIDTH), prm['g_mla_out'], prm['w_out'],
                                   prm['g_ffn'], prm['w_r_p'], prm['b_r_p'], bsz, length)
    return (x1, hn, idx, gates,
            lat.reshape(bsz, length, KV_LORA), k_rope.reshape(bsz, length, QK_ROPE), h_re, h_im)


def _prepare(g_mix, w_in, lam_re, lam_im, log_dt, b_s5_re, b_s5_im, c_s5_re, c_s5_im, d_s5, w_glu, b_glu,
             g_q_lat, w_uq, g_kv_lat, w_ukv, g_qn_nope, g_qn_rope, g_kn_nope, g_kn_rope, g_s5_out,
             g_mla_out, w_out, g_ffn, w_router, b_router):
    c0 = S5_WIDTH + Q_LORA + KV_LORA
    w_pe = w_in[:, c0:]
    z = jnp.zeros((D_MODEL, 32), F32)
    w_in_p = jnp.concatenate([w_in[:, :c0], w_pe, z, w_pe, z], axis=1).astype(BF16)
    ar, ai, bbr, bbi = _s5_prep(lam_re, lam_im, log_dt, b_s5_re, b_s5_im)
    wb, wc = _s5_weights(bbr, bbi, c_s5_re, c_s5_im)
    w_uq_p = jnp.pad(w_uq.reshape(Q_LORA, MLA_HEADS, QK_HEAD), ((0, 0), (0, 0), (0, HEAD_PAD - QK_HEAD)))
    w_kv = w_ukv.reshape(KV_LORA, MLA_HEADS, QK_NOPE + V_HEAD)
    w_k = jnp.pad(w_kv[:, :, :QK_NOPE], ((0, 0), (0, 0), (0, HEAD_PAD - QK_NOPE)))
    w_v = w_kv[:, :, QK_NOPE:]
    return dict(
        g_mix=g_mix.reshape(1, -1), w_in_p=w_in_p,
        g_q_lat=g_q_lat.reshape(1, -1), g_kv_lat=g_kv_lat.reshape(1, -1),
        lam=_state_layout(ar, ai).reshape(1, S5_STATE_LANES), wb=wb, wc=wc,
        d_skip=d_s5.reshape(1, -1), w_glu=w_glu.astype(BF16), b_glu=b_glu.reshape(1, -1),
        g_s5_out=g_s5_out.reshape(1, -1),
        w_uq_p=w_uq_p.reshape(Q_LORA, MLA_HEADS * HEAD_PAD).astype(BF16),
        w_k=w_k.reshape(KV_LORA, MLA_HEADS * HEAD_PAD).astype(BF16),
        w_v=w_v.reshape(KV_LORA, MLA_WIDTH).astype(BF16),
        gq=_head_gain(g_qn_nope, g_qn_rope), gk=_head_gain(g_kn_nope, g_kn_rope),
        g_mla_out=g_mla_out.reshape(1, -1), w_out=w_out.astype(BF16), g_ffn=g_ffn.reshape(1, -1),
        w_r_p=jnp.pad(w_router, ((0, 0), (0, LANES - N_EXPERTS))),
        b_r_p=jnp.pad(b_router, (0, LANES - N_EXPERTS)).reshape(1, LANES),
    )


def _layer(xp, xs, cache_lat, cache_kr, st_re, st_im, mixer_w, w1, b1, w2, b2):
    prm = _prepare(*mixer_w)
    bp, lp, _ = xp.shape
    bs, ls, _ = xs.shape
    x1p, hnp, idxp, gatep, latp, krp, hrp, hip = _mixer_stage(xp, None, prm, bp, lp)
    x1s, hns, idxs, gates, lats, krs, hrs, his = _mixer_stage(xs, (cache_lat, cache_kr, st_re, st_im),
                                                              prm, bs, ls)
    n_p, n_s = bp * lp, bs * ls
    n_tok = n_p + n_s
    hn = jnp.concatenate([hnp, hns], axis=0)
    x1 = jnp.concatenate([x1p, x1s], axis=0)
    idx = jnp.concatenate([idxp, idxs], axis=0)
    gate = jnp.concatenate([gatep, gates], axis=0)
    rows, block_e, slot, nb_used, n_blocks = _moe(hn, idx[:, :TOP_K], n_tok)
    xs_sorted = _dispatch(nb_used, rows, hn, n_blocks)
    yb = _experts(block_e, nb_used, xs_sorted, w1, b1, w2, b2, n_blocks)
    tmc = 128
    slot_km = slot.reshape(n_tok // tmc, tmc, TOP_K).transpose(0, 2, 1).reshape(n_tok // tmc, 1, TOP_K * tmc)
    yp = _combine(slot_km, gate, x1, yb, 0, n_p, tmc).reshape(bp, lp, D_MODEL)
    ys = _combine(slot_km, gate, x1, yb, n_p, n_s, tmc).reshape(bs, ls, D_MODEL)
    return yp, ys, latp, krp, hrp, hip, lats, krs, hrs, his


def kernel(x_prompt, x_sample, cache_kv_latent, cache_k_rope, state_s5_re, state_s5_im, g_mix, w_in, lam_re,
           lam_im, log_dt, b_s5_re, b_s5_im, c_s5_re, c_s5_im, d_s5, w_glu, b_glu, g_q_lat, w_uq, g_kv_lat,
           w_ukv, g_qn_nope, g_qn_rope, g_kn_nope, g_kn_rope, g_s5_out, g_mla_out, w_out, g_ffn, w_router,
           b_router, w_mlp1, b_mlp1, w_mlp2, b_mlp2):
    depth = g_mix.shape[0]
    yp, ys = x_prompt, x_sample
    outs = [[] for _ in range(8)]
    for l in range(depth):
        mixer_w = (g_mix[l], w_in[l], lam_re[l], lam_im[l], log_dt[l], b_s5_re[l], b_s5_im[l], c_s5_re[l],
                   c_s5_im[l], d_s5[l], w_glu[l], b_glu[l], g_q_lat[l], w_uq[l], g_kv_lat[l], w_ukv[l],
                   g_qn_nope[l], g_qn_rope[l], g_kn_nope[l], g_kn_rope[l], g_s5_out[l], g_mla_out[l],
                   w_out[l], g_ffn[l], w_router[l], b_router[l])
        res = _layer(yp, ys, cache_kv_latent[l], cache_k_rope[l], state_s5_re[l], state_s5_im[l], mixer_w,
                     w_mlp1[l], b_mlp1[l], w_mlp2[l], b_mlp2[l])
        yp, ys = res[0], res[1]
        for o, r in zip(outs, res[2:]):
            o.append(r)
    return (yp, ys) + tuple(jnp.stack(o) for o in outs)
```

```python
import functools
import math

import jax
import jax.numpy as jnp
from jax import lax
from jax.experimental import pallas as pl
from jax.experimental.pallas import tpu as pltpu

F32 = jnp.float32
BF16 = jnp.bfloat16

D_MODEL = 1024
S5_WIDTH = 512
S5_GROUP = 16
S5_GROUPS = 32
S5_STATE = 64
MLA_HEADS = 8
QK_NOPE = 64
QK_ROPE = 32
QK_HEAD = QK_NOPE + QK_ROPE
V_HEAD = 64
MLA_WIDTH = MLA_HEADS * V_HEAD
Q_LORA = 384
KV_LORA = 256
ROPE_THETA = 10000.0
CHUNK = 64
N_EXPERTS = 32
TOP_K = 4
D_FF = D_MODEL
SWIGLU_LIMIT = 7.0
SWIGLU_ALPHA = 1.702
EPS = 1e-6

LANES = 128
HEAD_PAD = 128
D_IN_PAD = 1280
S5_CHUNKS = 4
S5_CHUNK_LANES = 1024
S5_STATE_LANES = S5_CHUNKS * S5_CHUNK_LANES
MOE_ROWS = 512
VMEM_LIMIT = 56 * 1024 * 1024


def _cparams(sem, vmem=None):
    return pltpu.CompilerParams(dimension_semantics=sem, vmem_limit_bytes=vmem)


def _rms(x, g, n=None):
    n = x.shape[-1] if n is None else n
    ms = jnp.sum(x * x, axis=-1, keepdims=True) * (1.0 / n)
    return x * lax.rsqrt(ms + EPS) * g


def _prep_kernel(lr_ref, li_ref, ldt_ref, br_ref, bi_ref, ar_ref, ai_ref, bbr_ref, bbi_ref):
    lr = lr_ref[...]
    li = li_ref[...]
    dt = jnp.exp(ldt_ref[...])
    mag = jnp.exp(lr * dt)
    ar = mag * jnp.cos(li * dt)
    ai = mag * jnp.sin(li * dt)
    ar_ref[...] = ar
    ai_ref[...] = ai
    den = lr * lr + li * li
    cr = ((ar - 1.0) * lr + ai * li) / den
    ci = (ai * lr - (ar - 1.0) * li) / den
    br = br_ref[...]
    bi = bi_ref[...]
    bbr_ref[...] = cr[:, None, :] * br - ci[:, None, :] * bi
    bbi_ref[...] = cr[:, None, :] * bi + ci[:, None, :] * br


def _s5_prep(lam_re, lam_im, log_dt, b_re, b_im):
    g, n = lam_re.shape
    p = b_re.shape[-1]
    out = pl.pallas_call(
        _prep_kernel,
        out_shape=(jax.ShapeDtypeStruct((g, n), F32), jax.ShapeDtypeStruct((g, n), F32),
                   jax.ShapeDtypeStruct((g, p, n), F32), jax.ShapeDtypeStruct((g, p, n), F32)),
        name="s5_prep",
    )(lam_re, lam_im, log_dt.reshape(g, 1), jnp.swapaxes(b_re, 1, 2), jnp.swapaxes(b_im, 1, 2))
    return out


def _state_layout(re, im):
    lead = re.shape[:-2]
    re = re.reshape(lead + (S5_CHUNKS, 512))
    im = im.reshape(lead + (S5_CHUNKS, 512))
    return jnp.stack([re, im], axis=-2).reshape(lead + (S5_STATE_LANES,))


def _state_unlayout(h):
    lead = h.shape[:-1]
    h = h.reshape(lead + (S5_CHUNKS, 2, 512))
    re = h[..., 0, :].reshape(lead + (S5_GROUPS, S5_STATE))
    im = h[..., 1, :].reshape(lead + (S5_GROUPS, S5_STATE))
    return re, im


def _s5_weights(bbr, bbi, c_re, c_im):
    eye8 = jnp.eye(8, dtype=F32)

    def blockdiag(m):
        a, b = m.shape[1], m.shape[2]
        return (eye8[:, None, :, None] * m[:, :, None, :]).reshape(8 * a, 8 * b)

    wb, wc = [], []
    for c in range(S5_CHUNKS):
        sl = slice(8 * c, 8 * c + 8)
        wb.append(jnp.concatenate([blockdiag(bbr[sl]), blockdiag(bbi[sl])], axis=1))
        cr_t = jnp.swapaxes(c_re[sl], 1, 2)
        ci_t = jnp.swapaxes(c_im[sl], 1, 2)
        wc.append(jnp.concatenate([blockdiag(cr_t), -blockdiag(ci_t)], axis=0))
    return jnp.stack(wb).astype(BF16), jnp.stack(wc).astype(BF16)


def _in_proj_kernel(x_ref, gmix_ref, w_ref, gq_ref, gkv_ref, ck_ref, sk_ref,
                    u_ref, qn_ref, lat_ref, krope_ref, krot_ref):
    x = x_ref[...]
    xn = _rms(x, gmix_ref[...]).astype(BF16)
    z = jnp.dot(xn, w_ref[...], preferred_element_type=F32)
    u_ref[...] = z[:, :S5_WIDTH]
    qn_ref[...] = _rms(z[:, S5_WIDTH:S5_WIDTH + Q_LORA], gq_ref[...]).astype(BF16)
    c0 = S5_WIDTH + Q_LORA
    lat_ref[...] = _rms(z[:, c0:c0 + KV_LORA], gkv_ref[...])
    kp = z[:, c0 + KV_LORA:]
    lane = lax.broadcasted_iota(jnp.int32, kp.shape, 1)
    first_half = (lane % 64) < 16
    sw = jnp.where(first_half, pltpu.roll(kp, LANES - 16, axis=1), pltpu.roll(kp, 16, axis=1))
    kr = kp * ck_ref[...] + sw * sk_ref[...]
    krope_ref[...] = kr[:, :QK_ROPE]
    krot_ref[...] = jnp.where(lane >= 64, kr, 0.0)


def _in_proj(x, g_mix, w_in_p, g_q_lat, g_kv_lat, ck, sk, bsz, length):
    t = bsz * length
    tm = min(512, length)
    n_l = length // tm
    row = lambda i: (i, 0)
    tab = lambda i: (i % n_l, 0)
    full = lambda i: (0, 0)
    return pl.pallas_call(
        _in_proj_kernel,
        grid=(t // tm,),
        in_specs=[pl.BlockSpec((tm, D_MODEL), row),
                  pl.BlockSpec((1, D_MODEL), full),
                  pl.BlockSpec((D_MODEL, D_IN_PAD), full),
                  pl.BlockSpec((1, Q_LORA), full),
                  pl.BlockSpec((1, KV_LORA), full),
                  pl.BlockSpec((tm, LANES), tab),
                  pl.BlockSpec((tm, LANES), tab)],
        out_specs=[pl.BlockSpec((tm, S5_WIDTH), lambda i: (i % n_l, i // n_l)),
                   pl.BlockSpec((tm, Q_LORA), row),
                   pl.BlockSpec((tm, KV_LORA), row),
                   pl.BlockSpec((tm, QK_ROPE), row),
                   pl.BlockSpec((tm, LANES), row)],
        out_shape=(jax.ShapeDtypeStruct((length, bsz * S5_WIDTH), F32),
                   jax.ShapeDtypeStruct((t, Q_LORA), BF16),
                   jax.ShapeDtypeStruct((t, KV_LORA), F32),
                   jax.ShapeDtypeStruct((t, QK_ROPE), F32),
                   jax.ShapeDtypeStruct((t, LANES), F32)),
        compiler_params=_cparams(("parallel",), VMEM_LIMIT),
        name="in_proj",
    )(x, g_mix, w_in_p, g_q_lat, g_kv_lat, ck, sk)


def _s5_kernel(u_ref, h0_ref, lam_ref, wb_ref, wc_ref, dskip_ref, wglu_ref, bglu_ref, gout_ref,
               y_ref, hlast_ref, bu_ref, h_ref, *, bsz, steps):
    c_id = pl.program_id(0)

    @pl.when(c_id == 0)
    def _():
        h_ref[...] = h0_ref[...]

    u = u_ref[...]
    ub = u.astype(BF16)
    for c in range(S5_CHUNKS):
        bu_ref[:, c * S5_CHUNK_LANES:(c + 1) * S5_CHUNK_LANES] = jnp.dot(
            ub[:, c * LANES:(c + 1) * LANES], wb_ref[c], preferred_element_type=F32)

    for c in range(S5_CHUNKS):
        re = slice(c * S5_CHUNK_LANES, c * S5_CHUNK_LANES + 512)
        im = slice(c * S5_CHUNK_LANES + 512, (c + 1) * S5_CHUNK_LANES)
        lam_r = jnp.broadcast_to(lam_ref[:, re], (bsz, 512))
        lam_i = jnp.broadcast_to(lam_ref[:, im], (bsz, 512))

        def step(t, carry, re=re, im=im, lam_r=lam_r, lam_i=lam_i):
            hr, hi = carry
            r0 = pl.multiple_of(t * bsz, bsz)
            nr = lam_r * hr - lam_i * hi + bu_ref[pl.ds(r0, bsz), re]
            ni = lam_r * hi + lam_i * hr + bu_ref[pl.ds(r0, bsz), im]
            bu_ref[pl.ds(r0, bsz), re] = nr
            bu_ref[pl.ds(r0, bsz), im] = ni
            return nr, ni

        hr, hi = lax.fori_loop(0, steps, step, (h_ref[:, re], h_ref[:, im]), unroll=2)
        h_ref[:, re] = hr
        h_ref[:, im] = hi

    ys = []
    for c in range(S5_CHUNKS):
        hs = bu_ref[:, c * S5_CHUNK_LANES:(c + 1) * S5_CHUNK_LANES].astype(BF16)
        ys.append(jnp.dot(hs, wc_ref[c], preferred_element_type=F32))
    y = jnp.concatenate(ys, axis=1) + dskip_ref[...] * u
    y = jax.nn.gelu(y)
    gate = jnp.dot(y.astype(BF16), wglu_ref[...], preferred_element_type=F32) + bglu_ref[...]
    y = y * jax.nn.sigmoid(gate)
    y_ref[...] = _rms(y, gout_ref[...])

    @pl.when(c_id == pl.num_programs(0) - 1)
    def _():
        hlast_ref[...] = h_ref[...]


def _s5(u_tm, h0, lam, wb, wc, d_skip, w_glu, b_glu, g_out, bsz, length):
    steps = min(32, length)
    rows = steps * bsz
    full2 = lambda c: (0, 0)
    full3 = lambda c: (0, 0, 0)
    return pl.pallas_call(
        functools.partial(_s5_kernel, bsz=bsz, steps=steps),
        grid=(length // steps,),
        in_specs=[pl.BlockSpec((rows, S5_WIDTH), lambda c: (c, 0)),
                  pl.BlockSpec((bsz, S5_STATE_LANES), full2),
                  pl.BlockSpec((1, S5_STATE_LANES), full2),
                  pl.BlockSpec((S5_CHUNKS, LANES, S5_CHUNK_LANES), full3),
                  pl.BlockSpec((S5_CHUNKS, S5_CHUNK_LANES, LANES), full3),
                  pl.BlockSpec((1, S5_WIDTH), full2),
                  pl.BlockSpec((S5_WIDTH, S5_WIDTH), full2),
                  pl.BlockSpec((1, S5_WIDTH), full2),
                  pl.BlockSpec((1, S5_WIDTH), full2)],
        out_specs=[pl.BlockSpec((rows, S5_WIDTH), lambda c: (c, 0)),
                   pl.BlockSpec((bsz, S5_STATE_LANES), full2)],
        out_shape=(jax.ShapeDtypeStruct((length * bsz, S5_WIDTH), F32),
                   jax.ShapeDtypeStruct((bsz, S5_STATE_LANES), F32)),
        scratch_shapes=[pltpu.VMEM((rows, S5_STATE_LANES), F32),
                        pltpu.VMEM((bsz, S5_STATE_LANES), F32)],
        compiler_params=_cparams(("arbitrary",), VMEM_LIMIT),
        name="s5_mixer",
    )(u_tm, h0, lam, wb, wc, d_skip, w_glu, b_glu, g_out)


def _q_proj_kernel(qn_ref, w_ref, cq_ref, sq_ref, g_ref, q_ref):
    qf = jnp.dot(qn_ref[...], w_ref[...], preferred_element_type=F32)
    lane = lax.broadcasted_iota(jnp.int32, (qf.shape[0], HEAD_PAD), 1)
    first_half = lane < QK_NOPE + 16
    cq, sq, g = cq_ref[...], sq_ref[...], g_ref[...]
    for h in range(MLA_HEADS):
        x = qf[:, h * HEAD_PAD:(h + 1) * HEAD_PAD]
        sw = jnp.where(first_half, pltpu.roll(x, LANES - 16, axis=1), pltpu.roll(x, 16, axis=1))
        xr = x * cq + sw * sq
        q_ref[0, h] = (_rms(xr, g, QK_HEAD) * (QK_HEAD ** -0.5)).astype(BF16)


def _q_proj(qn, w_uq_p, cq, sq, gq, bsz, length):
    tm = min(512, length)
    n_l = length // tm
    full = lambda b, i: (0, 0)
    return pl.pallas_call(
        _q_proj_kernel,
        grid=(bsz, n_l),
        in_specs=[pl.BlockSpec((tm, Q_LORA), lambda b, i: (b * n_l + i, 0)),
                  pl.BlockSpec((Q_LORA, MLA_HEADS * HEAD_PAD), full),
                  pl.BlockSpec((tm, HEAD_PAD), lambda b, i: (i, 0)),
                  pl.BlockSpec((tm, HEAD_PAD), lambda b, i: (i, 0)),
                  pl.BlockSpec((1, HEAD_PAD), full)],
        out_specs=pl.BlockSpec((1, MLA_HEADS, tm, HEAD_PAD), lambda b, i: (b, 0, i, 0)),
        out_shape=jax.ShapeDtypeStruct((bsz, MLA_HEADS, length, HEAD_PAD), BF16),
        compiler_params=_cparams(("parallel", "parallel"), VMEM_LIMIT),
        name="q_proj",
    )(qn, w_uq_p, cq, sq, gq)


def _kv_proj_kernel(lat_ref, krot_ref, wk_ref, wv_ref, g_ref, k_ref, v_ref):
    lat = lat_ref[...].astype(BF16)
    kf = jnp.dot(lat, wk_ref[...], preferred_element_type=F32)
    vf = jnp.dot(lat, wv_ref[...], preferred_element_type=F32)
    krot = krot_ref[...]
    g = g_ref[...]
    for h in range(MLA_HEADS):
        k = kf[:, h * HEAD_PAD:(h + 1) * HEAD_PAD] + krot
        k_ref[0, h] = _rms(k, g, QK_HEAD).astype(BF16)
    for p in range(MLA_HEADS // 2):
        v_ref[0, p] = vf[:, p * LANES:(p + 1) * LANES].astype(BF16)


def _kv_proj(lat, krot, w_k, w_v, gk, bsz, length):
    tm = min(512, length)
    n_l = length // tm
    full = lambda b, i: (0, 0)
    row = lambda b, i: (b * n_l + i, 0)
    return pl.pallas_call(
        _kv_proj_kernel,
        grid=(bsz, n_l),
        in_specs=[pl.BlockSpec((tm, KV_LORA), row),
                  pl.BlockSpec((tm, HEAD_PAD), row),
                  pl.BlockSpec((KV_LORA, MLA_HEADS * HEAD_PAD), full),
                  pl.BlockSpec((KV_LORA, MLA_WIDTH), full),
                  pl.BlockSpec((1, HEAD_PAD), full)],
        out_specs=[pl.BlockSpec((1, MLA_HEADS, tm, HEAD_PAD), lambda b, i: (b, 0, i, 0)),
                   pl.BlockSpec((1, MLA_HEADS // 2, tm, LANES), lambda b, i: (b, 0, i, 0))],
        out_shape=(jax.ShapeDtypeStruct((bsz, MLA_HEADS, length, HEAD_PAD), BF16),
                   jax.ShapeDtypeStruct((bsz, MLA_HEADS // 2, length, LANES), BF16)),
        compiler_params=_cparams(("parallel", "parallel"), VMEM_LIMIT),
        name="kv_proj",
    )(lat, krot, w_k, w_v, gk)


def _scores(q, k):
    return lax.dot_general(q, k, (((1,), (1,)), ((), ())), preferred_element_type=F32)


def _attn_prompt_kernel(q_ref, k_ref, v_ref, o_ref, *, length, tq):
    n_q = length // tq
    row = lax.broadcasted_iota(jnp.int32, (tq, tq), 0)
    col = lax.broadcasted_iota(jnp.int32, (tq, tq), 1)
    visible = (col // CHUNK) <= (row // CHUNK)
    lane = lax.broadcasted_iota(jnp.int32, (tq, LANES), 1)
    for qi in range(n_q):
        q0 = qi * tq
        outs = []
        for hh in range(2):
            q = q_ref[0, hh, q0:q0 + tq, :]
            sd = jnp.where(visible, _scores(q, k_ref[0, hh, q0:q0 + tq, :]), -jnp.inf)
            m = jnp.max(sd, axis=-1, keepdims=True)
            if qi:
                so = _scores(q, k_ref[0, hh, 0:q0, :])
                m = jnp.maximum(m, jnp.max(so, axis=-1, keepdims=True))
            pd = jnp.exp(sd - m)
            l = jnp.sum(pd, axis=-1, keepdims=True)
            acc = jnp.dot(pd.astype(BF16), v_ref[0, 0, q0:q0 + tq, :], preferred_element_type=F32)
            if qi:
                po = jnp.exp(so - m)
                l = l + jnp.sum(po, axis=-1, keepdims=True)
                acc = acc + jnp.dot(po.astype(BF16), v_ref[0, 0, 0:q0, :], preferred_element_type=F32)
            outs.append(acc / l)
        o_ref[0, q0:q0 + tq, :] = jnp.where(lane < V_HEAD, outs[0], outs[1])


def _attn_prompt(q, k, v, bsz, length):
    tq = 256
    pairs = MLA_HEADS // 2
    return pl.pallas_call(
        functools.partial(_attn_prompt_kernel, length=length, tq=tq),
        grid=(bsz, pairs),
        in_specs=[pl.BlockSpec((1, 2, length, HEAD_PAD), lambda b, p: (b, p, 0, 0)),
                  pl.BlockSpec((1, 2, length, HEAD_PAD), lambda b, p: (b, p, 0, 0)),
                  pl.BlockSpec((1, 1, length, LANES), lambda b, p: (b, p, 0, 0))],
        out_specs=pl.BlockSpec((1, length, LANES), lambda b, p: (b, 0, p)),
        out_shape=jax.ShapeDtypeStruct((bsz, length, MLA_WIDTH), F32),
        compiler_params=_cparams(("parallel", "parallel"), VMEM_LIMIT),
        name="attn_prompt",
    )(q, k, v)


def _attn_sample_kernel(q_ref, kc_ref, vc_ref, kn_ref, vn_ref, o_ref):
    lq = q_ref.shape[2]
    lane = lax.broadcasted_iota(jnp.int32, (lq, LANES), 1)
    outs = []
    for hh in range(2):
        q = q_ref[0, hh]
        sc = _scores(q, kc_ref[0, hh])
        sn = _scores(q, kn_ref[0, hh])
        m = jnp.maximum(jnp.max(sc, axis=-1, keepdims=True), jnp.max(sn, axis=-1, keepdims=True))
        pc = jnp.exp(sc - m)
        pn = jnp.exp(sn - m)
        l = jnp.sum(pc, axis=-1, keepdims=True) + jnp.sum(pn, axis=-1, keepdims=True)
        acc = (jnp.dot(pc.astype(BF16), vc_ref[0, 0], preferred_element_type=F32)
               + jnp.dot(pn.astype(BF16), vn_ref[0, 0], preferred_element_type=F32))
        outs.append(acc / l)
    o_ref[0] = jnp.where(lane < V_HEAD, outs[0], outs[1])


def _attn_sample(q, kc, vc, kn, vn, bsz, lq, past):
    pairs = MLA_HEADS // 2
    hp = lambda b, p: (b, p, 0, 0)
    return pl.pallas_call(
        _attn_sample_kernel,
        grid=(bsz, pairs),
        in_specs=[pl.BlockSpec((1, 2, lq, HEAD_PAD), hp),
                  pl.BlockSpec((1, 2, past, HEAD_PAD), hp),
                  pl.BlockSpec((1, 1, past, LANES), hp),
                  pl.BlockSpec((1, 2, lq, HEAD_PAD), hp),
                  pl.BlockSpec((1, 1, lq, LANES), hp)],
        out_specs=pl.BlockSpec((1, lq, LANES), lambda b, p: (b, 0, p)),
        out_shape=jax.ShapeDtypeStruct((bsz, lq, MLA_WIDTH), F32),
        compiler_params=_cparams(("parallel", "parallel"), VMEM_LIMIT),
        name="attn_sample",
    )(q, kc, vc, kn, vn)


def _out_proj_kernel(x_ref, s5_ref, at_ref, gmla_ref, wout_ref, gffn_ref, wr_ref, br_ref, tri_ref, cin_ref,
                     x1_ref, hn_ref, route_ref, gate_ref, cout_ref, run_ref):
    @pl.when(pl.program_id(0) == 0)
    def _():
        run_ref[...] = cin_ref[...]

    an = _rms(at_ref[...], gmla_ref[...])
    merged = jnp.concatenate([s5_ref[...].astype(BF16), an.astype(BF16)], axis=1)
    x1 = x_ref[...] + jnp.dot(merged, wout_ref[...], preferred_element_type=F32)
    x1_ref[...] = x1
    hn = _rms(x1, gffn_ref[...])
    hn_ref[...] = hn
    logits = jnp.dot(hn, wr_ref[...], precision=lax.Precision.HIGHEST,
                     preferred_element_type=F32) + br_ref[...]
    lane = lax.broadcasted_iota(jnp.int32, logits.shape, 1)
    lane_f = lane.astype(F32)
    work = jnp.where(lane < N_EXPERTS, logits, -jnp.inf)
    vals, idxs = [], []
    for _ in range(TOP_K):
        m = jnp.max(work, axis=-1, keepdims=True)
        i = jnp.min(jnp.where(work == m, lane_f, float(LANES)), axis=-1, keepdims=True)
        vals.append(m)
        idxs.append(i)
        work = jnp.where(lane_f == i, -jnp.inf, work)
    es = [jnp.exp(v - vals[0]) for v in vals]
    den = es[0] + es[1] + es[2] + es[3]
    onehots = [(lane_f == idxs[k]).astype(F32) for k in range(TOP_K)]
    e_all = onehots[0] + onehots[1] + onehots[2] + onehots[3]
    before = jnp.dot(tri_ref[...], e_all.astype(BF16), preferred_element_type=F32) + run_ref[...]
    run_ref[...] = run_ref[...] + jnp.sum(e_all, axis=0, keepdims=True)
    cout_ref[...] = run_ref[...]
    route = jnp.zeros(logits.shape, F32)
    gate_out = jnp.zeros(logits.shape, F32)
    for k in range(TOP_K):
        rank = jnp.sum(onehots[k] * before, axis=-1, keepdims=True)
        route = jnp.where(lane == k, idxs[k], route)
        route = jnp.where(lane == TOP_K + k, rank, route)
        gate_out = jnp.where(lane == k, es[k] / den, gate_out)
    route_ref[...] = route.astype(jnp.int32)
    gate_ref[...] = gate_out


def _out_proj(x, s5n_tm, attn, g_mla, w_out, g_ffn, w_r_p, b_r_p, counts_in, bsz, length):
    t = bsz * length
    tm = min(512, length)
    n_l = length // tm
    row = lambda i: (i, 0)
    full = lambda i: (0, 0)
    tri = (jnp.arange(tm)[:, None] > jnp.arange(tm)[None, :]).astype(BF16)
    return pl.pallas_call(
        _out_proj_kernel,
        grid=(t // tm,),
        in_specs=[pl.BlockSpec((tm, D_MODEL), row),
                  pl.BlockSpec((tm, S5_WIDTH), lambda i: (i % n_l, i // n_l)),
                  pl.BlockSpec((tm, MLA_WIDTH), row),
                  pl.BlockSpec((1, MLA_WIDTH), full),
                  pl.BlockSpec((D_MODEL, D_MODEL), full),
                  pl.BlockSpec((1, D_MODEL), full),
                  pl.BlockSpec((D_MODEL, LANES), full),
                  pl.BlockSpec((1, LANES), full),
                  pl.BlockSpec((tm, tm), full),
                  pl.BlockSpec((1, LANES), full)],
        out_specs=[pl.BlockSpec((tm, D_MODEL), row),
                   pl.BlockSpec((tm, D_MODEL), row),
                   pl.BlockSpec((tm, LANES), row),
                   pl.BlockSpec((tm, LANES), row),
                   pl.BlockSpec((1, LANES), full)],
        out_shape=(jax.ShapeDtypeStruct((t, D_MODEL), F32),
                   jax.ShapeDtypeStruct((t, D_MODEL), F32),
                   jax.ShapeDtypeStruct((t, LANES), jnp.int32),
                   jax.ShapeDtypeStruct((t, LANES), F32),
                   jax.ShapeDtypeStruct((1, LANES), F32)),
        scratch_shapes=[pltpu.VMEM((1, LANES), F32)],
        compiler_params=_cparams(("arbitrary",), VMEM_LIMIT),
        name="out_proj",
    )(x, s5n_tm, attn, g_mla, w_out, g_ffn, w_r_p, b_r_p, tri, counts_in)


def _row_copy(src, dst, sem, src_row, dst_row):
    return pltpu.make_async_copy(src.at[pl.ds(src_row, 1)], dst.at[pl.ds(dst_row, 1)], sem)


def _gather_rows(idx_ref, src_hbm, buf, sem, n_rows):
    def issue(r, c):
        _row_copy(src_hbm, buf, sem, idx_ref[0, 0, r], r).start()
        return c

    def drain(r, c):
        _row_copy(src_hbm, buf, sem, 0, r).wait()
        return c

    lax.fori_loop(0, n_rows, issue, 0, unroll=8)
    lax.fori_loop(0, n_rows, drain, 0, unroll=8)


def _dispatch_kernel(zs_ref, zc_ref, nb_ref, dest_ref, hp_ref, hs_ref, xs_hbm, zero_ref, sem, *,
                     n_p_tiles, n_blocks):
    i = pl.program_id(0)
    tm = hp_ref.shape[0]

    @pl.when(i == 0)
    def _():
        zero_ref[...] = jnp.zeros(zero_ref.shape, zero_ref.dtype)

        def tail_copy(b):
            return pltpu.make_async_copy(zero_ref, xs_hbm.at[pl.ds(pl.multiple_of(b * MOE_ROWS, MOE_ROWS),
                                                                   MOE_ROWS)], sem)

        def tail_issue(b, c):
            tail_copy(b).start()
            return c

        def tail_drain(b, c):
            tail_copy(b).wait()
            return c

        lax.fori_loop(nb_ref[0], n_blocks, tail_issue, 0)
        lax.fori_loop(nb_ref[0], n_blocks, tail_drain, 0)

        def per_expert(e, c):
            n, s0 = zc_ref[e], zs_ref[e]

            def issue(j, c2):
                _row_copy(zero_ref, xs_hbm, sem, 0, s0 + j).start()
                return c2

            def drain(j, c2):
                _row_copy(zero_ref, xs_hbm, sem, 0, s0).wait()
                return c2

            lax.fori_loop(0, n, issue, 0)
            lax.fori_loop(0, n, drain, 0)
            return c

        lax.fori_loop(0, N_EXPERTS, per_expert, 0)

    def scatter(src_ref):
        def issue(r, c):
            for k in range(TOP_K):
                _row_copy(src_ref, xs_hbm, sem, r, dest_ref[0, 0, TOP_K * r + k]).start()
            return c

        def drain(r, c):
            for k in range(TOP_K):
                _row_copy(src_ref, xs_hbm, sem, r, 0).wait()
            return c

        lax.fori_loop(0, tm, issue, 0, unroll=2)
        lax.fori_loop(0, tm, drain, 0, unroll=2)

    @pl.when(i < n_p_tiles)
    def _():
        scatter(hp_ref)

    @pl.when(i >= n_p_tiles)
    def _():
        scatter(hs_ref)


def _dispatch(zero_start, zero_count, nb_used, dest, hn_p, hn_s, n_blocks, tm):
    n_p_tiles = hn_p.shape[0] // tm
    n_tiles = n_p_tiles + hn_s.shape[0] // tm
    return pl.pallas_call(
        functools.partial(_dispatch_kernel, n_p_tiles=n_p_tiles, n_blocks=n_blocks),
        grid_spec=pltpu.PrefetchScalarGridSpec(
            num_scalar_prefetch=3,
            grid=(n_tiles,),
            in_specs=[pl.BlockSpec((1, 1, TOP_K * tm), lambda i, *_: (i, 0, 0), memory_space=pltpu.SMEM),
                      pl.BlockSpec((tm, D_MODEL), lambda i, *_: (jnp.minimum(i, n_p_tiles - 1), 0)),
                      pl.BlockSpec((tm, D_MODEL), lambda i, *_: (jnp.maximum(i - n_p_tiles, 0), 0))],
            out_specs=pl.BlockSpec(memory_space=pl.ANY),
            scratch_shapes=[pltpu.VMEM((MOE_ROWS, D_MODEL), F32), pltpu.SemaphoreType.DMA(())]),
        out_shape=jax.ShapeDtypeStruct((n_blocks * MOE_ROWS, D_MODEL), F32),
        compiler_params=_cparams(("arbitrary",), VMEM_LIMIT),
        name="moe_dispatch",
    )(zero_start, zero_count, nb_used, dest.reshape(n_tiles, 1, TOP_K * tm), hn_p, hn_s)


def _expert_kernel(be_ref, nb_ref, x_ref, w1_ref, b1_ref, w2_ref, b2_ref, y_ref, w1b_ref, w2b_ref):
    b = pl.program_id(0)
    prev = be_ref[jnp.maximum(b - 1, 0)]

    @pl.when(jnp.logical_or(b == 0, be_ref[b] != prev))
    def _():
        w1b_ref[...] = w1_ref[0].astype(BF16)
        w2b_ref[...] = w2_ref[0].astype(BF16)

    @pl.when(b < nb_ref[0])
    def _():
        hg = jnp.dot(x_ref[...].astype(BF16), w1b_ref[...], preferred_element_type=F32) + b1_ref[0]
        gate = jnp.minimum(hg[:, :D_FF], SWIGLU_LIMIT)
        lin = jnp.clip(hg[:, D_FF:], -SWIGLU_LIMIT, SWIGLU_LIMIT)
        act = gate * jax.nn.sigmoid(SWIGLU_ALPHA * gate) * (lin + 1.0)
        y_ref[...] = jnp.dot(act.astype(BF16), w2b_ref[...], preferred_element_type=F32) + b2_ref[0]

    @pl.when(b >= nb_ref[0])
    def _():
        y_ref[...] = jnp.zeros(y_ref.shape, y_ref.dtype)


def _experts(block_e, nb_used, xs, w1, b1, w2, b2, n_blocks):
    return pl.pallas_call(
        _expert_kernel,
        grid_spec=pltpu.PrefetchScalarGridSpec(
            num_scalar_prefetch=2,
            grid=(n_blocks,),
            in_specs=[pl.BlockSpec((MOE_ROWS, D_MODEL), lambda b, be, nb: (jnp.minimum(b, nb[0] - 1), 0)),
                      pl.BlockSpec((1, D_MODEL, 2 * D_FF), lambda b, be, nb: (be[b], 0, 0)),
                      pl.BlockSpec((1, 1, 2 * D_FF), lambda b, be, nb: (be[b], 0, 0)),
                      pl.BlockSpec((1, D_FF, D_MODEL), lambda b, be, nb: (be[b], 0, 0)),
                      pl.BlockSpec((1, 1, D_MODEL), lambda b, be, nb: (be[b], 0, 0))],
            out_specs=pl.BlockSpec((MOE_ROWS, D_MODEL), lambda b, be, nb: (b, 0)),
            scratch_shapes=[pltpu.VMEM((D_MODEL, 2 * D_FF), BF16), pltpu.VMEM((D_FF, D_MODEL), BF16)]),
        out_shape=jax.ShapeDtypeStruct((n_blocks * MOE_ROWS, D_MODEL), F32),
        compiler_params=_cparams(("arbitrary",), VMEM_LIMIT),
        name="moe_experts",
    )(block_e, nb_used, xs, w1, b1.reshape(N_EXPERTS, 1, 2 * D_FF), w2, b2.reshape(N_EXPERTS, 1, D_MODEL))


def _combine_kernel(slot_ref, gate_ref, x1_ref, yb_hbm, o_ref, buf, sem):
    tmc = o_ref.shape[0]
    _gather_rows(slot_ref, yb_hbm, buf, sem, TOP_K * tmc)
    g = gate_ref[...]
    acc = x1_ref[...]
    for k in range(TOP_K):
        acc = acc + g[:, k:k + 1] * buf[k * tmc:(k + 1) * tmc, :]
    o_ref[...] = acc


def _combine(dest, gates, x1, yb, tmc):
    n_tok = x1.shape[0]
    slot_km = dest.reshape(n_tok // tmc, tmc, TOP_K).transpose(0, 2, 1).reshape(n_tok // tmc, 1, TOP_K * tmc)
    return pl.pallas_call(
        _combine_kernel,
        grid=(n_tok // tmc,),
        in_specs=[pl.BlockSpec((1, 1, TOP_K * tmc), lambda i: (i, 0, 0), memory_space=pltpu.SMEM),
                  pl.BlockSpec((tmc, LANES), lambda i: (i, 0)),
                  pl.BlockSpec((tmc, D_MODEL), lambda i: (i, 0)),
                  pl.BlockSpec(memory_space=pl.ANY)],
        out_specs=pl.BlockSpec((tmc, D_MODEL), lambda i: (i, 0)),
        out_shape=jax.ShapeDtypeStruct((n_tok, D_MODEL), F32),
        scratch_shapes=[pltpu.VMEM((TOP_K * tmc, D_MODEL), F32), pltpu.SemaphoreType.DMA(())],
        compiler_params=_cparams(("arbitrary",), VMEM_LIMIT),
        name="moe_combine",
    )(slot_km, gates, x1, yb)


def _route_tables(counts_f, n_assign):
    counts = counts_f[0, :N_EXPERTS].astype(jnp.int32)
    padded = (counts + MOE_ROWS - 1) // MOE_ROWS * MOE_ROWS
    pad_end = jnp.cumsum(padded)
    pad_start = pad_end - padded
    n_blocks = -(-(n_assign + N_EXPERTS * (MOE_ROWS - 1)) // MOE_ROWS)
    block_e = jnp.minimum(jnp.searchsorted(pad_end, jnp.arange(n_blocks) * MOE_ROWS, side='right'),
                          N_EXPERTS - 1).astype(jnp.int32)
    nb_used = (pad_end[-1] // MOE_ROWS).astype(jnp.int32).reshape(1)
    return pad_start, pad_start + counts, padded - counts, block_e, nb_used, n_blocks


def _dest(route, pad_start):
    idx, rank = route[:, :TOP_K], route[:, TOP_K:2 * TOP_K]
    onehot = idx[:, :, None] == jnp.arange(N_EXPERTS, dtype=jnp.int32)[None, None, :]
    return rank + jnp.sum(jnp.where(onehot, pad_start[None, None, :], 0), axis=-1)


def _rope_tables(offset, length):
    pos = offset + jnp.arange(length, dtype=F32)
    inv = ROPE_THETA ** (-jnp.arange(QK_ROPE // 2, dtype=F32) * (2.0 / QK_ROPE))
    ang = pos[:, None] * inv[None, :]
    cos, sin = jnp.cos(ang), jnp.sin(ang)
    z32 = jnp.zeros((length, 32), F32)
    rot_c = jnp.concatenate([cos, cos], axis=1)
    rot_s = jnp.concatenate([-sin, sin], axis=1)
    ck = jnp.concatenate([rot_c, z32, rot_c, z32], axis=1)
    sk = jnp.concatenate([rot_s, z32, rot_s, z32], axis=1)
    cq = jnp.concatenate([jnp.ones((length, QK_NOPE), F32), rot_c, z32], axis=1)
    sq = jnp.concatenate([jnp.zeros((length, QK_NOPE), F32), rot_s, z32], axis=1)
    return ck, sk, cq, sq


def _head_gain(g_nope, g_rope):
    return jnp.concatenate([g_nope, g_rope, g_rope, jnp.zeros((HEAD_PAD - QK_HEAD,), F32)]).reshape(1, HEAD_PAD)


def _mixer_stage(x, past, prm, counts_in, bsz, length):
    offset = 0 if past is None else past[0].shape[1]
    ck, sk, cq, sq = _rope_tables(offset, length)
    x2 = x.reshape(bsz * length, D_MODEL)
    u_tm, qn, lat, k_rope, krot = _in_proj(x2, prm['g_mix'], prm['w_in_p'], prm['g_q_lat'], prm['g_kv_lat'],
                                           ck, sk, bsz, length)
    if past is None:
        h0 = jnp.zeros((bsz, S5_STATE_LANES), F32)
    else:
        h0 = _state_layout(past[2], past[3])
    s5n_tm, h_last = _s5(u_tm.reshape(length * bsz, S5_WIDTH), h0, prm['lam'], prm['wb'], prm['wc'],
                         prm['d_skip'], prm['w_glu'], prm['b_glu'], prm['g_s5_out'], bsz, length)
    h_re, h_im = _state_unlayout(h_last)
    q = _q_proj(qn, prm['w_uq_p'], cq, sq, prm['gq'], bsz, length)
    k, v = _kv_proj(lat, krot, prm['w_k'], prm['w_v'], prm['gk'], bsz, length)
    if past is None:
        attn = _attn_prompt(q, k, v, bsz, length)
    else:
        n_past = past[0].shape[1]
        c_lat = past[0].reshape(bsz * n_past, KV_LORA)
        c_rot = jnp.pad(past[1].reshape(bsz * n_past, QK_ROPE), ((0, 0), (QK_NOPE, HEAD_PAD - QK_HEAD)))
        kc, vc = _kv_proj(c_lat, c_rot, prm['w_k'], prm['w_v'], prm['gk'], bsz, n_past)
        attn = _attn_sample(q, kc, vc, k, v, bsz, length, n_past)
    x1, hn, route, gates, counts = _out_proj(x2, s5n_tm.reshape(length, bsz * S5_WIDTH),
                                             attn.reshape(bsz * length, MLA_WIDTH), prm['g_mla_out'],
                                             prm['w_out'], prm['g_ffn'], prm['w_r_p'], prm['b_r_p'],
                                             counts_in, bsz, length)
    return (x1, hn, route, gates, counts,
            lat.reshape(bsz, length, KV_LORA), k_rope.reshape(bsz, length, QK_ROPE), h_re, h_im)


def _prepare(g_mix, w_in, lam_re, lam_im, log_dt, b_s5_re, b_s5_im, c_s5_re, c_s5_im, d_s5, w_glu, b_glu,
             g_q_lat, w_uq, g_kv_lat, w_ukv, g_qn_nope, g_qn_rope, g_kn_nope, g_kn_rope, g_s5_out,
             g_mla_out, w_out, g_ffn, w_router, b_router):
    c0 = S5_WIDTH + Q_LORA + KV_LORA
    w_pe = w_in[:, c0:]
    z = jnp.zeros((D_MODEL, 32), F32)
    w_in_p = jnp.concatenate([w_in[:, :c0], w_pe, z, w_pe, z], axis=1).astype(BF16)
    ar, ai, bbr, bbi = _s5_prep(lam_re, lam_im, log_dt, b_s5_re, b_s5_im)
    wb, wc = _s5_weights(bbr, bbi, c_s5_re, c_s5_im)
    w_uq_p = jnp.pad(w_uq.reshape(Q_LORA, MLA_HEADS, QK_HEAD), ((0, 0), (0, 0), (0, HEAD_PAD - QK_HEAD)))
    w_kv = w_ukv.reshape(KV_LORA, MLA_HEADS, QK_NOPE + V_HEAD)
    w_k = jnp.pad(w_kv[:, :, :QK_NOPE], ((0, 0), (0, 0), (0, HEAD_PAD - QK_NOPE)))
    w_v = w_kv[:, :, QK_NOPE:]
    return dict(
        g_mix=g_mix.reshape(1, -1), w_in_p=w_in_p,
        g_q_lat=g_q_lat.reshape(1, -1), g_kv_lat=g_kv_lat.reshape(1, -1),
        lam=_state_layout(ar, ai).reshape(1, S5_STATE_LANES), wb=wb, wc=wc,
        d_skip=d_s5.reshape(1, -1), w_glu=w_glu.astype(BF16), b_glu=b_glu.reshape(1, -1),
        g_s5_out=g_s5_out.reshape(1, -1),
        w_uq_p=w_uq_p.reshape(Q_LORA, MLA_HEADS * HEAD_PAD).astype(BF16),
        w_k=w_k.reshape(KV_LORA, MLA_HEADS * HEAD_PAD).astype(BF16),
        w_v=w_v.reshape(KV_LORA, MLA_WIDTH).astype(BF16),
        gq=_head_gain(g_qn_nope, g_qn_rope), gk=_head_gain(g_kn_nope, g_kn_rope),
        g_mla_out=g_mla_out.reshape(1, -1), w_out=w_out.astype(BF16), g_ffn=g_ffn.reshape(1, -1),
        w_r_p=jnp.pad(w_router, ((0, 0), (0, LANES - N_EXPERTS))),
        b_r_p=jnp.pad(b_router, (0, LANES - N_EXPERTS)).reshape(1, LANES),
    )


def _layer(xp, xs, cache_lat, cache_kr, st_re, st_im, mixer_w, w1, b1, w2, b2):
    prm = _prepare(*mixer_w)
    bp, lp, _ = xp.shape
    bs, ls, _ = xs.shape
    zero_counts = jnp.zeros((1, LANES), F32)
    x1p, hnp, routep, gatep, counts_p, latp, krp, hrp, hip = _mixer_stage(xp, None, prm, zero_counts, bp, lp)
    x1s, hns, routes, gates, counts, lats, krs, hrs, his = _mixer_stage(
        xs, (cache_lat, cache_kr, st_re, st_im), prm, counts_p, bs, ls)
    n_tok = bp * lp + bs * ls
    pad_start, zero_start, zero_count, block_e, nb_used, n_blocks = _route_tables(counts, n_tok * TOP_K)
    dest_p = _dest(routep, pad_start)
    dest_s = _dest(routes, pad_start)
    xs_sorted = _dispatch(zero_start, zero_count, nb_used, jnp.concatenate([dest_p, dest_s], axis=0), hnp, hns,
                          n_blocks, 128)
    yb = _experts(block_e, nb_used, xs_sorted, w1, b1, w2, b2, n_blocks)
    yp = _combine(dest_p, gatep, x1p, yb, 128).reshape(bp, lp, D_MODEL)
    ys = _combine(dest_s, gates, x1s, yb, 128).reshape(bs, ls, D_MODEL)
    return yp, ys, latp, krp, hrp, hip, lats, krs, hrs, his


def kernel(x_prompt, x_sample, cache_kv_latent, cache_k_rope, state_s5_re, state_s5_im, g_mix, w_in, lam_re,
           lam_im, log_dt, b_s5_re, b_s5_im, c_s5_re, c_s5_im, d_s5, w_glu, b_glu, g_q_lat, w_uq, g_kv_lat,
           w_ukv, g_qn_nope, g_qn_rope, g_kn_nope, g_kn_rope, g_s5_out, g_mla_out, w_out, g_ffn, w_router,
           b_router, w_mlp1, b_mlp1, w_mlp2, b_mlp2):
    depth = g_mix.shape[0]
    yp, ys = x_prompt, x_sample
    outs = [[] for _ in range(8)]
    for l in range(depth):
        mixer_w = (g_mix[l], w_in[l], lam_re[l], lam_im[l], log_dt[l], b_s5_re[l], b_s5_im[l], c_s5_re[l],
                   c_s5_im[l], d_s5[l], w_glu[l], b_glu[l], g_q_lat[l], w_uq[l], g_kv_lat[l], w_ukv[l],
                   g_qn_nope[l], g_qn_rope[l], g_kn_nope[l], g_kn_rope[l], g_s5_out[l], g_mla_out[l],
                   w_out[l], g_ffn[l], w_router[l], b_router[l])
        res = _layer(yp, ys, cache_kv_latent[l], cache_k_rope[l], state_s5_re[l], state_s5_im[l], mixer_w,
                     w_mlp1[l], b_mlp1[l], w_mlp2[l], b_mlp2[l])
        yp, ys = res[0], res[1]
        for o, r in zip(outs, res[2:]):
            o.append(r)
    return (yp, ys) + tuple(jnp.stack(o) for o in outs)
```

```python
import functools
import math

import jax
import jax.numpy as jnp
from jax import lax
from jax.experimental import pallas as pl
from jax.experimental.pallas import tpu as pltpu
from jax.experimental.pallas import tpu_sc as plsc

F32 = jnp.float32
BF16 = jnp.bfloat16
U32 = jnp.uint32

D_MODEL = 1024
S5_WIDTH = 512
S5_GROUP = 16
S5_GROUPS = 32
S5_STATE = 64
MLA_HEADS = 8
QK_NOPE = 64
QK_ROPE = 32
QK_HEAD = QK_NOPE + QK_ROPE
V_HEAD = 64
MLA_WIDTH = MLA_HEADS * V_HEAD
Q_LORA = 384
KV_LORA = 256
ROPE_THETA = 10000.0
CHUNK = 64
N_EXPERTS = 32
TOP_K = 4
D_FF = D_MODEL
SWIGLU_LIMIT = 7.0
SWIGLU_ALPHA = 1.702
EPS = 1e-6

LANES = 128
HEAD_PAD = 128
D_IN_PAD = 1280
S5_CHUNKS = 4
S5_CHUNK_LANES = 1024
S5_STATE_LANES = S5_CHUNKS * S5_CHUNK_LANES
MOE_ROWS = 512
VMEM_LIMIT = 56 * 1024 * 1024
SC_CORES = 2
SC_SUBCORES = 16
SC_WORKERS = SC_CORES * SC_SUBCORES
SC_ROWS = 128
D_PACK = D_MODEL // 2


def _cparams(sem, vmem=None):
    return pltpu.CompilerParams(dimension_semantics=sem, vmem_limit_bytes=vmem)


def _rms(x, g, n=None):
    n = x.shape[-1] if n is None else n
    ms = jnp.sum(x * x, axis=-1, keepdims=True) * (1.0 / n)
    return x * lax.rsqrt(ms + EPS) * g


def _pack_rows(x):
    lo = lax.bitcast_convert_type(x[:, :D_PACK].astype(BF16).astype(F32), U32)
    hi = lax.bitcast_convert_type(x[:, D_PACK:].astype(BF16).astype(F32), U32)
    return (lo >> 16) | (hi & jnp.uint32(0xFFFF0000))


def _unpack_rows(w):
    lo = lax.bitcast_convert_type(w << 16, F32)
    hi = lax.bitcast_convert_type(w & jnp.uint32(0xFFFF0000), F32)
    return lo, hi


def _prep_kernel(lr_ref, li_ref, ldt_ref, br_ref, bi_ref, ar_ref, ai_ref, bbr_ref, bbi_ref):
    lr = lr_ref[...]
    li = li_ref[...]
    dt = jnp.exp(ldt_ref[...])
    mag = jnp.exp(lr * dt)
    ar = mag * jnp.cos(li * dt)
    ai = mag * jnp.sin(li * dt)
    ar_ref[...] = ar
    ai_ref[...] = ai
    den = lr * lr + li * li
    cr = ((ar - 1.0) * lr + ai * li) / den
    ci = (ai * lr - (ar - 1.0) * li) / den
    br = br_ref[...]
    bi = bi_ref[...]
    bbr_ref[...] = cr[:, None, :] * br - ci[:, None, :] * bi
    bbi_ref[...] = cr[:, None, :] * bi + ci[:, None, :] * br


def _s5_prep(lam_re, lam_im, log_dt, b_re, b_im):
    g, n = lam_re.shape
    p = b_re.shape[-1]
    out = pl.pallas_call(
        _prep_kernel,
        out_shape=(jax.ShapeDtypeStruct((g, n), F32), jax.ShapeDtypeStruct((g, n), F32),
                   jax.ShapeDtypeStruct((g, p, n), F32), jax.ShapeDtypeStruct((g, p, n), F32)),
        name="s5_prep",
    )(lam_re, lam_im, log_dt.reshape(g, 1), jnp.swapaxes(b_re, 1, 2), jnp.swapaxes(b_im, 1, 2))
    return out


def _state_layout(re, im):
    lead = re.shape[:-2]
    re = re.reshape(lead + (S5_CHUNKS, 512))
    im = im.reshape(lead + (S5_CHUNKS, 512))
    return jnp.stack([re, im], axis=-2).reshape(lead + (S5_STATE_LANES,))


def _state_unlayout(h):
    lead = h.shape[:-1]
    h = h.reshape(lead + (S5_CHUNKS, 2, 512))
    re = h[..., 0, :].reshape(lead + (S5_GROUPS, S5_STATE))
    im = h[..., 1, :].reshape(lead + (S5_GROUPS, S5_STATE))
    return re, im


def _s5_weights(bbr, bbi, c_re, c_im):
    eye8 = jnp.eye(8, dtype=F32)

    def blockdiag(m):
        a, b = m.shape[1], m.shape[2]
        return (eye8[:, None, :, None] * m[:, :, None, :]).reshape(8 * a, 8 * b)

    wb, wc = [], []
    for c in range(S5_CHUNKS):
        sl = slice(8 * c, 8 * c + 8)
        wb.append(jnp.concatenate([blockdiag(bbr[sl]), blockdiag(bbi[sl])], axis=1))
        cr_t = jnp.swapaxes(c_re[sl], 1, 2)
        ci_t = jnp.swapaxes(c_im[sl], 1, 2)
        wc.append(jnp.concatenate([blockdiag(cr_t), -blockdiag(ci_t)], axis=0))
    return jnp.stack(wb).astype(BF16), jnp.stack(wc).astype(BF16)


def _in_proj_kernel(x_ref, gmix_ref, w_ref, gq_ref, gkv_ref, ck_ref, sk_ref,
                    u_ref, qn_ref, lat_ref, krope_ref, krot_ref):
    x = x_ref[...]
    xn = _rms(x, gmix_ref[...]).astype(BF16)
    z = jnp.dot(xn, w_ref[...], preferred_element_type=F32)
    u_ref[...] = z[:, :S5_WIDTH]
    qn_ref[...] = _rms(z[:, S5_WIDTH:S5_WIDTH + Q_LORA], gq_ref[...]).astype(BF16)
    c0 = S5_WIDTH + Q_LORA
    lat_ref[...] = _rms(z[:, c0:c0 + KV_LORA], gkv_ref[...])
    kp = z[:, c0 + KV_LORA:]
    lane = lax.broadcasted_iota(jnp.int32, kp.shape, 1)
    first_half = (lane % 64) < 16
    sw = jnp.where(first_half, pltpu.roll(kp, LANES - 16, axis=1), pltpu.roll(kp, 16, axis=1))
    kr = kp * ck_ref[...] + sw * sk_ref[...]
    krope_ref[...] = kr[:, :QK_ROPE]
    krot_ref[...] = jnp.where(lane >= 64, kr, 0.0)


def _in_proj(x, g_mix, w_in_p, g_q_lat, g_kv_lat, ck, sk, bsz, length):
    t = bsz * length
    tm = min(512, length)
    n_l = length // tm
    row = lambda i: (i, 0)
    tab = lambda i: (i % n_l, 0)
    full = lambda i: (0, 0)
    return pl.pallas_call(
        _in_proj_kernel,
        grid=(t // tm,),
        in_specs=[pl.BlockSpec((tm, D_MODEL), row),
                  pl.BlockSpec((1, D_MODEL), full),
                  pl.BlockSpec((D_MODEL, D_IN_PAD), full),
                  pl.BlockSpec((1, Q_LORA), full),
                  pl.BlockSpec((1, KV_LORA), full),
                  pl.BlockSpec((tm, LANES), tab),
                  pl.BlockSpec((tm, LANES), tab)],
        out_specs=[pl.BlockSpec((tm, S5_WIDTH), lambda i: (i % n_l, i // n_l)),
                   pl.BlockSpec((tm, Q_LORA), row),
                   pl.BlockSpec((tm, KV_LORA), row),
                   pl.BlockSpec((tm, QK_ROPE), row),
                   pl.BlockSpec((tm, LANES), row)],
        out_shape=(jax.ShapeDtypeStruct((length, bsz * S5_WIDTH), F32),
                   jax.ShapeDtypeStruct((t, Q_LORA), BF16),
                   jax.ShapeDtypeStruct((t, KV_LORA), F32),
                   jax.ShapeDtypeStruct((t, QK_ROPE), F32),
                   jax.ShapeDtypeStruct((t, LANES), F32)),
        compiler_params=_cparams(("parallel",), VMEM_LIMIT),
        name="in_proj",
    )(x, g_mix, w_in_p, g_q_lat, g_kv_lat, ck, sk)


def _s5_kernel(u_ref, h0_ref, lam_ref, wb_ref, wc_ref, dskip_ref, wglu_ref, bglu_ref, gout_ref,
               y_ref, hlast_ref, bu_ref, h_ref, *, bsz, steps):
    c_id = pl.program_id(0)

    @pl.when(c_id == 0)
    def _():
        h_ref[...] = h0_ref[...]

    u = u_ref[...]
    ub = u.astype(BF16)
    for c in range(S5_CHUNKS):
        bu_ref[:, c * S5_CHUNK_LANES:(c + 1) * S5_CHUNK_LANES] = jnp.dot(
            ub[:, c * LANES:(c + 1) * LANES], wb_ref[c], preferred_element_type=F32)

    for c in range(S5_CHUNKS):
        re = slice(c * S5_CHUNK_LANES, c * S5_CHUNK_LANES + 512)
        im = slice(c * S5_CHUNK_LANES + 512, (c + 1) * S5_CHUNK_LANES)
        lam_r = jnp.broadcast_to(lam_ref[:, re], (bsz, 512))
        lam_i = jnp.broadcast_to(lam_ref[:, im], (bsz, 512))

        def step(t, carry, re=re, im=im, lam_r=lam_r, lam_i=lam_i):
            hr, hi = carry
            r0 = pl.multiple_of(t * bsz, bsz)
            nr = lam_r * hr - lam_i * hi + bu_ref[pl.ds(r0, bsz), re]
            ni = lam_r * hi + lam_i * hr + bu_ref[pl.ds(r0, bsz), im]
            bu_ref[pl.ds(r0, bsz), re] = nr
            bu_ref[pl.ds(r0, bsz), im] = ni
            return nr, ni

        hr, hi = lax.fori_loop(0, steps, step, (h_ref[:, re], h_ref[:, im]), unroll=2)
        h_ref[:, re] = hr
        h_ref[:, im] = hi

    ys = []
    for c in range(S5_CHUNKS):
        hs = bu_ref[:, c * S5_CHUNK_LANES:(c + 1) * S5_CHUNK_LANES].astype(BF16)
        ys.append(jnp.dot(hs, wc_ref[c], preferred_element_type=F32))
    y = jnp.concatenate(ys, axis=1) + dskip_ref[...] * u
    y = jax.nn.gelu(y)
    gate = jnp.dot(y.astype(BF16), wglu_ref[...], preferred_element_type=F32) + bglu_ref[...]
    y = y * jax.nn.sigmoid(gate)
    y_ref[...] = _rms(y, gout_ref[...])

    @pl.when(c_id == pl.num_programs(0) - 1)
    def _():
        hlast_ref[...] = h_ref[...]


def _s5(u_tm, h0, lam, wb, wc, d_skip, w_glu, b_glu, g_out, bsz, length):
    steps = min(32, length)
    rows = steps * bsz
    full2 = lambda c: (0, 0)
    full3 = lambda c: (0, 0, 0)
    return pl.pallas_call(
        functools.partial(_s5_kernel, bsz=bsz, steps=steps),
        grid=(length // steps,),
        in_specs=[pl.BlockSpec((rows, S5_WIDTH), lambda c: (c, 0)),
                  pl.BlockSpec((bsz, S5_STATE_LANES), full2),
                  pl.BlockSpec((1, S5_STATE_LANES), full2),
                  pl.BlockSpec((S5_CHUNKS, LANES, S5_CHUNK_LANES), full3),
                  pl.BlockSpec((S5_CHUNKS, S5_CHUNK_LANES, LANES), full3),
                  pl.BlockSpec((1, S5_WIDTH), full2),
                  pl.BlockSpec((S5_WIDTH, S5_WIDTH), full2),
                  pl.BlockSpec((1, S5_WIDTH), full2),
                  pl.BlockSpec((1, S5_WIDTH), full2)],
        out_specs=[pl.BlockSpec((rows, S5_WIDTH), lambda c: (c, 0)),
                   pl.BlockSpec((bsz, S5_STATE_LANES), full2)],
        out_shape=(jax.ShapeDtypeStruct((length * bsz, S5_WIDTH), F32),
                   jax.ShapeDtypeStruct((bsz, S5_STATE_LANES), F32)),
        scratch_shapes=[pltpu.VMEM((rows, S5_STATE_LANES), F32),
                        pltpu.VMEM((bsz, S5_STATE_LANES), F32)],
        compiler_params=_cparams(("arbitrary",), VMEM_LIMIT),
        name="s5_mixer",
    )(u_tm, h0, lam, wb, wc, d_skip, w_glu, b_glu, g_out)


def _q_proj_kernel(qn_ref, w_ref, cq_ref, sq_ref, g_ref, q_ref):
    qf = jnp.dot(qn_ref[...], w_ref[...], preferred_element_type=F32)
    lane = lax.broadcasted_iota(jnp.int32, (qf.shape[0], HEAD_PAD), 1)
    first_half = lane < QK_NOPE + 16
    cq, sq, g = cq_ref[...], sq_ref[...], g_ref[...]
    for h in range(MLA_HEADS):
        x = qf[:, h * HEAD_PAD:(h + 1) * HEAD_PAD]
        sw = jnp.where(first_half, pltpu.roll(x, LANES - 16, axis=1), pltpu.roll(x, 16, axis=1))
        xr = x * cq + sw * sq
        q_ref[0, h] = (_rms(xr, g, QK_HEAD) * (QK_HEAD ** -0.5)).astype(BF16)


def _q_proj(qn, w_uq_p, cq, sq, gq, bsz, length):
    tm = min(512, length)
    n_l = length // tm
    full = lambda b, i: (0, 0)
    return pl.pallas_call(
        _q_proj_kernel,
        grid=(bsz, n_l),
        in_specs=[pl.BlockSpec((tm, Q_LORA), lambda b, i: (b * n_l + i, 0)),
                  pl.BlockSpec((Q_LORA, MLA_HEADS * HEAD_PAD), full),
                  pl.BlockSpec((tm, HEAD_PAD), lambda b, i: (i, 0)),
                  pl.BlockSpec((tm, HEAD_PAD), lambda b, i: (i, 0)),
                  pl.BlockSpec((1, HEAD_PAD), full)],
        out_specs=pl.BlockSpec((1, MLA_HEADS, tm, HEAD_PAD), lambda b, i: (b, 0, i, 0)),
        out_shape=jax.ShapeDtypeStruct((bsz, MLA_HEADS, length, HEAD_PAD), BF16),
        compiler_params=_cparams(("parallel", "parallel"), VMEM_LIMIT),
        name="q_proj",
    )(qn, w_uq_p, cq, sq, gq)


def _kv_proj_kernel(lat_ref, krot_ref, wk_ref, wv_ref, g_ref, k_ref, v_ref):
    lat = lat_ref[...].astype(BF16)
    kf = jnp.dot(lat, wk_ref[...], preferred_element_type=F32)
    vf = jnp.dot(lat, wv_ref[...], preferred_element_type=F32)
    krot = krot_ref[...]
    g = g_ref[...]
    for h in range(MLA_HEADS):
        k = kf[:, h * HEAD_PAD:(h + 1) * HEAD_PAD] + krot
        k_ref[0, h] = _rms(k, g, QK_HEAD).astype(BF16)
    for p in range(MLA_HEADS // 2):
        v_ref[0, p] = vf[:, p * LANES:(p + 1) * LANES].astype(BF16)


def _kv_proj(lat, krot, w_k, w_v, gk, bsz, length):
    tm = min(512, length)
    n_l = length // tm
    full = lambda b, i: (0, 0)
    row = lambda b, i: (b * n_l + i, 0)
    return pl.pallas_call(
        _kv_proj_kernel,
        grid=(bsz, n_l),
        in_specs=[pl.BlockSpec((tm, KV_LORA), row),
                  pl.BlockSpec((tm, HEAD_PAD), row),
                  pl.BlockSpec((KV_LORA, MLA_HEADS * HEAD_PAD), full),
                  pl.BlockSpec((KV_LORA, MLA_WIDTH), full),
                  pl.BlockSpec((1, HEAD_PAD), full)],
        out_specs=[pl.BlockSpec((1, MLA_HEADS, tm, HEAD_PAD), lambda b, i: (b, 0, i, 0)),
                   pl.BlockSpec((1, MLA_HEADS // 2, tm, LANES), lambda b, i: (b, 0, i, 0))],
        out_shape=(jax.ShapeDtypeStruct((bsz, MLA_HEADS, length, HEAD_PAD), BF16),
                   jax.ShapeDtypeStruct((bsz, MLA_HEADS // 2, length, LANES), BF16)),
        compiler_params=_cparams(("parallel", "parallel"), VMEM_LIMIT),
        name="kv_proj",
    )(lat, krot, w_k, w_v, gk)


def _scores(q, k):
    return lax.dot_general(q, k, (((1,), (1,)), ((), ())), preferred_element_type=F32)


def _attn_prompt_kernel(q_ref, k_ref, v_ref, o_ref, *, length, tq):
    n_q = length // tq
    row = lax.broadcasted_iota(jnp.int32, (tq, tq), 0)
    col = lax.broadcasted_iota(jnp.int32, (tq, tq), 1)
    visible = (col // CHUNK) <= (row // CHUNK)
    lane = lax.broadcasted_iota(jnp.int32, (tq, LANES), 1)
    for qi in range(n_q):
        q0 = qi * tq
        outs = []
        for hh in range(2):
            q = q_ref[0, hh, q0:q0 + tq, :]
            sd = jnp.where(visible, _scores(q, k_ref[0, hh, q0:q0 + tq, :]), -jnp.inf)
            m = jnp.max(sd, axis=-1, keepdims=True)
            if qi:
                so = _scores(q, k_ref[0, hh, 0:q0, :])
                m = jnp.maximum(m, jnp.max(so, axis=-1, keepdims=True))
            pd = jnp.exp(sd - m)
            l = jnp.sum(pd, axis=-1, keepdims=True)
            acc = jnp.dot(pd.astype(BF16), v_ref[0, 0, q0:q0 + tq, :], preferred_element_type=F32)
            if qi:
                po = jnp.exp(so - m)
                l = l + jnp.sum(po, axis=-1, keepdims=True)
                acc = acc + jnp.dot(po.astype(BF16), v_ref[0, 0, 0:q0, :], preferred_element_type=F32)
            outs.append(acc / l)
        o_ref[0, q0:q0 + tq, :] = jnp.where(lane < V_HEAD, outs[0], outs[1])


def _attn_prompt(q, k, v, bsz, length):
    tq = 256
    pairs = MLA_HEADS // 2
    return pl.pallas_call(
        functools.partial(_attn_prompt_kernel, length=length, tq=tq),
        grid=(bsz, pairs),
        in_specs=[pl.BlockSpec((1, 2, length, HEAD_PAD), lambda b, p: (b, p, 0, 0)),
                  pl.BlockSpec((1, 2, length, HEAD_PAD), lambda b, p: (b, p, 0, 0)),
                  pl.BlockSpec((1, 1, length, LANES), lambda b, p: (b, p, 0, 0))],
        out_specs=pl.BlockSpec((1, length, LANES), lambda b, p: (b, 0, p)),
        out_shape=jax.ShapeDtypeStruct((bsz, length, MLA_WIDTH), F32),
        compiler_params=_cparams(("parallel", "parallel"), VMEM_LIMIT),
        name="attn_prompt",
    )(q, k, v)


def _attn_sample_kernel(q_ref, kc_ref, vc_ref, kn_ref, vn_ref, o_ref):
    lq = q_ref.shape[2]
    lane = lax.broadcasted_iota(jnp.int32, (lq, LANES), 1)
    outs = []
    for hh in range(2):
        q = q_ref[0, hh]
        sc = _scores(q, kc_ref[0, hh])
        sn = _scores(q, kn_ref[0, hh])
        m = jnp.maximum(jnp.max(sc, axis=-1, keepdims=True), jnp.max(sn, axis=-1, keepdims=True))
        pc = jnp.exp(sc - m)
        pn = jnp.exp(sn - m)
        l = jnp.sum(pc, axis=-1, keepdims=True) + jnp.sum(pn, axis=-1, keepdims=True)
        acc = (jnp.dot(pc.astype(BF16), vc_ref[0, 0], preferred_element_type=F32)
               + jnp.dot(pn.astype(BF16), vn_ref[0, 0], preferred_element_type=F32))
        outs.append(acc / l)
    o_ref[0] = jnp.where(lane < V_HEAD, outs[0], outs[1])


def _attn_sample(q, kc, vc, kn, vn, bsz, lq, past):
    pairs = MLA_HEADS // 2
    hp = lambda b, p: (b, p, 0, 0)
    return pl.pallas_call(
        _attn_sample_kernel,
        grid=(bsz, pairs),
        in_specs=[pl.BlockSpec((1, 2, lq, HEAD_PAD), hp),
                  pl.BlockSpec((1, 2, past, HEAD_PAD), hp),
                  pl.BlockSpec((1, 1, past, LANES), hp),
                  pl.BlockSpec((1, 2, lq, HEAD_PAD), hp),
                  pl.BlockSpec((1, 1, lq, LANES), hp)],
        out_specs=pl.BlockSpec((1, lq, LANES), lambda b, p: (b, 0, p)),
        out_shape=jax.ShapeDtypeStruct((bsz, lq, MLA_WIDTH), F32),
        compiler_params=_cparams(("parallel", "parallel"), VMEM_LIMIT),
        name="attn_sample",
    )(q, kc, vc, kn, vn)


def _out_proj_kernel(x_ref, s5_ref, at_ref, gmla_ref, wout_ref, gffn_ref, wr_ref, br_ref, tri_ref, cin_ref,
                     x1_ref, hn_ref, route_ref, gate_ref, cout_ref, run_ref):
    @pl.when(pl.program_id(0) == 0)
    def _():
        run_ref[...] = cin_ref[...]

    an = _rms(at_ref[...], gmla_ref[...])
    merged = jnp.concatenate([s5_ref[...].astype(BF16), an.astype(BF16)], axis=1)
    x1 = x_ref[...] + jnp.dot(merged, wout_ref[...], preferred_element_type=F32)
    x1_ref[...] = x1
    hn = _rms(x1, gffn_ref[...])
    hn_ref[...] = _pack_rows(hn)
    logits = jnp.dot(hn, wr_ref[...], precision=lax.Precision.HIGHEST,
                     preferred_element_type=F32) + br_ref[...]
    lane = lax.broadcasted_iota(jnp.int32, logits.shape, 1)
    lane_f = lane.astype(F32)
    work = jnp.where(lane < N_EXPERTS, logits, -jnp.inf)
    vals, idxs = [], []
    for _ in range(TOP_K):
        m = jnp.max(work, axis=-1, keepdims=True)
        i = jnp.min(jnp.where(work == m, lane_f, float(LANES)), axis=-1, keepdims=True)
        vals.append(m)
        idxs.append(i)
        work = jnp.where(lane_f == i, -jnp.inf, work)
    es = [jnp.exp(v - vals[0]) for v in vals]
    den = es[0] + es[1] + es[2] + es[3]
    onehots = [(lane_f == idxs[k]).astype(F32) for k in range(TOP_K)]
    e_all = onehots[0] + onehots[1] + onehots[2] + onehots[3]
    before = jnp.dot(tri_ref[...], e_all.astype(BF16), preferred_element_type=F32) + run_ref[...]
    run_ref[...] = run_ref[...] + jnp.sum(e_all, axis=0, keepdims=True)
    cout_ref[...] = run_ref[...]
    route = jnp.zeros(logits.shape, F32)
    gate_out = jnp.zeros(logits.shape, F32)
    for k in range(TOP_K):
        rank = jnp.sum(onehots[k] * before, axis=-1, keepdims=True)
        route = jnp.where(lane == k, idxs[k], route)
        route = jnp.where(lane == TOP_K + k, rank, route)
        gate_out = jnp.where(lane == k, es[k] / den, gate_out)
    route_ref[...] = route.astype(jnp.int32)
    gate_ref[...] = gate_out


def _out_proj(x, s5n_tm, attn, g_mla, w_out, g_ffn, w_r_p, b_r_p, counts_in, bsz, length):
    t = bsz * length
    tm = min(512, length)
    n_l = length // tm
    row = lambda i: (i, 0)
    full = lambda i: (0, 0)
    tri = (jnp.arange(tm)[:, None] > jnp.arange(tm)[None, :]).astype(BF16)
    return pl.pallas_call(
        _out_proj_kernel,
        grid=(t // tm,),
        in_specs=[pl.BlockSpec((tm, D_MODEL), row),
                  pl.BlockSpec((tm, S5_WIDTH), lambda i: (i % n_l, i // n_l)),
                  pl.BlockSpec((tm, MLA_WIDTH), row),
                  pl.BlockSpec((1, MLA_WIDTH), full),
                  pl.BlockSpec((D_MODEL, D_MODEL), full),
                  pl.BlockSpec((1, D_MODEL), full),
                  pl.BlockSpec((D_MODEL, LANES), full),
                  pl.BlockSpec((1, LANES), full),
                  pl.BlockSpec((tm, tm), full),
                  pl.BlockSpec((1, LANES), full)],
        out_specs=[pl.BlockSpec((tm, D_MODEL), row),
                   pl.BlockSpec((tm, D_PACK), row),
                   pl.BlockSpec((tm, LANES), row),
                   pl.BlockSpec((tm, LANES), row),
                   pl.BlockSpec((1, LANES), full)],
        out_shape=(jax.ShapeDtypeStruct((t, D_MODEL), F32),
                   jax.ShapeDtypeStruct((t, D_PACK), U32),
                   jax.ShapeDtypeStruct((t, LANES), jnp.int32),
                   jax.ShapeDtypeStruct((t, LANES), F32),
                   jax.ShapeDtypeStruct((1, LANES), F32)),
        scratch_shapes=[pltpu.VMEM((1, LANES), F32)],
        compiler_params=_cparams(("arbitrary",), VMEM_LIMIT),
        name="out_proj",
    )(x, s5n_tm, attn, g_mla, w_out, g_ffn, w_r_p, b_r_p, tri, counts_in)


def _sc_mesh():
    return plsc.VectorSubcoreMesh(core_axis_name="c", subcore_axis_name="s",
                                  num_cores=SC_CORES, num_subcores=SC_SUBCORES)


def _sc_for_chunks(n_chunks, fn):
    wid = lax.axis_index("s") * SC_CORES + lax.axis_index("c")
    full, rem = divmod(n_chunks, SC_WORKERS)
    if full:
        @pl.loop(0, full)
        def _(j):
            fn(j * SC_WORKERS + wid)
    if rem:
        @pl.when(wid < rem)
        def _():
            fn(full * SC_WORKERS + wid)


def _chunk_rows(c):
    return pl.ds(pl.multiple_of(c * SC_ROWS, SC_ROWS), SC_ROWS)


def _sc_dispatch_body(dp_hbm, ds_hbm, hp_hbm, hs_hbm, xs_hbm, idx_v, rows_v, sem):
    def chunk(d_hbm, h_hbm, c):
        pltpu.sync_copy(d_hbm.at[c], idx_v)
        pltpu.sync_copy(h_hbm.at[_chunk_rows(c)], rows_v)
        copies = [pltpu.async_copy(rows_v, xs_hbm.at[idx_v.at[k]], sem) for k in range(TOP_K)]
        for cp in copies:
            cp.wait()

    _sc_for_chunks(dp_hbm.shape[0], functools.partial(chunk, dp_hbm, hp_hbm))
    _sc_for_chunks(ds_hbm.shape[0], functools.partial(chunk, ds_hbm, hs_hbm))


def _dispatch(dest_p3, dest_s3, hn_p, hn_s, n_blocks):
    return pl.kernel(
        _sc_dispatch_body,
        out_type=jax.ShapeDtypeStruct((n_blocks * MOE_ROWS, D_PACK), U32),
        mesh=_sc_mesh(),
        scratch_types=[pltpu.VMEM((TOP_K, SC_ROWS), jnp.int32), pltpu.VMEM((SC_ROWS, D_PACK), U32),
                       pltpu.SemaphoreType.DMA],
        name="moe_dispatch_sc",
    )(dest_p3, dest_s3, hn_p, hn_s)


def _expert_kernel(be_ref, nb_ref, nv_ref, x_ref, w1_ref, b1_ref, w2_ref, b2_ref, y_ref, w1b_ref, w2b_ref):
    b = pl.program_id(0)
    prev = be_ref[jnp.maximum(b - 1, 0)]

    @pl.when(jnp.logical_or(b == 0, be_ref[b] != prev))
    def _():
        w1b_ref[...] = w1_ref[0].astype(BF16)
        w2b_ref[...] = w2_ref[0].astype(BF16)

    @pl.when(b < nb_ref[0])
    def _():
        live = lax.broadcasted_iota(jnp.int32, x_ref.shape, 0) < nv_ref[b]
        lo, hi = _unpack_rows(jnp.where(live, x_ref[...], jnp.uint32(0)))
        hg = (jnp.dot(lo.astype(BF16), w1b_ref[:D_PACK, :], preferred_element_type=F32)
              + jnp.dot(hi.astype(BF16), w1b_ref[D_PACK:, :], preferred_element_type=F32) + b1_ref[0])
        gate = jnp.minimum(hg[:, :D_FF], SWIGLU_LIMIT)
        lin = jnp.clip(hg[:, D_FF:], -SWIGLU_LIMIT, SWIGLU_LIMIT)
        act = gate * jax.nn.sigmoid(SWIGLU_ALPHA * gate) * (lin + 1.0)
        y = jnp.dot(act.astype(BF16), w2b_ref[...], preferred_element_type=F32) + b2_ref[0]
        y_ref[...] = _pack_rows(y)

    @pl.when(b >= nb_ref[0])
    def _():
        y_ref[...] = jnp.zeros(y_ref.shape, y_ref.dtype)


def _experts(block_e, nb_used, n_valid, xs, w1, b1, w2, b2, n_blocks):
    last = lambda b, be, nb, nv: (jnp.maximum(jnp.minimum(b, nb[0] - 1), 0), 0)
    wsel = lambda b, be, nb, nv: (be[b], 0, 0)
    return pl.pallas_call(
        _expert_kernel,
        grid_spec=pltpu.PrefetchScalarGridSpec(
            num_scalar_prefetch=3,
            grid=(n_blocks,),
            in_specs=[pl.BlockSpec((MOE_ROWS, D_PACK), last),
                      pl.BlockSpec((1, D_MODEL, 2 * D_FF), wsel),
                      pl.BlockSpec((1, 1, 2 * D_FF), wsel),
                      pl.BlockSpec((1, D_FF, D_MODEL), wsel),
                      pl.BlockSpec((1, 1, D_MODEL), wsel)],
            out_specs=pl.BlockSpec((MOE_ROWS, D_PACK), lambda b, be, nb, nv: (b, 0)),
            scratch_shapes=[pltpu.VMEM((D_MODEL, 2 * D_FF), BF16), pltpu.VMEM((D_FF, D_MODEL), BF16)]),
        out_shape=jax.ShapeDtypeStruct((n_blocks * MOE_ROWS, D_PACK), U32),
        compiler_params=_cparams(("arbitrary",), VMEM_LIMIT),
        name="moe_experts",
    )(block_e, nb_used, n_valid, xs, w1, b1.reshape(N_EXPERTS, 1, 2 * D_FF), w2, b2.reshape(N_EXPERTS, 1, D_MODEL))


def _sc_gather_body(dp_hbm, ds_hbm, yb_hbm, gp_hbm, gs_hbm, idx_v, rows_v, sem):
    def chunk(d_hbm, g_hbm, c):
        pltpu.sync_copy(d_hbm.at[c], idx_v)
        for k in range(TOP_K):
            pltpu.async_copy(yb_hbm.at[idx_v.at[k]], rows_v, sem).wait()
            pltpu.sync_copy(rows_v, g_hbm.at[k, _chunk_rows(c)])

    _sc_for_chunks(dp_hbm.shape[0], functools.partial(chunk, dp_hbm, gp_hbm))
    _sc_for_chunks(ds_hbm.shape[0], functools.partial(chunk, ds_hbm, gs_hbm))


def _gather_expert_rows(dest_p3, dest_s3, yb):
    n_p, n_s = dest_p3.shape[0] * SC_ROWS, dest_s3.shape[0] * SC_ROWS
    return pl.kernel(
        _sc_gather_body,
        out_type=(jax.ShapeDtypeStruct((TOP_K, n_p, D_PACK), U32),
                  jax.ShapeDtypeStruct((TOP_K, n_s, D_PACK), U32)),
        mesh=_sc_mesh(),
        scratch_types=[pltpu.VMEM((TOP_K, SC_ROWS), jnp.int32), pltpu.VMEM((SC_ROWS, D_PACK), U32),
                       pltpu.SemaphoreType.DMA],
        name="moe_gather_sc",
    )(dest_p3, dest_s3, yb)


def _combine_kernel(g_ref, gate_ref, x1_ref, o_ref):
    gate = gate_ref[...]
    x1 = x1_ref[...]
    acc_lo, acc_hi = x1[:, :D_PACK], x1[:, D_PACK:]
    for k in range(TOP_K):
        lo, hi = _unpack_rows(g_ref[k])
        acc_lo = acc_lo + gate[:, k:k + 1] * lo
        acc_hi = acc_hi + gate[:, k:k + 1] * hi
    o_ref[:, :D_PACK] = acc_lo
    o_ref[:, D_PACK:] = acc_hi


def _combine(g, gates, x1):
    n_tok = x1.shape[0]
    tm = min(512, n_tok)
    return pl.pallas_call(
        _combine_kernel,
        grid=(n_tok // tm,),
        in_specs=[pl.BlockSpec((TOP_K, tm, D_PACK), lambda i: (0, i, 0)),
                  pl.BlockSpec((tm, LANES), lambda i: (i, 0)),
                  pl.BlockSpec((tm, D_MODEL), lambda i: (i, 0))],
        out_specs=pl.BlockSpec((tm, D_MODEL), lambda i: (i, 0)),
        out_shape=jax.ShapeDtypeStruct((n_tok, D_MODEL), F32),
        compiler_params=_cparams(("parallel",), VMEM_LIMIT),
        name="moe_combine",
    )(g, gates, x1)


def _route_tables(counts_f, n_assign):
    counts = counts_f[0, :N_EXPERTS].astype(jnp.int32)
    padded = (counts + MOE_ROWS - 1) // MOE_ROWS * MOE_ROWS
    pad_end = jnp.cumsum(padded)
    pad_start = pad_end - padded
    n_blocks = -(-(n_assign + N_EXPERTS * (MOE_ROWS - 1)) // MOE_ROWS)
    row0 = jnp.arange(n_blocks, dtype=jnp.int32) * MOE_ROWS
    block_e = jnp.minimum(jnp.sum(row0[:, None] >= pad_end[None, :], axis=1), N_EXPERTS - 1).astype(jnp.int32)
    n_valid = jnp.clip((pad_start + counts)[block_e] - row0, 0, MOE_ROWS).astype(jnp.int32)
    nb_used = (pad_end[-1] // MOE_ROWS).astype(jnp.int32).reshape(1)
    return pad_start, block_e, n_valid, nb_used, n_blocks


def _dest(route, pad_start):
    idx, rank = route[:, :TOP_K], route[:, TOP_K:2 * TOP_K]
    onehot = idx[:, :, None] == jnp.arange(N_EXPERTS, dtype=jnp.int32)[None, None, :]
    dest = rank + jnp.sum(jnp.where(onehot, pad_start[None, None, :], 0), axis=-1)
    return dest.reshape(-1, SC_ROWS, TOP_K).transpose(0, 2, 1)


def _rope_tables(offset, length):
    pos = offset + jnp.arange(length, dtype=F32)
    inv = ROPE_THETA ** (-jnp.arange(QK_ROPE // 2, dtype=F32) * (2.0 / QK_ROPE))
    ang = pos[:, None] * inv[None, :]
    cos, sin = jnp.cos(ang), jnp.sin(ang)
    z32 = jnp.zeros((length, 32), F32)
    rot_c = jnp.concatenate([cos, cos], axis=1)
    rot_s = jnp.concatenate([-sin, sin], axis=1)
    ck = jnp.concatenate([rot_c, z32, rot_c, z32], axis=1)
    sk = jnp.concatenate([rot_s, z32, rot_s, z32], axis=1)
    cq = jnp.concatenate([jnp.ones((length, QK_NOPE), F32), rot_c, z32], axis=1)
    sq = jnp.concatenate([jnp.zeros((length, QK_NOPE), F32), rot_s, z32], axis=1)
    return ck, sk, cq, sq


def _head_gain(g_nope, g_rope):
    return jnp.concatenate([g_nope, g_rope, g_rope, jnp.zeros((HEAD_PAD - QK_HEAD,), F32)]).reshape(1, HEAD_PAD)


def _mixer_stage(x, past, prm, counts_in, bsz, length):
    offset = 0 if past is None else past[0].shape[1]
    ck, sk, cq, sq = _rope_tables(offset, length)
    x2 = x.reshape(bsz * length, D_MODEL)
    u_tm, qn, lat, k_rope, krot = _in_proj(x2, prm['g_mix'], prm['w_in_p'], prm['g_q_lat'], prm['g_kv_lat'],
                                           ck, sk, bsz, length)
    if past is None:
        h0 = jnp.zeros((bsz, S5_STATE_LANES), F32)
    else:
        h0 = _state_layout(past[2], past[3])
    s5n_tm, h_last = _s5(u_tm.reshape(length * bsz, S5_WIDTH), h0, prm['lam'], prm['wb'], prm['wc'],
                         prm['d_skip'], prm['w_glu'], prm['b_glu'], prm['g_s5_out'], bsz, length)
    h_re, h_im = _state_unlayout(h_last)
    q = _q_proj(qn, prm['w_uq_p'], cq, sq, prm['gq'], bsz, length)
    k, v = _kv_proj(lat, krot, prm['w_k'], prm['w_v'], prm['gk'], bsz, length)
    if past is None:
        attn = _attn_prompt(q, k, v, bsz, length)
    else:
        n_past = past[0].shape[1]
        c_lat = past[0].reshape(bsz * n_past, KV_LORA)
        c_rot = jnp.pad(past[1].reshape(bsz * n_past, QK_ROPE), ((0, 0), (QK_NOPE, HEAD_PAD - QK_HEAD)))
        kc, vc = _kv_proj(c_lat, c_rot, prm['w_k'], prm['w_v'], prm['gk'], bsz, n_past)
        attn = _attn_sample(q, kc, vc, k, v, bsz, length, n_past)
    x1, hn, route, gates, counts = _out_proj(x2, s5n_tm.reshape(length, bsz * S5_WIDTH),
                                             attn.reshape(bsz * length, MLA_WIDTH), prm['g_mla_out'],
                                             prm['w_out'], prm['g_ffn'], prm['w_r_p'], prm['b_r_p'],
                                             counts_in, bsz, length)
    return (x1, hn, route, gates, counts,
            lat.reshape(bsz, length, KV_LORA), k_rope.reshape(bsz, length, QK_ROPE), h_re, h_im)


def _prepare(g_mix, w_in, lam_re, lam_im, log_dt, b_s5_re, b_s5_im, c_s5_re, c_s5_im, d_s5, w_glu, b_glu,
             g_q_lat, w_uq, g_kv_lat, w_ukv, g_qn_nope, g_qn_rope, g_kn_nope, g_kn_rope, g_s5_out,
             g_mla_out, w_out, g_ffn, w_router, b_router):
    c0 = S5_WIDTH + Q_LORA + KV_LORA
    w_pe = w_in[:, c0:]
    z = jnp.zeros((D_MODEL, 32), F32)
    w_in_p = jnp.concatenate([w_in[:, :c0], w_pe, z, w_pe, z], axis=1).astype(BF16)
    ar, ai, bbr, bbi = _s5_prep(lam_re, lam_im, log_dt, b_s5_re, b_s5_im)
    wb, wc = _s5_weights(bbr, bbi, c_s5_re, c_s5_im)
    w_uq_p = jnp.pad(w_uq.reshape(Q_LORA, MLA_HEADS, QK_HEAD), ((0, 0), (0, 0), (0, HEAD_PAD - QK_HEAD)))
    w_kv = w_ukv.reshape(KV_LORA, MLA_HEADS, QK_NOPE + V_HEAD)
    w_k = jnp.pad(w_kv[:, :, :QK_NOPE], ((0, 0), (0, 0), (0, HEAD_PAD - QK_NOPE)))
    w_v = w_kv[:, :, QK_NOPE:]
    return dict(
        g_mix=g_mix.reshape(1, -1), w_in_p=w_in_p,
        g_q_lat=g_q_lat.reshape(1, -1), g_kv_lat=g_kv_lat.reshape(1, -1),
        lam=_state_layout(ar, ai).reshape(1, S5_STATE_LANES), wb=wb, wc=wc,
        d_skip=d_s5.reshape(1, -1), w_glu=w_glu.astype(BF16), b_glu=b_glu.reshape(1, -1),
        g_s5_out=g_s5_out.reshape(1, -1),
        w_uq_p=w_uq_p.reshape(Q_LORA, MLA_HEADS * HEAD_PAD).astype(BF16),
        w_k=w_k.reshape(KV_LORA, MLA_HEADS * HEAD_PAD).astype(BF16),
        w_v=w_v.reshape(KV_LORA, MLA_WIDTH).astype(BF16),
        gq=_head_gain(g_qn_nope, g_qn_rope), gk=_head_gain(g_kn_nope, g_kn_rope),
        g_mla_out=g_mla_out.reshape(1, -1), w_out=w_out.astype(BF16), g_ffn=g_ffn.reshape(1, -1),
        w_r_p=jnp.pad(w_router, ((0, 0), (0, LANES - N_EXPERTS))),
        b_r_p=jnp.pad(b_router, (0, LANES - N_EXPERTS)).reshape(1, LANES),
    )


def _layer(xp, xs, cache_lat, cache_kr, st_re, st_im, mixer_w, w1, b1, w2, b2):
    prm = _prepare(*mixer_w)
    bp, lp, _ = xp.shape
    bs, ls, _ = xs.shape
    zero_counts = jnp.zeros((1, LANES), F32)
    x1p, hnp, routep, gatep, counts_p, latp, krp, hrp, hip = _mixer_stage(xp, None, prm, zero_counts, bp, lp)
    x1s, hns, routes, gates, counts, lats, krs, hrs, his = _mixer_stage(
        xs, (cache_lat, cache_kr, st_re, st_im), prm, counts_p, bs, ls)
    n_tok = bp * lp + bs * ls
    pad_start, block_e, n_valid, nb_used, n_blocks = _route_tables(counts, n_tok * TOP_K)
    dest_p = _dest(routep, pad_start)
    dest_s = _dest(routes, pad_start)
    xs_sorted = _dispatch(dest_p, dest_s, hnp, hns, n_blocks)
    yb = _experts(block_e, nb_used, n_valid, xs_sorted, w1, b1, w2, b2, n_blocks)
    g_p, g_s = _gather_expert_rows(dest_p, dest_s, yb)
    yp = _combine(g_p, gatep, x1p).reshape(bp, lp, D_MODEL)
    ys = _combine(g_s, gates, x1s).reshape(bs, ls, D_MODEL)
    return yp, ys, latp, krp, hrp, hip, lats, krs, hrs, his


def kernel(x_prompt, x_sample, cache_kv_latent, cache_k_rope, state_s5_re, state_s5_im, g_mix, w_in, lam_re,
           lam_im, log_dt, b_s5_re, b_s5_im, c_s5_re, c_s5_im, d_s5, w_glu, b_glu, g_q_lat, w_uq, g_kv_lat,
           w_ukv, g_qn_nope, g_qn_rope, g_kn_nope, g_kn_rope, g_s5_out, g_mla_out, w_out, g_ffn, w_router,
           b_router, w_mlp1, b_mlp1, w_mlp2, b_mlp2):
    depth = g_mix.shape[0]
    yp, ys = x_prompt, x_sample
    outs = [[] for _ in range(8)]
    for l in range(depth):
        mixer_w = (g_mix[l], w_in[l], lam_re[l], lam_im[l], log_dt[l], b_s5_re[l], b_s5_im[l], c_s5_re[l],
                   c_s5_im[l], d_s5[l], w_glu[l], b_glu[l], g_q_lat[l], w_uq[l], g_kv_lat[l], w_ukv[l],
                   g_qn_nope[l], g_qn_rope[l], g_kn_nope[l], g_kn_rope[l], g_s5_out[l], g_mla_out[l],
                   w_out[l], g_ffn[l], w_router[l], b_router[l])
        res = _layer(yp, ys, cache_kv_latent[l], cache_k_rope[l], state_s5_re[l], state_s5_im[l], mixer_w,
                     w_mlp1[l], b_mlp1[l], w_mlp2[l], b_mlp2[l])
        yp, ys = res[0], res[1]
        for o, r in zip(outs, res[2:]):
            o.append(r)
    return (yp, ys) + tuple(jnp.stack(o) for o in outs)
```

```python
import functools
import math

import jax
import jax.numpy as jnp
from jax import lax
from jax.experimental import pallas as pl
from jax.experimental.pallas import tpu as pltpu
from jax.experimental.pallas import tpu_sc as plsc

F32 = jnp.float32
BF16 = jnp.bfloat16
U32 = jnp.uint32

D_MODEL = 1024
S5_WIDTH = 512
S5_GROUP = 16
S5_GROUPS = 32
S5_STATE = 64
MLA_HEADS = 8
QK_NOPE = 64
QK_ROPE = 32
QK_HEAD = QK_NOPE + QK_ROPE
V_HEAD = 64
MLA_WIDTH = MLA_HEADS * V_HEAD
Q_LORA = 384
KV_LORA = 256
ROPE_THETA = 10000.0
CHUNK = 64
N_EXPERTS = 32
TOP_K = 4
D_FF = D_MODEL
SWIGLU_LIMIT = 7.0
SWIGLU_ALPHA = 1.702
EPS = 1e-6

LANES = 128
HEAD_PAD = 128
D_IN_PAD = 1280
S5_CHUNKS = 4
S5_CHUNK_LANES = 1024
S5_STATE_LANES = S5_CHUNKS * S5_CHUNK_LANES
ROW_TILE = 1024
SUB_TILE = 512
MOE_ROWS = 512
MOE_COL_CHUNKS = 2
VMEM_LIMIT = 56 * 1024 * 1024
SC_CORES = 2
SC_SUBCORES = 16
SC_WORKERS = SC_CORES * SC_SUBCORES
SC_ROWS = 128
D_PACK = D_MODEL // 2


def _cparams(sem, vmem=None):
    return pltpu.CompilerParams(dimension_semantics=sem, vmem_limit_bytes=vmem)


def _rms(x, g, n=None):
    n = x.shape[-1] if n is None else n
    ms = jnp.sum(x * x, axis=-1, keepdims=True) * (1.0 / n)
    return x * lax.rsqrt(ms + EPS) * g


def _sub_tiles(rows):
    sub = min(SUB_TILE, rows)
    return [slice(r, r + sub) for r in range(0, rows, sub)]


def _pack_rows(x):
    lo = lax.bitcast_convert_type(x[:, :D_PACK].astype(BF16).astype(F32), U32)
    hi = lax.bitcast_convert_type(x[:, D_PACK:].astype(BF16).astype(F32), U32)
    return (lo >> 16) | (hi & jnp.uint32(0xFFFF0000))


def _unpack_rows(w):
    lo = lax.bitcast_convert_type(w << 16, F32)
    hi = lax.bitcast_convert_type(w & jnp.uint32(0xFFFF0000), F32)
    return lo, hi


def _prep_kernel(lr_ref, li_ref, ldt_ref, br_ref, bi_ref, ar_ref, ai_ref, bbr_ref, bbi_ref):
    lr = lr_ref[...]
    li = li_ref[...]
    dt = jnp.exp(ldt_ref[...])
    mag = jnp.exp(lr * dt)
    ar = mag * jnp.cos(li * dt)
    ai = mag * jnp.sin(li * dt)
    ar_ref[...] = ar
    ai_ref[...] = ai
    den = lr * lr + li * li
    cr = ((ar - 1.0) * lr + ai * li) / den
    ci = (ai * lr - (ar - 1.0) * li) / den
    br = br_ref[...]
    bi = bi_ref[...]
    bbr_ref[...] = cr[:, None, :] * br - ci[:, None, :] * bi
    bbi_ref[...] = cr[:, None, :] * bi + ci[:, None, :] * br


def _s5_prep(lam_re, lam_im, log_dt, b_re, b_im):
    g, n = lam_re.shape
    p = b_re.shape[-1]
    out = pl.pallas_call(
        _prep_kernel,
        out_shape=(jax.ShapeDtypeStruct((g, n), F32), jax.ShapeDtypeStruct((g, n), F32),
                   jax.ShapeDtypeStruct((g, p, n), F32), jax.ShapeDtypeStruct((g, p, n), F32)),
        name="s5_prep",
    )(lam_re, lam_im, log_dt.reshape(g, 1), jnp.swapaxes(b_re, 1, 2), jnp.swapaxes(b_im, 1, 2))
    return out


def _state_layout(re, im):
    lead = re.shape[:-2]
    re = re.reshape(lead + (S5_CHUNKS, 512))
    im = im.reshape(lead + (S5_CHUNKS, 512))
    return jnp.stack([re, im], axis=-2).reshape(lead + (S5_STATE_LANES,))


def _state_unlayout(h):
    lead = h.shape[:-1]
    h = h.reshape(lead + (S5_CHUNKS, 2, 512))
    re = h[..., 0, :].reshape(lead + (S5_GROUPS, S5_STATE))
    im = h[..., 1, :].reshape(lead + (S5_GROUPS, S5_STATE))
    return re, im


def _s5_weights(bbr, bbi, c_re, c_im):
    eye8 = jnp.eye(8, dtype=F32)

    def blockdiag(m):
        a, b = m.shape[1], m.shape[2]
        return (eye8[:, None, :, None] * m[:, :, None, :]).reshape(8 * a, 8 * b)

    wb, wc = [], []
    for c in range(S5_CHUNKS):
        sl = slice(8 * c, 8 * c + 8)
        wb.append(jnp.concatenate([blockdiag(bbr[sl]), blockdiag(bbi[sl])], axis=1))
        cr_t = jnp.swapaxes(c_re[sl], 1, 2)
        ci_t = jnp.swapaxes(c_im[sl], 1, 2)
        wc.append(jnp.concatenate([blockdiag(cr_t), -blockdiag(ci_t)], axis=0))
    return jnp.stack(wb).astype(BF16), jnp.stack(wc).astype(BF16)


def _in_proj_kernel(x_ref, gmix_ref, w_ref, gq_ref, gkv_ref, ck_ref, sk_ref,
                    u_ref, qn_ref, lat_ref, krope_ref, krot_ref):
    for rs in _sub_tiles(x_ref.shape[0]):
        xn = _rms(x_ref[rs, :], gmix_ref[...]).astype(BF16)
        z = jnp.dot(xn, w_ref[...], preferred_element_type=F32)
        u_ref[rs, :] = z[:, :S5_WIDTH]
        qn_ref[rs, :] = _rms(z[:, S5_WIDTH:S5_WIDTH + Q_LORA], gq_ref[...]).astype(BF16)
        c0 = S5_WIDTH + Q_LORA
        lat_ref[rs, :] = _rms(z[:, c0:c0 + KV_LORA], gkv_ref[...])
        kp = z[:, c0 + KV_LORA:]
        lane = lax.broadcasted_iota(jnp.int32, kp.shape, 1)
        first_half = (lane % 64) < 16
        sw = jnp.where(first_half, pltpu.roll(kp, LANES - 16, axis=1), pltpu.roll(kp, 16, axis=1))
        kr = kp * ck_ref[rs, :] + sw * sk_ref[rs, :]
        krope_ref[rs, :] = kr[:, :QK_ROPE]
        krot_ref[rs, :] = jnp.where(lane >= 64, kr, 0.0)


def _in_proj(x, g_mix, w_in_p, g_q_lat, g_kv_lat, ck, sk, bsz, length):
    t = bsz * length
    tm = min(ROW_TILE, length)
    n_l = length // tm
    row = lambda i: (i, 0)
    tab = lambda i: (i % n_l, 0)
    full = lambda i: (0, 0)
    return pl.pallas_call(
        _in_proj_kernel,
        grid=(t // tm,),
        in_specs=[pl.BlockSpec((tm, D_MODEL), row),
                  pl.BlockSpec((1, D_MODEL), full),
                  pl.BlockSpec((D_MODEL, D_IN_PAD), full),
                  pl.BlockSpec((1, Q_LORA), full),
                  pl.BlockSpec((1, KV_LORA), full),
                  pl.BlockSpec((tm, LANES), tab),
                  pl.BlockSpec((tm, LANES), tab)],
        out_specs=[pl.BlockSpec((tm, S5_WIDTH), lambda i: (i % n_l, i // n_l)),
                   pl.BlockSpec((tm, Q_LORA), row),
                   pl.BlockSpec((tm, KV_LORA), row),
                   pl.BlockSpec((tm, QK_ROPE), row),
                   pl.BlockSpec((tm, LANES), row)],
        out_shape=(jax.ShapeDtypeStruct((length, bsz * S5_WIDTH), F32),
                   jax.ShapeDtypeStruct((t, Q_LORA), BF16),
                   jax.ShapeDtypeStruct((t, KV_LORA), F32),
                   jax.ShapeDtypeStruct((t, QK_ROPE), F32),
                   jax.ShapeDtypeStruct((t, LANES), F32)),
        compiler_params=_cparams(("parallel",), VMEM_LIMIT),
        name="in_proj",
    )(x, g_mix, w_in_p, g_q_lat, g_kv_lat, ck, sk)


def _s5_kernel(u_ref, h0_ref, lam_ref, wb_ref, wc_ref, dskip_ref, wglu_ref, bglu_ref, gout_ref,
               y_ref, hlast_ref, bu_ref, h_ref, *, bsz, steps):
    c_id = pl.program_id(0)

    @pl.when(c_id == 0)
    def _():
        h_ref[...] = h0_ref[...]

    u = u_ref[...]
    ub = u.astype(BF16)
    for c in range(S5_CHUNKS):
        bu_ref[:, c * S5_CHUNK_LANES:(c + 1) * S5_CHUNK_LANES] = jnp.dot(
            ub[:, c * LANES:(c + 1) * LANES], wb_ref[c], preferred_element_type=F32)

    for c in range(S5_CHUNKS):
        re = slice(c * S5_CHUNK_LANES, c * S5_CHUNK_LANES + 512)
        im = slice(c * S5_CHUNK_LANES + 512, (c + 1) * S5_CHUNK_LANES)
        lam_r = jnp.broadcast_to(lam_ref[:, re], (bsz, 512))
        lam_i = jnp.broadcast_to(lam_ref[:, im], (bsz, 512))

        def step(t, carry, re=re, im=im, lam_r=lam_r, lam_i=lam_i):
            hr, hi = carry
            r0 = pl.multiple_of(t * bsz, bsz)
            nr = lam_r * hr - lam_i * hi + bu_ref[pl.ds(r0, bsz), re]
            ni = lam_r * hi + lam_i * hr + bu_ref[pl.ds(r0, bsz), im]
            bu_ref[pl.ds(r0, bsz), re] = nr
            bu_ref[pl.ds(r0, bsz), im] = ni
            return nr, ni

        hr, hi = lax.fori_loop(0, steps, step, (h_ref[:, re], h_ref[:, im]), unroll=2)
        h_ref[:, re] = hr
        h_ref[:, im] = hi

    ys = []
    for c in range(S5_CHUNKS):
        hs = bu_ref[:, c * S5_CHUNK_LANES:(c + 1) * S5_CHUNK_LANES].astype(BF16)
        ys.append(jnp.dot(hs, wc_ref[c], preferred_element_type=F32))
    y = jnp.concatenate(ys, axis=1) + dskip_ref[...] * u
    y = jax.nn.gelu(y)
    gate = jnp.dot(y.astype(BF16), wglu_ref[...], preferred_element_type=F32) + bglu_ref[...]
    y = y * jax.nn.sigmoid(gate)
    y_ref[...] = _rms(y, gout_ref[...])

    @pl.when(c_id == pl.num_programs(0) - 1)
    def _():
        hlast_ref[...] = h_ref[...]


def _s5(u_tm, h0, lam, wb, wc, d_skip, w_glu, b_glu, g_out, bsz, length):
    steps = min(32, length)
    rows = steps * bsz
    full2 = lambda c: (0, 0)
    full3 = lambda c: (0, 0, 0)
    return pl.pallas_call(
        functools.partial(_s5_kernel, bsz=bsz, steps=steps),
        grid=(length // steps,),
        in_specs=[pl.BlockSpec((rows, S5_WIDTH), lambda c: (c, 0)),
                  pl.BlockSpec((bsz, S5_STATE_LANES), full2),
                  pl.BlockSpec((1, S5_STATE_LANES), full2),
                  pl.BlockSpec((S5_CHUNKS, LANES, S5_CHUNK_LANES), full3),
                  pl.BlockSpec((S5_CHUNKS, S5_CHUNK_LANES, LANES), full3),
                  pl.BlockSpec((1, S5_WIDTH), full2),
                  pl.BlockSpec((S5_WIDTH, S5_WIDTH), full2),
                  pl.BlockSpec((1, S5_WIDTH), full2),
                  pl.BlockSpec((1, S5_WIDTH), full2)],
        out_specs=[pl.BlockSpec((rows, S5_WIDTH), lambda c: (c, 0)),
                   pl.BlockSpec((bsz, S5_STATE_LANES), full2)],
        out_shape=(jax.ShapeDtypeStruct((length * bsz, S5_WIDTH), F32),
                   jax.ShapeDtypeStruct((bsz, S5_STATE_LANES), F32)),
        scratch_shapes=[pltpu.VMEM((rows, S5_STATE_LANES), F32),
                        pltpu.VMEM((bsz, S5_STATE_LANES), F32)],
        compiler_params=_cparams(("arbitrary",), VMEM_LIMIT),
        name="s5_mixer",
    )(u_tm, h0, lam, wb, wc, d_skip, w_glu, b_glu, g_out)


def _q_proj_kernel(qn_ref, w_ref, cq_ref, sq_ref, g_ref, q_ref):
    g = g_ref[...]
    for rs in _sub_tiles(qn_ref.shape[0]):
        qf = jnp.dot(qn_ref[rs, :], w_ref[...], preferred_element_type=F32)
        lane = lax.broadcasted_iota(jnp.int32, (qf.shape[0], HEAD_PAD), 1)
        first_half = lane < QK_NOPE + 16
        cq, sq = cq_ref[rs, :], sq_ref[rs, :]
        for h in range(MLA_HEADS):
            x = qf[:, h * HEAD_PAD:(h + 1) * HEAD_PAD]
            sw = jnp.where(first_half, pltpu.roll(x, LANES - 16, axis=1), pltpu.roll(x, 16, axis=1))
            xr = x * cq + sw * sq
            q_ref[0, h, rs, :] = (_rms(xr, g, QK_HEAD) * (QK_HEAD ** -0.5)).astype(BF16)


def _q_proj(qn, w_uq_p, cq, sq, gq, bsz, length):
    tm = min(ROW_TILE, length)
    n_l = length // tm
    full = lambda b, i: (0, 0)
    return pl.pallas_call(
        _q_proj_kernel,
        grid=(bsz, n_l),
        in_specs=[pl.BlockSpec((tm, Q_LORA), lambda b, i: (b * n_l + i, 0)),
                  pl.BlockSpec((Q_LORA, MLA_HEADS * HEAD_PAD), full),
                  pl.BlockSpec((tm, HEAD_PAD), lambda b, i: (i, 0)),
                  pl.BlockSpec((tm, HEAD_PAD), lambda b, i: (i, 0)),
                  pl.BlockSpec((1, HEAD_PAD), full)],
        out_specs=pl.BlockSpec((1, MLA_HEADS, tm, HEAD_PAD), lambda b, i: (b, 0, i, 0)),
        out_shape=jax.ShapeDtypeStruct((bsz, MLA_HEADS, length, HEAD_PAD), BF16),
        compiler_params=_cparams(("parallel", "parallel"), VMEM_LIMIT),
        name="q_proj",
    )(qn, w_uq_p, cq, sq, gq)


def _kv_proj_kernel(lat_ref, krot_ref, wk_ref, wv_ref, g_ref, k_ref, v_ref):
    g = g_ref[...]
    for rs in _sub_tiles(lat_ref.shape[0]):
        lat = lat_ref[rs, :].astype(BF16)
        kf = jnp.dot(lat, wk_ref[...], preferred_element_type=F32)
        vf = jnp.dot(lat, wv_ref[...], preferred_element_type=F32)
        krot = krot_ref[rs, :]
        for h in range(MLA_HEADS):
            k = kf[:, h * HEAD_PAD:(h + 1) * HEAD_PAD] + krot
            k_ref[0, h, rs, :] = _rms(k, g, QK_HEAD).astype(BF16)
        for p in range(MLA_HEADS // 2):
            v_ref[0, p, rs, :] = vf[:, p * LANES:(p + 1) * LANES].astype(BF16)


def _kv_proj(lat, krot, w_k, w_v, gk, bsz, length):
    tm = min(ROW_TILE, length)
    n_l = length // tm
    full = lambda b, i: (0, 0)
    row = lambda b, i: (b * n_l + i, 0)
    return pl.pallas_call(
        _kv_proj_kernel,
        grid=(bsz, n_l),
        in_specs=[pl.BlockSpec((tm, KV_LORA), row),
                  pl.BlockSpec((tm, HEAD_PAD), row),
                  pl.BlockSpec((KV_LORA, MLA_HEADS * HEAD_PAD), full),
                  pl.BlockSpec((KV_LORA, MLA_WIDTH), full),
                  pl.BlockSpec((1, HEAD_PAD), full)],
        out_specs=[pl.BlockSpec((1, MLA_HEADS, tm, HEAD_PAD), lambda b, i: (b, 0, i, 0)),
                   pl.BlockSpec((1, MLA_HEADS // 2, tm, LANES), lambda b, i: (b, 0, i, 0))],
        out_shape=(jax.ShapeDtypeStruct((bsz, MLA_HEADS, length, HEAD_PAD), BF16),
                   jax.ShapeDtypeStruct((bsz, MLA_HEADS // 2, length, LANES), BF16)),
        compiler_params=_cparams(("parallel", "parallel"), VMEM_LIMIT),
        name="kv_proj",
    )(lat, krot, w_k, w_v, gk)


def _scores(q, k):
    return lax.dot_general(q, k, (((1,), (1,)), ((), ())), preferred_element_type=F32)


def _attn_prompt_kernel(q_ref, k_ref, v_ref, o_ref, *, length, tq):
    n_q = length // tq
    row = lax.broadcasted_iota(jnp.int32, (tq, tq), 0)
    col = lax.broadcasted_iota(jnp.int32, (tq, tq), 1)
    visible = (col // CHUNK) <= (row // CHUNK)
    lane = lax.broadcasted_iota(jnp.int32, (tq, LANES), 1)
    for qi in range(n_q):
        q0 = qi * tq
        outs = []
        for hh in range(2):
            q = q_ref[0, hh, q0:q0 + tq, :]
            sd = jnp.where(visible, _scores(q, k_ref[0, hh, q0:q0 + tq, :]), -jnp.inf)
            m = jnp.max(sd, axis=-1, keepdims=True)
            if qi:
                so = _scores(q, k_ref[0, hh, 0:q0, :])
                m = jnp.maximum(m, jnp.max(so, axis=-1, keepdims=True))
            pd = jnp.exp(sd - m)
            l = jnp.sum(pd, axis=-1, keepdims=True)
            acc = jnp.dot(pd.astype(BF16), v_ref[0, 0, q0:q0 + tq, :], preferred_element_type=F32)
            if qi:
                po = jnp.exp(so - m)
                l = l + jnp.sum(po, axis=-1, keepdims=True)
                acc = acc + jnp.dot(po.astype(BF16), v_ref[0, 0, 0:q0, :], preferred_element_type=F32)
            outs.append(acc / l)
        o_ref[0, q0:q0 + tq, :] = jnp.where(lane < V_HEAD, outs[0], outs[1])


def _attn_prompt(q, k, v, bsz, length):
    tq = min(512, length)
    pairs = MLA_HEADS // 2
    return pl.pallas_call(
        functools.partial(_attn_prompt_kernel, length=length, tq=tq),
        grid=(bsz, pairs),
        in_specs=[pl.BlockSpec((1, 2, length, HEAD_PAD), lambda b, p: (b, p, 0, 0)),
                  pl.BlockSpec((1, 2, length, HEAD_PAD), lambda b, p: (b, p, 0, 0)),
                  pl.BlockSpec((1, 1, length, LANES), lambda b, p: (b, p, 0, 0))],
        out_specs=pl.BlockSpec((1, length, LANES), lambda b, p: (b, 0, p)),
        out_shape=jax.ShapeDtypeStruct((bsz, length, MLA_WIDTH), F32),
        compiler_params=_cparams(("parallel", "parallel"), VMEM_LIMIT),
        name="attn_prompt",
    )(q, k, v)


def _attn_sample_kernel(q_ref, kc_ref, vc_ref, kn_ref, vn_ref, o_ref):
    lq = q_ref.shape[2]
    lane = lax.broadcasted_iota(jnp.int32, (lq, LANES), 1)
    outs = []
    for hh in range(2):
        q = q_ref[0, hh]
        sc = _scores(q, kc_ref[0, hh])
        sn = _scores(q, kn_ref[0, hh])
        m = jnp.maximum(jnp.max(sc, axis=-1, keepdims=True), jnp.max(sn, axis=-1, keepdims=True))
        pc = jnp.exp(sc - m)
        pn = jnp.exp(sn - m)
        l = jnp.sum(pc, axis=-1, keepdims=True) + jnp.sum(pn, axis=-1, keepdims=True)
        acc = (jnp.dot(pc.astype(BF16), vc_ref[0, 0], preferred_element_type=F32)
               + jnp.dot(pn.astype(BF16), vn_ref[0, 0], preferred_element_type=F32))
        outs.append(acc / l)
    o_ref[0] = jnp.where(lane < V_HEAD, outs[0], outs[1])


def _attn_sample(q, kc, vc, kn, vn, bsz, lq, past):
    pairs = MLA_HEADS // 2
    hp = lambda b, p: (b, p, 0, 0)
    return pl.pallas_call(
        _attn_sample_kernel,
        grid=(bsz, pairs),
        in_specs=[pl.BlockSpec((1, 2, lq, HEAD_PAD), hp),
                  pl.BlockSpec((1, 2, past, HEAD_PAD), hp),
                  pl.BlockSpec((1, 1, past, LANES), hp),
                  pl.BlockSpec((1, 2, lq, HEAD_PAD), hp),
                  pl.BlockSpec((1, 1, lq, LANES), hp)],
        out_specs=pl.BlockSpec((1, lq, LANES), lambda b, p: (b, 0, p)),
        out_shape=jax.ShapeDtypeStruct((bsz, lq, MLA_WIDTH), F32),
        compiler_params=_cparams(("parallel", "parallel"), VMEM_LIMIT),
        name="attn_sample",
    )(q, kc, vc, kn, vn)


def _out_proj_kernel(x_ref, s5_ref, at_ref, gmla_ref, wout_ref, gffn_ref, wrh_ref, wrl_ref, br_ref, tri_ref,
                     cin_ref,
                     x1_ref, hn_ref, route_ref, gate_ref, cout_ref, run_ref):
    @pl.when(pl.program_id(0) == 0)
    def _():
        run_ref[...] = cin_ref[...]

    for rs in _sub_tiles(x_ref.shape[0]):
        an = _rms(at_ref[rs, :], gmla_ref[...])
        merged = jnp.concatenate([s5_ref[rs, :].astype(BF16), an.astype(BF16)], axis=1)
        x1 = x_ref[rs, :] + jnp.dot(merged, wout_ref[...], preferred_element_type=F32)
        x1_ref[rs, :] = x1
        hn = _rms(x1, gffn_ref[...])
        hn_ref[rs, :] = _pack_rows(hn)
        hn_hi = hn.astype(BF16)
        hn_lo = (hn - hn_hi.astype(F32)).astype(BF16)
        logits = (jnp.dot(hn_hi, wrh_ref[...], preferred_element_type=F32)
                  + jnp.dot(hn_lo, wrh_ref[...], preferred_element_type=F32)
                  + jnp.dot(hn_hi, wrl_ref[...], preferred_element_type=F32) + br_ref[...])
        lane = lax.broadcasted_iota(jnp.int32, logits.shape, 1)
        lane_f = lane.astype(F32)
        work = jnp.where(lane < N_EXPERTS, logits, -jnp.inf)
        vals, idxs = [], []
        for _ in range(TOP_K):
            m = jnp.max(work, axis=-1, keepdims=True)
            i = jnp.min(jnp.where(work == m, lane_f, float(LANES)), axis=-1, keepdims=True)
            vals.append(m)
            idxs.append(i)
            work = jnp.where(lane_f == i, -jnp.inf, work)
        es = [jnp.exp(v - vals[0]) for v in vals]
        den = es[0] + es[1] + es[2] + es[3]
        onehots = [(lane_f == idxs[k]).astype(F32) for k in range(TOP_K)]
        e_all = onehots[0] + onehots[1] + onehots[2] + onehots[3]
        before = jnp.dot(tri_ref[...], e_all.astype(BF16), preferred_element_type=F32) + run_ref[...]
        run_ref[...] = run_ref[...] + jnp.sum(e_all, axis=0, keepdims=True)
        route = jnp.zeros(logits.shape, F32)
        gate_out = jnp.zeros(logits.shape, F32)
        for k in range(TOP_K):
            rank = jnp.sum(onehots[k] * before, axis=-1, keepdims=True)
            route = jnp.where(lane == k, idxs[k], route)
            route = jnp.where(lane == TOP_K + k, rank, route)
            gate_out = jnp.where(lane == k, es[k] / den, gate_out)
        route_ref[rs, :] = route.astype(jnp.int32)
        gate_ref[rs, :] = gate_out
    cout_ref[...] = run_ref[...]


def _out_proj(x, s5n_tm, attn, g_mla, w_out, g_ffn, w_r_hi, w_r_lo, b_r_p, counts_in, bsz, length):
    t = bsz * length
    tm = min(ROW_TILE, length)
    n_l = length // tm
    row = lambda i: (i, 0)
    full = lambda i: (0, 0)
    sub = min(SUB_TILE, tm)
    tri = (jnp.arange(sub)[:, None] > jnp.arange(sub)[None, :]).astype(BF16)
    return pl.pallas_call(
        _out_proj_kernel,
        grid=(t // tm,),
        in_specs=[pl.BlockSpec((tm, D_MODEL), row),
                  pl.BlockSpec((tm, S5_WIDTH), lambda i: (i % n_l, i // n_l)),
                  pl.BlockSpec((tm, MLA_WIDTH), row),
                  pl.BlockSpec((1, MLA_WIDTH), full),
                  pl.BlockSpec((D_MODEL, D_MODEL), full),
                  pl.BlockSpec((1, D_MODEL), full),
                  pl.BlockSpec((D_MODEL, LANES), full),
                  pl.BlockSpec((D_MODEL, LANES), full),
                  pl.BlockSpec((1, LANES), full),
                  pl.BlockSpec((sub, sub), full),
                  pl.BlockSpec((1, LANES), full)],
        out_specs=[pl.BlockSpec((tm, D_MODEL), row),
                   pl.BlockSpec((tm, D_PACK), row),
                   pl.BlockSpec((tm, LANES), row),
                   pl.BlockSpec((tm, LANES), row),
                   pl.BlockSpec((1, LANES), full)],
        out_shape=(jax.ShapeDtypeStruct((t, D_MODEL), F32),
                   jax.ShapeDtypeStruct((t, D_PACK), U32),
                   jax.ShapeDtypeStruct((t, LANES), jnp.int32),
                   jax.ShapeDtypeStruct((t, LANES), F32),
                   jax.ShapeDtypeStruct((1, LANES), F32)),
        scratch_shapes=[pltpu.VMEM((1, LANES), F32)],
        compiler_params=_cparams(("arbitrary",), VMEM_LIMIT),
        name="out_proj",
    )(x, s5n_tm, attn, g_mla, w_out, g_ffn, w_r_hi, w_r_lo, b_r_p, tri, counts_in)


def _sc_mesh():
    return plsc.VectorSubcoreMesh(core_axis_name="c", subcore_axis_name="s",
                                  num_cores=SC_CORES, num_subcores=SC_SUBCORES)


def _sc_for_chunks(n_chunks, fn):
    wid = lax.axis_index("s") * SC_CORES + lax.axis_index("c")
    full, rem = divmod(n_chunks, SC_WORKERS)
    if full:
        @pl.loop(0, full)
        def _(j):
            fn(j * SC_WORKERS + wid)
    if rem:
        @pl.when(wid < rem)
        def _():
            fn(full * SC_WORKERS + wid)


def _chunk_rows(c):
    return pl.ds(pl.multiple_of(c * SC_ROWS, SC_ROWS), SC_ROWS)


def _sc_dispatch_body(dp_hbm, ds_hbm, hp_hbm, hs_hbm, xs_hbm, idx_v, rows_v, sem):
    def chunk(d_hbm, h_hbm, c):
        pltpu.sync_copy(d_hbm.at[c], idx_v)
        pltpu.sync_copy(h_hbm.at[_chunk_rows(c)], rows_v)
        copies = [pltpu.async_copy(rows_v, xs_hbm.at[idx_v.at[k]], sem) for k in range(TOP_K)]
        for cp in copies:
            cp.wait()

    _sc_for_chunks(dp_hbm.shape[0], functools.partial(chunk, dp_hbm, hp_hbm))
    _sc_for_chunks(ds_hbm.shape[0], functools.partial(chunk, ds_hbm, hs_hbm))


def _dispatch(dest_p3, dest_s3, hn_p, hn_s, n_blocks):
    return pl.kernel(
        _sc_dispatch_body,
        out_type=jax.ShapeDtypeStruct((n_blocks * MOE_ROWS, D_PACK), U32),
        mesh=_sc_mesh(),
        scratch_types=[pltpu.VMEM((TOP_K, SC_ROWS), jnp.int32), pltpu.VMEM((SC_ROWS, D_PACK), U32),
                       pltpu.SemaphoreType.DMA],
        name="moe_dispatch_sc",
    )(dest_p3, dest_s3, hn_p, hn_s)


def _expert_kernel(be_ref, nb_ref, nv_ref, x_ref, w1_ref, b1_ref, w2_ref, b2_ref, y_ref, w1b_ref, w2b_ref):
    b = pl.program_id(0)
    prev = be_ref[jnp.maximum(b - 1, 0)]

    @pl.when(jnp.logical_or(b == 0, be_ref[b] != prev))
    def _():
        w1b_ref[...] = w1_ref[0].astype(BF16)
        w2b_ref[...] = w2_ref[0].astype(BF16)

    @pl.when(b < nb_ref[0])
    def _():
        live = lax.broadcasted_iota(jnp.int32, x_ref.shape, 0) < nv_ref[b]
        lo, hi = _unpack_rows(jnp.where(live, x_ref[...], jnp.uint32(0)))
        lo, hi = lo.astype(BF16), hi.astype(BF16)
        b1 = b1_ref[0]
        cw = D_FF // MOE_COL_CHUNKS

        def up(c0):
            return (jnp.dot(lo, w1b_ref[:D_PACK, c0:c0 + cw], preferred_element_type=F32)
                    + jnp.dot(hi, w1b_ref[D_PACK:, c0:c0 + cw], preferred_element_type=F32) + b1[:, c0:c0 + cw])

        y = b2_ref[0]
        for j in range(MOE_COL_CHUNKS):
            gate = jnp.minimum(up(j * cw), SWIGLU_LIMIT)
            lin = jnp.clip(up(D_FF + j * cw), -SWIGLU_LIMIT, SWIGLU_LIMIT)
            act = gate * jax.nn.sigmoid(SWIGLU_ALPHA * gate) * (lin + 1.0)
            y = y + jnp.dot(act.astype(BF16), w2b_ref[j * cw:(j + 1) * cw, :], preferred_element_type=F32)
        y_ref[...] = _pack_rows(y)

    @pl.when(b >= nb_ref[0])
    def _():
        y_ref[...] = jnp.zeros(y_ref.shape, y_ref.dtype)


def _experts(block_e, nb_used, n_valid, xs, w1, b1, w2, b2, n_blocks):
    last = lambda b, be, nb, nv: (jnp.maximum(jnp.minimum(b, nb[0] - 1), 0), 0)
    wsel = lambda b, be, nb, nv: (be[b], 0, 0)
    return pl.pallas_call(
        _expert_kernel,
        grid_spec=pltpu.PrefetchScalarGridSpec(
            num_scalar_prefetch=3,
            grid=(n_blocks,),
            in_specs=[pl.BlockSpec((MOE_ROWS, D_PACK), last),
                      pl.BlockSpec((1, D_MODEL, 2 * D_FF), wsel),
                      pl.BlockSpec((1, 1, 2 * D_FF), wsel),
                      pl.BlockSpec((1, D_FF, D_MODEL), wsel),
                      pl.BlockSpec((1, 1, D_MODEL), wsel)],
            out_specs=pl.BlockSpec((MOE_ROWS, D_PACK), lambda b, be, nb, nv: (b, 0)),
            scratch_shapes=[pltpu.VMEM((D_MODEL, 2 * D_FF), BF16), pltpu.VMEM((D_FF, D_MODEL), BF16)]),
        out_shape=jax.ShapeDtypeStruct((n_blocks * MOE_ROWS, D_PACK), U32),
        compiler_params=_cparams(("arbitrary",), VMEM_LIMIT),
        name="moe_experts",
    )(block_e, nb_used, n_valid, xs, w1, b1.reshape(N_EXPERTS, 1, 2 * D_FF), w2, b2.reshape(N_EXPERTS, 1, D_MODEL))


def _sc_gather_body(dp_hbm, ds_hbm, yb_hbm, gp_hbm, gs_hbm, idx_v, rows_v, sem):
    def chunk(d_hbm, g_hbm, c):
        pltpu.sync_copy(d_hbm.at[c], idx_v)
        for k in range(TOP_K):
            pltpu.async_copy(yb_hbm.at[idx_v.at[k]], rows_v, sem).wait()
            pltpu.sync_copy(rows_v, g_hbm.at[k, _chunk_rows(c)])

    _sc_for_chunks(dp_hbm.shape[0], functools.partial(chunk, dp_hbm, gp_hbm))
    _sc_for_chunks(ds_hbm.shape[0], functools.partial(chunk, ds_hbm, gs_hbm))


def _gather_expert_rows(dest_p3, dest_s3, yb):
    n_p, n_s = dest_p3.shape[0] * SC_ROWS, dest_s3.shape[0] * SC_ROWS
    return pl.kernel(
        _sc_gather_body,
        out_type=(jax.ShapeDtypeStruct((TOP_K, n_p, D_PACK), U32),
                  jax.ShapeDtypeStruct((TOP_K, n_s, D_PACK), U32)),
        mesh=_sc_mesh(),
        scratch_types=[pltpu.VMEM((TOP_K, SC_ROWS), jnp.int32), pltpu.VMEM((SC_ROWS, D_PACK), U32),
                       pltpu.SemaphoreType.DMA],
        name="moe_gather_sc",
    )(dest_p3, dest_s3, yb)


def _combine_kernel(g_ref, gate_ref, x1_ref, o_ref):
    gate = gate_ref[...]
    x1 = x1_ref[...]
    acc_lo, acc_hi = x1[:, :D_PACK], x1[:, D_PACK:]
    for k in range(TOP_K):
        lo, hi = _unpack_rows(g_ref[k])
        acc_lo = acc_lo + gate[:, k:k + 1] * lo
        acc_hi = acc_hi + gate[:, k:k + 1] * hi
    o_ref[:, :D_PACK] = acc_lo
    o_ref[:, D_PACK:] = acc_hi


def _combine(g, gates, x1):
    n_tok = x1.shape[0]
    tm = min(512, n_tok)
    return pl.pallas_call(
        _combine_kernel,
        grid=(n_tok // tm,),
        in_specs=[pl.BlockSpec((TOP_K, tm, D_PACK), lambda i: (0, i, 0)),
                  pl.BlockSpec((tm, LANES), lambda i: (i, 0)),
                  pl.BlockSpec((tm, D_MODEL), lambda i: (i, 0))],
        out_specs=pl.BlockSpec((tm, D_MODEL), lambda i: (i, 0)),
        out_shape=jax.ShapeDtypeStruct((n_tok, D_MODEL), F32),
        compiler_params=_cparams(("parallel",), VMEM_LIMIT),
        name="moe_combine",
    )(g, gates, x1)


def _route_tables(counts_f, n_assign):
    counts = counts_f[0, :N_EXPERTS].astype(jnp.int32)
    padded = (counts + MOE_ROWS - 1) // MOE_ROWS * MOE_ROWS
    pad_end = jnp.cumsum(padded)
    pad_start = pad_end - padded
    n_blocks = -(-(n_assign + N_EXPERTS * (MOE_ROWS - 1)) // MOE_ROWS)
    row0 = jnp.arange(n_blocks, dtype=jnp.int32) * MOE_ROWS
    block_e = jnp.minimum(jnp.sum(row0[:, None] >= pad_end[None, :], axis=1), N_EXPERTS - 1).astype(jnp.int32)
    n_valid = jnp.clip((pad_start + counts)[block_e] - row0, 0, MOE_ROWS).astype(jnp.int32)
    nb_used = (pad_end[-1] // MOE_ROWS).astype(jnp.int32).reshape(1)
    return pad_start, block_e, n_valid, nb_used, n_blocks


def _dest(route, pad_start):
    idx, rank = route[:, :TOP_K], route[:, TOP_K:2 * TOP_K]
    onehot = idx[:, :, None] == jnp.arange(N_EXPERTS, dtype=jnp.int32)[None, None, :]
    dest = rank + jnp.sum(jnp.where(onehot, pad_start[None, None, :], 0), axis=-1)
    return dest.reshape(-1, SC_ROWS, TOP_K).transpose(0, 2, 1)


def _rope_tables(offset, length):
    pos = offset + jnp.arange(length, dtype=F32)
    inv = ROPE_THETA ** (-jnp.arange(QK_ROPE // 2, dtype=F32) * (2.0 / QK_ROPE))
    ang = pos[:, None] * inv[None, :]
    cos, sin = jnp.cos(ang), jnp.sin(ang)
    z32 = jnp.zeros((length, 32), F32)
    rot_c = jnp.concatenate([cos, cos], axis=1)
    rot_s = jnp.concatenate([-sin, sin], axis=1)
    ck = jnp.concatenate([rot_c, z32, rot_c, z32], axis=1)
    sk = jnp.concatenate([rot_s, z32, rot_s, z32], axis=1)
    cq = jnp.concatenate([jnp.ones((length, QK_NOPE), F32), rot_c, z32], axis=1)
    sq = jnp.concatenate([jnp.zeros((length, QK_NOPE), F32), rot_s, z32], axis=1)
    return ck, sk, cq, sq


def _head_gain(g_nope, g_rope):
    return jnp.concatenate([g_nope, g_rope, g_rope, jnp.zeros((HEAD_PAD - QK_HEAD,), F32)]).reshape(1, HEAD_PAD)


def _mixer_stage(x, past, prm, counts_in, bsz, length):
    offset = 0 if past is None else past[0].shape[1]
    ck, sk, cq, sq = _rope_tables(offset, length)
    x2 = x.reshape(bsz * length, D_MODEL)
    u_tm, qn, lat, k_rope, krot = _in_proj(x2, prm['g_mix'], prm['w_in_p'], prm['g_q_lat'], prm['g_kv_lat'],
                                           ck, sk, bsz, length)
    if past is None:
        h0 = jnp.zeros((bsz, S5_STATE_LANES), F32)
    else:
        h0 = _state_layout(past[2], past[3])
    s5n_tm, h_last = _s5(u_tm.reshape(length * bsz, S5_WIDTH), h0, prm['lam'], prm['wb'], prm['wc'],
                         prm['d_skip'], prm['w_glu'], prm['b_glu'], prm['g_s5_out'], bsz, length)
    h_re, h_im = _state_unlayout(h_last)
    q = _q_proj(qn, prm['w_uq_p'], cq, sq, prm['gq'], bsz, length)
    k, v = _kv_proj(lat, krot, prm['w_k'], prm['w_v'], prm['gk'], bsz, length)
    if past is None:
        attn = _attn_prompt(q, k, v, bsz, length)
    else:
        n_past = past[0].shape[1]
        c_lat = past[0].reshape(bsz * n_past, KV_LORA)
        c_rot = jnp.pad(past[1].reshape(bsz * n_past, QK_ROPE), ((0, 0), (QK_NOPE, HEAD_PAD - QK_HEAD)))
        kc, vc = _kv_proj(c_lat, c_rot, prm['w_k'], prm['w_v'], prm['gk'], bsz, n_past)
        attn = _attn_sample(q, kc, vc, k, v, bsz, length, n_past)
    x1, hn, route, gates, counts = _out_proj(x2, s5n_tm.reshape(length, bsz * S5_WIDTH),
                                             attn.reshape(bsz * length, MLA_WIDTH), prm['g_mla_out'],
                                             prm['w_out'], prm['g_ffn'], prm['w_r_hi'], prm['w_r_lo'], prm['b_r_p'],
                                             counts_in, bsz, length)
    return (x1, hn, route, gates, counts,
            lat.reshape(bsz, length, KV_LORA), k_rope.reshape(bsz, length, QK_ROPE), h_re, h_im)


def _prepare(g_mix, w_in, lam_re, lam_im, log_dt, b_s5_re, b_s5_im, c_s5_re, c_s5_im, d_s5, w_glu, b_glu,
             g_q_lat, w_uq, g_kv_lat, w_ukv, g_qn_nope, g_qn_rope, g_kn_nope, g_kn_rope, g_s5_out,
             g_mla_out, w_out, g_ffn, w_router, b_router):
    c0 = S5_WIDTH + Q_LORA + KV_LORA
    w_pe = w_in[:, c0:]
    z = jnp.zeros((D_MODEL, 32), F32)
    w_in_p = jnp.concatenate([w_in[:, :c0], w_pe, z, w_pe, z], axis=1).astype(BF16)
    ar, ai, bbr, bbi = _s5_prep(lam_re, lam_im, log_dt, b_s5_re, b_s5_im)
    wb, wc = _s5_weights(bbr, bbi, c_s5_re, c_s5_im)
    w_uq_p = jnp.pad(w_uq.reshape(Q_LORA, MLA_HEADS, QK_HEAD), ((0, 0), (0, 0), (0, HEAD_PAD - QK_HEAD)))
    w_kv = w_ukv.reshape(KV_LORA, MLA_HEADS, QK_NOPE + V_HEAD)
    w_k = jnp.pad(w_kv[:, :, :QK_NOPE], ((0, 0), (0, 0), (0, HEAD_PAD - QK_NOPE)))
    w_v = w_kv[:, :, QK_NOPE:]
    w_r_p = jnp.pad(w_router, ((0, 0), (0, LANES - N_EXPERTS)))
    w_r_hi = w_r_p.astype(BF16)
    return dict(
        g_mix=g_mix.reshape(1, -1), w_in_p=w_in_p,
        g_q_lat=g_q_lat.reshape(1, -1), g_kv_lat=g_kv_lat.reshape(1, -1),
        lam=_state_layout(ar, ai).reshape(1, S5_STATE_LANES), wb=wb, wc=wc,
        d_skip=d_s5.reshape(1, -1), w_glu=w_glu.astype(BF16), b_glu=b_glu.reshape(1, -1),
        g_s5_out=g_s5_out.reshape(1, -1),
        w_uq_p=w_uq_p.reshape(Q_LORA, MLA_HEADS * HEAD_PAD).astype(BF16),
        w_k=w_k.reshape(KV_LORA, MLA_HEADS * HEAD_PAD).astype(BF16),
        w_v=w_v.reshape(KV_LORA, MLA_WIDTH).astype(BF16),
        gq=_head_gain(g_qn_nope, g_qn_rope), gk=_head_gain(g_kn_nope, g_kn_rope),
        g_mla_out=g_mla_out.reshape(1, -1), w_out=w_out.astype(BF16), g_ffn=g_ffn.reshape(1, -1),
        w_r_hi=w_r_hi, w_r_lo=(w_r_p - w_r_hi.astype(F32)).astype(BF16),
        b_r_p=jnp.pad(b_router, (0, LANES - N_EXPERTS)).reshape(1, LANES),
    )


def _layer(xp, xs, cache_lat, cache_kr, st_re, st_im, mixer_w, w1, b1, w2, b2):
    prm = _prepare(*mixer_w)
    bp, lp, _ = xp.shape
    bs, ls, _ = xs.shape
    zero_counts = jnp.zeros((1, LANES), F32)
    x1p, hnp, routep, gatep, counts_p, latp, krp, hrp, hip = _mixer_stage(xp, None, prm, zero_counts, bp, lp)
    x1s, hns, routes, gates, counts, lats, krs, hrs, his = _mixer_stage(
        xs, (cache_lat, cache_kr, st_re, st_im), prm, counts_p, bs, ls)
    n_tok = bp * lp + bs * ls
    pad_start, block_e, n_valid, nb_used, n_blocks = _route_tables(counts, n_tok * TOP_K)
    dest_p = _dest(routep, pad_start)
    dest_s = _dest(routes, pad_start)
    xs_sorted = _dispatch(dest_p, dest_s, hnp, hns, n_blocks)
    yb = _experts(block_e, nb_used, n_valid, xs_sorted, w1, b1, w2, b2, n_blocks)
    g_p, g_s = _gather_expert_rows(dest_p, dest_s, yb)
    yp = _combine(g_p, gatep, x1p).reshape(bp, lp, D_MODEL)
    ys = _combine(g_s, gates, x1s).reshape(bs, ls, D_MODEL)
    return yp, ys, latp, krp, hrp, hip, lats, krs, hrs, his


def kernel(x_prompt, x_sample, cache_kv_latent, cache_k_rope, state_s5_re, state_s5_im, g_mix, w_in, lam_re,
           lam_im, log_dt, b_s5_re, b_s5_im, c_s5_re, c_s5_im, d_s5, w_glu, b_glu, g_q_lat, w_uq, g_kv_lat,
           w_ukv, g_qn_nope, g_qn_rope, g_kn_nope, g_kn_rope, g_s5_out, g_mla_out, w_out, g_ffn, w_router,
           b_router, w_mlp1, b_mlp1, w_mlp2, b_mlp2):
    depth = g_mix.shape[0]
    yp, ys = x_prompt, x_sample
    outs = [[] for _ in range(8)]
    for l in range(depth):
        mixer_w = (g_mix[l], w_in[l], lam_re[l], lam_im[l], log_dt[l], b_s5_re[l], b_s5_im[l], c_s5_re[l],
                   c_s5_im[l], d_s5[l], w_glu[l], b_glu[l], g_q_lat[l], w_uq[l], g_kv_lat[l], w_ukv[l],
                   g_qn_nope[l], g_qn_rope[l], g_kn_nope[l], g_kn_rope[l], g_s5_out[l], g_mla_out[l],
                   w_out[l], g_ffn[l], w_router[l], b_router[l])
        res = _layer(yp, ys, cache_kv_latent[l], cache_k_rope[l], state_s5_re[l], state_s5_im[l], mixer_w,
                     w_mlp1[l], b_mlp1[l], w_mlp2[l], b_mlp2[l])
        yp, ys = res[0], res[1]
        for o, r in zip(outs, res[2:]):
            o.append(r)
    return (yp, ys) + tuple(jnp.stack(o) for o in outs)
```

```python
import functools
import math

import jax
import jax.numpy as jnp
from jax import lax
from jax.experimental import pallas as pl
from jax.experimental.pallas import tpu as pltpu
from jax.experimental.pallas import tpu_sc as plsc

F32 = jnp.float32
BF16 = jnp.bfloat16
U32 = jnp.uint32

D_MODEL = 1024
S5_WIDTH = 512
S5_GROUP = 16
S5_GROUPS = 32
S5_STATE = 64
MLA_HEADS = 8
QK_NOPE = 64
QK_ROPE = 32
QK_HEAD = QK_NOPE + QK_ROPE
V_HEAD = 64
MLA_WIDTH = MLA_HEADS * V_HEAD
Q_LORA = 384
KV_LORA = 256
ROPE_THETA = 10000.0
CHUNK = 64
N_EXPERTS = 32
TOP_K = 4
D_FF = D_MODEL
SWIGLU_LIMIT = 7.0
SWIGLU_ALPHA = 1.702
EPS = 1e-6

LANES = 128
HEAD_PAD = 128
D_IN_PAD = 1280
S5_CHUNKS = 4
S5_CHUNK_LANES = 1024
S5_STATE_LANES = S5_CHUNKS * S5_CHUNK_LANES
ROW_TILE = 1024
SUB_TILE = 512
MOE_ROWS = 512
MOE_COL_CHUNKS = 2
MOE_ROW_CHUNKS = 1
VMEM_LIMIT = 56 * 1024 * 1024
SC_CORES = 2
SC_SUBCORES = 16
SC_WORKERS = SC_CORES * SC_SUBCORES
SC_ROWS = 128
D_PACK = D_MODEL // 2


def _cparams(sem, vmem=None):
    return pltpu.CompilerParams(dimension_semantics=sem, vmem_limit_bytes=vmem)


def _rms(x, g, n=None):
    n = x.shape[-1] if n is None else n
    ms = jnp.sum(x * x, axis=-1, keepdims=True) * (1.0 / n)
    return x * lax.rsqrt(ms + EPS) * g


def _sub_tiles(rows):
    sub = min(SUB_TILE, rows)
    return [slice(r, r + sub) for r in range(0, rows, sub)]


def _pack_rows(x):
    lo = lax.bitcast_convert_type(x[:, :D_PACK].astype(BF16).astype(F32), U32)
    hi = lax.bitcast_convert_type(x[:, D_PACK:].astype(BF16).astype(F32), U32)
    return (lo >> 16) | (hi & jnp.uint32(0xFFFF0000))


def _unpack_rows(w):
    lo = lax.bitcast_convert_type(w << 16, F32)
    hi = lax.bitcast_convert_type(w & jnp.uint32(0xFFFF0000), F32)
    return lo, hi


def _prep_kernel(lr_ref, li_ref, ldt_ref, br_ref, bi_ref, ar_ref, ai_ref, bbr_ref, bbi_ref):
    lr = lr_ref[...]
    li = li_ref[...]
    dt = jnp.exp(ldt_ref[...])
    mag = jnp.exp(lr * dt)
    ar = mag * jnp.cos(li * dt)
    ai = mag * jnp.sin(li * dt)
    ar_ref[...] = ar
    ai_ref[...] = ai
    den = lr * lr + li * li
    cr = ((ar - 1.0) * lr + ai * li) / den
    ci = (ai * lr - (ar - 1.0) * li) / den
    br = br_ref[...]
    bi = bi_ref[...]
    bbr_ref[...] = cr[:, None, :] * br - ci[:, None, :] * bi
    bbi_ref[...] = cr[:, None, :] * bi + ci[:, None, :] * br


def _s5_prep(lam_re, lam_im, log_dt, b_re, b_im):
    g, n = lam_re.shape
    p = b_re.shape[-1]
    out = pl.pallas_call(
        _prep_kernel,
        out_shape=(jax.ShapeDtypeStruct((g, n), F32), jax.ShapeDtypeStruct((g, n), F32),
                   jax.ShapeDtypeStruct((g, p, n), F32), jax.ShapeDtypeStruct((g, p, n), F32)),
        name="s5_prep",
    )(lam_re, lam_im, log_dt.reshape(g, 1), jnp.swapaxes(b_re, 1, 2), jnp.swapaxes(b_im, 1, 2))
    return out


def _state_layout(re, im):
    lead = re.shape[:-2]
    re = re.reshape(lead + (S5_CHUNKS, 512))
    im = im.reshape(lead + (S5_CHUNKS, 512))
    return jnp.stack([re, im], axis=-2).reshape(lead + (S5_STATE_LANES,))


def _state_unlayout(h):
    lead = h.shape[:-1]
    h = h.reshape(lead + (S5_CHUNKS, 2, 512))
    re = h[..., 0, :].reshape(lead + (S5_GROUPS, S5_STATE))
    im = h[..., 1, :].reshape(lead + (S5_GROUPS, S5_STATE))
    return re, im


def _s5_weights(bbr, bbi, c_re, c_im):
    eye8 = jnp.eye(8, dtype=F32)

    def blockdiag(m):
        a, b = m.shape[1], m.shape[2]
        return (eye8[:, None, :, None] * m[:, :, None, :]).reshape(8 * a, 8 * b)

    wb, wc = [], []
    for c in range(S5_CHUNKS):
        sl = slice(8 * c, 8 * c + 8)
        wb.append(jnp.concatenate([blockdiag(bbr[sl]), blockdiag(bbi[sl])], axis=1))
        cr_t = jnp.swapaxes(c_re[sl], 1, 2)
        ci_t = jnp.swapaxes(c_im[sl], 1, 2)
        wc.append(jnp.concatenate([blockdiag(cr_t), -blockdiag(ci_t)], axis=0))
    return jnp.stack(wb).astype(BF16), jnp.stack(wc).astype(BF16)


def _in_proj_kernel(x_ref, gmix_ref, w_ref, gq_ref, gkv_ref, ck_ref, sk_ref,
                    u_ref, qn_ref, lat_ref, krope_ref, krot_ref):
    for rs in _sub_tiles(x_ref.shape[0]):
        xn = _rms(x_ref[rs, :], gmix_ref[...]).astype(BF16)
        z = jnp.dot(xn, w_ref[...], preferred_element_type=F32)
        u_ref[rs, :] = z[:, :S5_WIDTH]
        qn_ref[rs, :] = _rms(z[:, S5_WIDTH:S5_WIDTH + Q_LORA], gq_ref[...]).astype(BF16)
        c0 = S5_WIDTH + Q_LORA
        lat_ref[rs, :] = _rms(z[:, c0:c0 + KV_LORA], gkv_ref[...])
        kp = z[:, c0 + KV_LORA:]
        lane = lax.broadcasted_iota(jnp.int32, kp.shape, 1)
        first_half = (lane % 64) < 16
        sw = jnp.where(first_half, pltpu.roll(kp, LANES - 16, axis=1), pltpu.roll(kp, 16, axis=1))
        kr = kp * ck_ref[rs, :] + sw * sk_ref[rs, :]
        krope_ref[rs, :] = kr[:, :QK_ROPE]
        krot_ref[rs, :] = jnp.where(lane >= 64, kr, 0.0)


def _in_proj(x, g_mix, w_in_p, g_q_lat, g_kv_lat, ck, sk, bsz, length):
    t = bsz * length
    tm = min(ROW_TILE, length)
    n_l = length // tm
    row = lambda i: (i, 0)
    tab = lambda i: (i % n_l, 0)
    full = lambda i: (0, 0)
    return pl.pallas_call(
        _in_proj_kernel,
        grid=(t // tm,),
        in_specs=[pl.BlockSpec((tm, D_MODEL), row),
                  pl.BlockSpec((1, D_MODEL), full),
                  pl.BlockSpec((D_MODEL, D_IN_PAD), full),
                  pl.BlockSpec((1, Q_LORA), full),
                  pl.BlockSpec((1, KV_LORA), full),
                  pl.BlockSpec((tm, LANES), tab),
                  pl.BlockSpec((tm, LANES), tab)],
        out_specs=[pl.BlockSpec((tm, S5_WIDTH), lambda i: (i % n_l, i // n_l)),
                   pl.BlockSpec((tm, Q_LORA), row),
                   pl.BlockSpec((tm, KV_LORA), row),
                   pl.BlockSpec((tm, QK_ROPE), row),
                   pl.BlockSpec((tm, LANES), row)],
        out_shape=(jax.ShapeDtypeStruct((length, bsz * S5_WIDTH), F32),
                   jax.ShapeDtypeStruct((t, Q_LORA), BF16),
                   jax.ShapeDtypeStruct((t, KV_LORA), F32),
                   jax.ShapeDtypeStruct((t, QK_ROPE), F32),
                   jax.ShapeDtypeStruct((t, LANES), F32)),
        compiler_params=_cparams(("parallel",), VMEM_LIMIT),
        name="in_proj",
    )(x, g_mix, w_in_p, g_q_lat, g_kv_lat, ck, sk)


def _s5_kernel(u_ref, h0_ref, lam_ref, wb_ref, wc_ref, dskip_ref, wglu_ref, bglu_ref, gout_ref,
               y_ref, hlast_ref, bu_ref, h_ref, *, bsz, steps):
    c_id = pl.program_id(0)

    @pl.when(c_id == 0)
    def _():
        h_ref[...] = h0_ref[...]

    u = u_ref[...]
    ub = u.astype(BF16)
    for c in range(S5_CHUNKS):
        bu_ref[:, c * S5_CHUNK_LANES:(c + 1) * S5_CHUNK_LANES] = jnp.dot(
            ub[:, c * LANES:(c + 1) * LANES], wb_ref[c], preferred_element_type=F32)

    for c in range(S5_CHUNKS):
        re = slice(c * S5_CHUNK_LANES, c * S5_CHUNK_LANES + 512)
        im = slice(c * S5_CHUNK_LANES + 512, (c + 1) * S5_CHUNK_LANES)
        lam_r = jnp.broadcast_to(lam_ref[:, re], (bsz, 512))
        lam_i = jnp.broadcast_to(lam_ref[:, im], (bsz, 512))

        def step(t, carry, re=re, im=im, lam_r=lam_r, lam_i=lam_i):
            hr, hi = carry
            r0 = pl.multiple_of(t * bsz, bsz)
            nr = lam_r * hr - lam_i * hi + bu_ref[pl.ds(r0, bsz), re]
            ni = lam_r * hi + lam_i * hr + bu_ref[pl.ds(r0, bsz), im]
            bu_ref[pl.ds(r0, bsz), re] = nr
            bu_ref[pl.ds(r0, bsz), im] = ni
            return nr, ni

        hr, hi = lax.fori_loop(0, steps, step, (h_ref[:, re], h_ref[:, im]), unroll=2)
        h_ref[:, re] = hr
        h_ref[:, im] = hi

    ys = []
    for c in range(S5_CHUNKS):
        hs = bu_ref[:, c * S5_CHUNK_LANES:(c + 1) * S5_CHUNK_LANES].astype(BF16)
        ys.append(jnp.dot(hs, wc_ref[c], preferred_element_type=F32))
    y = jnp.concatenate(ys, axis=1) + dskip_ref[...] * u
    y = jax.nn.gelu(y)
    gate = jnp.dot(y.astype(BF16), wglu_ref[...], preferred_element_type=F32) + bglu_ref[...]
    y = y * jax.nn.sigmoid(gate)
    y_ref[...] = _rms(y, gout_ref[...])

    @pl.when(c_id == pl.num_programs(0) - 1)
    def _():
        hlast_ref[...] = h_ref[...]


def _s5(u_tm, h0, lam, wb, wc, d_skip, w_glu, b_glu, g_out, bsz, length):
    steps = min(32, length)
    rows = steps * bsz
    full2 = lambda c: (0, 0)
    full3 = lambda c: (0, 0, 0)
    return pl.pallas_call(
        functools.partial(_s5_kernel, bsz=bsz, steps=steps),
        grid=(length // steps,),
        in_specs=[pl.BlockSpec((rows, S5_WIDTH), lambda c: (c, 0)),
                  pl.BlockSpec((bsz, S5_STATE_LANES), full2),
                  pl.BlockSpec((1, S5_STATE_LANES), full2),
                  pl.BlockSpec((S5_CHUNKS, LANES, S5_CHUNK_LANES), full3),
                  pl.BlockSpec((S5_CHUNKS, S5_CHUNK_LANES, LANES), full3),
                  pl.BlockSpec((1, S5_WIDTH), full2),
                  pl.BlockSpec((S5_WIDTH, S5_WIDTH), full2),
                  pl.BlockSpec((1, S5_WIDTH), full2),
                  pl.BlockSpec((1, S5_WIDTH), full2)],
        out_specs=[pl.BlockSpec((rows, S5_WIDTH), lambda c: (c, 0)),
                   pl.BlockSpec((bsz, S5_STATE_LANES), full2)],
        out_shape=(jax.ShapeDtypeStruct((length * bsz, S5_WIDTH), F32),
                   jax.ShapeDtypeStruct((bsz, S5_STATE_LANES), F32)),
        scratch_shapes=[pltpu.VMEM((rows, S5_STATE_LANES), F32),
                        pltpu.VMEM((bsz, S5_STATE_LANES), F32)],
        compiler_params=_cparams(("arbitrary",), VMEM_LIMIT),
        name="s5_mixer",
    )(u_tm, h0, lam, wb, wc, d_skip, w_glu, b_glu, g_out)


def _q_proj_kernel(qn_ref, w_ref, wsw_ref, ones_ref, cq_ref, sq_ref, g_ref, q_ref):
    g = g_ref[...] * (QK_HEAD ** -0.5)
    for rs in _sub_tiles(qn_ref.shape[0]):
        qn = qn_ref[rs, :]
        qf = jnp.dot(qn, w_ref[...], preferred_element_type=F32)
        qs = jnp.dot(qn, wsw_ref[...], preferred_element_type=F32)
        cq, sq = cq_ref[rs, :], sq_ref[rs, :]
        for p in range(MLA_HEADS // 2):
            cols = slice(2 * p * HEAD_PAD, 2 * (p + 1) * HEAD_PAD)
            xr = qf[:, cols] * cq + qs[:, cols] * sq
            sqr = xr * xr
            hi = sqr.astype(BF16)
            lo = (sqr - hi.astype(F32)).astype(BF16)
            ss = (jnp.dot(hi, ones_ref[...], preferred_element_type=F32)
                  + jnp.dot(lo, ones_ref[...], preferred_element_type=F32))
            q = (xr * lax.rsqrt(ss * (1.0 / QK_HEAD) + EPS) * g).astype(BF16)
            q_ref[0, 2 * p, rs, :] = q[:, :HEAD_PAD]
            q_ref[0, 2 * p + 1, rs, :] = q[:, HEAD_PAD:]


def _q_proj(qn, w_uq_p, w_uq_sw, cq, sq, gq, bsz, length):
    tm = min(ROW_TILE, length)
    n_l = length // tm
    full = lambda b, i: (0, 0)
    pair = 2 * HEAD_PAD
    two = lambda a: jnp.concatenate([a, a], axis=1)
    ones2 = (jnp.arange(pair)[:, None] // HEAD_PAD == jnp.arange(pair)[None, :] // HEAD_PAD).astype(BF16)
    return pl.pallas_call(
        _q_proj_kernel,
        grid=(bsz, n_l),
        in_specs=[pl.BlockSpec((tm, Q_LORA), lambda b, i: (b * n_l + i, 0)),
                  pl.BlockSpec((Q_LORA, MLA_HEADS * HEAD_PAD), full),
                  pl.BlockSpec((Q_LORA, MLA_HEADS * HEAD_PAD), full),
                  pl.BlockSpec((pair, pair), full),
                  pl.BlockSpec((tm, pair), lambda b, i: (i, 0)),
                  pl.BlockSpec((tm, pair), lambda b, i: (i, 0)),
                  pl.BlockSpec((1, pair), full)],
        out_specs=pl.BlockSpec((1, MLA_HEADS, tm, HEAD_PAD), lambda b, i: (b, 0, i, 0)),
        out_shape=jax.ShapeDtypeStruct((bsz, MLA_HEADS, length, HEAD_PAD), BF16),
        compiler_params=_cparams(("parallel", "parallel"), VMEM_LIMIT),
        name="q_proj",
    )(qn, w_uq_p, w_uq_sw, ones2, two(cq), two(sq), two(gq))


def _kv_proj_kernel(lat_ref, krot_ref, wk_ref, wv_ref, g_ref, k_ref, v_ref):
    g = g_ref[...]
    for rs in _sub_tiles(lat_ref.shape[0]):
        lat = lat_ref[rs, :].astype(BF16)
        kf = jnp.dot(lat, wk_ref[...], preferred_element_type=F32)
        vf = jnp.dot(lat, wv_ref[...], preferred_element_type=F32)
        krot = krot_ref[rs, :]
        for h in range(MLA_HEADS):
            k = kf[:, h * HEAD_PAD:(h + 1) * HEAD_PAD] + krot
            k_ref[0, h, rs, :] = _rms(k, g, QK_HEAD).astype(BF16)
        for p in range(MLA_HEADS // 2):
            v_ref[0, p, rs, :] = vf[:, p * LANES:(p + 1) * LANES].astype(BF16)


def _kv_proj(lat, krot, w_k, w_v, gk, bsz, length):
    tm = min(ROW_TILE, length)
    n_l = length // tm
    full = lambda b, i: (0, 0)
    row = lambda b, i: (b * n_l + i, 0)
    return pl.pallas_call(
        _kv_proj_kernel,
        grid=(bsz, n_l),
        in_specs=[pl.BlockSpec((tm, KV_LORA), row),
                  pl.BlockSpec((tm, HEAD_PAD), row),
                  pl.BlockSpec((KV_LORA, MLA_HEADS * HEAD_PAD), full),
                  pl.BlockSpec((KV_LORA, MLA_WIDTH), full),
                  pl.BlockSpec((1, HEAD_PAD), full)],
        out_specs=[pl.BlockSpec((1, MLA_HEADS, tm, HEAD_PAD), lambda b, i: (b, 0, i, 0)),
                   pl.BlockSpec((1, MLA_HEADS // 2, tm, LANES), lambda b, i: (b, 0, i, 0))],
        out_shape=(jax.ShapeDtypeStruct((bsz, MLA_HEADS, length, HEAD_PAD), BF16),
                   jax.ShapeDtypeStruct((bsz, MLA_HEADS // 2, length, LANES), BF16)),
        compiler_params=_cparams(("parallel", "parallel"), VMEM_LIMIT),
        name="kv_proj",
    )(lat, krot, w_k, w_v, gk)


def _scores(q, k):
    return lax.dot_general(q, k, (((1,), (1,)), ((), ())), preferred_element_type=F32)


def _attn_prompt_kernel(q_ref, k_ref, v_ref, o_ref, *, length, tq):
    n_q = length // tq
    row = lax.broadcasted_iota(jnp.int32, (tq, tq), 0)
    col = lax.broadcasted_iota(jnp.int32, (tq, tq), 1)
    visible = (col // CHUNK) <= (row // CHUNK)
    lane = lax.broadcasted_iota(jnp.int32, (tq, LANES), 1)
    for qi in range(n_q):
        q0 = qi * tq
        outs = []
        for hh in range(2):
            q = q_ref[0, hh, q0:q0 + tq, :]
            sd = jnp.where(visible, _scores(q, k_ref[0, hh, q0:q0 + tq, :]), -jnp.inf)
            m = jnp.max(sd, axis=-1, keepdims=True)
            if qi:
                so = _scores(q, k_ref[0, hh, 0:q0, :])
                m = jnp.maximum(m, jnp.max(so, axis=-1, keepdims=True))
            pd = jnp.exp(sd - m)
            l = jnp.sum(pd, axis=-1, keepdims=True)
            acc = jnp.dot(pd.astype(BF16), v_ref[0, 0, q0:q0 + tq, :], preferred_element_type=F32)
            if qi:
                po = jnp.exp(so - m)
                l = l + jnp.sum(po, axis=-1, keepdims=True)
                acc = acc + jnp.dot(po.astype(BF16), v_ref[0, 0, 0:q0, :], preferred_element_type=F32)
            outs.append(acc / l)
        o_ref[0, q0:q0 + tq, :] = jnp.where(lane < V_HEAD, outs[0], outs[1])


def _attn_prompt(q, k, v, bsz, length):
    tq = min(512, length)
    pairs = MLA_HEADS // 2
    return pl.pallas_call(
        functools.partial(_attn_prompt_kernel, length=length, tq=tq),
        grid=(bsz, pairs),
        in_specs=[pl.BlockSpec((1, 2, length, HEAD_PAD), lambda b, p: (b, p, 0, 0)),
                  pl.BlockSpec((1, 2, length, HEAD_PAD), lambda b, p: (b, p, 0, 0)),
                  pl.BlockSpec((1, 1, length, LANES), lambda b, p: (b, p, 0, 0))],
        out_specs=pl.BlockSpec((1, length, LANES), lambda b, p: (b, 0, p)),
        out_shape=jax.ShapeDtypeStruct((bsz, length, MLA_WIDTH), F32),
        compiler_params=_cparams(("parallel", "parallel"), VMEM_LIMIT),
        name="attn_prompt",
    )(q, k, v)


def _attn_sample_kernel(q_ref, kc_ref, vc_ref, kn_ref, vn_ref, o_ref):
    lq = q_ref.shape[2]
    lane = lax.broadcasted_iota(jnp.int32, (lq, LANES), 1)
    outs = []
    for hh in range(2):
        q = q_ref[0, hh]
        sc = _scores(q, kc_ref[0, hh])
        sn = _scores(q, kn_ref[0, hh])
        m = jnp.maximum(jnp.max(sc, axis=-1, keepdims=True), jnp.max(sn, axis=-1, keepdims=True))
        pc = jnp.exp(sc - m)
        pn = jnp.exp(sn - m)
        l = jnp.sum(pc, axis=-1, keepdims=True) + jnp.sum(pn, axis=-1, keepdims=True)
        acc = (jnp.dot(pc.astype(BF16), vc_ref[0, 0], preferred_element_type=F32)
               + jnp.dot(pn.astype(BF16), vn_ref[0, 0], preferred_element_type=F32))
        outs.append(acc / l)
    o_ref[0] = jnp.where(lane < V_HEAD, outs[0], outs[1])


def _attn_sample(q, kc, vc, kn, vn, bsz, lq, past):
    pairs = MLA_HEADS // 2
    hp = lambda b, p: (b, p, 0, 0)
    return pl.pallas_call(
        _attn_sample_kernel,
        grid=(bsz, pairs),
        in_specs=[pl.BlockSpec((1, 2, lq, HEAD_PAD), hp),
                  pl.BlockSpec((1, 2, past, HEAD_PAD), hp),
                  pl.BlockSpec((1, 1, past, LANES), hp),
                  pl.BlockSpec((1, 2, lq, HEAD_PAD), hp),
                  pl.BlockSpec((1, 1, lq, LANES), hp)],
        out_specs=pl.BlockSpec((1, lq, LANES), lambda b, p: (b, 0, p)),
        out_shape=jax.ShapeDtypeStruct((bsz, lq, MLA_WIDTH), F32),
        compiler_params=_cparams(("parallel", "parallel"), VMEM_LIMIT),
        name="attn_sample",
    )(q, kc, vc, kn, vn)


def _out_proj_kernel(x_ref, s5_ref, at_ref, gmla_ref, wout_ref, gffn_ref, wrh_ref, wrl_ref, br_ref, tri_ref,
                     cin_ref,
                     x1_ref, hn_ref, route_ref, gate_ref, cout_ref, run_ref):
    @pl.when(pl.program_id(0) == 0)
    def _():
        run_ref[...] = cin_ref[...]

    for rs in _sub_tiles(x_ref.shape[0]):
        an = _rms(at_ref[rs, :], gmla_ref[...])
        merged = jnp.concatenate([s5_ref[rs, :].astype(BF16), an.astype(BF16)], axis=1)
        x1 = x_ref[rs, :] + jnp.dot(merged, wout_ref[...], preferred_element_type=F32)
        x1_ref[rs, :] = x1
        hn = _rms(x1, gffn_ref[...])
        hn_ref[rs, :] = _pack_rows(hn)
        hn_hi = hn.astype(BF16)
        hn_lo = (hn - hn_hi.astype(F32)).astype(BF16)
        logits = (jnp.dot(hn_hi, wrh_ref[...], preferred_element_type=F32)
                  + jnp.dot(hn_lo, wrh_ref[...], preferred_element_type=F32)
                  + jnp.dot(hn_hi, wrl_ref[...], preferred_element_type=F32) + br_ref[...])
        lane = lax.broadcasted_iota(jnp.int32, logits.shape, 1)
        lane_f = lane.astype(F32)
        work = jnp.where(lane < N_EXPERTS, logits, -jnp.inf)
        vals, idxs = [], []
        for _ in range(TOP_K):
            m = jnp.max(work, axis=-1, keepdims=True)
            i = jnp.min(jnp.where(work == m, lane_f, float(LANES)), axis=-1, keepdims=True)
            vals.append(m)
            idxs.append(i)
            work = jnp.where(lane_f == i, -jnp.inf, work)
        es = [jnp.exp(v - vals[0]) for v in vals]
        den = es[0] + es[1] + es[2] + es[3]
        onehots = [(lane_f == idxs[k]).astype(F32) for k in range(TOP_K)]
        e_all = onehots[0] + onehots[1] + onehots[2] + onehots[3]
        before = jnp.dot(tri_ref[...], e_all.astype(BF16), preferred_element_type=F32) + run_ref[...]
        run_ref[...] = run_ref[...] + jnp.sum(e_all, axis=0, keepdims=True)
        route = jnp.zeros(logits.shape, F32)
        gate_out = jnp.zeros(logits.shape, F32)
        for k in range(TOP_K):
            rank = jnp.sum(onehots[k] * before, axis=-1, keepdims=True)
            route = jnp.where(lane == k, idxs[k], route)
            route = jnp.where(lane == TOP_K + k, rank, route)
            gate_out = jnp.where(lane == k, es[k] / den, gate_out)
        route_ref[rs, :] = route.astype(jnp.int32)
        gate_ref[rs, :] = gate_out
    cout_ref[...] = run_ref[...]


def _out_proj(x, s5n_tm, attn, g_mla, w_out, g_ffn, w_r_hi, w_r_lo, b_r_p, counts_in, bsz, length):
    t = bsz * length
    tm = min(ROW_TILE, length)
    n_l = length // tm
    row = lambda i: (i, 0)
    full = lambda i: (0, 0)
    sub = min(SUB_TILE, tm)
    tri = (jnp.arange(sub)[:, None] > jnp.arange(sub)[None, :]).astype(BF16)
    return pl.pallas_call(
        _out_proj_kernel,
        grid=(t // tm,),
        in_specs=[pl.BlockSpec((tm, D_MODEL), row),
                  pl.BlockSpec((tm, S5_WIDTH), lambda i: (i % n_l, i // n_l)),
                  pl.BlockSpec((tm, MLA_WIDTH), row),
                  pl.BlockSpec((1, MLA_WIDTH), full),
                  pl.BlockSpec((D_MODEL, D_MODEL), full),
                  pl.BlockSpec((1, D_MODEL), full),
                  pl.BlockSpec((D_MODEL, LANES), full),
                  pl.BlockSpec((D_MODEL, LANES), full),
                  pl.BlockSpec((1, LANES), full),
                  pl.BlockSpec((sub, sub), full),
                  pl.BlockSpec((1, LANES), full)],
        out_specs=[pl.BlockSpec((tm, D_MODEL), row),
                   pl.BlockSpec((tm, D_PACK), row),
                   pl.BlockSpec((tm, LANES), row),
                   pl.BlockSpec((tm, LANES), row),
                   pl.BlockSpec((1, LANES), full)],
        out_shape=(jax.ShapeDtypeStruct((t, D_MODEL), F32),
                   jax.ShapeDtypeStruct((t, D_PACK), U32),
                   jax.ShapeDtypeStruct((t, LANES), jnp.int32),
                   jax.ShapeDtypeStruct((t, LANES), F32),
                   jax.ShapeDtypeStruct((1, LANES), F32)),
        scratch_shapes=[pltpu.VMEM((1, LANES), F32)],
        compiler_params=_cparams(("arbitrary",), VMEM_LIMIT),
        name="out_proj",
    )(x, s5n_tm, attn, g_mla, w_out, g_ffn, w_r_hi, w_r_lo, b_r_p, tri, counts_in)


def _sc_mesh():
    return plsc.VectorSubcoreMesh(core_axis_name="c", subcore_axis_name="s",
                                  num_cores=SC_CORES, num_subcores=SC_SUBCORES)


def _sc_for_chunks(n_chunks, fn):
    wid = lax.axis_index("s") * SC_CORES + lax.axis_index("c")
    full, rem = divmod(n_chunks, SC_WORKERS)
    if full:
        @pl.loop(0, full)
        def _(j):
            fn(j * SC_WORKERS + wid)
    if rem:
        @pl.when(wid < rem)
        def _():
            fn(full * SC_WORKERS + wid)


def _chunk_rows(c):
    return pl.ds(pl.multiple_of(c * SC_ROWS, SC_ROWS), SC_ROWS)


def _sc_dispatch_body(dp_hbm, ds_hbm, hp_hbm, hs_hbm, xs_hbm, idx_v, rows_v, sem):
    def chunk(d_hbm, h_hbm, c):
        pltpu.sync_copy(d_hbm.at[c], idx_v)
        pltpu.sync_copy(h_hbm.at[_chunk_rows(c)], rows_v)
        copies = [pltpu.async_copy(rows_v, xs_hbm.at[idx_v.at[k]], sem) for k in range(TOP_K)]
        for cp in copies:
            cp.wait()

    _sc_for_chunks(dp_hbm.shape[0], functools.partial(chunk, dp_hbm, hp_hbm))
    _sc_for_chunks(ds_hbm.shape[0], functools.partial(chunk, ds_hbm, hs_hbm))


def _dispatch(dest_p3, dest_s3, hn_p, hn_s, n_blocks):
    return pl.kernel(
        _sc_dispatch_body,
        out_type=jax.ShapeDtypeStruct((n_blocks * MOE_ROWS, D_PACK), U32),
        mesh=_sc_mesh(),
        scratch_types=[pltpu.VMEM((TOP_K, SC_ROWS), jnp.int32), pltpu.VMEM((SC_ROWS, D_PACK), U32),
                       pltpu.SemaphoreType.DMA],
        name="moe_dispatch_sc",
    )(dest_p3, dest_s3, hn_p, hn_s)


def _expert_kernel(be_ref, nb_ref, nv_ref, first_ref, slot_ref, nxt_ref,
                   x_ref, w1_hbm, b1_ref, w2_hbm, b2_ref, y_ref, w1f_ref, w2f_ref, w1b_ref, w2b_ref, sem):
    b = pl.program_id(0)

    def weight_copies(e, s):
        return (pltpu.make_async_copy(w1_hbm.at[e], w1f_ref.at[s], sem.at[s]),
                pltpu.make_async_copy(w2_hbm.at[e], w2f_ref.at[s], sem.at[s]))

    @pl.when(jnp.logical_and(b == 0, nb_ref[0] > 0))
    def _():
        for cp in weight_copies(be_ref[0], 0):
            cp.start()

    @pl.when(jnp.logical_and(b < nb_ref[0], first_ref[b] == 1))
    def _():
        s = slot_ref[b]
        for cp in weight_copies(be_ref[b], s):
            cp.wait()
        w1b_ref[...] = w1f_ref[s].astype(BF16)
        w2b_ref[...] = w2f_ref[s].astype(BF16)

        @pl.when(nxt_ref[b] >= 0)
        def _():
            for cp in weight_copies(nxt_ref[b], 1 - s):
                cp.start()

    @pl.when(b < nb_ref[0])
    def _():
        b1 = b1_ref[0]
        cw = D_FF // MOE_COL_CHUNKS
        sub = MOE_ROWS // MOE_ROW_CHUNKS
        for r0 in range(0, MOE_ROWS, sub):
            rs = slice(r0, r0 + sub)
            live = lax.broadcasted_iota(jnp.int32, (sub, D_PACK), 0) < nv_ref[b] - r0
            lo, hi = _unpack_rows(jnp.where(live, x_ref[rs, :], jnp.uint32(0)))
            lo, hi = lo.astype(BF16), hi.astype(BF16)

            def up(c0, lo=lo, hi=hi):
                return (jnp.dot(lo, w1b_ref[:D_PACK, c0:c0 + cw], preferred_element_type=F32)
                        + jnp.dot(hi, w1b_ref[D_PACK:, c0:c0 + cw], preferred_element_type=F32)
                        + b1[:, c0:c0 + cw])

            y = b2_ref[0]
            for j in range(MOE_COL_CHUNKS):
                gate = jnp.minimum(up(j * cw), SWIGLU_LIMIT)
                lin = jnp.clip(up(D_FF + j * cw), -SWIGLU_LIMIT, SWIGLU_LIMIT)
                act = gate * jax.nn.sigmoid(SWIGLU_ALPHA * gate) * (lin + 1.0)
                y = y + jnp.dot(act.astype(BF16), w2b_ref[j * cw:(j + 1) * cw, :], preferred_element_type=F32)
            y_ref[rs, :] = _pack_rows(y)

    @pl.when(b >= nb_ref[0])
    def _():
        y_ref[...] = jnp.zeros(y_ref.shape, y_ref.dtype)


def _experts(tables, xs, w1, b1, w2, b2, n_blocks):
    last = lambda b, be, nb, *_: (jnp.maximum(jnp.minimum(b, nb[0] - 1), 0), 0)
    bsel = lambda b, be, *_: (be[b], 0, 0)
    return pl.pallas_call(
        _expert_kernel,
        grid_spec=pltpu.PrefetchScalarGridSpec(
            num_scalar_prefetch=6,
            grid=(n_blocks,),
            in_specs=[pl.BlockSpec((MOE_ROWS, D_PACK), last),
                      pl.BlockSpec(memory_space=pl.ANY),
                      pl.BlockSpec((1, 1, 2 * D_FF), bsel),
                      pl.BlockSpec(memory_space=pl.ANY),
                      pl.BlockSpec((1, 1, D_MODEL), bsel)],
            out_specs=pl.BlockSpec((MOE_ROWS, D_PACK), lambda b, *_: (b, 0)),
            scratch_shapes=[pltpu.VMEM((2, D_MODEL, 2 * D_FF), F32), pltpu.VMEM((2, D_FF, D_MODEL), F32),
                            pltpu.VMEM((D_MODEL, 2 * D_FF), BF16), pltpu.VMEM((D_FF, D_MODEL), BF16),
                            pltpu.SemaphoreType.DMA((2,))]),
        out_shape=jax.ShapeDtypeStruct((n_blocks * MOE_ROWS, D_PACK), U32),
        compiler_params=_cparams(("arbitrary",), VMEM_LIMIT),
        name="moe_experts",
    )(*tables, xs, w1, b1.reshape(N_EXPERTS, 1, 2 * D_FF), w2, b2.reshape(N_EXPERTS, 1, D_MODEL))


def _sc_gather_body(dp_hbm, ds_hbm, yb_hbm, gp_hbm, gs_hbm, idx_v, rows_v, sem):
    def chunk(d_hbm, g_hbm, c):
        pltpu.sync_copy(d_hbm.at[c], idx_v)
        for k in range(TOP_K):
            pltpu.async_copy(yb_hbm.at[idx_v.at[k]], rows_v, sem).wait()
            pltpu.sync_copy(rows_v, g_hbm.at[k, _chunk_rows(c)])

    _sc_for_chunks(dp_hbm.shape[0], functools.partial(chunk, dp_hbm, gp_hbm))
    _sc_for_chunks(ds_hbm.shape[0], functools.partial(chunk, ds_hbm, gs_hbm))


def _gather_expert_rows(dest_p3, dest_s3, yb):
    n_p, n_s = dest_p3.shape[0] * SC_ROWS, dest_s3.shape[0] * SC_ROWS
    return pl.kernel(
        _sc_gather_body,
        out_type=(jax.ShapeDtypeStruct((TOP_K, n_p, D_PACK), U32),
                  jax.ShapeDtypeStruct((TOP_K, n_s, D_PACK), U32)),
        mesh=_sc_mesh(),
        scratch_types=[pltpu.VMEM((TOP_K, SC_ROWS), jnp.int32), pltpu.VMEM((SC_ROWS, D_PACK), U32),
                       pltpu.SemaphoreType.DMA],
        name="moe_gather_sc",
    )(dest_p3, dest_s3, yb)


def _combine_kernel(g_ref, gate_ref, x1_ref, o_ref):
    gate = gate_ref[...]
    x1 = x1_ref[...]
    acc_lo, acc_hi = x1[:, :D_PACK], x1[:, D_PACK:]
    for k in range(TOP_K):
        lo, hi = _unpack_rows(g_ref[k])
        acc_lo = acc_lo + gate[:, k:k + 1] * lo
        acc_hi = acc_hi + gate[:, k:k + 1] * hi
    o_ref[:, :D_PACK] = acc_lo
    o_ref[:, D_PACK:] = acc_hi


def _combine(g, gates, x1):
    n_tok = x1.shape[0]
    tm = min(512, n_tok)
    return pl.pallas_call(
        _combine_kernel,
        grid=(n_tok // tm,),
        in_specs=[pl.BlockSpec((TOP_K, tm, D_PACK), lambda i: (0, i, 0)),
                  pl.BlockSpec((tm, LANES), lambda i: (i, 0)),
                  pl.BlockSpec((tm, D_MODEL), lambda i: (i, 0))],
        out_specs=pl.BlockSpec((tm, D_MODEL), lambda i: (i, 0)),
        out_shape=jax.ShapeDtypeStruct((n_tok, D_MODEL), F32),
        compiler_params=_cparams(("parallel",), VMEM_LIMIT),
        name="moe_combine",
    )(g, gates, x1)


def _route_tables(counts_f, n_assign):
    counts = counts_f[0, :N_EXPERTS].astype(jnp.int32)
    padded = (counts + MOE_ROWS - 1) // MOE_ROWS * MOE_ROWS
    pad_end = jnp.cumsum(padded)
    pad_start = pad_end - padded
    n_blocks = -(-(n_assign + N_EXPERTS * (MOE_ROWS - 1)) // MOE_ROWS)
    row0 = jnp.arange(n_blocks, dtype=jnp.int32) * MOE_ROWS
    block_e = jnp.minimum(jnp.sum(row0[:, None] >= pad_end[None, :], axis=1), N_EXPERTS - 1).astype(jnp.int32)
    n_valid = jnp.clip((pad_start + counts)[block_e] - row0, 0, MOE_ROWS).astype(jnp.int32)
    nb_used = (pad_end[-1] // MOE_ROWS).astype(jnp.int32).reshape(1)
    e_ids = jnp.arange(N_EXPERTS, dtype=jnp.int32)
    nonempty = counts > 0
    slot_e = (jnp.cumsum(nonempty.astype(jnp.int32)) - 1) % 2
    later = jnp.where(nonempty[None, :] & (e_ids[None, :] > e_ids[:, None]), e_ids[None, :], N_EXPERTS)
    next_e = jnp.min(later, axis=1)
    next_e = jnp.where(next_e == N_EXPERTS, -1, next_e)
    first = (row0 == pad_start[block_e]).astype(jnp.int32)
    tables = (block_e, nb_used, n_valid, first, slot_e[block_e].astype(jnp.int32),
              next_e[block_e].astype(jnp.int32))
    return pad_start, tables, n_blocks


def _dest(route, pad_start):
    idx, rank = route[:, :TOP_K], route[:, TOP_K:2 * TOP_K]
    onehot = idx[:, :, None] == jnp.arange(N_EXPERTS, dtype=jnp.int32)[None, None, :]
    dest = rank + jnp.sum(jnp.where(onehot, pad_start[None, None, :], 0), axis=-1)
    return dest.reshape(-1, SC_ROWS, TOP_K).transpose(0, 2, 1)


def _rope_tables(offset, length):
    pos = offset + jnp.arange(length, dtype=F32)
    inv = ROPE_THETA ** (-jnp.arange(QK_ROPE // 2, dtype=F32) * (2.0 / QK_ROPE))
    ang = pos[:, None] * inv[None, :]
    cos, sin = jnp.cos(ang), jnp.sin(ang)
    z32 = jnp.zeros((length, 32), F32)
    rot_c = jnp.concatenate([cos, cos], axis=1)
    rot_s = jnp.concatenate([-sin, sin], axis=1)
    ck = jnp.concatenate([rot_c, z32, rot_c, z32], axis=1)
    sk = jnp.concatenate([rot_s, z32, rot_s, z32], axis=1)
    cq = jnp.concatenate([jnp.ones((length, QK_NOPE), F32), rot_c, z32], axis=1)
    sq = jnp.concatenate([jnp.zeros((length, QK_NOPE), F32), rot_s, z32], axis=1)
    return ck, sk, cq, sq


def _head_gain(g_nope, g_rope):
    return jnp.concatenate([g_nope, g_rope, g_rope, jnp.zeros((HEAD_PAD - QK_HEAD,), F32)]).reshape(1, HEAD_PAD)


def _mixer_stage(x, past, prm, counts_in, bsz, length):
    offset = 0 if past is None else past[0].shape[1]
    ck, sk, cq, sq = _rope_tables(offset, length)
    x2 = x.reshape(bsz * length, D_MODEL)
    u_tm, qn, lat, k_rope, krot = _in_proj(x2, prm['g_mix'], prm['w_in_p'], prm['g_q_lat'], prm['g_kv_lat'],
                                           ck, sk, bsz, length)
    if past is None:
        h0 = jnp.zeros((bsz, S5_STATE_LANES), F32)
    else:
        h0 = _state_layout(past[2], past[3])
    s5n_tm, h_last = _s5(u_tm.reshape(length * bsz, S5_WIDTH), h0, prm['lam'], prm['wb'], prm['wc'],
                         prm['d_skip'], prm['w_glu'], prm['b_glu'], prm['g_s5_out'], bsz, length)
    h_re, h_im = _state_unlayout(h_last)
    q = _q_proj(qn, prm['w_uq_p'], prm['w_uq_sw'], cq, sq, prm['gq'], bsz, length)
    k, v = _kv_proj(lat, krot, prm['w_k'], prm['w_v'], prm['gk'], bsz, length)
    if past is None:
        attn = _attn_prompt(q, k, v, bsz, length)
    else:
        n_past = past[0].shape[1]
        c_lat = past[0].reshape(bsz * n_past, KV_LORA)
        c_rot = jnp.pad(past[1].reshape(bsz * n_past, QK_ROPE), ((0, 0), (QK_NOPE, HEAD_PAD - QK_HEAD)))
        kc, vc = _kv_proj(c_lat, c_rot, prm['w_k'], prm['w_v'], prm['gk'], bsz, n_past)
        attn = _attn_sample(q, kc, vc, k, v, bsz, length, n_past)
    x1, hn, route, gates, counts = _out_proj(x2, s5n_tm.reshape(length, bsz * S5_WIDTH),
                                             attn.reshape(bsz * length, MLA_WIDTH), prm['g_mla_out'],
                                             prm['w_out'], prm['g_ffn'], prm['w_r_hi'], prm['w_r_lo'], prm['b_r_p'],
                                             counts_in, bsz, length)
    return (x1, hn, route, gates, counts,
            lat.reshape(bsz, length, KV_LORA), k_rope.reshape(bsz, length, QK_ROPE), h_re, h_im)


def _prepare(g_mix, w_in, lam_re, lam_im, log_dt, b_s5_re, b_s5_im, c_s5_re, c_s5_im, d_s5, w_glu, b_glu,
             g_q_lat, w_uq, g_kv_lat, w_ukv, g_qn_nope, g_qn_rope, g_kn_nope, g_kn_rope, g_s5_out,
             g_mla_out, w_out, g_ffn, w_router, b_router):
    c0 = S5_WIDTH + Q_LORA + KV_LORA
    w_pe = w_in[:, c0:]
    z = jnp.zeros((D_MODEL, 32), F32)
    w_in_p = jnp.concatenate([w_in[:, :c0], w_pe, z, w_pe, z], axis=1).astype(BF16)
    ar, ai, bbr, bbi = _s5_prep(lam_re, lam_im, log_dt, b_s5_re, b_s5_im)
    wb, wc = _s5_weights(bbr, bbi, c_s5_re, c_s5_im)
    w_uq_p = jnp.pad(w_uq.reshape(Q_LORA, MLA_HEADS, QK_HEAD), ((0, 0), (0, 0), (0, HEAD_PAD - QK_HEAD)))
    r0, r1, r2 = QK_NOPE, QK_NOPE + QK_ROPE // 2, QK_HEAD
    w_uq_sw = jnp.zeros_like(w_uq_p).at[:, :, r0:r1].set(w_uq_p[:, :, r1:r2]).at[:, :, r1:r2].set(w_uq_p[:, :, r0:r1])
    w_kv = w_ukv.reshape(KV_LORA, MLA_HEADS, QK_NOPE + V_HEAD)
    w_k = jnp.pad(w_kv[:, :, :QK_NOPE], ((0, 0), (0, 0), (0, HEAD_PAD - QK_NOPE)))
    w_v = w_kv[:, :, QK_NOPE:]
    w_r_p = jnp.pad(w_router, ((0, 0), (0, LANES - N_EXPERTS)))
    w_r_hi = w_r_p.astype(BF16)
    return dict(
        g_mix=g_mix.reshape(1, -1), w_in_p=w_in_p,
        g_q_lat=g_q_lat.reshape(1, -1), g_kv_lat=g_kv_lat.reshape(1, -1),
        lam=_state_layout(ar, ai).reshape(1, S5_STATE_LANES), wb=wb, wc=wc,
        d_skip=d_s5.reshape(1, -1), w_glu=w_glu.astype(BF16), b_glu=b_glu.reshape(1, -1),
        g_s5_out=g_s5_out.reshape(1, -1),
        w_uq_p=w_uq_p.reshape(Q_LORA, MLA_HEADS * HEAD_PAD).astype(BF16),
        w_uq_sw=w_uq_sw.reshape(Q_LORA, MLA_HEADS * HEAD_PAD).astype(BF16),
        w_k=w_k.reshape(KV_LORA, MLA_HEADS * HEAD_PAD).astype(BF16),
        w_v=w_v.reshape(KV_LORA, MLA_WIDTH).astype(BF16),
        gq=_head_gain(g_qn_nope, g_qn_rope), gk=_head_gain(g_kn_nope, g_kn_rope),
        g_mla_out=g_mla_out.reshape(1, -1), w_out=w_out.astype(BF16), g_ffn=g_ffn.reshape(1, -1),
        w_r_hi=w_r_hi, w_r_lo=(w_r_p - w_r_hi.astype(F32)).astype(BF16),
        b_r_p=jnp.pad(b_router, (0, LANES - N_EXPERTS)).reshape(1, LANES),
    )


def _layer(xp, xs, cache_lat, cache_kr, st_re, st_im, mixer_w, w1, b1, w2, b2):
    prm = _prepare(*mixer_w)
    bp, lp, _ = xp.shape
    bs, ls, _ = xs.shape
    zero_counts = jnp.zeros((1, LANES), F32)
    x1p, hnp, routep, gatep, counts_p, latp, krp, hrp, hip = _mixer_stage(xp, None, prm, zero_counts, bp, lp)
    x1s, hns, routes, gates, counts, lats, krs, hrs, his = _mixer_stage(
        xs, (cache_lat, cache_kr, st_re, st_im), prm, counts_p, bs, ls)
    n_tok = bp * lp + bs * ls
    pad_start, tables, n_blocks = _route_tables(counts, n_tok * TOP_K)
    dest_p = _dest(routep, pad_start)
    dest_s = _dest(routes, pad_start)
    xs_sorted = _dispatch(dest_p, dest_s, hnp, hns, n_blocks)
    yb = _experts(tables, xs_sorted, w1, b1, w2, b2, n_blocks)
    g_p, g_s = _gather_expert_rows(dest_p, dest_s, yb)
    yp = _combine(g_p, gatep, x1p).reshape(bp, lp, D_MODEL)
    ys = _combine(g_s, gates, x1s).reshape(bs, ls, D_MODEL)
    return yp, ys, latp, krp, hrp, hip, lats, krs, hrs, his


def kernel(x_prompt, x_sample, cache_kv_latent, cache_k_rope, state_s5_re, state_s5_im, g_mix, w_in, lam_re,
           lam_im, log_dt, b_s5_re, b_s5_im, c_s5_re, c_s5_im, d_s5, w_glu, b_glu, g_q_lat, w_uq, g_kv_lat,
           w_ukv, g_qn_nope, g_qn_rope, g_kn_nope, g_kn_rope, g_s5_out, g_mla_out, w_out, g_ffn, w_router,
           b_router, w_mlp1, b_mlp1, w_mlp2, b_mlp2):
    depth = g_mix.shape[0]
    yp, ys = x_prompt, x_sample
    outs = [[] for _ in range(8)]
    for l in range(depth):
        mixer_w = (g_mix[l], w_in[l], lam_re[l], lam_im[l], log_dt[l], b_s5_re[l], b_s5_im[l], c_s5_re[l],
                   c_s5_im[l], d_s5[l], w_glu[l], b_glu[l], g_q_lat[l], w_uq[l], g_kv_lat[l], w_ukv[l],
                   g_qn_nope[l], g_qn_rope[l], g_kn_nope[l], g_kn_rope[l], g_s5_out[l], g_mla_out[l],
                   w_out[l], g_ffn[l], w_router[l], b_router[l])
        res = _layer(yp, ys, cache_kv_latent[l], cache_k_rope[l], state_s5_re[l], state_s5_im[l], mixer_w,
                     w_mlp1[l], b_mlp1[l], w_mlp2[l], b_mlp2[l])
        yp, ys = res[0], res[1]
        for o, r in zip(outs, res[2:]):
            o.append(r)
    return (yp, ys) + tuple(jnp.stack(o) for o in outs)
```

```python
import functools
import math

import jax
import jax.numpy as jnp
from jax import lax
from jax.experimental import pallas as pl
from jax.experimental.pallas import tpu as pltpu
from jax.experimental.pallas import tpu_sc as plsc

F32 = jnp.float32
BF16 = jnp.bfloat16
U32 = jnp.uint32

D_MODEL = 1024
S5_WIDTH = 512
S5_GROUP = 16
S5_GROUPS = 32
S5_STATE = 64
MLA_HEADS = 8
QK_NOPE = 64
QK_ROPE = 32
QK_HEAD = QK_NOPE + QK_ROPE
V_HEAD = 64
MLA_WIDTH = MLA_HEADS * V_HEAD
Q_LORA = 384
KV_LORA = 256
ROPE_THETA = 10000.0
CHUNK = 64
N_EXPERTS = 32
TOP_K = 4
D_FF = D_MODEL
SWIGLU_LIMIT = 7.0
SWIGLU_ALPHA = 1.702
EPS = 1e-6

LANES = 128
HEAD_PAD = 128
D_IN_PAD = 1280
S5_CHUNKS = 4
S5_CHUNK_LANES = 1024
S5_STATE_LANES = S5_CHUNKS * S5_CHUNK_LANES
ROW_TILE = 1024
SUB_TILE = 512
MOE_ROWS = 512
MOE_COL_CHUNKS = 2
MOE_ROW_CHUNKS = 1
VMEM_LIMIT = 56 * 1024 * 1024
SC_CORES = 2
SC_SUBCORES = 16
SC_WORKERS = SC_CORES * SC_SUBCORES
SC_ROWS = 128
SC_GATHER_ROWS = 64
D_PACK = D_MODEL // 2


def _cparams(sem, vmem=None):
    return pltpu.CompilerParams(dimension_semantics=sem, vmem_limit_bytes=vmem)


def _rms(x, g, n=None):
    n = x.shape[-1] if n is None else n
    ms = jnp.sum(x * x, axis=-1, keepdims=True) * (1.0 / n)
    return x * lax.rsqrt(ms + EPS) * g


def _sub_tiles(rows):
    sub = min(SUB_TILE, rows)
    return [slice(r, r + sub) for r in range(0, rows, sub)]


def _pack_rows(x):
    lo = lax.bitcast_convert_type(x[:, :D_PACK].astype(BF16).astype(F32), U32)
    hi = lax.bitcast_convert_type(x[:, D_PACK:].astype(BF16).astype(F32), U32)
    return (lo >> 16) | (hi & jnp.uint32(0xFFFF0000))


def _unpack_rows(w):
    lo = lax.bitcast_convert_type(w << 16, F32)
    hi = lax.bitcast_convert_type(w & jnp.uint32(0xFFFF0000), F32)
    return lo, hi


def _prep_kernel(lr_ref, li_ref, ldt_ref, br_ref, bi_ref, ar_ref, ai_ref, bbr_ref, bbi_ref):
    lr = lr_ref[...]
    li = li_ref[...]
    dt = jnp.exp(ldt_ref[...])
    mag = jnp.exp(lr * dt)
    ar = mag * jnp.cos(li * dt)
    ai = mag * jnp.sin(li * dt)
    ar_ref[...] = ar
    ai_ref[...] = ai
    den = lr * lr + li * li
    cr = ((ar - 1.0) * lr + ai * li) / den
    ci = (ai * lr - (ar - 1.0) * li) / den
    br = br_ref[...]
    bi = bi_ref[...]
    bbr_ref[...] = cr[:, None, :] * br - ci[:, None, :] * bi
    bbi_ref[...] = cr[:, None, :] * bi + ci[:, None, :] * br


def _s5_prep(lam_re, lam_im, log_dt, b_re, b_im):
    g, n = lam_re.shape
    p = b_re.shape[-1]
    out = pl.pallas_call(
        _prep_kernel,
        out_shape=(jax.ShapeDtypeStruct((g, n), F32), jax.ShapeDtypeStruct((g, n), F32),
                   jax.ShapeDtypeStruct((g, p, n), F32), jax.ShapeDtypeStruct((g, p, n), F32)),
        name="s5_prep",
    )(lam_re, lam_im, log_dt.reshape(g, 1), jnp.swapaxes(b_re, 1, 2), jnp.swapaxes(b_im, 1, 2))
    return out


def _state_layout(re, im):
    lead = re.shape[:-2]
    re = re.reshape(lead + (S5_CHUNKS, 512))
    im = im.reshape(lead + (S5_CHUNKS, 512))
    return jnp.stack([re, im], axis=-2).reshape(lead + (S5_STATE_LANES,))


def _state_unlayout(h):
    lead = h.shape[:-1]
    h = h.reshape(lead + (S5_CHUNKS, 2, 512))
    re = h[..., 0, :].reshape(lead + (S5_GROUPS, S5_STATE))
    im = h[..., 1, :].reshape(lead + (S5_GROUPS, S5_STATE))
    return re, im


def _s5_weights(bbr, bbi, c_re, c_im):
    eye8 = jnp.eye(8, dtype=F32)

    def blockdiag(m):
        a, b = m.shape[1], m.shape[2]
        return (eye8[:, None, :, None] * m[:, :, None, :]).reshape(8 * a, 8 * b)

    wb, wc = [], []
    for c in range(S5_CHUNKS):
        sl = slice(8 * c, 8 * c + 8)
        wb.append(jnp.concatenate([blockdiag(bbr[sl]), blockdiag(bbi[sl])], axis=1))
        cr_t = jnp.swapaxes(c_re[sl], 1, 2)
        ci_t = jnp.swapaxes(c_im[sl], 1, 2)
        wc.append(jnp.concatenate([blockdiag(cr_t), -blockdiag(ci_t)], axis=0))
    return jnp.stack(wb).astype(BF16), jnp.stack(wc).astype(BF16)


def _in_proj_kernel(x_ref, gmix_ref, w_ref, gq_ref, gkv_ref, ck_ref, sk_ref,
                    u_ref, qn_ref, lat_ref, krope_ref, krot_ref):
    for rs in _sub_tiles(x_ref.shape[0]):
        xn = _rms(x_ref[rs, :], gmix_ref[...]).astype(BF16)
        z = jnp.dot(xn, w_ref[...], preferred_element_type=F32)
        u_ref[rs, :] = z[:, :S5_WIDTH]
        qn_ref[rs, :] = _rms(z[:, S5_WIDTH:S5_WIDTH + Q_LORA], gq_ref[...]).astype(BF16)
        c0 = S5_WIDTH + Q_LORA
        lat_ref[rs, :] = _rms(z[:, c0:c0 + KV_LORA], gkv_ref[...])
        kp = z[:, c0 + KV_LORA:]
        lane = lax.broadcasted_iota(jnp.int32, kp.shape, 1)
        first_half = (lane % 64) < 16
        sw = jnp.where(first_half, pltpu.roll(kp, LANES - 16, axis=1), pltpu.roll(kp, 16, axis=1))
        kr = kp * ck_ref[rs, :] + sw * sk_ref[rs, :]
        krope_ref[rs, :] = kr[:, :QK_ROPE]
        krot_ref[rs, :] = jnp.where(lane >= 64, kr, 0.0)


def _in_proj(x, g_mix, w_in_p, g_q_lat, g_kv_lat, ck, sk, bsz, length):
    t = bsz * length
    tm = min(ROW_TILE, length)
    n_l = length // tm
    row = lambda i: (i, 0)
    tab = lambda i: (i % n_l, 0)
    full = lambda i: (0, 0)
    return pl.pallas_call(
        _in_proj_kernel,
        grid=(t // tm,),
        in_specs=[pl.BlockSpec((tm, D_MODEL), row),
                  pl.BlockSpec((1, D_MODEL), full),
                  pl.BlockSpec((D_MODEL, D_IN_PAD), full),
                  pl.BlockSpec((1, Q_LORA), full),
                  pl.BlockSpec((1, KV_LORA), full),
                  pl.BlockSpec((tm, LANES), tab),
                  pl.BlockSpec((tm, LANES), tab)],
        out_specs=[pl.BlockSpec((tm, S5_WIDTH), lambda i: (i % n_l, i // n_l)),
                   pl.BlockSpec((tm, Q_LORA), row),
                   pl.BlockSpec((tm, KV_LORA), row),
                   pl.BlockSpec((tm, QK_ROPE), row),
                   pl.BlockSpec((tm, LANES), row)],
        out_shape=(jax.ShapeDtypeStruct((length, bsz * S5_WIDTH), F32),
                   jax.ShapeDtypeStruct((t, Q_LORA), BF16),
                   jax.ShapeDtypeStruct((t, KV_LORA), F32),
                   jax.ShapeDtypeStruct((t, QK_ROPE), F32),
                   jax.ShapeDtypeStruct((t, LANES), F32)),
        compiler_params=_cparams(("parallel",), VMEM_LIMIT),
        name="in_proj",
    )(x, g_mix, w_in_p, g_q_lat, g_kv_lat, ck, sk)


def _s5_kernel(u_ref, h0_ref, lam_ref, wb_ref, wc_ref, dskip_ref, wglu_ref, bglu_ref, gout_ref,
               y_ref, hlast_ref, bu_ref, h_ref, *, bsz, steps):
    c_id = pl.program_id(0)

    @pl.when(c_id == 0)
    def _():
        h_ref[...] = h0_ref[...]

    u = u_ref[...]
    ub = u.astype(BF16)
    for c in range(S5_CHUNKS):
        bu_ref[:, c * S5_CHUNK_LANES:(c + 1) * S5_CHUNK_LANES] = jnp.dot(
            ub[:, c * LANES:(c + 1) * LANES], wb_ref[c], preferred_element_type=F32)

    for c in range(S5_CHUNKS):
        re = slice(c * S5_CHUNK_LANES, c * S5_CHUNK_LANES + 512)
        im = slice(c * S5_CHUNK_LANES + 512, (c + 1) * S5_CHUNK_LANES)
        lam_r = jnp.broadcast_to(lam_ref[:, re], (bsz, 512))
        lam_i = jnp.broadcast_to(lam_ref[:, im], (bsz, 512))

        hr, hi = h_ref[:, re], h_ref[:, im]
        for t in range(steps):
            rows = slice(t * bsz, (t + 1) * bsz)
            hr, hi = (lam_r * hr - lam_i * hi + bu_ref[rows, re],
                      lam_r * hi + lam_i * hr + bu_ref[rows, im])
            bu_ref[rows, re] = hr
            bu_ref[rows, im] = hi
        h_ref[:, re] = hr
        h_ref[:, im] = hi

    ys = []
    for c in range(S5_CHUNKS):
        hs = bu_ref[:, c * S5_CHUNK_LANES:(c + 1) * S5_CHUNK_LANES].astype(BF16)
        ys.append(jnp.dot(hs, wc_ref[c], preferred_element_type=F32))
    y = jnp.concatenate(ys, axis=1) + dskip_ref[...] * u
    y = jax.nn.gelu(y)
    gate = jnp.dot(y.astype(BF16), wglu_ref[...], preferred_element_type=F32) + bglu_ref[...]
    y = y * jax.nn.sigmoid(gate)
    y_ref[...] = _rms(y, gout_ref[...])

    @pl.when(c_id == pl.num_programs(0) - 1)
    def _():
        hlast_ref[...] = h_ref[...]


def _s5(u_tm, h0, lam, wb, wc, d_skip, w_glu, b_glu, g_out, bsz, length):
    steps = min(32, length)
    rows = steps * bsz
    full2 = lambda c: (0, 0)
    full3 = lambda c: (0, 0, 0)
    return pl.pallas_call(
        functools.partial(_s5_kernel, bsz=bsz, steps=steps),
        grid=(length // steps,),
        in_specs=[pl.BlockSpec((rows, S5_WIDTH), lambda c: (c, 0)),
                  pl.BlockSpec((bsz, S5_STATE_LANES), full2),
                  pl.BlockSpec((1, S5_STATE_LANES), full2),
                  pl.BlockSpec((S5_CHUNKS, LANES, S5_CHUNK_LANES), full3),
                  pl.BlockSpec((S5_CHUNKS, S5_CHUNK_LANES, LANES), full3),
                  pl.BlockSpec((1, S5_WIDTH), full2),
                  pl.BlockSpec((S5_WIDTH, S5_WIDTH), full2),
                  pl.BlockSpec((1, S5_WIDTH), full2),
                  pl.BlockSpec((1, S5_WIDTH), full2)],
        out_specs=[pl.BlockSpec((rows, S5_WIDTH), lambda c: (c, 0)),
                   pl.BlockSpec((bsz, S5_STATE_LANES), full2)],
        out_shape=(jax.ShapeDtypeStruct((length * bsz, S5_WIDTH), F32),
                   jax.ShapeDtypeStruct((bsz, S5_STATE_LANES), F32)),
        scratch_shapes=[pltpu.VMEM((rows, S5_STATE_LANES), F32),
                        pltpu.VMEM((bsz, S5_STATE_LANES), F32)],
        compiler_params=_cparams(("arbitrary",), VMEM_LIMIT),
        name="s5_mixer",
    )(u_tm, h0, lam, wb, wc, d_skip, w_glu, b_glu, g_out)


def _q_proj_kernel(qn_ref, w_ref, wsw_ref, ones_ref, cq_ref, sq_ref, g_ref, q_ref):
    g = g_ref[...] * (QK_HEAD ** -0.5)
    for rs in _sub_tiles(qn_ref.shape[0]):
        qn = qn_ref[rs, :]
        qf = jnp.dot(qn, w_ref[...], preferred_element_type=F32)
        qs = jnp.dot(qn, wsw_ref[...], preferred_element_type=F32)
        cq, sq = cq_ref[rs, :], sq_ref[rs, :]
        for p in range(MLA_HEADS // 2):
            cols = slice(2 * p * HEAD_PAD, 2 * (p + 1) * HEAD_PAD)
            xr = qf[:, cols] * cq + qs[:, cols] * sq
            sqr = xr * xr
            hi = sqr.astype(BF16)
            lo = (sqr - hi.astype(F32)).astype(BF16)
            ss = (jnp.dot(hi, ones_ref[...], preferred_element_type=F32)
                  + jnp.dot(lo, ones_ref[...], preferred_element_type=F32))
            q = (xr * lax.rsqrt(ss * (1.0 / QK_HEAD) + EPS) * g).astype(BF16)
            q_ref[0, 2 * p, rs, :] = q[:, :HEAD_PAD]
            q_ref[0, 2 * p + 1, rs, :] = q[:, HEAD_PAD:]


def _q_proj(qn, w_uq_p, w_uq_sw, cq, sq, gq, bsz, length):
    tm = min(ROW_TILE, length)
    n_l = length // tm
    full = lambda b, i: (0, 0)
    pair = 2 * HEAD_PAD
    two = lambda a: jnp.concatenate([a, a], axis=1)
    ones2 = (jnp.arange(pair)[:, None] // HEAD_PAD == jnp.arange(pair)[None, :] // HEAD_PAD).astype(BF16)
    return pl.pallas_call(
        _q_proj_kernel,
        grid=(bsz, n_l),
        in_specs=[pl.BlockSpec((tm, Q_LORA), lambda b, i: (b * n_l + i, 0)),
                  pl.BlockSpec((Q_LORA, MLA_HEADS * HEAD_PAD), full),
                  pl.BlockSpec((Q_LORA, MLA_HEADS * HEAD_PAD), full),
                  pl.BlockSpec((pair, pair), full),
                  pl.BlockSpec((tm, pair), lambda b, i: (i, 0)),
                  pl.BlockSpec((tm, pair), lambda b, i: (i, 0)),
                  pl.BlockSpec((1, pair), full)],
        out_specs=pl.BlockSpec((1, MLA_HEADS, tm, HEAD_PAD), lambda b, i: (b, 0, i, 0)),
        out_shape=jax.ShapeDtypeStruct((bsz, MLA_HEADS, length, HEAD_PAD), BF16),
        compiler_params=_cparams(("parallel", "parallel"), VMEM_LIMIT),
        name="q_proj",
    )(qn, w_uq_p, w_uq_sw, ones2, two(cq), two(sq), two(gq))


def _kv_proj_kernel(lat_ref, krot_ref, wk_ref, wv_ref, g_ref, k_ref, v_ref):
    g = g_ref[...]
    for rs in _sub_tiles(lat_ref.shape[0]):
        lat = lat_ref[rs, :].astype(BF16)
        kf = jnp.dot(lat, wk_ref[...], preferred_element_type=F32)
        vf = jnp.dot(lat, wv_ref[...], preferred_element_type=F32)
        krot = krot_ref[rs, :]
        for h in range(MLA_HEADS):
            k = kf[:, h * HEAD_PAD:(h + 1) * HEAD_PAD] + krot
            k_ref[0, h, rs, :] = _rms(k, g, QK_HEAD).astype(BF16)
        for p in range(MLA_HEADS // 2):
            v_ref[0, p, rs, :] = vf[:, p * LANES:(p + 1) * LANES].astype(BF16)


def _kv_proj(lat, krot, w_k, w_v, gk, bsz, length):
    tm = min(ROW_TILE, length)
    n_l = length // tm
    full = lambda b, i: (0, 0)
    row = lambda b, i: (b * n_l + i, 0)
    return pl.pallas_call(
        _kv_proj_kernel,
        grid=(bsz, n_l),
        in_specs=[pl.BlockSpec((tm, KV_LORA), row),
                  pl.BlockSpec((tm, HEAD_PAD), row),
                  pl.BlockSpec((KV_LORA, MLA_HEADS * HEAD_PAD), full),
                  pl.BlockSpec((KV_LORA, MLA_WIDTH), full),
                  pl.BlockSpec((1, HEAD_PAD), full)],
        out_specs=[pl.BlockSpec((1, MLA_HEADS, tm, HEAD_PAD), lambda b, i: (b, 0, i, 0)),
                   pl.BlockSpec((1, MLA_HEADS // 2, tm, LANES), lambda b, i: (b, 0, i, 0))],
        out_shape=(jax.ShapeDtypeStruct((bsz, MLA_HEADS, length, HEAD_PAD), BF16),
                   jax.ShapeDtypeStruct((bsz, MLA_HEADS // 2, length, LANES), BF16)),
        compiler_params=_cparams(("parallel", "parallel"), VMEM_LIMIT),
        name="kv_proj",
    )(lat, krot, w_k, w_v, gk)


def _scores(q, k):
    return lax.dot_general(q, k, (((1,), (1,)), ((), ())), preferred_element_type=F32)


def _attn_prompt_kernel(q_ref, k_ref, v_ref, o_ref, *, length, tq):
    n_q = length // tq
    row = lax.broadcasted_iota(jnp.int32, (tq, tq), 0)
    col = lax.broadcasted_iota(jnp.int32, (tq, tq), 1)
    visible = (col // CHUNK) <= (row // CHUNK)
    lane = lax.broadcasted_iota(jnp.int32, (tq, LANES), 1)
    for qi in range(n_q):
        q0 = qi * tq
        outs = []
        for hh in range(2):
            q = q_ref[0, hh, q0:q0 + tq, :]
            sd = jnp.where(visible, _scores(q, k_ref[0, hh, q0:q0 + tq, :]), -jnp.inf)
            m = jnp.max(sd, axis=-1, keepdims=True)
            if qi:
                so = _scores(q, k_ref[0, hh, 0:q0, :])
                m = jnp.maximum(m, jnp.max(so, axis=-1, keepdims=True))
            pd = jnp.exp(sd - m)
            l = jnp.sum(pd, axis=-1, keepdims=True)
            acc = jnp.dot(pd.astype(BF16), v_ref[0, 0, q0:q0 + tq, :], preferred_element_type=F32)
            if qi:
                po = jnp.exp(so - m)
                l = l + jnp.sum(po, axis=-1, keepdims=True)
                acc = acc + jnp.dot(po.astype(BF16), v_ref[0, 0, 0:q0, :], preferred_element_type=F32)
            outs.append(acc / l)
        o_ref[0, q0:q0 + tq, :] = jnp.where(lane < V_HEAD, outs[0], outs[1])


def _attn_prompt(q, k, v, bsz, length):
    tq = min(512, length)
    pairs = MLA_HEADS // 2
    return pl.pallas_call(
        functools.partial(_attn_prompt_kernel, length=length, tq=tq),
        grid=(bsz, pairs),
        in_specs=[pl.BlockSpec((1, 2, length, HEAD_PAD), lambda b, p: (b, p, 0, 0)),
                  pl.BlockSpec((1, 2, length, HEAD_PAD), lambda b, p: (b, p, 0, 0)),
                  pl.BlockSpec((1, 1, length, LANES), lambda b, p: (b, p, 0, 0))],
        out_specs=pl.BlockSpec((1, length, LANES), lambda b, p: (b, 0, p)),
        out_shape=jax.ShapeDtypeStruct((bsz, length, MLA_WIDTH), F32),
        compiler_params=_cparams(("parallel", "parallel"), VMEM_LIMIT),
        name="attn_prompt",
    )(q, k, v)


def _attn_sample_kernel(q_ref, kc_ref, vc_ref, kn_ref, vn_ref, o_ref):
    lq = q_ref.shape[2]
    lane = lax.broadcasted_iota(jnp.int32, (lq, LANES), 1)
    outs = []
    for hh in range(2):
        q = q_ref[0, hh]
        sc = _scores(q, kc_ref[0, hh])
        sn = _scores(q, kn_ref[0, hh])
        m = jnp.maximum(jnp.max(sc, axis=-1, keepdims=True), jnp.max(sn, axis=-1, keepdims=True))
        pc = jnp.exp(sc - m)
        pn = jnp.exp(sn - m)
        l = jnp.sum(pc, axis=-1, keepdims=True) + jnp.sum(pn, axis=-1, keepdims=True)
        acc = (jnp.dot(pc.astype(BF16), vc_ref[0, 0], preferred_element_type=F32)
               + jnp.dot(pn.astype(BF16), vn_ref[0, 0], preferred_element_type=F32))
        outs.append(acc / l)
    o_ref[0] = jnp.where(lane < V_HEAD, outs[0], outs[1])


def _attn_sample(q, kc, vc, kn, vn, bsz, lq, past):
    pairs = MLA_HEADS // 2
    hp = lambda b, p: (b, p, 0, 0)
    return pl.pallas_call(
        _attn_sample_kernel,
        grid=(bsz, pairs),
        in_specs=[pl.BlockSpec((1, 2, lq, HEAD_PAD), hp),
                  pl.BlockSpec((1, 2, past, HEAD_PAD), hp),
                  pl.BlockSpec((1, 1, past, LANES), hp),
                  pl.BlockSpec((1, 2, lq, HEAD_PAD), hp),
                  pl.BlockSpec((1, 1, lq, LANES), hp)],
        out_specs=pl.BlockSpec((1, lq, LANES), lambda b, p: (b, 0, p)),
        out_shape=jax.ShapeDtypeStruct((bsz, lq, MLA_WIDTH), F32),
        compiler_params=_cparams(("parallel", "parallel"), VMEM_LIMIT),
        name="attn_sample",
    )(q, kc, vc, kn, vn)


def _out_proj_kernel(x_ref, s5_ref, at_ref, gmla_ref, wout_ref, gffn_ref, wrh_ref, wrl_ref, br_ref, tri_ref,
                     cin_ref,
                     x1_ref, hn_ref, route_ref, gate_ref, cout_ref, run_ref):
    @pl.when(pl.program_id(0) == 0)
    def _():
        run_ref[...] = cin_ref[...]

    for rs in _sub_tiles(x_ref.shape[0]):
        an = _rms(at_ref[rs, :], gmla_ref[...])
        merged = jnp.concatenate([s5_ref[rs, :].astype(BF16), an.astype(BF16)], axis=1)
        x1 = x_ref[rs, :] + jnp.dot(merged, wout_ref[...], preferred_element_type=F32)
        x1_ref[rs, :] = x1
        hn = _rms(x1, gffn_ref[...])
        hn_ref[rs, :] = _pack_rows(hn)
        hn_hi = hn.astype(BF16)
        hn_lo = (hn - hn_hi.astype(F32)).astype(BF16)
        logits = (jnp.dot(hn_hi, wrh_ref[...], preferred_element_type=F32)
                  + jnp.dot(hn_lo, wrh_ref[...], preferred_element_type=F32)
                  + jnp.dot(hn_hi, wrl_ref[...], preferred_element_type=F32) + br_ref[...])
        lane = lax.broadcasted_iota(jnp.int32, logits.shape, 1)
        lane_f = lane.astype(F32)
        work = jnp.where(lane < N_EXPERTS, logits, -jnp.inf)
        vals, idxs = [], []
        for _ in range(TOP_K):
            m = jnp.max(work, axis=-1, keepdims=True)
            i = jnp.min(jnp.where(work == m, lane_f, float(LANES)), axis=-1, keepdims=True)
            vals.append(m)
            idxs.append(i)
            work = jnp.where(lane_f == i, -jnp.inf, work)
        es = [jnp.exp(v - vals[0]) for v in vals]
        den = es[0] + es[1] + es[2] + es[3]
        onehots = [(lane_f == idxs[k]).astype(F32) for k in range(TOP_K)]
        e_all = onehots[0] + onehots[1] + onehots[2] + onehots[3]
        before = jnp.dot(tri_ref[...], e_all.astype(BF16), preferred_element_type=F32) + run_ref[...]
        run_ref[...] = run_ref[...] + jnp.sum(e_all, axis=0, keepdims=True)
        route = jnp.zeros(logits.shape, F32)
        gate_out = jnp.zeros(logits.shape, F32)
        for k in range(TOP_K):
            rank = jnp.sum(onehots[k] * before, axis=-1, keepdims=True)
            route = jnp.where(lane == k, idxs[k], route)
            route = jnp.where(lane == TOP_K + k, rank, route)
            gate_out = jnp.where(lane == k, es[k] / den, gate_out)
        route_ref[rs, :] = route.astype(jnp.int32)
        gate_ref[rs, :] = gate_out
    cout_ref[...] = run_ref[...]


def _out_proj(x, s5n_tm, attn, g_mla, w_out, g_ffn, w_r_hi, w_r_lo, b_r_p, counts_in, bsz, length):
    t = bsz * length
    tm = min(ROW_TILE, length)
    n_l = length // tm
    row = lambda i: (i, 0)
    full = lambda i: (0, 0)
    sub = min(SUB_TILE, tm)
    tri = (jnp.arange(sub)[:, None] > jnp.arange(sub)[None, :]).astype(BF16)
    return pl.pallas_call(
        _out_proj_kernel,
        grid=(t // tm,),
        in_specs=[pl.BlockSpec((tm, D_MODEL), row),
                  pl.BlockSpec((tm, S5_WIDTH), lambda i: (i % n_l, i // n_l)),
                  pl.BlockSpec((tm, MLA_WIDTH), row),
                  pl.BlockSpec((1, MLA_WIDTH), full),
                  pl.BlockSpec((D_MODEL, D_MODEL), full),
                  pl.BlockSpec((1, D_MODEL), full),
                  pl.BlockSpec((D_MODEL, LANES), full),
                  pl.BlockSpec((D_MODEL, LANES), full),
                  pl.BlockSpec((1, LANES), full),
                  pl.BlockSpec((sub, sub), full),
                  pl.BlockSpec((1, LANES), full)],
        out_specs=[pl.BlockSpec((tm, D_MODEL), row),
                   pl.BlockSpec((tm, D_PACK), row),
                   pl.BlockSpec((tm, LANES), row),
                   pl.BlockSpec((tm, LANES), row),
                   pl.BlockSpec((1, LANES), full)],
        out_shape=(jax.ShapeDtypeStruct((t, D_MODEL), F32),
                   jax.ShapeDtypeStruct((t, D_PACK), U32),
                   jax.ShapeDtypeStruct((t, LANES), jnp.int32),
                   jax.ShapeDtypeStruct((t, LANES), F32),
                   jax.ShapeDtypeStruct((1, LANES), F32)),
        scratch_shapes=[pltpu.VMEM((1, LANES), F32)],
        compiler_params=_cparams(("arbitrary",), VMEM_LIMIT),
        name="out_proj",
    )(x, s5n_tm, attn, g_mla, w_out, g_ffn, w_r_hi, w_r_lo, b_r_p, tri, counts_in)


def _sc_mesh():
    return plsc.VectorSubcoreMesh(core_axis_name="c", subcore_axis_name="s",
                                  num_cores=SC_CORES, num_subcores=SC_SUBCORES)


def _sc_for_chunks(n_chunks, fn):
    wid = lax.axis_index("s") * SC_CORES + lax.axis_index("c")
    full, rem = divmod(n_chunks, SC_WORKERS)
    if full:
        @pl.loop(0, full)
        def _(j):
            fn(j * SC_WORKERS + wid)
    if rem:
        @pl.when(wid < rem)
        def _():
            fn(full * SC_WORKERS + wid)


def _chunk_rows(c):
    return pl.ds(pl.multiple_of(c * SC_ROWS, SC_ROWS), SC_ROWS)


def _sc_dispatch_body(dp_hbm, ds_hbm, hp_hbm, hs_hbm, xs_hbm, idx_v, rows_v, sem):
    def chunk(d_hbm, h_hbm, c):
        pltpu.sync_copy(d_hbm.at[c], idx_v)
        pltpu.sync_copy(h_hbm.at[_chunk_rows(c)], rows_v)
        copies = [pltpu.async_copy(rows_v, xs_hbm.at[idx_v.at[k]], sem) for k in range(TOP_K)]
        for cp in copies:
            cp.wait()

    _sc_for_chunks(dp_hbm.shape[0], functools.partial(chunk, dp_hbm, hp_hbm))
    _sc_for_chunks(ds_hbm.shape[0], functools.partial(chunk, ds_hbm, hs_hbm))


def _dispatch(dest_p3, dest_s3, hn_p, hn_s, n_blocks):
    return pl.kernel(
        _sc_dispatch_body,
        out_type=jax.ShapeDtypeStruct((n_blocks * MOE_ROWS, D_PACK), U32),
        mesh=_sc_mesh(),
        scratch_types=[pltpu.VMEM((TOP_K, SC_ROWS), jnp.int32), pltpu.VMEM((SC_ROWS, D_PACK), U32),
                       pltpu.SemaphoreType.DMA],
        name="moe_dispatch_sc",
    )(dest_p3, dest_s3, hn_p, hn_s)


def _expert_kernel(be_ref, nb_ref, nv_ref, first_ref, slot_ref, nxt_ref,
                   x_ref, w1_hbm, b1_ref, w2_hbm, b2_ref, y_ref, w1f_ref, w2f_ref, w1b_ref, w2b_ref, sem):
    b = pl.program_id(0)

    def weight_copies(e, s):
        return (pltpu.make_async_copy(w1_hbm.at[e], w1f_ref.at[s], sem.at[s]),
                pltpu.make_async_copy(w2_hbm.at[e], w2f_ref.at[s], sem.at[s]))

    @pl.when(jnp.logical_and(b == 0, nb_ref[0] > 0))
    def _():
        for cp in weight_copies(be_ref[0], 0):
            cp.start()

    @pl.when(jnp.logical_and(b < nb_ref[0], first_ref[b] == 1))
    def _():
        s = slot_ref[b]
        for cp in weight_copies(be_ref[b], s):
            cp.wait()
        w1b_ref[...] = w1f_ref[s].astype(BF16)
        w2b_ref[...] = w2f_ref[s].astype(BF16)

        @pl.when(nxt_ref[b] >= 0)
        def _():
            for cp in weight_copies(nxt_ref[b], 1 - s):
                cp.start()

    @pl.when(b < nb_ref[0])
    def _():
        b1 = b1_ref[0]
        cw = D_FF // MOE_COL_CHUNKS
        sub = MOE_ROWS // MOE_ROW_CHUNKS
        for r0 in range(0, MOE_ROWS, sub):
            rs = slice(r0, r0 + sub)
            live = lax.broadcasted_iota(jnp.int32, (sub, D_PACK), 0) < nv_ref[b] - r0
            lo, hi = _unpack_rows(jnp.where(live, x_ref[rs, :], jnp.uint32(0)))
            lo, hi = lo.astype(BF16), hi.astype(BF16)

            def up(c0, lo=lo, hi=hi):
                return (jnp.dot(lo, w1b_ref[:D_PACK, c0:c0 + cw], preferred_element_type=F32)
                        + jnp.dot(hi, w1b_ref[D_PACK:, c0:c0 + cw], preferred_element_type=F32)
                        + b1[:, c0:c0 + cw])

            y = b2_ref[0]
            for j in range(MOE_COL_CHUNKS):
                gate = jnp.minimum(up(j * cw), SWIGLU_LIMIT)
                lin = jnp.clip(up(D_FF + j * cw), -SWIGLU_LIMIT, SWIGLU_LIMIT)
                act = gate * jax.nn.sigmoid(SWIGLU_ALPHA * gate) * (lin + 1.0)
                y = y + jnp.dot(act.astype(BF16), w2b_ref[j * cw:(j + 1) * cw, :], preferred_element_type=F32)
            y_ref[rs, :] = _pack_rows(y)

    @pl.when(b >= nb_ref[0])
    def _():
        y_ref[...] = jnp.zeros(y_ref.shape, y_ref.dtype)


def _experts(tables, xs, w1, b1, w2, b2, n_blocks):
    last = lambda b, be, nb, *_: (jnp.maximum(jnp.minimum(b, nb[0] - 1), 0), 0)
    bsel = lambda b, be, *_: (be[b], 0, 0)
    return pl.pallas_call(
        _expert_kernel,
        grid_spec=pltpu.PrefetchScalarGridSpec(
            num_scalar_prefetch=6,
            grid=(n_blocks,),
            in_specs=[pl.BlockSpec((MOE_ROWS, D_PACK), last),
                      pl.BlockSpec(memory_space=pl.ANY),
                      pl.BlockSpec((1, 1, 2 * D_FF), bsel),
                      pl.BlockSpec(memory_space=pl.ANY),
                      pl.BlockSpec((1, 1, D_MODEL), bsel)],
            out_specs=pl.BlockSpec((MOE_ROWS, D_PACK), lambda b, *_: (b, 0)),
            scratch_shapes=[pltpu.VMEM((2, D_MODEL, 2 * D_FF), F32), pltpu.VMEM((2, D_FF, D_MODEL), F32),
                            pltpu.VMEM((D_MODEL, 2 * D_FF), BF16), pltpu.VMEM((D_FF, D_MODEL), BF16),
                            pltpu.SemaphoreType.DMA((2,))]),
        out_shape=jax.ShapeDtypeStruct((n_blocks * MOE_ROWS, D_PACK), U32),
        compiler_params=_cparams(("arbitrary",), VMEM_LIMIT),
        name="moe_experts",
    )(*tables, xs, w1, b1.reshape(N_EXPERTS, 1, 2 * D_FF), w2, b2.reshape(N_EXPERTS, 1, D_MODEL))


def _sc_gather_body(dp_hbm, ds_hbm, yb_hbm, gp_hbm, gs_hbm, idx_v, rows_v, sems):
    n = SC_GATHER_ROWS

    def chunk(d_hbm, g_hbm, c):
        pltpu.sync_copy(d_hbm.at[c], idx_v)
        rows = pl.ds(pl.multiple_of(c * n, n), n)
        gather = lambda k: pltpu.async_copy(yb_hbm.at[idx_v.at[k]], rows_v.at[k % 2], sems.at[k % 2])
        cp = gather(0)
        for k in range(TOP_K):
            cp.wait()
            if k + 1 < TOP_K:
                cp = gather(k + 1)
            pltpu.sync_copy(rows_v.at[k % 2], g_hbm.at[k, rows])

    _sc_for_chunks(dp_hbm.shape[0], functools.partial(chunk, dp_hbm, gp_hbm))
    _sc_for_chunks(ds_hbm.shape[0], functools.partial(chunk, ds_hbm, gs_hbm))


def _gather_expert_rows(dest_p3, dest_s3, yb):
    n_p, n_s = dest_p3.shape[0] * SC_GATHER_ROWS, dest_s3.shape[0] * SC_GATHER_ROWS
    return pl.kernel(
        _sc_gather_body,
        out_type=(jax.ShapeDtypeStruct((TOP_K, n_p, D_PACK), U32),
                  jax.ShapeDtypeStruct((TOP_K, n_s, D_PACK), U32)),
        mesh=_sc_mesh(),
        scratch_types=[pltpu.VMEM((TOP_K, SC_GATHER_ROWS), jnp.int32),
                       pltpu.VMEM((2, SC_GATHER_ROWS, D_PACK), U32),
                       pltpu.SemaphoreType.DMA((2,))],
        name="moe_gather_sc",
    )(dest_p3, dest_s3, yb)


def _combine_kernel(g_ref, gate_ref, x1_ref, o_ref):
    gate = gate_ref[...]
    x1 = x1_ref[...]
    acc_lo, acc_hi = x1[:, :D_PACK], x1[:, D_PACK:]
    for k in range(TOP_K):
        lo, hi = _unpack_rows(g_ref[k])
        acc_lo = acc_lo + gate[:, k:k + 1] * lo
        acc_hi = acc_hi + gate[:, k:k + 1] * hi
    o_ref[:, :D_PACK] = acc_lo
    o_ref[:, D_PACK:] = acc_hi


def _combine(g, gates, x1):
    n_tok = x1.shape[0]
    tm = min(512, n_tok)
    return pl.pallas_call(
        _combine_kernel,
        grid=(n_tok // tm,),
        in_specs=[pl.BlockSpec((TOP_K, tm, D_PACK), lambda i: (0, i, 0)),
                  pl.BlockSpec((tm, LANES), lambda i: (i, 0)),
                  pl.BlockSpec((tm, D_MODEL), lambda i: (i, 0))],
        out_specs=pl.BlockSpec((tm, D_MODEL), lambda i: (i, 0)),
        out_shape=jax.ShapeDtypeStruct((n_tok, D_MODEL), F32),
        compiler_params=_cparams(("parallel",), VMEM_LIMIT),
        name="moe_combine",
    )(g, gates, x1)


def _route_tables(counts_f, n_assign):
    counts = counts_f[0, :N_EXPERTS].astype(jnp.int32)
    padded = (counts + MOE_ROWS - 1) // MOE_ROWS * MOE_ROWS
    pad_end = jnp.cumsum(padded)
    pad_start = pad_end - padded
    n_blocks = -(-(n_assign + N_EXPERTS * (MOE_ROWS - 1)) // MOE_ROWS)
    row0 = jnp.arange(n_blocks, dtype=jnp.int32) * MOE_ROWS
    block_e = jnp.minimum(jnp.sum(row0[:, None] >= pad_end[None, :], axis=1), N_EXPERTS - 1).astype(jnp.int32)
    n_valid = jnp.clip((pad_start + counts)[block_e] - row0, 0, MOE_ROWS).astype(jnp.int32)
    nb_used = (pad_end[-1] // MOE_ROWS).astype(jnp.int32).reshape(1)
    e_ids = jnp.arange(N_EXPERTS, dtype=jnp.int32)
    nonempty = counts > 0
    slot_e = (jnp.cumsum(nonempty.astype(jnp.int32)) - 1) % 2
    later = jnp.where(nonempty[None, :] & (e_ids[None, :] > e_ids[:, None]), e_ids[None, :], N_EXPERTS)
    next_e = jnp.min(later, axis=1)
    next_e = jnp.where(next_e == N_EXPERTS, -1, next_e)
    first = (row0 == pad_start[block_e]).astype(jnp.int32)
    tables = (block_e, nb_used, n_valid, first, slot_e[block_e].astype(jnp.int32),
              next_e[block_e].astype(jnp.int32))
    return pad_start, tables, n_blocks


def _dest(route, pad_start):
    idx, rank = route[:, :TOP_K], route[:, TOP_K:2 * TOP_K]
    onehot = idx[:, :, None] == jnp.arange(N_EXPERTS, dtype=jnp.int32)[None, None, :]
    return rank + jnp.sum(jnp.where(onehot, pad_start[None, None, :], 0), axis=-1)


def _chunked(dest, rows):
    return dest.reshape(-1, rows, TOP_K).transpose(0, 2, 1)


def _rope_tables(offset, length):
    pos = offset + jnp.arange(length, dtype=F32)
    inv = ROPE_THETA ** (-jnp.arange(QK_ROPE // 2, dtype=F32) * (2.0 / QK_ROPE))
    ang = pos[:, None] * inv[None, :]
    cos, sin = jnp.cos(ang), jnp.sin(ang)
    z32 = jnp.zeros((length, 32), F32)
    rot_c = jnp.concatenate([cos, cos], axis=1)
    rot_s = jnp.concatenate([-sin, sin], axis=1)
    ck = jnp.concatenate([rot_c, z32, rot_c, z32], axis=1)
    sk = jnp.concatenate([rot_s, z32, rot_s, z32], axis=1)
    cq = jnp.concatenate([jnp.ones((length, QK_NOPE), F32), rot_c, z32], axis=1)
    sq = jnp.concatenate([jnp.zeros((length, QK_NOPE), F32), rot_s, z32], axis=1)
    return ck, sk, cq, sq


def _head_gain(g_nope, g_rope):
    return jnp.concatenate([g_nope, g_rope, g_rope, jnp.zeros((HEAD_PAD - QK_HEAD,), F32)]).reshape(1, HEAD_PAD)


def _mixer_stage(x, past, prm, counts_in, bsz, length):
    offset = 0 if past is None else past[0].shape[1]
    ck, sk, cq, sq = _rope_tables(offset, length)
    x2 = x.reshape(bsz * length, D_MODEL)
    u_tm, qn, lat, k_rope, krot = _in_proj(x2, prm['g_mix'], prm['w_in_p'], prm['g_q_lat'], prm['g_kv_lat'],
                                           ck, sk, bsz, length)
    if past is None:
        h0 = jnp.zeros((bsz, S5_STATE_LANES), F32)
    else:
        h0 = _state_layout(past[2], past[3])
    s5n_tm, h_last = _s5(u_tm.reshape(length * bsz, S5_WIDTH), h0, prm['lam'], prm['wb'], prm['wc'],
                         prm['d_skip'], prm['w_glu'], prm['b_glu'], prm['g_s5_out'], bsz, length)
    h_re, h_im = _state_unlayout(h_last)
    q = _q_proj(qn, prm['w_uq_p'], prm['w_uq_sw'], cq, sq, prm['gq'], bsz, length)
    k, v = _kv_proj(lat, krot, prm['w_k'], prm['w_v'], prm['gk'], bsz, length)
    if past is None:
        attn = _attn_prompt(q, k, v, bsz, length)
    else:
        n_past = past[0].shape[1]
        c_lat = past[0].reshape(bsz * n_past, KV_LORA)
        c_rot = jnp.pad(past[1].reshape(bsz * n_past, QK_ROPE), ((0, 0), (QK_NOPE, HEAD_PAD - QK_HEAD)))
        kc, vc = _kv_proj(c_lat, c_rot, prm['w_k'], prm['w_v'], prm['gk'], bsz, n_past)
        attn = _attn_sample(q, kc, vc, k, v, bsz, length, n_past)
    x1, hn, route, gates, counts = _out_proj(x2, s5n_tm.reshape(length, bsz * S5_WIDTH),
                                             attn.reshape(bsz * length, MLA_WIDTH), prm['g_mla_out'],
                                             prm['w_out'], prm['g_ffn'], prm['w_r_hi'], prm['w_r_lo'], prm['b_r_p'],
                                             counts_in, bsz, length)
    return (x1, hn, route, gates, counts,
            lat.reshape(bsz, length, KV_LORA), k_rope.reshape(bsz, length, QK_ROPE), h_re, h_im)


def _prepare(g_mix, w_in, lam_re, lam_im, log_dt, b_s5_re, b_s5_im, c_s5_re, c_s5_im, d_s5, w_glu, b_glu,
             g_q_lat, w_uq, g_kv_lat, w_ukv, g_qn_nope, g_qn_rope, g_kn_nope, g_kn_rope, g_s5_out,
             g_mla_out, w_out, g_ffn, w_router, b_router):
    c0 = S5_WIDTH + Q_LORA + KV_LORA
    w_pe = w_in[:, c0:]
    z = jnp.zeros((D_MODEL, 32), F32)
    w_in_p = jnp.concatenate([w_in[:, :c0], w_pe, z, w_pe, z], axis=1).astype(BF16)
    ar, ai, bbr, bbi = _s5_prep(lam_re, lam_im, log_dt, b_s5_re, b_s5_im)
    wb, wc = _s5_weights(bbr, bbi, c_s5_re, c_s5_im)
    w_uq_p = jnp.pad(w_uq.reshape(Q_LORA, MLA_HEADS, QK_HEAD), ((0, 0), (0, 0), (0, HEAD_PAD - QK_HEAD)))
    r0, r1, r2 = QK_NOPE, QK_NOPE + QK_ROPE // 2, QK_HEAD
    w_uq_sw = jnp.zeros_like(w_uq_p).at[:, :, r0:r1].set(w_uq_p[:, :, r1:r2]).at[:, :, r1:r2].set(w_uq_p[:, :, r0:r1])
    w_kv = w_ukv.reshape(KV_LORA, MLA_HEADS, QK_NOPE + V_HEAD)
    w_k = jnp.pad(w_kv[:, :, :QK_NOPE], ((0, 0), (0, 0), (0, HEAD_PAD - QK_NOPE)))
    w_v = w_kv[:, :, QK_NOPE:]
    w_r_p = jnp.pad(w_router, ((0, 0), (0, LANES - N_EXPERTS)))
    w_r_hi = w_r_p.astype(BF16)
    return dict(
        g_mix=g_mix.reshape(1, -1), w_in_p=w_in_p,
        g_q_lat=g_q_lat.reshape(1, -1), g_kv_lat=g_kv_lat.reshape(1, -1),
        lam=_state_layout(ar, ai).reshape(1, S5_STATE_LANES), wb=wb, wc=wc,
        d_skip=d_s5.reshape(1, -1), w_glu=w_glu.astype(BF16), b_glu=b_glu.reshape(1, -1),
        g_s5_out=g_s5_out.reshape(1, -1),
        w_uq_p=w_uq_p.reshape(Q_LORA, MLA_HEADS * HEAD_PAD).astype(BF16),
        w_uq_sw=w_uq_sw.reshape(Q_LORA, MLA_HEADS * HEAD_PAD).astype(BF16),
        w_k=w_k.reshape(KV_LORA, MLA_HEADS * HEAD_PAD).astype(BF16),
        w_v=w_v.reshape(KV_LORA, MLA_WIDTH).astype(BF16),
        gq=_head_gain(g_qn_nope, g_qn_rope), gk=_head_gain(g_kn_nope, g_kn_rope),
        g_mla_out=g_mla_out.reshape(1, -1), w_out=w_out.astype(BF16), g_ffn=g_ffn.reshape(1, -1),
        w_r_hi=w_r_hi, w_r_lo=(w_r_p - w_r_hi.astype(F32)).astype(BF16),
        b_r_p=jnp.pad(b_router, (0, LANES - N_EXPERTS)).reshape(1, LANES),
    )


def _layer(xp, xs, cache_lat, cache_kr, st_re, st_im, mixer_w, w1, b1, w2, b2):
    prm = _prepare(*mixer_w)
    bp, lp, _ = xp.shape
    bs, ls, _ = xs.shape
    zero_counts = jnp.zeros((1, LANES), F32)
    x1p, hnp, routep, gatep, counts_p, latp, krp, hrp, hip = _mixer_stage(xp, None, prm, zero_counts, bp, lp)
    x1s, hns, routes, gates, counts, lats, krs, hrs, his = _mixer_stage(
        xs, (cache_lat, cache_kr, st_re, st_im), prm, counts_p, bs, ls)
    n_tok = bp * lp + bs * ls
    pad_start, tables, n_blocks = _route_tables(counts, n_tok * TOP_K)
    dest_p = _dest(routep, pad_start)
    dest_s = _dest(routes, pad_start)
    xs_sorted = _dispatch(_chunked(dest_p, SC_ROWS), _chunked(dest_s, SC_ROWS), hnp, hns, n_blocks)
    yb = _experts(tables, xs_sorted, w1, b1, w2, b2, n_blocks)
    g_p, g_s = _gather_expert_rows(_chunked(dest_p, SC_GATHER_ROWS), _chunked(dest_s, SC_GATHER_ROWS), yb)
    yp = _combine(g_p, gatep, x1p).reshape(bp, lp, D_MODEL)
    ys = _combine(g_s, gates, x1s).reshape(bs, ls, D_MODEL)
    return yp, ys, latp, krp, hrp, hip, lats, krs, hrs, his


def kernel(x_prompt, x_sample, cache_kv_latent, cache_k_rope, state_s5_re, state_s5_im, g_mix, w_in, lam_re,
           lam_im, log_dt, b_s5_re, b_s5_im, c_s5_re, c_s5_im, d_s5, w_glu, b_glu, g_q_lat, w_uq, g_kv_lat,
           w_ukv, g_qn_nope, g_qn_rope, g_kn_nope, g_kn_rope, g_s5_out, g_mla_out, w_out, g_ffn, w_router,
           b_router, w_mlp1, b_mlp1, w_mlp2, b_mlp2):
    depth = g_mix.shape[0]
    yp, ys = x_prompt, x_sample
    outs = [[] for _ in range(8)]
    for l in range(depth):
        mixer_w = (g_mix[l], w_in[l], lam_re[l], lam_im[l], log_dt[l], b_s5_re[l], b_s5_im[l], c_s5_re[l],
                   c_s5_im[l], d_s5[l], w_glu[l], b_glu[l], g_q_lat[l], w_uq[l], g_kv_lat[l], w_ukv[l],
                   g_qn_nope[l], g_qn_rope[l], g_kn_nope[l], g_kn_rope[l], g_s5_out[l], g_mla_out[l],
                   w_out[l], g_ffn[l], w_router[l], b_router[l])
        res = _layer(yp, ys, cache_kv_latent[l], cache_k_rope[l], state_s5_re[l], state_s5_im[l], mixer_w,
                     w_mlp1[l], b_mlp1[l], w_mlp2[l], b_mlp2[l])
        yp, ys = res[0], res[1]
        for o, r in zip(outs, res[2:]):
            o.append(r)
    return (yp, ys) + tuple(jnp.stack(o) for o in outs)
```

```python
import functools
import math

import jax
import jax.numpy as jnp
from jax import lax
from jax.experimental import pallas as pl
from jax.experimental.pallas import tpu as pltpu
from jax.experimental.pallas import tpu_sc as plsc

F32 = jnp.float32
BF16 = jnp.bfloat16
U32 = jnp.uint32

D_MODEL = 1024
S5_WIDTH = 512
S5_GROUP = 16
S5_GROUPS = 32
S5_STATE = 64
MLA_HEADS = 8
QK_NOPE = 64
QK_ROPE = 32
QK_HEAD = QK_NOPE + QK_ROPE
V_HEAD = 64
MLA_WIDTH = MLA_HEADS * V_HEAD
Q_LORA = 384
KV_LORA = 256
ROPE_THETA = 10000.0
CHUNK = 64
N_EXPERTS = 32
TOP_K = 4
D_FF = D_MODEL
SWIGLU_LIMIT = 7.0
SWIGLU_ALPHA = 1.702
EPS = 1e-6

LANES = 128
HEAD_PAD = 128
D_IN_PAD = 1280
S5_CHUNKS = 4
S5_CHUNK_LANES = 1024
S5_STATE_LANES = S5_CHUNKS * S5_CHUNK_LANES
ROW_TILE = 1024
SUB_TILE = 512
TIME_TILE = 64
S5_STEPS = 32
MOE_ROWS = 512
MOE_COL_CHUNKS = 2
MOE_ROW_CHUNKS = 1
VMEM_LIMIT = 56 * 1024 * 1024
SC_CORES = 2
SC_SUBCORES = 16
SC_WORKERS = SC_CORES * SC_SUBCORES
SC_ROWS = 128
SC_GATHER_ROWS = 64
D_PACK = D_MODEL // 2


def _cparams(sem, vmem=None):
    return pltpu.CompilerParams(dimension_semantics=sem, vmem_limit_bytes=vmem)


def _rms(x, g, n=None):
    n = x.shape[-1] if n is None else n
    ms = jnp.sum(x * x, axis=-1, keepdims=True) * (1.0 / n)
    return x * lax.rsqrt(ms + EPS) * g


def _sub_tiles(rows):
    sub = min(SUB_TILE, rows)
    return [slice(r, r + sub) for r in range(0, rows, sub)]


def _pack_rows(x):
    lo = lax.bitcast_convert_type(x[:, :D_PACK].astype(BF16).astype(F32), U32)
    hi = lax.bitcast_convert_type(x[:, D_PACK:].astype(BF16).astype(F32), U32)
    return (lo >> 16) | (hi & jnp.uint32(0xFFFF0000))


def _unpack_rows(w):
    lo = lax.bitcast_convert_type(w << 16, F32)
    hi = lax.bitcast_convert_type(w & jnp.uint32(0xFFFF0000), F32)
    return lo, hi


def _prep_kernel(lr_ref, li_ref, ldt_ref, br_ref, bi_ref, ar_ref, ai_ref, bbr_ref, bbi_ref):
    lr = lr_ref[...]
    li = li_ref[...]
    dt = jnp.exp(ldt_ref[...])
    mag = jnp.exp(lr * dt)
    ar = mag * jnp.cos(li * dt)
    ai = mag * jnp.sin(li * dt)
    ar_ref[...] = ar
    ai_ref[...] = ai
    den = lr * lr + li * li
    cr = ((ar - 1.0) * lr + ai * li) / den
    ci = (ai * lr - (ar - 1.0) * li) / den
    br = br_ref[...]
    bi = bi_ref[...]
    bbr_ref[...] = cr[:, None, :] * br - ci[:, None, :] * bi
    bbi_ref[...] = cr[:, None, :] * bi + ci[:, None, :] * br


def _s5_prep(lam_re, lam_im, log_dt, b_re, b_im):
    g, n = lam_re.shape
    p = b_re.shape[-1]
    out = pl.pallas_call(
        _prep_kernel,
        out_shape=(jax.ShapeDtypeStruct((g, n), F32), jax.ShapeDtypeStruct((g, n), F32),
                   jax.ShapeDtypeStruct((g, p, n), F32), jax.ShapeDtypeStruct((g, p, n), F32)),
        name="s5_prep",
    )(lam_re, lam_im, log_dt.reshape(g, 1), jnp.swapaxes(b_re, 1, 2), jnp.swapaxes(b_im, 1, 2))
    return out


def _state_layout(re, im):
    lead = re.shape[:-2]
    re = re.reshape(lead + (S5_CHUNKS, 512))
    im = im.reshape(lead + (S5_CHUNKS, 512))
    return jnp.stack([re, im], axis=-2).reshape(lead + (S5_STATE_LANES,))


def _state_unlayout(h):
    lead = h.shape[:-1]
    h = h.reshape(lead + (S5_CHUNKS, 2, 512))
    re = h[..., 0, :].reshape(lead + (S5_GROUPS, S5_STATE))
    im = h[..., 1, :].reshape(lead + (S5_GROUPS, S5_STATE))
    return re, im


def _s5_weights(bbr, bbi, c_re, c_im):
    eye8 = jnp.eye(8, dtype=F32)

    def blockdiag(m):
        a, b = m.shape[1], m.shape[2]
        return (eye8[:, None, :, None] * m[:, :, None, :]).reshape(8 * a, 8 * b)

    wb, wc = [], []
    for c in range(S5_CHUNKS):
        sl = slice(8 * c, 8 * c + 8)
        wb.append(jnp.concatenate([blockdiag(bbr[sl]), blockdiag(bbi[sl])], axis=1))
        cr_t = jnp.swapaxes(c_re[sl], 1, 2)
        ci_t = jnp.swapaxes(c_im[sl], 1, 2)
        wc.append(jnp.concatenate([blockdiag(cr_t), -blockdiag(ci_t)], axis=0))
    return jnp.stack(wb).astype(BF16), jnp.stack(wc).astype(BF16)


def _in_proj_kernel(x_ref, gmix_ref, w_ref, gq_ref, gkv_ref, ck_ref, sk_ref,
                    u_ref, qn_ref, lat_ref, krope_ref, krot_ref):
    bsz, steps, _ = x_ref.shape
    nb = max(1, min(bsz, SUB_TILE // steps))
    ck, sk = ck_ref[...][None], sk_ref[...][None]
    for b0 in range(0, bsz, nb):
        bs = slice(b0, b0 + nb)
        rows = nb * steps
        xn = _rms(x_ref[bs].reshape(rows, D_MODEL), gmix_ref[...]).astype(BF16)
        z = jnp.dot(xn, w_ref[...], preferred_element_type=F32)
        for bl in range(nb):
            for c in range(S5_CHUNKS):
                u_ref[c, pl.ds(b0 + bl, steps, stride=bsz), :] = z[bl * steps:(bl + 1) * steps,
                                                                   c * LANES:(c + 1) * LANES]
        qn = _rms(z[:, S5_WIDTH:S5_WIDTH + Q_LORA], gq_ref[...]).astype(BF16)
        qn_ref[bs] = qn.reshape(nb, steps, Q_LORA)
        c0 = S5_WIDTH + Q_LORA
        lat_ref[bs] = _rms(z[:, c0:c0 + KV_LORA], gkv_ref[...]).reshape(nb, steps, KV_LORA)
        kp = z[:, c0 + KV_LORA:]
        lane = lax.broadcasted_iota(jnp.int32, kp.shape, 1)
        first_half = (lane % 64) < 16
        sw = jnp.where(first_half, pltpu.roll(kp, LANES - 16, axis=1), pltpu.roll(kp, 16, axis=1))
        kr = kp.reshape(nb, steps, LANES) * ck + sw.reshape(nb, steps, LANES) * sk
        krope_ref[bs] = kr[:, :, :QK_ROPE]
        krot_ref[bs] = jnp.where(lane.reshape(nb, steps, LANES) >= 64, kr, 0.0)


def _in_proj(x, g_mix, w_in_p, g_q_lat, g_kv_lat, ck, sk):
    bsz, length, _ = x.shape
    steps = min(TIME_TILE, length)
    full = lambda i: (0, 0)
    blk = lambda w: pl.BlockSpec((bsz, steps, w), lambda i: (0, i, 0))
    return pl.pallas_call(
        _in_proj_kernel,
        grid=(length // steps,),
        in_specs=[blk(D_MODEL),
                  pl.BlockSpec((1, D_MODEL), full),
                  pl.BlockSpec((D_MODEL, D_IN_PAD), full),
                  pl.BlockSpec((1, Q_LORA), full),
                  pl.BlockSpec((1, KV_LORA), full),
                  pl.BlockSpec((steps, LANES), lambda i: (i, 0)),
                  pl.BlockSpec((steps, LANES), lambda i: (i, 0))],
        out_specs=[pl.BlockSpec((S5_CHUNKS, steps * bsz, LANES), lambda i: (0, i, 0)),
                   blk(Q_LORA), blk(KV_LORA), blk(QK_ROPE), blk(LANES)],
        out_shape=(jax.ShapeDtypeStruct((S5_CHUNKS, length * bsz, LANES), F32),
                   jax.ShapeDtypeStruct((bsz, length, Q_LORA), BF16),
                   jax.ShapeDtypeStruct((bsz, length, KV_LORA), F32),
                   jax.ShapeDtypeStruct((bsz, length, QK_ROPE), F32),
                   jax.ShapeDtypeStruct((bsz, length, LANES), F32)),
        compiler_params=_cparams(("parallel",), VMEM_LIMIT),
        name="in_proj",
    )(x, g_mix, w_in_p, g_q_lat, g_kv_lat, ck, sk)


def _s5_kernel(u_ref, h0_ref, lam_ref, wb_ref, wc_ref, dskip_ref, wglu_ref, bglu_ref, gout_ref,
               y_ref, hlast_ref, bu_ref, h_ref, *, bsz, steps):
    c_id = pl.program_id(0)

    @pl.when(c_id == 0)
    def _():
        h_ref[...] = h0_ref[...]

    for c in range(S5_CHUNKS):
        bu_ref[:, c * S5_CHUNK_LANES:(c + 1) * S5_CHUNK_LANES] = jnp.dot(
            u_ref[c].astype(BF16), wb_ref[c], preferred_element_type=F32)

    for c in range(S5_CHUNKS):
        re = slice(c * S5_CHUNK_LANES, c * S5_CHUNK_LANES + 512)
        im = slice(c * S5_CHUNK_LANES + 512, (c + 1) * S5_CHUNK_LANES)
        lam_r = jnp.broadcast_to(lam_ref[:, re], (bsz, 512))
        lam_i = jnp.broadcast_to(lam_ref[:, im], (bsz, 512))

        hr, hi = h_ref[:, re], h_ref[:, im]
        for t in range(steps):
            rows = slice(t * bsz, (t + 1) * bsz)
            hr, hi = (lam_r * hr - lam_i * hi + bu_ref[rows, re],
                      lam_r * hi + lam_i * hr + bu_ref[rows, im])
            bu_ref[rows, re] = hr
            bu_ref[rows, im] = hi
        h_ref[:, re] = hr
        h_ref[:, im] = hi

    ys = []
    for c in range(S5_CHUNKS):
        hs = bu_ref[:, c * S5_CHUNK_LANES:(c + 1) * S5_CHUNK_LANES].astype(BF16)
        ys.append(jnp.dot(hs, wc_ref[c], preferred_element_type=F32))
    u = jnp.concatenate([u_ref[c] for c in range(S5_CHUNKS)], axis=1)
    y = jnp.concatenate(ys, axis=1) + dskip_ref[...] * u
    y = jax.nn.gelu(y)
    gate = jnp.dot(y.astype(BF16), wglu_ref[...], preferred_element_type=F32) + bglu_ref[...]
    y = _rms(y * jax.nn.sigmoid(gate), gout_ref[...])
    for c in range(S5_CHUNKS):
        y_ref[c] = y[:, c * LANES:(c + 1) * LANES]

    @pl.when(c_id == pl.num_programs(0) - 1)
    def _():
        hlast_ref[...] = h_ref[...]


def _s5(u_tm, h0, lam, wb, wc, d_skip, w_glu, b_glu, g_out, bsz, length):
    steps = min(S5_STEPS, length)
    rows = steps * bsz
    full2 = lambda c: (0, 0)
    full3 = lambda c: (0, 0, 0)
    return pl.pallas_call(
        functools.partial(_s5_kernel, bsz=bsz, steps=steps),
        grid=(length // steps,),
        in_specs=[pl.BlockSpec((S5_CHUNKS, rows, LANES), lambda c: (0, c, 0)),
                  pl.BlockSpec((bsz, S5_STATE_LANES), full2),
                  pl.BlockSpec((1, S5_STATE_LANES), full2),
                  pl.BlockSpec((S5_CHUNKS, LANES, S5_CHUNK_LANES), full3),
                  pl.BlockSpec((S5_CHUNKS, S5_CHUNK_LANES, LANES), full3),
                  pl.BlockSpec((1, S5_WIDTH), full2),
                  pl.BlockSpec((S5_WIDTH, S5_WIDTH), full2),
                  pl.BlockSpec((1, S5_WIDTH), full2),
                  pl.BlockSpec((1, S5_WIDTH), full2)],
        out_specs=[pl.BlockSpec((S5_CHUNKS, rows, LANES), lambda c: (0, c, 0)),
                   pl.BlockSpec((bsz, S5_STATE_LANES), full2)],
        out_shape=(jax.ShapeDtypeStruct((S5_CHUNKS, length * bsz, LANES), F32),
                   jax.ShapeDtypeStruct((bsz, S5_STATE_LANES), F32)),
        scratch_shapes=[pltpu.VMEM((rows, S5_STATE_LANES), F32),
                        pltpu.VMEM((bsz, S5_STATE_LANES), F32)],
        compiler_params=_cparams(("arbitrary",), VMEM_LIMIT),
        name="s5_mixer",
    )(u_tm, h0, lam, wb, wc, d_skip, w_glu, b_glu, g_out)


def _q_proj_kernel(qn_ref, w_ref, wsw_ref, ones_ref, cq_ref, sq_ref, g_ref, q_ref):
    g = g_ref[...] * (QK_HEAD ** -0.5)
    for rs in _sub_tiles(qn_ref.shape[0]):
        qn = qn_ref[rs, :]
        qf = jnp.dot(qn, w_ref[...], preferred_element_type=F32)
        qs = jnp.dot(qn, wsw_ref[...], preferred_element_type=F32)
        cq, sq = cq_ref[rs, :], sq_ref[rs, :]
        for p in range(MLA_HEADS // 2):
            cols = slice(2 * p * HEAD_PAD, 2 * (p + 1) * HEAD_PAD)
            xr = qf[:, cols] * cq + qs[:, cols] * sq
            sqr = xr * xr
            hi = sqr.astype(BF16)
            lo = (sqr - hi.astype(F32)).astype(BF16)
            ss = (jnp.dot(hi, ones_ref[...], preferred_element_type=F32)
                  + jnp.dot(lo, ones_ref[...], preferred_element_type=F32))
            q = (xr * lax.rsqrt(ss * (1.0 / QK_HEAD) + EPS) * g).astype(BF16)
            q_ref[0, 2 * p, rs, :] = q[:, :HEAD_PAD]
            q_ref[0, 2 * p + 1, rs, :] = q[:, HEAD_PAD:]


def _q_proj(qn, w_uq_p, w_uq_sw, cq, sq, gq, bsz, length):
    tm = min(ROW_TILE, length)
    n_l = length // tm
    full = lambda b, i: (0, 0)
    pair = 2 * HEAD_PAD
    two = lambda a: jnp.concatenate([a, a], axis=1)
    ones2 = (jnp.arange(pair)[:, None] // HEAD_PAD == jnp.arange(pair)[None, :] // HEAD_PAD).astype(BF16)
    return pl.pallas_call(
        _q_proj_kernel,
        grid=(bsz, n_l),
        in_specs=[pl.BlockSpec((tm, Q_LORA), lambda b, i: (b * n_l + i, 0)),
                  pl.BlockSpec((Q_LORA, MLA_HEADS * HEAD_PAD), full),
                  pl.BlockSpec((Q_LORA, MLA_HEADS * HEAD_PAD), full),
                  pl.BlockSpec((pair, pair), full),
                  pl.BlockSpec((tm, pair), lambda b, i: (i, 0)),
                  pl.BlockSpec((tm, pair), lambda b, i: (i, 0)),
                  pl.BlockSpec((1, pair), full)],
        out_specs=pl.BlockSpec((1, MLA_HEADS, tm, HEAD_PAD), lambda b, i: (b, 0, i, 0)),
        out_shape=jax.ShapeDtypeStruct((bsz, MLA_HEADS, length, HEAD_PAD), BF16),
        compiler_params=_cparams(("parallel", "parallel"), VMEM_LIMIT),
        name="q_proj",
    )(qn, w_uq_p, w_uq_sw, ones2, two(cq), two(sq), two(gq))


def _kv_proj_kernel(lat_ref, krot_ref, wk_ref, wv_ref, g_ref, k_ref, v_ref):
    g = g_ref[...]
    for rs in _sub_tiles(lat_ref.shape[0]):
        lat = lat_ref[rs, :].astype(BF16)
        kf = jnp.dot(lat, wk_ref[...], preferred_element_type=F32)
        vf = jnp.dot(lat, wv_ref[...], preferred_element_type=F32)
        krot = krot_ref[rs, :]
        for h in range(MLA_HEADS):
            k = kf[:, h * HEAD_PAD:(h + 1) * HEAD_PAD] + krot
            k_ref[0, h, rs, :] = _rms(k, g, QK_HEAD).astype(BF16)
        for p in range(MLA_HEADS // 2):
            v_ref[0, p, rs, :] = vf[:, p * LANES:(p + 1) * LANES].astype(BF16)


def _kv_proj(lat, krot, w_k, w_v, gk, bsz, length):
    tm = min(ROW_TILE, length)
    n_l = length // tm
    full = lambda b, i: (0, 0)
    row = lambda b, i: (b * n_l + i, 0)
    return pl.pallas_call(
        _kv_proj_kernel,
        grid=(bsz, n_l),
        in_specs=[pl.BlockSpec((tm, KV_LORA), row),
                  pl.BlockSpec((tm, HEAD_PAD), row),
                  pl.BlockSpec((KV_LORA, MLA_HEADS * HEAD_PAD), full),
                  pl.BlockSpec((KV_LORA, MLA_WIDTH), full),
                  pl.BlockSpec((1, HEAD_PAD), full)],
        out_specs=[pl.BlockSpec((1, MLA_HEADS, tm, HEAD_PAD), lambda b, i: (b, 0, i, 0)),
                   pl.BlockSpec((1, MLA_HEADS // 2, tm, LANES), lambda b, i: (b, 0, i, 0))],
        out_shape=(jax.ShapeDtypeStruct((bsz, MLA_HEADS, length, HEAD_PAD), BF16),
                   jax.ShapeDtypeStruct((bsz, MLA_HEADS // 2, length, LANES), BF16)),
        compiler_params=_cparams(("parallel", "parallel"), VMEM_LIMIT),
        name="kv_proj",
    )(lat, krot, w_k, w_v, gk)


def _scores(q, k):
    return lax.dot_general(q, k, (((1,), (1,)), ((), ())), preferred_element_type=F32)


def _attn_prompt_kernel(q_ref, k_ref, v_ref, o_ref, *, length, tq):
    n_q = length // tq
    row = lax.broadcasted_iota(jnp.int32, (tq, tq), 0)
    col = lax.broadcasted_iota(jnp.int32, (tq, tq), 1)
    visible = (col // CHUNK) <= (row // CHUNK)
    lane = lax.broadcasted_iota(jnp.int32, (tq, LANES), 1)
    for qi in range(n_q):
        q0 = qi * tq
        outs = []
        for hh in range(2):
            q = q_ref[0, hh, q0:q0 + tq, :]
            sd = jnp.where(visible, _scores(q, k_ref[0, hh, q0:q0 + tq, :]), -jnp.inf)
            m = jnp.max(sd, axis=-1, keepdims=True)
            if qi:
                so = _scores(q, k_ref[0, hh, 0:q0, :])
                m = jnp.maximum(m, jnp.max(so, axis=-1, keepdims=True))
            pd = jnp.exp(sd - m)
            l = jnp.sum(pd, axis=-1, keepdims=True)
            acc = jnp.dot(pd.astype(BF16), v_ref[0, 0, q0:q0 + tq, :], preferred_element_type=F32)
            if qi:
                po = jnp.exp(so - m)
                l = l + jnp.sum(po, axis=-1, keepdims=True)
                acc = acc + jnp.dot(po.astype(BF16), v_ref[0, 0, 0:q0, :], preferred_element_type=F32)
            outs.append(acc / l)
        o_ref[0, q0:q0 + tq, :] = jnp.where(lane < V_HEAD, outs[0], outs[1])


def _attn_prompt(q, k, v, bsz, length):
    tq = min(512, length)
    pairs = MLA_HEADS // 2
    return pl.pallas_call(
        functools.partial(_attn_prompt_kernel, length=length, tq=tq),
        grid=(bsz, pairs),
        in_specs=[pl.BlockSpec((1, 2, length, HEAD_PAD), lambda b, p: (b, p, 0, 0)),
                  pl.BlockSpec((1, 2, length, HEAD_PAD), lambda b, p: (b, p, 0, 0)),
                  pl.BlockSpec((1, 1, length, LANES), lambda b, p: (b, p, 0, 0))],
        out_specs=pl.BlockSpec((1, length, LANES), lambda b, p: (b, 0, p)),
        out_shape=jax.ShapeDtypeStruct((bsz, length, MLA_WIDTH), F32),
        compiler_params=_cparams(("parallel", "parallel"), VMEM_LIMIT),
        name="attn_prompt",
    )(q, k, v)


def _attn_sample_kernel(q_ref, kc_ref, vc_ref, kn_ref, vn_ref, o_ref):
    lq = q_ref.shape[2]
    lane = lax.broadcasted_iota(jnp.int32, (lq, LANES), 1)
    outs = []
    for hh in range(2):
        q = q_ref[0, hh]
        sc = _scores(q, kc_ref[0, hh])
        sn = _scores(q, kn_ref[0, hh])
        m = jnp.maximum(jnp.max(sc, axis=-1, keepdims=True), jnp.max(sn, axis=-1, keepdims=True))
        pc = jnp.exp(sc - m)
        pn = jnp.exp(sn - m)
        l = jnp.sum(pc, axis=-1, keepdims=True) + jnp.sum(pn, axis=-1, keepdims=True)
        acc = (jnp.dot(pc.astype(BF16), vc_ref[0, 0], preferred_element_type=F32)
               + jnp.dot(pn.astype(BF16), vn_ref[0, 0], preferred_element_type=F32))
        outs.append(acc / l)
    o_ref[0] = jnp.where(lane < V_HEAD, outs[0], outs[1])


def _attn_sample(q, kc, vc, kn, vn, bsz, lq, past):
    pairs = MLA_HEADS // 2
    hp = lambda b, p: (b, p, 0, 0)
    return pl.pallas_call(
        _attn_sample_kernel,
        grid=(bsz, pairs),
        in_specs=[pl.BlockSpec((1, 2, lq, HEAD_PAD), hp),
                  pl.BlockSpec((1, 2, past, HEAD_PAD), hp),
                  pl.BlockSpec((1, 1, past, LANES), hp),
                  pl.BlockSpec((1, 2, lq, HEAD_PAD), hp),
                  pl.BlockSpec((1, 1, lq, LANES), hp)],
        out_specs=pl.BlockSpec((1, lq, LANES), lambda b, p: (b, 0, p)),
        out_shape=jax.ShapeDtypeStruct((bsz, lq, MLA_WIDTH), F32),
        compiler_params=_cparams(("parallel", "parallel"), VMEM_LIMIT),
        name="attn_sample",
    )(q, kc, vc, kn, vn)


def _out_proj_kernel(x_ref, s5_ref, at_ref, gmla_ref, wout_ref, gffn_ref, wr_ref, br_ref, tri_ref, cin_ref,
                     x1_ref, hn_ref, route_ref, gate_ref, cout_ref, run_ref):
    @pl.when(pl.program_id(0) == 0)
    def _():
        run_ref[...] = cin_ref[...]

    bsz, steps, _ = x_ref.shape
    nb = max(1, min(bsz, SUB_TILE // steps))
    for b0 in range(0, bsz, nb):
        bs = slice(b0, b0 + nb)
        rows = nb * steps
        s5 = jnp.concatenate(
            [jnp.concatenate([s5_ref[c, pl.ds(b0 + bl, steps, stride=bsz), :] for c in range(S5_CHUNKS)], axis=1)
             for bl in range(nb)], axis=0)
        an = _rms(at_ref[bs].reshape(rows, MLA_WIDTH), gmla_ref[...])
        merged = jnp.concatenate([s5.astype(BF16), an.astype(BF16)], axis=1)
        x1 = x_ref[bs].reshape(rows, D_MODEL) + jnp.dot(merged, wout_ref[...], preferred_element_type=F32)
        x1_ref[bs] = x1.reshape(nb, steps, D_MODEL)
        hn = _rms(x1, gffn_ref[...])
        hn_ref[bs] = _pack_rows(hn).reshape(nb, steps, D_PACK)
        hn_hi = hn.astype(BF16)
        hn_lo = (hn - hn_hi.astype(F32)).astype(BF16)
        parts = (jnp.dot(hn_hi, wr_ref[...], preferred_element_type=F32)
                 + jnp.dot(hn_lo, wr_ref[...], preferred_element_type=F32))
        logits = parts[:, :LANES] + parts[:, LANES:] + br_ref[...]
        lane = lax.broadcasted_iota(jnp.int32, logits.shape, 1)
        lane_f = lane.astype(F32)
        work = jnp.where(lane < N_EXPERTS, logits, -jnp.inf)
        vals, idxs = [], []
        for _ in range(TOP_K):
            m = jnp.max(work, axis=-1, keepdims=True)
            i = jnp.min(jnp.where(work == m, lane_f, float(LANES)), axis=-1, keepdims=True)
            vals.append(m)
            idxs.append(i)
            work = jnp.where(lane_f == i, -jnp.inf, work)
        es = [jnp.exp(v - vals[0]) for v in vals]
        den = es[0] + es[1] + es[2] + es[3]
        onehots = [(lane_f == idxs[k]).astype(F32) for k in range(TOP_K)]
        e_all = onehots[0] + onehots[1] + onehots[2] + onehots[3]
        before = jnp.dot(tri_ref[...], e_all.astype(BF16), preferred_element_type=F32) + run_ref[...]
        run_ref[...] = run_ref[...] + jnp.sum(e_all, axis=0, keepdims=True)
        route = jnp.zeros(logits.shape, F32)
        gate_out = jnp.zeros(logits.shape, F32)
        for k in range(TOP_K):
            rank = jnp.sum(onehots[k] * before, axis=-1, keepdims=True)
            route = jnp.where(lane == k, idxs[k], route)
            route = jnp.where(lane == TOP_K + k, rank, route)
            gate_out = jnp.where(lane == k, es[k] / den, gate_out)
        route_ref[bs] = route.astype(jnp.int32).reshape(nb, steps, LANES)
        gate_ref[bs] = gate_out.reshape(nb, steps, LANES)
    cout_ref[...] = run_ref[...]


def _out_proj(x, s5n_tm, attn, g_mla, w_out, g_ffn, w_r, b_r_p, counts_in):
    bsz, length, _ = x.shape
    steps = min(TIME_TILE, length)
    full = lambda i: (0, 0)
    blk = lambda w: pl.BlockSpec((bsz, steps, w), lambda i: (0, i, 0))
    sub = max(1, min(bsz, SUB_TILE // steps)) * steps
    tri = (jnp.arange(sub)[:, None] > jnp.arange(sub)[None, :]).astype(BF16)
    return pl.pallas_call(
        _out_proj_kernel,
        grid=(length // steps,),
        in_specs=[blk(D_MODEL),
                  pl.BlockSpec((S5_CHUNKS, steps * bsz, LANES), lambda i: (0, i, 0)),
                  blk(MLA_WIDTH),
                  pl.BlockSpec((1, MLA_WIDTH), full),
                  pl.BlockSpec((D_MODEL, D_MODEL), full),
                  pl.BlockSpec((1, D_MODEL), full),
                  pl.BlockSpec((D_MODEL, 2 * LANES), full),
                  pl.BlockSpec((1, LANES), full),
                  pl.BlockSpec((sub, sub), full),
                  pl.BlockSpec((1, LANES), full)],
        out_specs=[blk(D_MODEL), blk(D_PACK), blk(LANES), blk(LANES),
                   pl.BlockSpec((1, LANES), full)],
        out_shape=(jax.ShapeDtypeStruct((bsz, length, D_MODEL), F32),
                   jax.ShapeDtypeStruct((bsz, length, D_PACK), U32),
                   jax.ShapeDtypeStruct((bsz, length, LANES), jnp.int32),
                   jax.ShapeDtypeStruct((bsz, length, LANES), F32),
                   jax.ShapeDtypeStruct((1, LANES), F32)),
        scratch_shapes=[pltpu.VMEM((1, LANES), F32)],
        compiler_params=_cparams(("arbitrary",), VMEM_LIMIT),
        name="out_proj",
    )(x, s5n_tm, attn, g_mla, w_out, g_ffn, w_r, b_r_p, tri, counts_in)


def _sc_mesh():
    return plsc.VectorSubcoreMesh(core_axis_name="c", subcore_axis_name="s",
                                  num_cores=SC_CORES, num_subcores=SC_SUBCORES)


def _sc_for_chunks(n_chunks, fn):
    wid = lax.axis_index("s") * SC_CORES + lax.axis_index("c")
    full, rem = divmod(n_chunks, SC_WORKERS)
    if full:
        @pl.loop(0, full)
        def _(j):
            fn(j * SC_WORKERS + wid)
    if rem:
        @pl.when(wid < rem)
        def _():
            fn(full * SC_WORKERS + wid)


def _chunk_rows(c):
    return pl.ds(pl.multiple_of(c * SC_ROWS, SC_ROWS), SC_ROWS)


def _sc_dispatch_body(dp_hbm, ds_hbm, hp_hbm, hs_hbm, xs_hbm, idx_v, rows_v, sem):
    def chunk(d_hbm, h_hbm, c):
        pltpu.sync_copy(d_hbm.at[c], idx_v)
        pltpu.sync_copy(h_hbm.at[_chunk_rows(c)], rows_v)
        copies = [pltpu.async_copy(rows_v, xs_hbm.at[idx_v.at[k]], sem) for k in range(TOP_K)]
        for cp in copies:
            cp.wait()

    _sc_for_chunks(dp_hbm.shape[0], functools.partial(chunk, dp_hbm, hp_hbm))
    _sc_for_chunks(ds_hbm.shape[0], functools.partial(chunk, ds_hbm, hs_hbm))


def _dispatch(dest_p3, dest_s3, hn_p, hn_s, n_blocks):
    return pl.kernel(
        _sc_dispatch_body,
        out_type=jax.ShapeDtypeStruct((n_blocks * MOE_ROWS, D_PACK), U32),
        mesh=_sc_mesh(),
        scratch_types=[pltpu.VMEM((TOP_K, SC_ROWS), jnp.int32), pltpu.VMEM((SC_ROWS, D_PACK), U32),
                       pltpu.SemaphoreType.DMA],
        name="moe_dispatch_sc",
    )(dest_p3, dest_s3, hn_p, hn_s)


def _expert_kernel(be_ref, nb_ref, nv_ref, first_ref, slot_ref, nxt_ref,
                   x_ref, w1_hbm, b1_ref, w2_hbm, b2_ref, y_ref, w1f_ref, w2f_ref, w1b_ref, w2b_ref, sem):
    b = pl.program_id(0)

    def weight_copies(e, s):
        return (pltpu.make_async_copy(w1_hbm.at[e], w1f_ref.at[s], sem.at[s]),
                pltpu.make_async_copy(w2_hbm.at[e], w2f_ref.at[s], sem.at[s]))

    @pl.when(jnp.logical_and(b == 0, nb_ref[0] > 0))
    def _():
        for cp in weight_copies(be_ref[0], 0):
            cp.start()

    @pl.when(jnp.logical_and(b < nb_ref[0], first_ref[b] == 1))
    def _():
        s = slot_ref[b]
        for cp in weight_copies(be_ref[b], s):
            cp.wait()
        w1b_ref[...] = w1f_ref[s].astype(BF16)
        w2b_ref[...] = w2f_ref[s].astype(BF16)

        @pl.when(nxt_ref[b] >= 0)
        def _():
            for cp in weight_copies(nxt_ref[b], 1 - s):
                cp.start()

    @pl.when(b < nb_ref[0])
    def _():
        b1 = b1_ref[0]
        cw = D_FF // MOE_COL_CHUNKS
        sub = MOE_ROWS // MOE_ROW_CHUNKS
        for r0 in range(0, MOE_ROWS, sub):
            rs = slice(r0, r0 + sub)
            live = lax.broadcasted_iota(jnp.int32, (sub, D_PACK), 0) < nv_ref[b] - r0
            lo, hi = _unpack_rows(jnp.where(live, x_ref[rs, :], jnp.uint32(0)))
            lo, hi = lo.astype(BF16), hi.astype(BF16)

            def up(c0, lo=lo, hi=hi):
                return (jnp.dot(lo, w1b_ref[:D_PACK, c0:c0 + cw], preferred_element_type=F32)
                        + jnp.dot(hi, w1b_ref[D_PACK:, c0:c0 + cw], preferred_element_type=F32)
                        + b1[:, c0:c0 + cw])

            y = b2_ref[0]
            for j in range(MOE_COL_CHUNKS):
                gate = jnp.minimum(up(j * cw), SWIGLU_LIMIT)
                lin = jnp.clip(up(D_FF + j * cw), -SWIGLU_LIMIT, SWIGLU_LIMIT)
                act = gate * jax.nn.sigmoid(SWIGLU_ALPHA * gate) * (lin + 1.0)
                y = y + jnp.dot(act.astype(BF16), w2b_ref[j * cw:(j + 1) * cw, :], preferred_element_type=F32)
            y_ref[rs, :] = _pack_rows(y)

    @pl.when(b >= nb_ref[0])
    def _():
        y_ref[...] = jnp.zeros(y_ref.shape, y_ref.dtype)


def _experts(tables, xs, w1, b1, w2, b2, n_blocks):
    last = lambda b, be, nb, *_: (jnp.maximum(jnp.minimum(b, nb[0] - 1), 0), 0)
    bsel = lambda b, be, *_: (be[b], 0, 0)
    return pl.pallas_call(
        _expert_kernel,
        grid_spec=pltpu.PrefetchScalarGridSpec(
            num_scalar_prefetch=6,
            grid=(n_blocks,),
            in_specs=[pl.BlockSpec((MOE_ROWS, D_PACK), last),
                      pl.BlockSpec(memory_space=pl.ANY),
                      pl.BlockSpec((1, 1, 2 * D_FF), bsel),
                      pl.BlockSpec(memory_space=pl.ANY),
                      pl.BlockSpec((1, 1, D_MODEL), bsel)],
            out_specs=pl.BlockSpec((MOE_ROWS, D_PACK), lambda b, *_: (b, 0)),
            scratch_shapes=[pltpu.VMEM((2, D_MODEL, 2 * D_FF), F32), pltpu.VMEM((2, D_FF, D_MODEL), F32),
                            pltpu.VMEM((D_MODEL, 2 * D_FF), BF16), pltpu.VMEM((D_FF, D_MODEL), BF16),
                            pltpu.SemaphoreType.DMA((2,))]),
        out_shape=jax.ShapeDtypeStruct((n_blocks * MOE_ROWS, D_PACK), U32),
        compiler_params=_cparams(("arbitrary",), VMEM_LIMIT),
        name="moe_experts",
    )(*tables, xs, w1, b1.reshape(N_EXPERTS, 1, 2 * D_FF), w2, b2.reshape(N_EXPERTS, 1, D_MODEL))


def _sc_gather_body(dp_hbm, ds_hbm, yb_hbm, gp_hbm, gs_hbm, idx_v, rows_v, sems):
    n = SC_GATHER_ROWS

    def chunk(d_hbm, g_hbm, c):
        pltpu.sync_copy(d_hbm.at[c], idx_v)
        rows = pl.ds(pl.multiple_of(c * n, n), n)
        gather = lambda k: pltpu.async_copy(yb_hbm.at[idx_v.at[k]], rows_v.at[k % 2], sems.at[k % 2])
        cp = gather(0)
        for k in range(TOP_K):
            cp.wait()
            if k + 1 < TOP_K:
                cp = gather(k + 1)
            pltpu.sync_copy(rows_v.at[k % 2], g_hbm.at[k, rows])

    _sc_for_chunks(dp_hbm.shape[0], functools.partial(chunk, dp_hbm, gp_hbm))
    _sc_for_chunks(ds_hbm.shape[0], functools.partial(chunk, ds_hbm, gs_hbm))


def _gather_expert_rows(dest_p3, dest_s3, yb):
    n_p, n_s = dest_p3.shape[0] * SC_GATHER_ROWS, dest_s3.shape[0] * SC_GATHER_ROWS
    return pl.kernel(
        _sc_gather_body,
        out_type=(jax.ShapeDtypeStruct((TOP_K, n_p, D_PACK), U32),
                  jax.ShapeDtypeStruct((TOP_K, n_s, D_PACK), U32)),
        mesh=_sc_mesh(),
        scratch_types=[pltpu.VMEM((TOP_K, SC_GATHER_ROWS), jnp.int32),
                       pltpu.VMEM((2, SC_GATHER_ROWS, D_PACK), U32),
                       pltpu.SemaphoreType.DMA((2,))],
        name="moe_gather_sc",
    )(dest_p3, dest_s3, yb)


def _combine_kernel(g_ref, gate_ref, x1_ref, o_ref):
    gate = gate_ref[...]
    x1 = x1_ref[...]
    acc_lo, acc_hi = x1[:, :D_PACK], x1[:, D_PACK:]
    for k in range(TOP_K):
        lo, hi = _unpack_rows(g_ref[k])
        acc_lo = acc_lo + gate[:, k:k + 1] * lo
        acc_hi = acc_hi + gate[:, k:k + 1] * hi
    o_ref[:, :D_PACK] = acc_lo
    o_ref[:, D_PACK:] = acc_hi


def _combine(g, gates, x1):
    n_tok = x1.shape[0]
    tm = min(512, n_tok)
    return pl.pallas_call(
        _combine_kernel,
        grid=(n_tok // tm,),
        in_specs=[pl.BlockSpec((TOP_K, tm, D_PACK), lambda i: (0, i, 0)),
                  pl.BlockSpec((tm, LANES), lambda i: (i, 0)),
                  pl.BlockSpec((tm, D_MODEL), lambda i: (i, 0))],
        out_specs=pl.BlockSpec((tm, D_MODEL), lambda i: (i, 0)),
        out_shape=jax.ShapeDtypeStruct((n_tok, D_MODEL), F32),
        compiler_params=_cparams(("parallel",), VMEM_LIMIT),
        name="moe_combine",
    )(g, gates, x1)


def _route_tables(counts_f, n_assign):
    counts = counts_f[0, :N_EXPERTS].astype(jnp.int32)
    padded = (counts + MOE_ROWS - 1) // MOE_ROWS * MOE_ROWS
    pad_end = jnp.cumsum(padded)
    pad_start = pad_end - padded
    n_blocks = -(-(n_assign + N_EXPERTS * (MOE_ROWS - 1)) // MOE_ROWS)
    row0 = jnp.arange(n_blocks, dtype=jnp.int32) * MOE_ROWS
    block_e = jnp.minimum(jnp.sum(row0[:, None] >= pad_end[None, :], axis=1), N_EXPERTS - 1).astype(jnp.int32)
    n_valid = jnp.clip((pad_start + counts)[block_e] - row0, 0, MOE_ROWS).astype(jnp.int32)
    nb_used = (pad_end[-1] // MOE_ROWS).astype(jnp.int32).reshape(1)
    e_ids = jnp.arange(N_EXPERTS, dtype=jnp.int32)
    nonempty = counts > 0
    slot_e = (jnp.cumsum(nonempty.astype(jnp.int32)) - 1) % 2
    later = jnp.where(nonempty[None, :] & (e_ids[None, :] > e_ids[:, None]), e_ids[None, :], N_EXPERTS)
    next_e = jnp.min(later, axis=1)
    next_e = jnp.where(next_e == N_EXPERTS, -1, next_e)
    first = (row0 == pad_start[block_e]).astype(jnp.int32)
    tables = (block_e, nb_used, n_valid, first, slot_e[block_e].astype(jnp.int32),
              next_e[block_e].astype(jnp.int32))
    return pad_start, tables, n_blocks


def _dest(route, pad_start):
    idx, rank = route[:, :TOP_K], route[:, TOP_K:2 * TOP_K]
    onehot = idx[:, :, None] == jnp.arange(N_EXPERTS, dtype=jnp.int32)[None, None, :]
    return rank + jnp.sum(jnp.where(onehot, pad_start[None, None, :], 0), axis=-1)


def _chunked(dest, rows):
    return dest.reshape(-1, rows, TOP_K).transpose(0, 2, 1)


def _rope_tables(offset, length):
    pos = offset + jnp.arange(length, dtype=F32)
    inv = ROPE_THETA ** (-jnp.arange(QK_ROPE // 2, dtype=F32) * (2.0 / QK_ROPE))
    ang = pos[:, None] * inv[None, :]
    cos, sin = jnp.cos(ang), jnp.sin(ang)
    z32 = jnp.zeros((length, 32), F32)
    rot_c = jnp.concatenate([cos, cos], axis=1)
    rot_s = jnp.concatenate([-sin, sin], axis=1)
    ck = jnp.concatenate([rot_c, z32, rot_c, z32], axis=1)
    sk = jnp.concatenate([rot_s, z32, rot_s, z32], axis=1)
    cq = jnp.concatenate([jnp.ones((length, QK_NOPE), F32), rot_c, z32], axis=1)
    sq = jnp.concatenate([jnp.zeros((length, QK_NOPE), F32), rot_s, z32], axis=1)
    return ck, sk, cq, sq


def _head_gain(g_nope, g_rope):
    return jnp.concatenate([g_nope, g_rope, g_rope, jnp.zeros((HEAD_PAD - QK_HEAD,), F32)]).reshape(1, HEAD_PAD)


def _mixer_stage(x, past, prm, counts_in):
    bsz, length, _ = x.shape
    offset = 0 if past is None else past[0].shape[1]
    ck, sk, cq, sq = _rope_tables(offset, length)
    u_tm, qn, lat, k_rope, krot = _in_proj(x, prm['g_mix'], prm['w_in_p'], prm['g_q_lat'], prm['g_kv_lat'], ck, sk)
    if past is None:
        h0 = jnp.zeros((bsz, S5_STATE_LANES), F32)
    else:
        h0 = _state_layout(past[2], past[3])
    s5n_tm, h_last = _s5(u_tm, h0, prm['lam'], prm['wb'], prm['wc'],
                         prm['d_skip'], prm['w_glu'], prm['b_glu'], prm['g_s5_out'], bsz, length)
    h_re, h_im = _state_unlayout(h_last)
    t = bsz * length
    q = _q_proj(qn.reshape(t, Q_LORA), prm['w_uq_p'], prm['w_uq_sw'], cq, sq, prm['gq'], bsz, length)
    k, v = _kv_proj(lat.reshape(t, KV_LORA), krot.reshape(t, LANES), prm['w_k'], prm['w_v'], prm['gk'],
                    bsz, length)
    if past is None:
        attn = _attn_prompt(q, k, v, bsz, length)
    else:
        n_past = past[0].shape[1]
        c_lat = past[0].reshape(bsz * n_past, KV_LORA)
        c_rot = jnp.pad(past[1].reshape(bsz * n_past, QK_ROPE), ((0, 0), (QK_NOPE, HEAD_PAD - QK_HEAD)))
        kc, vc = _kv_proj(c_lat, c_rot, prm['w_k'], prm['w_v'], prm['gk'], bsz, n_past)
        attn = _attn_sample(q, kc, vc, k, v, bsz, length, n_past)
    x1, hn, route, gates, counts = _out_proj(x, s5n_tm, attn, prm['g_mla_out'], prm['w_out'], prm['g_ffn'],
                                             prm['w_r'], prm['b_r_p'], counts_in)
    return (x1.reshape(t, D_MODEL), hn.reshape(t, D_PACK), route.reshape(t, LANES), gates.reshape(t, LANES),
            counts, lat, k_rope, h_re, h_im)


def _prepare(g_mix, w_in, lam_re, lam_im, log_dt, b_s5_re, b_s5_im, c_s5_re, c_s5_im, d_s5, w_glu, b_glu,
             g_q_lat, w_uq, g_kv_lat, w_ukv, g_qn_nope, g_qn_rope, g_kn_nope, g_kn_rope, g_s5_out,
             g_mla_out, w_out, g_ffn, w_router, b_router):
    c0 = S5_WIDTH + Q_LORA + KV_LORA
    w_pe = w_in[:, c0:]
    z = jnp.zeros((D_MODEL, 32), F32)
    w_in_p = jnp.concatenate([w_in[:, :c0], w_pe, z, w_pe, z], axis=1).astype(BF16)
    ar, ai, bbr, bbi = _s5_prep(lam_re, lam_im, log_dt, b_s5_re, b_s5_im)
    wb, wc = _s5_weights(bbr, bbi, c_s5_re, c_s5_im)
    w_uq_p = jnp.pad(w_uq.reshape(Q_LORA, MLA_HEADS, QK_HEAD), ((0, 0), (0, 0), (0, HEAD_PAD - QK_HEAD)))
    r0, r1, r2 = QK_NOPE, QK_NOPE + QK_ROPE // 2, QK_HEAD
    w_uq_sw = jnp.zeros_like(w_uq_p).at[:, :, r0:r1].set(w_uq_p[:, :, r1:r2]).at[:, :, r1:r2].set(w_uq_p[:, :, r0:r1])
    w_kv = w_ukv.reshape(KV_LORA, MLA_HEADS, QK_NOPE + V_HEAD)
    w_k = jnp.pad(w_kv[:, :, :QK_NOPE], ((0, 0), (0, 0), (0, HEAD_PAD - QK_NOPE)))
    w_v = w_kv[:, :, QK_NOPE:]
    w_r_p = jnp.pad(w_router, ((0, 0), (0, LANES - N_EXPERTS)))
    w_r_hi = w_r_p.astype(BF16)
    return dict(
        g_mix=g_mix.reshape(1, -1), w_in_p=w_in_p,
        g_q_lat=g_q_lat.reshape(1, -1), g_kv_lat=g_kv_lat.reshape(1, -1),
        lam=_state_layout(ar, ai).reshape(1, S5_STATE_LANES), wb=wb, wc=wc,
        d_skip=d_s5.reshape(1, -1), w_glu=w_glu.astype(BF16), b_glu=b_glu.reshape(1, -1),
        g_s5_out=g_s5_out.reshape(1, -1),
        w_uq_p=w_uq_p.reshape(Q_LORA, MLA_HEADS * HEAD_PAD).astype(BF16),
        w_uq_sw=w_uq_sw.reshape(Q_LORA, MLA_HEADS * HEAD_PAD).astype(BF16),
        w_k=w_k.reshape(KV_LORA, MLA_HEADS * HEAD_PAD).astype(BF16),
        w_v=w_v.reshape(KV_LORA, MLA_WIDTH).astype(BF16),
        gq=_head_gain(g_qn_nope, g_qn_rope), gk=_head_gain(g_kn_nope, g_kn_rope),
        g_mla_out=g_mla_out.reshape(1, -1), w_out=w_out.astype(BF16), g_ffn=g_ffn.reshape(1, -1),
        w_r=jnp.concatenate([w_r_hi, (w_r_p - w_r_hi.astype(F32)).astype(BF16)], axis=1),
        b_r_p=jnp.pad(b_router, (0, LANES - N_EXPERTS)).reshape(1, LANES),
    )


def _layer(xp, xs, cache_lat, cache_kr, st_re, st_im, mixer_w, w1, b1, w2, b2):
    prm = _prepare(*mixer_w)
    bp, lp, _ = xp.shape
    bs, ls, _ = xs.shape
    zero_counts = jnp.zeros((1, LANES), F32)
    x1p, hnp, routep, gatep, counts_p, latp, krp, hrp, hip = _mixer_stage(xp, None, prm, zero_counts)
    x1s, hns, routes, gates, counts, lats, krs, hrs, his = _mixer_stage(
        xs, (cache_lat, cache_kr, st_re, st_im), prm, counts_p)
    n_tok = bp * lp + bs * ls
    pad_start, tables, n_blocks = _route_tables(counts, n_tok * TOP_K)
    dest_p = _dest(routep, pad_start)
    dest_s = _dest(routes, pad_start)
    xs_sorted = _dispatch(_chunked(dest_p, SC_ROWS), _chunked(dest_s, SC_ROWS), hnp, hns, n_blocks)
    yb = _experts(tables, xs_sorted, w1, b1, w2, b2, n_blocks)
    g_p, g_s = _gather_expert_rows(_chunked(dest_p, SC_GATHER_ROWS), _chunked(dest_s, SC_GATHER_ROWS), yb)
    yp = _combine(g_p, gatep, x1p).reshape(bp, lp, D_MODEL)
    ys = _combine(g_s, gates, x1s).reshape(bs, ls, D_MODEL)
    return yp, ys, latp, krp, hrp, hip, lats, krs, hrs, his


def kernel(x_prompt, x_sample, cache_kv_latent, cache_k_rope, state_s5_re, state_s5_im, g_mix, w_in, lam_re,
           lam_im, log_dt, b_s5_re, b_s5_im, c_s5_re, c_s5_im, d_s5, w_glu, b_glu, g_q_lat, w_uq, g_kv_lat,
           w_ukv, g_qn_nope, g_qn_rope, g_kn_nope, g_kn_rope, g_s5_out, g_mla_out, w_out, g_ffn, w_router,
           b_router, w_mlp1, b_mlp1, w_mlp2, b_mlp2):
    depth = g_mix.shape[0]
    yp, ys = x_prompt, x_sample
    outs = [[] for _ in range(8)]
    for l in range(depth):
        mixer_w = (g_mix[l], w_in[l], lam_re[l], lam_im[l], log_dt[l], b_s5_re[l], b_s5_im[l], c_s5_re[l],
                   c_s5_im[l], d_s5[l], w_glu[l], b_glu[l], g_q_lat[l], w_uq[l], g_kv_lat[l], w_ukv[l],
                   g_qn_nope[l], g_qn_rope[l], g_kn_nope[l], g_kn_rope[l], g_s5_out[l], g_mla_out[l],
                   w_out[l], g_ffn[l], w_router[l], b_router[l])
        res = _layer(yp, ys, cache_kv_latent[l], cache_k_rope[l], state_s5_re[l], state_s5_im[l], mixer_w,
                     w_mlp1[l], b_mlp1[l], w_mlp2[l], b_mlp2[l])
        yp, ys = res[0], res[1]
        for o, r in zip(outs, res[2:]):
            o.append(r)
    return (yp, ys) + tuple(jnp.stack(o) for o in outs)
```

```python
import functools
import math

import jax
import jax.numpy as jnp
import numpy as np
from jax import lax
from jax.experimental import pallas as pl
from jax.experimental.pallas import tpu as pltpu
from jax.experimental.pallas import tpu_sc as plsc

F32 = jnp.float32
BF16 = jnp.bfloat16
U32 = jnp.uint32

D_MODEL = 1024
S5_WIDTH = 512
S5_GROUP = 16
S5_GROUPS = 32
S5_STATE = 64
MLA_HEADS = 8
QK_NOPE = 64
QK_ROPE = 32
QK_HEAD = QK_NOPE + QK_ROPE
V_HEAD = 64
MLA_WIDTH = MLA_HEADS * V_HEAD
Q_LORA = 384
KV_LORA = 256
ROPE_THETA = 10000.0
CHUNK = 64
N_EXPERTS = 32
TOP_K = 4
D_FF = D_MODEL
SWIGLU_LIMIT = 7.0
SWIGLU_ALPHA = 1.702
EPS = 1e-6

LANES = 128
HEAD_PAD = 128
D_IN_PAD = 1280
S5_CHUNKS = 4
S5_CHUNK_LANES = 1024
S5_STATE_LANES = S5_CHUNKS * S5_CHUNK_LANES
ROW_TILE = 1024
SUB_TILE = 512
TIME_TILE = 64
S5_STEPS = 32
MOE_ROWS = 512
MOE_COL_CHUNKS = 2
MOE_PAIR = 2
VMEM_LIMIT = 56 * 1024 * 1024
SC_CORES = 2
SC_SUBCORES = 16
SC_WORKERS = SC_CORES * SC_SUBCORES
SC_ROWS = 128
SC_GATHER_ROWS = 64
D_PACK = D_MODEL // 2


def _cparams(sem, vmem=None):
    return pltpu.CompilerParams(dimension_semantics=sem, vmem_limit_bytes=vmem)


def _rms(x, g, n=None):
    n = x.shape[-1] if n is None else n
    ms = jnp.sum(x * x, axis=-1, keepdims=True) * (1.0 / n)
    return x * lax.rsqrt(ms + EPS) * g


def _sub_tiles(rows):
    sub = min(SUB_TILE, rows)
    return [slice(r, r + sub) for r in range(0, rows, sub)]


def _pack_rows(x):
    lo = lax.bitcast_convert_type(x[:, :D_PACK].astype(BF16).astype(F32), U32)
    hi = lax.bitcast_convert_type(x[:, D_PACK:].astype(BF16).astype(F32), U32)
    return (lo >> 16) | (hi & jnp.uint32(0xFFFF0000))


def _unpack_rows(w):
    lo = lax.bitcast_convert_type(w << 16, F32)
    hi = lax.bitcast_convert_type(w & jnp.uint32(0xFFFF0000), F32)
    return lo, hi


def _prep_kernel(lr_ref, li_ref, ldt_ref, br_ref, bi_ref, ar_ref, ai_ref, bbr_ref, bbi_ref):
    lr = lr_ref[...]
    li = li_ref[...]
    dt = jnp.exp(ldt_ref[...])
    mag = jnp.exp(lr * dt)
    ar = mag * jnp.cos(li * dt)
    ai = mag * jnp.sin(li * dt)
    ar_ref[...] = ar
    ai_ref[...] = ai
    den = lr * lr + li * li
    cr = ((ar - 1.0) * lr + ai * li) / den
    ci = (ai * lr - (ar - 1.0) * li) / den
    br = br_ref[...]
    bi = bi_ref[...]
    bbr_ref[...] = cr[:, None, :] * br - ci[:, None, :] * bi
    bbi_ref[...] = cr[:, None, :] * bi + ci[:, None, :] * br


def _s5_prep(lam_re, lam_im, log_dt, b_re, b_im):
    g, n = lam_re.shape
    p = b_re.shape[-1]
    out = pl.pallas_call(
        _prep_kernel,
        out_shape=(jax.ShapeDtypeStruct((g, n), F32), jax.ShapeDtypeStruct((g, n), F32),
                   jax.ShapeDtypeStruct((g, p, n), F32), jax.ShapeDtypeStruct((g, p, n), F32)),
        name="s5_prep",
    )(lam_re, lam_im, log_dt.reshape(g, 1), jnp.swapaxes(b_re, 1, 2), jnp.swapaxes(b_im, 1, 2))
    return out


def _state_layout(re, im):
    lead = re.shape[:-2]
    re = re.reshape(lead + (S5_CHUNKS, 512))
    im = im.reshape(lead + (S5_CHUNKS, 512))
    return jnp.stack([re, im], axis=-2).reshape(lead + (S5_STATE_LANES,))


def _state_unlayout(h):
    lead = h.shape[:-1]
    h = h.reshape(lead + (S5_CHUNKS, 2, 512))
    re = h[..., 0, :].reshape(lead + (S5_GROUPS, S5_STATE))
    im = h[..., 1, :].reshape(lead + (S5_GROUPS, S5_STATE))
    return re, im


def _s5_weights(bbr, bbi, c_re, c_im):
    eye8 = jnp.eye(8, dtype=F32)

    def blockdiag(m):
        a, b = m.shape[1], m.shape[2]
        return (eye8[:, None, :, None] * m[:, :, None, :]).reshape(8 * a, 8 * b)

    wb, wc = [], []
    for c in range(S5_CHUNKS):
        sl = slice(8 * c, 8 * c + 8)
        wb.append(jnp.concatenate([blockdiag(bbr[sl]), blockdiag(bbi[sl])], axis=1))
        cr_t = jnp.swapaxes(c_re[sl], 1, 2)
        ci_t = jnp.swapaxes(c_im[sl], 1, 2)
        wc.append(jnp.concatenate([blockdiag(cr_t), -blockdiag(ci_t)], axis=0))
    return jnp.stack(wb).astype(BF16), jnp.stack(wc).astype(BF16)


def _in_proj_kernel(x_ref, gmix_ref, w_ref, gq_ref, gkv_ref, ck_ref, sk_ref,
                    u_ref, qn_ref, lat_ref, krope_ref, krot_ref):
    bsz, steps, _ = x_ref.shape
    nb = max(1, min(bsz, SUB_TILE // steps))
    ck, sk = ck_ref[...][None], sk_ref[...][None]
    for b0 in range(0, bsz, nb):
        bs = slice(b0, b0 + nb)
        rows = nb * steps
        xn = _rms(x_ref[bs].reshape(rows, D_MODEL), gmix_ref[...]).astype(BF16)
        z = jnp.dot(xn, w_ref[...], preferred_element_type=F32)
        for bl in range(nb):
            for c in range(S5_CHUNKS):
                u_ref[c, pl.ds(b0 + bl, steps, stride=bsz), :] = z[bl * steps:(bl + 1) * steps,
                                                                   c * LANES:(c + 1) * LANES]
        qn = _rms(z[:, S5_WIDTH:S5_WIDTH + Q_LORA], gq_ref[...]).astype(BF16)
        qn_ref[bs] = qn.reshape(nb, steps, Q_LORA)
        c0 = S5_WIDTH + Q_LORA
        lat_ref[bs] = _rms(z[:, c0:c0 + KV_LORA], gkv_ref[...]).reshape(nb, steps, KV_LORA)
        kp = z[:, c0 + KV_LORA:]
        lane = lax.broadcasted_iota(jnp.int32, kp.shape, 1)
        first_half = (lane % 64) < 16
        sw = jnp.where(first_half, pltpu.roll(kp, LANES - 16, axis=1), pltpu.roll(kp, 16, axis=1))
        kr = kp.reshape(nb, steps, LANES) * ck + sw.reshape(nb, steps, LANES) * sk
        krope_ref[bs] = kr[:, :, :QK_ROPE]
        krot_ref[bs] = jnp.where(lane.reshape(nb, steps, LANES) >= 64, kr, 0.0)


def _in_proj(x, g_mix, w_in_p, g_q_lat, g_kv_lat, ck, sk):
    bsz, length, _ = x.shape
    steps = min(TIME_TILE, length)
    full = lambda i: (0, 0)
    blk = lambda w: pl.BlockSpec((bsz, steps, w), lambda i: (0, i, 0))
    return pl.pallas_call(
        _in_proj_kernel,
        grid=(length // steps,),
        in_specs=[blk(D_MODEL),
                  pl.BlockSpec((1, D_MODEL), full),
                  pl.BlockSpec((D_MODEL, D_IN_PAD), full),
                  pl.BlockSpec((1, Q_LORA), full),
                  pl.BlockSpec((1, KV_LORA), full),
                  pl.BlockSpec((steps, LANES), lambda i: (i, 0)),
                  pl.BlockSpec((steps, LANES), lambda i: (i, 0))],
        out_specs=[pl.BlockSpec((S5_CHUNKS, steps * bsz, LANES), lambda i: (0, i, 0)),
                   blk(Q_LORA), blk(KV_LORA), blk(QK_ROPE), blk(LANES)],
        out_shape=(jax.ShapeDtypeStruct((S5_CHUNKS, length * bsz, LANES), F32),
                   jax.ShapeDtypeStruct((bsz, length, Q_LORA), BF16),
                   jax.ShapeDtypeStruct((bsz, length, KV_LORA), F32),
                   jax.ShapeDtypeStruct((bsz, length, QK_ROPE), F32),
                   jax.ShapeDtypeStruct((bsz, length, LANES), F32)),
        compiler_params=_cparams(("parallel",), VMEM_LIMIT),
        name="in_proj",
    )(x, g_mix, w_in_p, g_q_lat, g_kv_lat, ck, sk)


def _s5_kernel(u_ref, h0_ref, lam_ref, wb_ref, wc_ref, dskip_ref, wglu_ref, bglu_ref, gout_ref,
               y_ref, hlast_ref, bu_ref, h_ref, *, bsz, steps):
    c_id = pl.program_id(0)

    @pl.when(c_id == 0)
    def _():
        h_ref[...] = h0_ref[...]

    for c in range(S5_CHUNKS):
        bu_ref[:, c * S5_CHUNK_LANES:(c + 1) * S5_CHUNK_LANES] = jnp.dot(
            u_ref[c].astype(BF16), wb_ref[c], preferred_element_type=F32)

    for c in range(S5_CHUNKS):
        re = slice(c * S5_CHUNK_LANES, c * S5_CHUNK_LANES + 512)
        im = slice(c * S5_CHUNK_LANES + 512, (c + 1) * S5_CHUNK_LANES)
        lam_r = jnp.broadcast_to(lam_ref[:, re], (bsz, 512))
        lam_i = jnp.broadcast_to(lam_ref[:, im], (bsz, 512))

        hr, hi = h_ref[:, re], h_ref[:, im]
        for t in range(steps):
            rows = slice(t * bsz, (t + 1) * bsz)
            hr, hi = (lam_r * hr - lam_i * hi + bu_ref[rows, re],
                      lam_r * hi + lam_i * hr + bu_ref[rows, im])
            bu_ref[rows, re] = hr
            bu_ref[rows, im] = hi
        h_ref[:, re] = hr
        h_ref[:, im] = hi

    ys = []
    for c in range(S5_CHUNKS):
        hs = bu_ref[:, c * S5_CHUNK_LANES:(c + 1) * S5_CHUNK_LANES].astype(BF16)
        ys.append(jnp.dot(hs, wc_ref[c], preferred_element_type=F32))
    u = jnp.concatenate([u_ref[c] for c in range(S5_CHUNKS)], axis=1)
    y = jnp.concatenate(ys, axis=1) + dskip_ref[...] * u
    y = jax.nn.gelu(y)
    gate = jnp.dot(y.astype(BF16), wglu_ref[...], preferred_element_type=F32) + bglu_ref[...]
    y = _rms(y * jax.nn.sigmoid(gate), gout_ref[...])
    for c in range(S5_CHUNKS):
        y_ref[c] = y[:, c * LANES:(c + 1) * LANES]

    @pl.when(c_id == pl.num_programs(0) - 1)
    def _():
        hlast_ref[...] = h_ref[...]


def _s5(u_tm, h0, lam, wb, wc, d_skip, w_glu, b_glu, g_out, bsz, length):
    steps = min(S5_STEPS, length)
    rows = steps * bsz
    full2 = lambda c: (0, 0)
    full3 = lambda c: (0, 0, 0)
    return pl.pallas_call(
        functools.partial(_s5_kernel, bsz=bsz, steps=steps),
        grid=(length // steps,),
        in_specs=[pl.BlockSpec((S5_CHUNKS, rows, LANES), lambda c: (0, c, 0)),
                  pl.BlockSpec((bsz, S5_STATE_LANES), full2),
                  pl.BlockSpec((1, S5_STATE_LANES), full2),
                  pl.BlockSpec((S5_CHUNKS, LANES, S5_CHUNK_LANES), full3),
                  pl.BlockSpec((S5_CHUNKS, S5_CHUNK_LANES, LANES), full3),
                  pl.BlockSpec((1, S5_WIDTH), full2),
                  pl.BlockSpec((S5_WIDTH, S5_WIDTH), full2),
                  pl.BlockSpec((1, S5_WIDTH), full2),
                  pl.BlockSpec((1, S5_WIDTH), full2)],
        out_specs=[pl.BlockSpec((S5_CHUNKS, rows, LANES), lambda c: (0, c, 0)),
                   pl.BlockSpec((bsz, S5_STATE_LANES), full2)],
        out_shape=(jax.ShapeDtypeStruct((S5_CHUNKS, length * bsz, LANES), F32),
                   jax.ShapeDtypeStruct((bsz, S5_STATE_LANES), F32)),
        scratch_shapes=[pltpu.VMEM((rows, S5_STATE_LANES), F32),
                        pltpu.VMEM((bsz, S5_STATE_LANES), F32)],
        compiler_params=_cparams(("arbitrary",), VMEM_LIMIT),
        name="s5_mixer",
    )(u_tm, h0, lam, wb, wc, d_skip, w_glu, b_glu, g_out)


def _q_proj_kernel(qn_ref, w_ref, wsw_ref, ones_ref, cq_ref, sq_ref, g_ref, q_ref):
    g = g_ref[...] * (QK_HEAD ** -0.5)
    for rs in _sub_tiles(qn_ref.shape[0]):
        qn = qn_ref[rs, :]
        qf = jnp.dot(qn, w_ref[...], preferred_element_type=F32)
        qs = jnp.dot(qn, wsw_ref[...], preferred_element_type=F32)
        cq, sq = cq_ref[rs, :], sq_ref[rs, :]
        for p in range(MLA_HEADS // 2):
            cols = slice(2 * p * HEAD_PAD, 2 * (p + 1) * HEAD_PAD)
            xr = qf[:, cols] * cq + qs[:, cols] * sq
            sqr = xr * xr
            hi = sqr.astype(BF16)
            lo = (sqr - hi.astype(F32)).astype(BF16)
            ss = (jnp.dot(hi, ones_ref[...], preferred_element_type=F32)
                  + jnp.dot(lo, ones_ref[...], preferred_element_type=F32))
            q = (xr * lax.rsqrt(ss * (1.0 / QK_HEAD) + EPS) * g).astype(BF16)
            q_ref[0, 2 * p, rs, :] = q[:, :HEAD_PAD]
            q_ref[0, 2 * p + 1, rs, :] = q[:, HEAD_PAD:]


def _q_proj(qn, w_uq_p, w_uq_sw, cq, sq, gq, bsz, length):
    tm = min(ROW_TILE, length)
    n_l = length // tm
    full = lambda b, i: (0, 0)
    pair = 2 * HEAD_PAD
    two = lambda a: jnp.concatenate([a, a], axis=1)
    ones2 = (jnp.arange(pair)[:, None] // HEAD_PAD == jnp.arange(pair)[None, :] // HEAD_PAD).astype(BF16)
    return pl.pallas_call(
        _q_proj_kernel,
        grid=(bsz, n_l),
        in_specs=[pl.BlockSpec((tm, Q_LORA), lambda b, i: (b * n_l + i, 0)),
                  pl.BlockSpec((Q_LORA, MLA_HEADS * HEAD_PAD), full),
                  pl.BlockSpec((Q_LORA, MLA_HEADS * HEAD_PAD), full),
                  pl.BlockSpec((pair, pair), full),
                  pl.BlockSpec((tm, pair), lambda b, i: (i, 0)),
                  pl.BlockSpec((tm, pair), lambda b, i: (i, 0)),
                  pl.BlockSpec((1, pair), full)],
        out_specs=pl.BlockSpec((1, MLA_HEADS, tm, HEAD_PAD), lambda b, i: (b, 0, i, 0)),
        out_shape=jax.ShapeDtypeStruct((bsz, MLA_HEADS, length, HEAD_PAD), BF16),
        compiler_params=_cparams(("parallel", "parallel"), VMEM_LIMIT),
        name="q_proj",
    )(qn, w_uq_p, w_uq_sw, ones2, two(cq), two(sq), two(gq))


def _kv_proj_kernel(lat_ref, krot_ref, wk_ref, wv_ref, g_ref, k_ref, v_ref):
    g = g_ref[...]
    for rs in _sub_tiles(lat_ref.shape[0]):
        lat = lat_ref[rs, :].astype(BF16)
        kf = jnp.dot(lat, wk_ref[...], preferred_element_type=F32)
        vf = jnp.dot(lat, wv_ref[...], preferred_element_type=F32)
        krot = krot_ref[rs, :]
        for h in range(MLA_HEADS):
            k = kf[:, h * HEAD_PAD:(h + 1) * HEAD_PAD] + krot
            k_ref[0, h, rs, :] = _rms(k, g, QK_HEAD).astype(BF16)
        for p in range(MLA_HEADS // 2):
            v_ref[0, p, rs, :] = vf[:, p * LANES:(p + 1) * LANES].astype(BF16)


def _kv_proj(lat, krot, w_k, w_v, gk, bsz, length):
    tm = min(ROW_TILE, length)
    n_l = length // tm
    full = lambda b, i: (0, 0)
    row = lambda b, i: (b * n_l + i, 0)
    return pl.pallas_call(
        _kv_proj_kernel,
        grid=(bsz, n_l),
        in_specs=[pl.BlockSpec((tm, KV_LORA), row),
                  pl.BlockSpec((tm, HEAD_PAD), row),
                  pl.BlockSpec((KV_LORA, MLA_HEADS * HEAD_PAD), full),
                  pl.BlockSpec((KV_LORA, MLA_WIDTH), full),
                  pl.BlockSpec((1, HEAD_PAD), full)],
        out_specs=[pl.BlockSpec((1, MLA_HEADS, tm, HEAD_PAD), lambda b, i: (b, 0, i, 0)),
                   pl.BlockSpec((1, MLA_HEADS // 2, tm, LANES), lambda b, i: (b, 0, i, 0))],
        out_shape=(jax.ShapeDtypeStruct((bsz, MLA_HEADS, length, HEAD_PAD), BF16),
                   jax.ShapeDtypeStruct((bsz, MLA_HEADS // 2, length, LANES), BF16)),
        compiler_params=_cparams(("parallel", "parallel"), VMEM_LIMIT),
        name="kv_proj",
    )(lat, krot, w_k, w_v, gk)


def _scores(q, k):
    return lax.dot_general(q, k, (((1,), (1,)), ((), ())), preferred_element_type=F32)


def _attn_prompt_kernel(q_ref, k_ref, v_ref, o_ref, *, length, tq):
    n_q = length // tq
    row = lax.broadcasted_iota(jnp.int32, (tq, tq), 0)
    col = lax.broadcasted_iota(jnp.int32, (tq, tq), 1)
    visible = (col // CHUNK) <= (row // CHUNK)
    lane = lax.broadcasted_iota(jnp.int32, (tq, LANES), 1)
    for qi in range(n_q):
        q0 = qi * tq
        outs = []
        for hh in range(2):
            q = q_ref[0, hh, q0:q0 + tq, :]
            sd = jnp.where(visible, _scores(q, k_ref[0, hh, q0:q0 + tq, :]), -jnp.inf)
            m = jnp.max(sd, axis=-1, keepdims=True)
            if qi:
                so = _scores(q, k_ref[0, hh, 0:q0, :])
                m = jnp.maximum(m, jnp.max(so, axis=-1, keepdims=True))
            pd = jnp.exp(sd - m)
            l = jnp.sum(pd, axis=-1, keepdims=True)
            acc = jnp.dot(pd.astype(BF16), v_ref[0, 0, q0:q0 + tq, :], preferred_element_type=F32)
            if qi:
                po = jnp.exp(so - m)
                l = l + jnp.sum(po, axis=-1, keepdims=True)
                acc = acc + jnp.dot(po.astype(BF16), v_ref[0, 0, 0:q0, :], preferred_element_type=F32)
            outs.append(acc / l)
        o_ref[0, q0:q0 + tq, :] = jnp.where(lane < V_HEAD, outs[0], outs[1])


def _attn_prompt(q, k, v, bsz, length):
    tq = min(512, length)
    pairs = MLA_HEADS // 2
    return pl.pallas_call(
        functools.partial(_attn_prompt_kernel, length=length, tq=tq),
        grid=(bsz, pairs),
        in_specs=[pl.BlockSpec((1, 2, length, HEAD_PAD), lambda b, p: (b, p, 0, 0)),
                  pl.BlockSpec((1, 2, length, HEAD_PAD), lambda b, p: (b, p, 0, 0)),
                  pl.BlockSpec((1, 1, length, LANES), lambda b, p: (b, p, 0, 0))],
        out_specs=pl.BlockSpec((1, length, LANES), lambda b, p: (b, 0, p)),
        out_shape=jax.ShapeDtypeStruct((bsz, length, MLA_WIDTH), F32),
        compiler_params=_cparams(("parallel", "parallel"), VMEM_LIMIT),
        name="attn_prompt",
    )(q, k, v)


def _attn_sample_kernel(q_ref, kc_ref, vc_ref, kn_ref, vn_ref, o_ref):
    lq = q_ref.shape[2]
    lane = lax.broadcasted_iota(jnp.int32, (lq, LANES), 1)
    outs = []
    for hh in range(2):
        q = q_ref[0, hh]
        sc = _scores(q, kc_ref[0, hh])
        sn = _scores(q, kn_ref[0, hh])
        m = jnp.maximum(jnp.max(sc, axis=-1, keepdims=True), jnp.max(sn, axis=-1, keepdims=True))
        pc = jnp.exp(sc - m)
        pn = jnp.exp(sn - m)
        l = jnp.sum(pc, axis=-1, keepdims=True) + jnp.sum(pn, axis=-1, keepdims=True)
        acc = (jnp.dot(pc.astype(BF16), vc_ref[0, 0], preferred_element_type=F32)
               + jnp.dot(pn.astype(BF16), vn_ref[0, 0], preferred_element_type=F32))
        outs.append(acc / l)
    o_ref[0] = jnp.where(lane < V_HEAD, outs[0], outs[1])


def _attn_sample(q, kc, vc, kn, vn, bsz, lq, past):
    pairs = MLA_HEADS // 2
    hp = lambda b, p: (b, p, 0, 0)
    return pl.pallas_call(
        _attn_sample_kernel,
        grid=(bsz, pairs),
        in_specs=[pl.BlockSpec((1, 2, lq, HEAD_PAD), hp),
                  pl.BlockSpec((1, 2, past, HEAD_PAD), hp),
                  pl.BlockSpec((1, 1, past, LANES), hp),
                  pl.BlockSpec((1, 2, lq, HEAD_PAD), hp),
                  pl.BlockSpec((1, 1, lq, LANES), hp)],
        out_specs=pl.BlockSpec((1, lq, LANES), lambda b, p: (b, 0, p)),
        out_shape=jax.ShapeDtypeStruct((bsz, lq, MLA_WIDTH), F32),
        compiler_params=_cparams(("parallel", "parallel"), VMEM_LIMIT),
        name="attn_sample",
    )(q, kc, vc, kn, vn)


def _out_proj_kernel(x_ref, s5_ref, at_ref, gmla_ref, wout_ref, gffn_ref, wr_ref, br_ref, tri_ref, cin_ref,
                     x1_ref, hn_ref, route_ref, gate_ref, cout_ref, run_ref):
    @pl.when(pl.program_id(0) == 0)
    def _():
        run_ref[...] = cin_ref[...]

    bsz, steps, _ = x_ref.shape
    nb = max(1, min(bsz, SUB_TILE // steps))
    for b0 in range(0, bsz, nb):
        bs = slice(b0, b0 + nb)
        rows = nb * steps
        s5 = jnp.concatenate(
            [jnp.concatenate([s5_ref[c, pl.ds(b0 + bl, steps, stride=bsz), :] for c in range(S5_CHUNKS)], axis=1)
             for bl in range(nb)], axis=0)
        an = _rms(at_ref[bs].reshape(rows, MLA_WIDTH), gmla_ref[...])
        merged = jnp.concatenate([s5.astype(BF16), an.astype(BF16)], axis=1)
        x1 = x_ref[bs].reshape(rows, D_MODEL) + jnp.dot(merged, wout_ref[...], preferred_element_type=F32)
        x1_ref[bs] = x1.reshape(nb, steps, D_MODEL)
        hn = _rms(x1, gffn_ref[...])
        hn_ref[bs] = _pack_rows(hn).reshape(nb, steps, D_PACK)
        hn_hi = hn.astype(BF16)
        hn_lo = (hn - hn_hi.astype(F32)).astype(BF16)
        parts = (jnp.dot(hn_hi, wr_ref[...], preferred_element_type=F32)
                 + jnp.dot(hn_lo, wr_ref[...], preferred_element_type=F32))
        logits = parts[:, :LANES] + parts[:, LANES:] + br_ref[...]
        lane = lax.broadcasted_iota(jnp.int32, logits.shape, 1)
        lane_f = lane.astype(F32)
        work = jnp.where(lane < N_EXPERTS, logits, -jnp.inf)
        vals, idxs = [], []
        for _ in range(TOP_K):
            m = jnp.max(work, axis=-1, keepdims=True)
            i = jnp.min(jnp.where(work == m, lane_f, float(LANES)), axis=-1, keepdims=True)
            vals.append(m)
            idxs.append(i)
            work = jnp.where(lane_f == i, -jnp.inf, work)
        es = [jnp.exp(v - vals[0]) for v in vals]
        den = es[0] + es[1] + es[2] + es[3]
        onehots = [(lane_f == idxs[k]).astype(F32) for k in range(TOP_K)]
        e_all = onehots[0] + onehots[1] + onehots[2] + onehots[3]
        before = jnp.dot(tri_ref[...], e_all.astype(BF16), preferred_element_type=F32) + run_ref[...]
        run_ref[...] = run_ref[...] + jnp.sum(e_all, axis=0, keepdims=True)
        route = jnp.zeros(logits.shape, F32)
        gate_out = jnp.zeros(logits.shape, F32)
        for k in range(TOP_K):
            rank = jnp.sum(onehots[k] * before, axis=-1, keepdims=True)
            route = jnp.where(lane == k, idxs[k], route)
            route = jnp.where(lane == TOP_K + k, rank, route)
            gate_out = jnp.where(lane == k, es[k] / den, gate_out)
        route_ref[bs] = route.astype(jnp.int32).reshape(nb, steps, LANES)
        gate_ref[bs] = gate_out.reshape(nb, steps, LANES)
    cout_ref[...] = run_ref[...]


def _out_proj(x, s5n_tm, attn, g_mla, w_out, g_ffn, w_r, b_r_p, counts_in):
    bsz, length, _ = x.shape
    steps = min(TIME_TILE, length)
    full = lambda i: (0, 0)
    blk = lambda w: pl.BlockSpec((bsz, steps, w), lambda i: (0, i, 0))
    sub = max(1, min(bsz, SUB_TILE // steps)) * steps
    tri = (jnp.arange(sub)[:, None] > jnp.arange(sub)[None, :]).astype(BF16)
    return pl.pallas_call(
        _out_proj_kernel,
        grid=(length // steps,),
        in_specs=[blk(D_MODEL),
                  pl.BlockSpec((S5_CHUNKS, steps * bsz, LANES), lambda i: (0, i, 0)),
                  blk(MLA_WIDTH),
                  pl.BlockSpec((1, MLA_WIDTH), full),
                  pl.BlockSpec((D_MODEL, D_MODEL), full),
                  pl.BlockSpec((1, D_MODEL), full),
                  pl.BlockSpec((D_MODEL, 2 * LANES), full),
                  pl.BlockSpec((1, LANES), full),
                  pl.BlockSpec((sub, sub), full),
                  pl.BlockSpec((1, LANES), full)],
        out_specs=[blk(D_MODEL), blk(D_PACK), blk(LANES), blk(LANES),
                   pl.BlockSpec((1, LANES), full)],
        out_shape=(jax.ShapeDtypeStruct((bsz, length, D_MODEL), F32),
                   jax.ShapeDtypeStruct((bsz, length, D_PACK), U32),
                   jax.ShapeDtypeStruct((bsz, length, LANES), jnp.int32),
                   jax.ShapeDtypeStruct((bsz, length, LANES), F32),
                   jax.ShapeDtypeStruct((1, LANES), F32)),
        scratch_shapes=[pltpu.VMEM((1, LANES), F32)],
        compiler_params=_cparams(("arbitrary",), VMEM_LIMIT),
        name="out_proj",
    )(x, s5n_tm, attn, g_mla, w_out, g_ffn, w_r, b_r_p, tri, counts_in)


def _sc_mesh():
    return plsc.VectorSubcoreMesh(core_axis_name="c", subcore_axis_name="s",
                                  num_cores=SC_CORES, num_subcores=SC_SUBCORES)


def _sc_for_chunks(n_chunks, fn):
    wid = lax.axis_index("s") * SC_CORES + lax.axis_index("c")
    full, rem = divmod(n_chunks, SC_WORKERS)
    if full:
        @pl.loop(0, full)
        def _(j):
            fn(j * SC_WORKERS + wid)
    if rem:
        @pl.when(wid < rem)
        def _():
            fn(full * SC_WORKERS + wid)


def _chunk_rows(c):
    return pl.ds(pl.multiple_of(c * SC_ROWS, SC_ROWS), SC_ROWS)


def _sc_dispatch_body(dp_hbm, ds_hbm, hp_hbm, hs_hbm, xs_hbm, idx_v, rows_v, sem):
    def chunk(d_hbm, h_hbm, c):
        pltpu.sync_copy(d_hbm.at[c], idx_v)
        pltpu.sync_copy(h_hbm.at[_chunk_rows(c)], rows_v)
        copies = [pltpu.async_copy(rows_v, xs_hbm.at[idx_v.at[k]], sem) for k in range(TOP_K)]
        for cp in copies:
            cp.wait()

    _sc_for_chunks(dp_hbm.shape[0], functools.partial(chunk, dp_hbm, hp_hbm))
    _sc_for_chunks(ds_hbm.shape[0], functools.partial(chunk, ds_hbm, hs_hbm))


def _dispatch(dest_p3, dest_s3, hn_p, hn_s, n_blocks):
    return pl.kernel(
        _sc_dispatch_body,
        out_type=jax.ShapeDtypeStruct((n_blocks * MOE_ROWS, D_PACK), U32),
        mesh=_sc_mesh(),
        scratch_types=[pltpu.VMEM((TOP_K, SC_ROWS), jnp.int32), pltpu.VMEM((SC_ROWS, D_PACK), U32),
                       pltpu.SemaphoreType.DMA],
        name="moe_dispatch_sc",
    )(dest_p3, dest_s3, hn_p, hn_s)


def _expert_kernel(be_ref, nb_ref, nv_ref, first_ref, slot_ref, nxt_ref,
                   x_ref, w1_hbm, b1_ref, w2_hbm, b2_ref, y_ref, w1f_ref, w2f_ref, w1b_ref, w2b_ref, sem):
    step = pl.program_id(0)

    def weight_copies(e):
        return (pltpu.make_async_copy(w1_hbm.at[e], w1f_ref, sem.at[0]),
                pltpu.make_async_copy(w2_hbm.at[e], w2f_ref, sem.at[1]))

    @pl.when(jnp.logical_and(step == 0, nb_ref[0] > 0))
    def _():
        for cp in weight_copies(be_ref[0]):
            cp.start()

    for sb in range(MOE_PAIR):
        b = step * MOE_PAIR + sb

        @pl.when(jnp.logical_and(b < nb_ref[0], first_ref[b] == 1))
        def _(b=b):
            s = slot_ref[b]
            for cp in weight_copies(be_ref[b]):
                cp.wait()
            w1b_ref[s] = w1f_ref[...].astype(BF16)
            w2b_ref[s] = w2f_ref[...].astype(BF16)

            @pl.when(nxt_ref[b] >= 0)
            def _():
                for cp in weight_copies(nxt_ref[b]):
                    cp.start()

    @pl.when(step * MOE_PAIR < nb_ref[0])
    def _():
        cw = D_FF // MOE_COL_CHUNKS
        for sb in range(MOE_PAIR):
            b = step * MOE_PAIR + sb
            e, s = be_ref[b], slot_ref[b]
            rs = slice(sb * MOE_ROWS, (sb + 1) * MOE_ROWS)
            live = lax.broadcasted_iota(jnp.int32, (MOE_ROWS, D_PACK), 0) < nv_ref[b]
            lo, hi = _unpack_rows(jnp.where(live, x_ref[rs, :], jnp.uint32(0)))
            lo, hi = lo.astype(BF16), hi.astype(BF16)
            b1 = b1_ref[e]

            def up(c0, lo=lo, hi=hi, s=s, b1=b1):
                return (jnp.dot(lo, w1b_ref[s, :D_PACK, c0:c0 + cw], preferred_element_type=F32)
                        + jnp.dot(hi, w1b_ref[s, D_PACK:, c0:c0 + cw], preferred_element_type=F32)
                        + b1[:, c0:c0 + cw])

            y = b2_ref[e]
            for j in range(MOE_COL_CHUNKS):
                gate = jnp.minimum(up(j * cw), SWIGLU_LIMIT)
                lin = jnp.clip(up(D_FF + j * cw), -SWIGLU_LIMIT, SWIGLU_LIMIT)
                act = gate * jax.nn.sigmoid(SWIGLU_ALPHA * gate) * (lin + 1.0)
                y = y + jnp.dot(act.astype(BF16), w2b_ref[s, j * cw:(j + 1) * cw, :],
                                preferred_element_type=F32)
            y_ref[rs, :] = _pack_rows(y)

    @pl.when(step * MOE_PAIR >= nb_ref[0])
    def _():
        y_ref[...] = jnp.zeros(y_ref.shape, y_ref.dtype)


def _experts(tables, xs, w1, b1, w2, b2, n_blocks):
    rows = MOE_PAIR * MOE_ROWS
    last = lambda p, be, nb, *_: (jnp.maximum(jnp.minimum(p, (nb[0] - 1) // MOE_PAIR), 0), 0)
    whole = lambda p, *_: (0, 0, 0)
    return pl.pallas_call(
        _expert_kernel,
        grid_spec=pltpu.PrefetchScalarGridSpec(
            num_scalar_prefetch=6,
            grid=(n_blocks // MOE_PAIR,),
            in_specs=[pl.BlockSpec((rows, D_PACK), last),
                      pl.BlockSpec(memory_space=pl.ANY),
                      pl.BlockSpec((N_EXPERTS, 1, 2 * D_FF), whole),
                      pl.BlockSpec(memory_space=pl.ANY),
                      pl.BlockSpec((N_EXPERTS, 1, D_MODEL), whole)],
            out_specs=pl.BlockSpec((rows, D_PACK), lambda p, *_: (p, 0)),
            scratch_shapes=[pltpu.VMEM((D_MODEL, 2 * D_FF), F32), pltpu.VMEM((D_FF, D_MODEL), F32),
                            pltpu.VMEM((2, D_MODEL, 2 * D_FF), BF16), pltpu.VMEM((2, D_FF, D_MODEL), BF16),
                            pltpu.SemaphoreType.DMA((2,))]),
        out_shape=jax.ShapeDtypeStruct((n_blocks * MOE_ROWS, D_PACK), U32),
        compiler_params=_cparams(("arbitrary",), VMEM_LIMIT),
        name="moe_experts",
    )(*tables, xs, w1, b1.reshape(N_EXPERTS, 1, 2 * D_FF), w2, b2.reshape(N_EXPERTS, 1, D_MODEL))


def _sc_gather_body(dp_hbm, ds_hbm, yb_hbm, gp_hbm, gs_hbm, idx_v, rows_v, sems):
    n = SC_GATHER_ROWS

    def chunk(d_hbm, g_hbm, c):
        pltpu.sync_copy(d_hbm.at[c], idx_v)
        rows = pl.ds(pl.multiple_of(c * n, n), n)
        gather = lambda k: pltpu.async_copy(yb_hbm.at[idx_v.at[k]], rows_v.at[k % 2], sems.at[k % 2])
        cp = gather(0)
        for k in range(TOP_K):
            cp.wait()
            if k + 1 < TOP_K:
                cp = gather(k + 1)
            pltpu.sync_copy(rows_v.at[k % 2], g_hbm.at[k, rows])

    _sc_for_chunks(dp_hbm.shape[0], functools.partial(chunk, dp_hbm, gp_hbm))
    _sc_for_chunks(ds_hbm.shape[0], functools.partial(chunk, ds_hbm, gs_hbm))


def _gather_expert_rows(dest_p3, dest_s3, yb):
    n_p, n_s = dest_p3.shape[0] * SC_GATHER_ROWS, dest_s3.shape[0] * SC_GATHER_ROWS
    return pl.kernel(
        _sc_gather_body,
        out_type=(jax.ShapeDtypeStruct((TOP_K, n_p, D_PACK), U32),
                  jax.ShapeDtypeStruct((TOP_K, n_s, D_PACK), U32)),
        mesh=_sc_mesh(),
        scratch_types=[pltpu.VMEM((TOP_K, SC_GATHER_ROWS), jnp.int32),
                       pltpu.VMEM((2, SC_GATHER_ROWS, D_PACK), U32),
                       pltpu.SemaphoreType.DMA((2,))],
        name="moe_gather_sc",
    )(dest_p3, dest_s3, yb)


def _combine_kernel(g_ref, gate_ref, x1_ref, o_ref):
    gate = gate_ref[...]
    x1 = x1_ref[...]
    acc_lo, acc_hi = x1[:, :D_PACK], x1[:, D_PACK:]
    for k in range(TOP_K):
        lo, hi = _unpack_rows(g_ref[k])
        acc_lo = acc_lo + gate[:, k:k + 1] * lo
        acc_hi = acc_hi + gate[:, k:k + 1] * hi
    o_ref[:, :D_PACK] = acc_lo
    o_ref[:, D_PACK:] = acc_hi


def _combine(g, gates, x1):
    n_tok = x1.shape[0]
    tm = min(512, n_tok)
    return pl.pallas_call(
        _combine_kernel,
        grid=(n_tok // tm,),
        in_specs=[pl.BlockSpec((TOP_K, tm, D_PACK), lambda i: (0, i, 0)),
                  pl.BlockSpec((tm, LANES), lambda i: (i, 0)),
                  pl.BlockSpec((tm, D_MODEL), lambda i: (i, 0))],
        out_specs=pl.BlockSpec((tm, D_MODEL), lambda i: (i, 0)),
        out_shape=jax.ShapeDtypeStruct((n_tok, D_MODEL), F32),
        compiler_params=_cparams(("parallel",), VMEM_LIMIT),
        name="moe_combine",
    )(g, gates, x1)


def _route_tables(counts_f, n_assign):
    counts = counts_f[0, :N_EXPERTS].astype(jnp.int32)
    padded = (counts + MOE_ROWS - 1) // MOE_ROWS * MOE_ROWS
    pad_end = jnp.cumsum(padded)
    pad_start = pad_end - padded
    n_blocks = -(-(n_assign + N_EXPERTS * (MOE_ROWS - 1)) // (MOE_ROWS * MOE_PAIR)) * MOE_PAIR
    row0 = jnp.arange(n_blocks, dtype=jnp.int32) * MOE_ROWS
    block_e = jnp.minimum(jnp.sum(row0[:, None] >= pad_end[None, :], axis=1), N_EXPERTS - 1).astype(jnp.int32)
    n_valid = jnp.clip((pad_start + counts)[block_e] - row0, 0, MOE_ROWS).astype(jnp.int32)
    nb_used = (pad_end[-1] // MOE_ROWS).astype(jnp.int32).reshape(1)
    e_ids = jnp.arange(N_EXPERTS, dtype=jnp.int32)
    nonempty = counts > 0
    slot_e = (jnp.cumsum(nonempty.astype(jnp.int32)) - 1) % 2
    later = jnp.where(nonempty[None, :] & (e_ids[None, :] > e_ids[:, None]), e_ids[None, :], N_EXPERTS)
    next_e = jnp.min(later, axis=1)
    next_e = jnp.where(next_e == N_EXPERTS, -1, next_e)
    first = (row0 == pad_start[block_e]).astype(jnp.int32)
    tables = (block_e, nb_used, n_valid, first, slot_e[block_e].astype(jnp.int32),
              next_e[block_e].astype(jnp.int32))
    return pad_start, tables, n_blocks


def _dest(route, pad_start):
    idx, rank = route[:, :TOP_K], route[:, TOP_K:2 * TOP_K]
    onehot = idx[:, :, None] == jnp.arange(N_EXPERTS, dtype=jnp.int32)[None, None, :]
    return rank + jnp.sum(jnp.where(onehot, pad_start[None, None, :], 0), axis=-1)


def _chunked(dest, rows):
    return dest.reshape(-1, rows, TOP_K).transpose(0, 2, 1)


def _rope_tables(offset, length):
    f32 = np.float32
    pos = f32(offset) + np.arange(length, dtype=f32)
    inv = np.power(f32(ROPE_THETA), -np.arange(QK_ROPE // 2, dtype=f32) * f32(2.0 / QK_ROPE))
    ang = pos[:, None] * inv[None, :]
    cos, sin = np.cos(ang), np.sin(ang)
    z32 = np.zeros((length, 32), f32)
    rot_c = np.concatenate([cos, cos], axis=1)
    rot_s = np.concatenate([-sin, sin], axis=1)
    ck = np.concatenate([rot_c, z32, rot_c, z32], axis=1)
    sk = np.concatenate([rot_s, z32, rot_s, z32], axis=1)
    cq = np.concatenate([np.ones((length, QK_NOPE), f32), rot_c, z32], axis=1)
    sq = np.concatenate([np.zeros((length, QK_NOPE), f32), rot_s, z32], axis=1)
    return tuple(jnp.asarray(t, F32) for t in (ck, sk, cq, sq))


def _head_gain(g_nope, g_rope):
    return jnp.concatenate([g_nope, g_rope, g_rope, jnp.zeros((HEAD_PAD - QK_HEAD,), F32)]).reshape(1, HEAD_PAD)


def _mixer_stage(x, past, prm, counts_in):
    bsz, length, _ = x.shape
    offset = 0 if past is None else past[0].shape[1]
    ck, sk, cq, sq = _rope_tables(offset, length)
    u_tm, qn, lat, k_rope, krot = _in_proj(x, prm['g_mix'], prm['w_in_p'], prm['g_q_lat'], prm['g_kv_lat'], ck, sk)
    if past is None:
        h0 = jnp.zeros((bsz, S5_STATE_LANES), F32)
    else:
        h0 = _state_layout(past[2], past[3])
    s5n_tm, h_last = _s5(u_tm, h0, prm['lam'], prm['wb'], prm['wc'],
                         prm['d_skip'], prm['w_glu'], prm['b_glu'], prm['g_s5_out'], bsz, length)
    h_re, h_im = _state_unlayout(h_last)
    t = bsz * length
    q = _q_proj(qn.reshape(t, Q_LORA), prm['w_uq_p'], prm['w_uq_sw'], cq, sq, prm['gq'], bsz, length)
    k, v = _kv_proj(lat.reshape(t, KV_LORA), krot.reshape(t, LANES), prm['w_k'], prm['w_v'], prm['gk'],
                    bsz, length)
    if past is None:
        attn = _attn_prompt(q, k, v, bsz, length)
    else:
        n_past = past[0].shape[1]
        c_lat = past[0].reshape(bsz * n_past, KV_LORA)
        c_rot = jnp.pad(past[1].reshape(bsz * n_past, QK_ROPE), ((0, 0), (QK_NOPE, HEAD_PAD - QK_HEAD)))
        kc, vc = _kv_proj(c_lat, c_rot, prm['w_k'], prm['w_v'], prm['gk'], bsz, n_past)
        attn = _attn_sample(q, kc, vc, k, v, bsz, length, n_past)
    x1, hn, route, gates, counts = _out_proj(x, s5n_tm, attn, prm['g_mla_out'], prm['w_out'], prm['g_ffn'],
                                             prm['w_r'], prm['b_r_p'], counts_in)
    return (x1.reshape(t, D_MODEL), hn.reshape(t, D_PACK), route.reshape(t, LANES), gates.reshape(t, LANES),
            counts, lat, k_rope, h_re, h_im)


def _prepare(g_mix, w_in, lam_re, lam_im, log_dt, b_s5_re, b_s5_im, c_s5_re, c_s5_im, d_s5, w_glu, b_glu,
             g_q_lat, w_uq, g_kv_lat, w_ukv, g_qn_nope, g_qn_rope, g_kn_nope, g_kn_rope, g_s5_out,
             g_mla_out, w_out, g_ffn, w_router, b_router):
    c0 = S5_WIDTH + Q_LORA + KV_LORA
    w_pe = w_in[:, c0:]
    z = jnp.zeros((D_MODEL, 32), F32)
    w_in_p = jnp.concatenate([w_in[:, :c0], w_pe, z, w_pe, z], axis=1).astype(BF16)
    ar, ai, bbr, bbi = _s5_prep(lam_re, lam_im, log_dt, b_s5_re, b_s5_im)
    wb, wc = _s5_weights(bbr, bbi, c_s5_re, c_s5_im)
    w_uq_p = jnp.pad(w_uq.reshape(Q_LORA, MLA_HEADS, QK_HEAD), ((0, 0), (0, 0), (0, HEAD_PAD - QK_HEAD)))
    r0, r1, r2 = QK_NOPE, QK_NOPE + QK_ROPE // 2, QK_HEAD
    w_uq_sw = jnp.zeros_like(w_uq_p).at[:, :, r0:r1].set(w_uq_p[:, :, r1:r2]).at[:, :, r1:r2].set(w_uq_p[:, :, r0:r1])
    w_kv = w_ukv.reshape(KV_LORA, MLA_HEADS, QK_NOPE + V_HEAD)
    w_k = jnp.pad(w_kv[:, :, :QK_NOPE], ((0, 0), (0, 0), (0, HEAD_PAD - QK_NOPE)))
    w_v = w_kv[:, :, QK_NOPE:]
    w_r_p = jnp.pad(w_router, ((0, 0), (0, LANES - N_EXPERTS)))
    w_r_hi = w_r_p.astype(BF16)
    return dict(
        g_mix=g_mix.reshape(1, -1), w_in_p=w_in_p,
        g_q_lat=g_q_lat.reshape(1, -1), g_kv_lat=g_kv_lat.reshape(1, -1),
        lam=_state_layout(ar, ai).reshape(1, S5_STATE_LANES), wb=wb, wc=wc,
        d_skip=d_s5.reshape(1, -1), w_glu=w_glu.astype(BF16), b_glu=b_glu.reshape(1, -1),
        g_s5_out=g_s5_out.reshape(1, -1),
        w_uq_p=w_uq_p.reshape(Q_LORA, MLA_HEADS * HEAD_PAD).astype(BF16),
        w_uq_sw=w_uq_sw.reshape(Q_LORA, MLA_HEADS * HEAD_PAD).astype(BF16),
        w_k=w_k.reshape(KV_LORA, MLA_HEADS * HEAD_PAD).astype(BF16),
        w_v=w_v.reshape(KV_LORA, MLA_WIDTH).astype(BF16),
        gq=_head_gain(g_qn_nope, g_qn_rope), gk=_head_gain(g_kn_nope, g_kn_rope),
        g_mla_out=g_mla_out.reshape(1, -1), w_out=w_out.astype(BF16), g_ffn=g_ffn.reshape(1, -1),
        w_r=jnp.concatenate([w_r_hi, (w_r_p - w_r_hi.astype(F32)).astype(BF16)], axis=1),
        b_r_p=jnp.pad(b_router, (0, LANES - N_EXPERTS)).reshape(1, LANES),
    )


def _layer(xp, xs, cache_lat, cache_kr, st_re, st_im, mixer_w, w1, b1, w2, b2):
    prm = _prepare(*mixer_w)
    bp, lp, _ = xp.shape
    bs, ls, _ = xs.shape
    zero_counts = jnp.zeros((1, LANES), F32)
    x1p, hnp, routep, gatep, counts_p, latp, krp, hrp, hip = _mixer_stage(xp, None, prm, zero_counts)
    x1s, hns, routes, gates, counts, lats, krs, hrs, his = _mixer_stage(
        xs, (cache_lat, cache_kr, st_re, st_im), prm, counts_p)
    n_tok = bp * lp + bs * ls
    pad_start, tables, n_blocks = _route_tables(counts, n_tok * TOP_K)
    dest_p = _dest(routep, pad_start)
    dest_s = _dest(routes, pad_start)
    xs_sorted = _dispatch(_chunked(dest_p, SC_ROWS), _chunked(dest_s, SC_ROWS), hnp, hns, n_blocks)
    yb = _experts(tables, xs_sorted, w1, b1, w2, b2, n_blocks)
    g_p, g_s = _gather_expert_rows(_chunked(dest_p, SC_GATHER_ROWS), _chunked(dest_s, SC_GATHER_ROWS), yb)
    yp = _combine(g_p, gatep, x1p).reshape(bp, lp, D_MODEL)
    ys = _combine(g_s, gates, x1s).reshape(bs, ls, D_MODEL)
    return yp, ys, latp, krp, hrp, hip, lats, krs, hrs, his


def kernel(x_prompt, x_sample, cache_kv_latent, cache_k_rope, state_s5_re, state_s5_im, g_mix, w_in, lam_re,
           lam_im, log_dt, b_s5_re, b_s5_im, c_s5_re, c_s5_im, d_s5, w_glu, b_glu, g_q_lat, w_uq, g_kv_lat,
           w_ukv, g_qn_nope, g_qn_rope, g_kn_nope, g_kn_rope, g_s5_out, g_mla_out, w_out, g_ffn, w_router,
           b_router, w_mlp1, b_mlp1, w_mlp2, b_mlp2):
    depth = g_mix.shape[0]
    yp, ys = x_prompt, x_sample
    outs = [[] for _ in range(8)]
    for l in range(depth):
        mixer_w = (g_mix[l], w_in[l], lam_re[l], lam_im[l], log_dt[l], b_s5_re[l], b_s5_im[l], c_s5_re[l],
                   c_s5_im[l], d_s5[l], w_glu[l], b_glu[l], g_q_lat[l], w_uq[l], g_kv_lat[l], w_ukv[l],
                   g_qn_nope[l], g_qn_rope[l], g_kn_nope[l], g_kn_rope[l], g_s5_out[l], g_mla_out[l],
                   w_out[l], g_ffn[l], w_router[l], b_router[l])
        res = _layer(yp, ys, cache_kv_latent[l], cache_k_rope[l], state_s5_re[l], state_s5_im[l], mixer_w,
                     w_mlp1[l], b_mlp1[l], w_mlp2[l], b_mlp2[l])
        yp, ys = res[0], res[1]
        for o, r in zip(outs, res[2:]):
            o.append(r)
    return (yp, ys) + tuple(jnp.stack(o) for o in outs)
```

```python
import functools
import math

import jax
import jax.numpy as jnp
import numpy as np
from jax import lax
from jax.experimental import pallas as pl
from jax.experimental.pallas import tpu as pltpu
from jax.experimental.pallas import tpu_sc as plsc

F32 = jnp.float32
BF16 = jnp.bfloat16
U32 = jnp.uint32

D_MODEL = 1024
S5_WIDTH = 512
S5_GROUP = 16
S5_GROUPS = 32
S5_STATE = 64
MLA_HEADS = 8
QK_NOPE = 64
QK_ROPE = 32
QK_HEAD = QK_NOPE + QK_ROPE
V_HEAD = 64
MLA_WIDTH = MLA_HEADS * V_HEAD
Q_LORA = 384
KV_LORA = 256
ROPE_THETA = 10000.0
CHUNK = 64
N_EXPERTS = 32
TOP_K = 4
D_FF = D_MODEL
SWIGLU_LIMIT = 7.0
SWIGLU_ALPHA = 1.702
EPS = 1e-6

LANES = 128
HEAD_PAD = 128
D_IN_PAD = 1280
S5_CHUNKS = 4
S5_CHUNK_LANES = 1024
S5_STATE_LANES = S5_CHUNKS * S5_CHUNK_LANES
ROW_TILE = 1024
SUB_TILE = 512
TIME_TILE = 64
S5_STEPS = 32
MOE_ROWS = 512
MOE_COL_CHUNKS = 2
MOE_PAIR = 2
VMEM_LIMIT = 56 * 1024 * 1024
SC_CORES = 2
SC_SUBCORES = 16
SC_WORKERS = SC_CORES * SC_SUBCORES
SC_ROWS = 128
SC_GATHER_ROWS = 64
D_PACK = D_MODEL // 2


def _cparams(sem, vmem=None):
    return pltpu.CompilerParams(dimension_semantics=sem, vmem_limit_bytes=vmem)


def _rms(x, g, n=None):
    n = x.shape[-1] if n is None else n
    ms = jnp.sum(x * x, axis=-1, keepdims=True) * (1.0 / n)
    return x * lax.rsqrt(ms + EPS) * g


def _sub_tiles(rows):
    sub = min(SUB_TILE, rows)
    return [slice(r, r + sub) for r in range(0, rows, sub)]


def _pack_rows(x):
    lo = lax.bitcast_convert_type(x[:, :D_PACK].astype(BF16).astype(F32), U32)
    hi = lax.bitcast_convert_type(x[:, D_PACK:].astype(BF16).astype(F32), U32)
    return (lo >> 16) | (hi & jnp.uint32(0xFFFF0000))


def _unpack_rows(w):
    lo = lax.bitcast_convert_type(w << 16, F32)
    hi = lax.bitcast_convert_type(w & jnp.uint32(0xFFFF0000), F32)
    return lo, hi


def _prep_kernel(lr_ref, li_ref, ldt_ref, br_ref, bi_ref, ar_ref, ai_ref, bbr_ref, bbi_ref):
    lr = lr_ref[...]
    li = li_ref[...]
    dt = jnp.exp(ldt_ref[...])
    mag = jnp.exp(lr * dt)
    ar = mag * jnp.cos(li * dt)
    ai = mag * jnp.sin(li * dt)
    ar_ref[...] = ar
    ai_ref[...] = ai
    den = lr * lr + li * li
    cr = ((ar - 1.0) * lr + ai * li) / den
    ci = (ai * lr - (ar - 1.0) * li) / den
    br = br_ref[...]
    bi = bi_ref[...]
    bbr_ref[...] = cr[:, None, :] * br - ci[:, None, :] * bi
    bbi_ref[...] = cr[:, None, :] * bi + ci[:, None, :] * br


def _s5_prep(lam_re, lam_im, log_dt, b_re, b_im):
    g, n = lam_re.shape
    p = b_re.shape[-1]
    out = pl.pallas_call(
        _prep_kernel,
        out_shape=(jax.ShapeDtypeStruct((g, n), F32), jax.ShapeDtypeStruct((g, n), F32),
                   jax.ShapeDtypeStruct((g, p, n), F32), jax.ShapeDtypeStruct((g, p, n), F32)),
        name="s5_prep",
    )(lam_re, lam_im, log_dt.reshape(g, 1), jnp.swapaxes(b_re, 1, 2), jnp.swapaxes(b_im, 1, 2))
    return out


def _state_layout(re, im):
    lead = re.shape[:-2]
    re = re.reshape(lead + (S5_CHUNKS, 512))
    im = im.reshape(lead + (S5_CHUNKS, 512))
    return jnp.stack([re, im], axis=-2).reshape(lead + (S5_STATE_LANES,))


def _state_unlayout(h):
    lead = h.shape[:-1]
    h = h.reshape(lead + (S5_CHUNKS, 2, 512))
    re = h[..., 0, :].reshape(lead + (S5_GROUPS, S5_STATE))
    im = h[..., 1, :].reshape(lead + (S5_GROUPS, S5_STATE))
    return re, im


def _s5_weights(bbr, bbi, c_re, c_im):
    eye8 = jnp.eye(8, dtype=F32)

    def blockdiag(m):
        a, b = m.shape[1], m.shape[2]
        return (eye8[:, None, :, None] * m[:, :, None, :]).reshape(8 * a, 8 * b)

    wb, wc = [], []
    for c in range(S5_CHUNKS):
        sl = slice(8 * c, 8 * c + 8)
        wb.append(jnp.concatenate([blockdiag(bbr[sl]), blockdiag(bbi[sl])], axis=1))
        cr_t = jnp.swapaxes(c_re[sl], 1, 2)
        ci_t = jnp.swapaxes(c_im[sl], 1, 2)
        wc.append(jnp.concatenate([blockdiag(cr_t), -blockdiag(ci_t)], axis=0))
    return jnp.stack(wb).astype(BF16), jnp.stack(wc).astype(BF16)


def _in_proj_kernel(x_ref, gmix_ref, w_ref, gq_ref, gkv_ref, ck_ref, sk_ref,
                    u_ref, qn_ref, lat_ref, krope_ref, krot_ref):
    bsz, steps, _ = x_ref.shape
    nb = max(1, min(bsz, SUB_TILE // steps))
    ck, sk = ck_ref[...][None], sk_ref[...][None]
    for b0 in range(0, bsz, nb):
        bs = slice(b0, b0 + nb)
        rows = nb * steps
        xn = _rms(x_ref[bs].reshape(rows, D_MODEL), gmix_ref[...]).astype(BF16)
        z = jnp.dot(xn, w_ref[...], preferred_element_type=F32)
        for bl in range(nb):
            for c in range(S5_CHUNKS):
                u_ref[c, pl.ds(b0 + bl, steps, stride=bsz), :] = z[bl * steps:(bl + 1) * steps,
                                                                   c * LANES:(c + 1) * LANES]
        qn = _rms(z[:, S5_WIDTH:S5_WIDTH + Q_LORA], gq_ref[...]).astype(BF16)
        qn_ref[bs] = qn.reshape(nb, steps, Q_LORA)
        c0 = S5_WIDTH + Q_LORA
        lat_ref[bs] = _rms(z[:, c0:c0 + KV_LORA], gkv_ref[...]).reshape(nb, steps, KV_LORA)
        kp = z[:, c0 + KV_LORA:]
        lane = lax.broadcasted_iota(jnp.int32, kp.shape, 1)
        first_half = (lane % 64) < 16
        sw = jnp.where(first_half, pltpu.roll(kp, LANES - 16, axis=1), pltpu.roll(kp, 16, axis=1))
        kr = kp.reshape(nb, steps, LANES) * ck + sw.reshape(nb, steps, LANES) * sk
        krope_ref[bs] = kr[:, :, :QK_ROPE]
        krot_ref[bs] = jnp.where(lane.reshape(nb, steps, LANES) >= 64, kr, 0.0)


def _in_proj(x, g_mix, w_in_p, g_q_lat, g_kv_lat, ck, sk):
    bsz, length, _ = x.shape
    steps = min(TIME_TILE, length)
    full = lambda i: (0, 0)
    blk = lambda w: pl.BlockSpec((bsz, steps, w), lambda i: (0, i, 0))
    return pl.pallas_call(
        _in_proj_kernel,
        grid=(length // steps,),
        in_specs=[blk(D_MODEL),
                  pl.BlockSpec((1, D_MODEL), full),
                  pl.BlockSpec((D_MODEL, D_IN_PAD), full),
                  pl.BlockSpec((1, Q_LORA), full),
                  pl.BlockSpec((1, KV_LORA), full),
                  pl.BlockSpec((steps, LANES), lambda i: (i, 0)),
                  pl.BlockSpec((steps, LANES), lambda i: (i, 0))],
        out_specs=[pl.BlockSpec((S5_CHUNKS, steps * bsz, LANES), lambda i: (0, i, 0)),
                   blk(Q_LORA), blk(KV_LORA), blk(QK_ROPE), blk(LANES)],
        out_shape=(jax.ShapeDtypeStruct((S5_CHUNKS, length * bsz, LANES), F32),
                   jax.ShapeDtypeStruct((bsz, length, Q_LORA), BF16),
                   jax.ShapeDtypeStruct((bsz, length, KV_LORA), F32),
                   jax.ShapeDtypeStruct((bsz, length, QK_ROPE), F32),
                   jax.ShapeDtypeStruct((bsz, length, LANES), F32)),
        compiler_params=_cparams(("parallel",), VMEM_LIMIT),
        name="in_proj",
    )(x, g_mix, w_in_p, g_q_lat, g_kv_lat, ck, sk)


def _s5_kernel(u_ref, h0_ref, lam_ref, wb_ref, wc_ref, dskip_ref, wglu_ref, bglu_ref, gout_ref,
               y_ref, hlast_ref, bu_ref, h_ref, *, bsz, steps):
    c_id = pl.program_id(0)

    @pl.when(c_id == 0)
    def _():
        h_ref[...] = h0_ref[...]

    for c in range(S5_CHUNKS):
        bu_ref[:, c * S5_CHUNK_LANES:(c + 1) * S5_CHUNK_LANES] = jnp.dot(
            u_ref[c].astype(BF16), wb_ref[c], preferred_element_type=F32)

    for c in range(S5_CHUNKS):
        re = slice(c * S5_CHUNK_LANES, c * S5_CHUNK_LANES + 512)
        im = slice(c * S5_CHUNK_LANES + 512, (c + 1) * S5_CHUNK_LANES)
        lam_r = jnp.broadcast_to(lam_ref[:, re], (bsz, 512))
        lam_i = jnp.broadcast_to(lam_ref[:, im], (bsz, 512))

        hr, hi = h_ref[:, re], h_ref[:, im]
        for t in range(steps):
            rows = slice(t * bsz, (t + 1) * bsz)
            hr, hi = (lam_r * hr - lam_i * hi + bu_ref[rows, re],
                      lam_r * hi + lam_i * hr + bu_ref[rows, im])
            bu_ref[rows, re] = hr
            bu_ref[rows, im] = hi
        h_ref[:, re] = hr
        h_ref[:, im] = hi

    ys = []
    for c in range(S5_CHUNKS):
        hs = bu_ref[:, c * S5_CHUNK_LANES:(c + 1) * S5_CHUNK_LANES].astype(BF16)
        ys.append(jnp.dot(hs, wc_ref[c], preferred_element_type=F32))
    u = jnp.concatenate([u_ref[c] for c in range(S5_CHUNKS)], axis=1)
    y = jnp.concatenate(ys, axis=1) + dskip_ref[...] * u
    y = jax.nn.gelu(y)
    gate = jnp.dot(y.astype(BF16), wglu_ref[...], preferred_element_type=F32) + bglu_ref[...]
    y = _rms(y * jax.nn.sigmoid(gate), gout_ref[...])
    for c in range(S5_CHUNKS):
        y_ref[c] = y[:, c * LANES:(c + 1) * LANES]

    @pl.when(c_id == pl.num_programs(0) - 1)
    def _():
        hlast_ref[...] = h_ref[...]


def _s5(u_tm, h0, lam, wb, wc, d_skip, w_glu, b_glu, g_out, bsz, length):
    steps = min(S5_STEPS, length)
    rows = steps * bsz
    full2 = lambda c: (0, 0)
    full3 = lambda c: (0, 0, 0)
    return pl.pallas_call(
        functools.partial(_s5_kernel, bsz=bsz, steps=steps),
        grid=(length // steps,),
        in_specs=[pl.BlockSpec((S5_CHUNKS, rows, LANES), lambda c: (0, c, 0)),
                  pl.BlockSpec((bsz, S5_STATE_LANES), full2),
                  pl.BlockSpec((1, S5_STATE_LANES), full2),
                  pl.BlockSpec((S5_CHUNKS, LANES, S5_CHUNK_LANES), full3),
                  pl.BlockSpec((S5_CHUNKS, S5_CHUNK_LANES, LANES), full3),
                  pl.BlockSpec((1, S5_WIDTH), full2),
                  pl.BlockSpec((S5_WIDTH, S5_WIDTH), full2),
                  pl.BlockSpec((1, S5_WIDTH), full2),
                  pl.BlockSpec((1, S5_WIDTH), full2)],
        out_specs=[pl.BlockSpec((S5_CHUNKS, rows, LANES), lambda c: (0, c, 0)),
                   pl.BlockSpec((bsz, S5_STATE_LANES), full2)],
        out_shape=(jax.ShapeDtypeStruct((S5_CHUNKS, length * bsz, LANES), F32),
                   jax.ShapeDtypeStruct((bsz, S5_STATE_LANES), F32)),
        scratch_shapes=[pltpu.VMEM((rows, S5_STATE_LANES), F32),
                        pltpu.VMEM((bsz, S5_STATE_LANES), F32)],
        compiler_params=_cparams(("arbitrary",), VMEM_LIMIT),
        name="s5_mixer",
    )(u_tm, h0, lam, wb, wc, d_skip, w_glu, b_glu, g_out)


def _q_proj_kernel(qn_ref, w_ref, wsw_ref, ones_ref, cq_ref, sq_ref, g_ref, q_ref):
    g = g_ref[...] * (QK_HEAD ** -0.5)
    for rs in _sub_tiles(qn_ref.shape[0]):
        qn = qn_ref[rs, :]
        qf = jnp.dot(qn, w_ref[...], preferred_element_type=F32)
        qs = jnp.dot(qn, wsw_ref[...], preferred_element_type=F32)
        cq, sq = cq_ref[rs, :], sq_ref[rs, :]
        for p in range(MLA_HEADS // 2):
            cols = slice(2 * p * HEAD_PAD, 2 * (p + 1) * HEAD_PAD)
            xr = qf[:, cols] * cq + qs[:, cols] * sq
            sqr = xr * xr
            hi = sqr.astype(BF16)
            lo = (sqr - hi.astype(F32)).astype(BF16)
            ss = (jnp.dot(hi, ones_ref[...], preferred_element_type=F32)
                  + jnp.dot(lo, ones_ref[...], preferred_element_type=F32))
            q = (xr * lax.rsqrt(ss * (1.0 / QK_HEAD) + EPS) * g).astype(BF16)
            q_ref[0, 2 * p, rs, :] = q[:, :HEAD_PAD]
            q_ref[0, 2 * p + 1, rs, :] = q[:, HEAD_PAD:]


def _q_proj(qn, w_uq_p, w_uq_sw, cq, sq, gq, bsz, length):
    tm = min(ROW_TILE, length)
    n_l = length // tm
    full = lambda b, i: (0, 0)
    pair = 2 * HEAD_PAD
    two = lambda a: jnp.concatenate([a, a], axis=1)
    ones2 = (jnp.arange(pair)[:, None] // HEAD_PAD == jnp.arange(pair)[None, :] // HEAD_PAD).astype(BF16)
    return pl.pallas_call(
        _q_proj_kernel,
        grid=(bsz, n_l),
        in_specs=[pl.BlockSpec((tm, Q_LORA), lambda b, i: (b * n_l + i, 0)),
                  pl.BlockSpec((Q_LORA, MLA_HEADS * HEAD_PAD), full),
                  pl.BlockSpec((Q_LORA, MLA_HEADS * HEAD_PAD), full),
                  pl.BlockSpec((pair, pair), full),
                  pl.BlockSpec((tm, pair), lambda b, i: (i, 0)),
                  pl.BlockSpec((tm, pair), lambda b, i: (i, 0)),
                  pl.BlockSpec((1, pair), full)],
        out_specs=pl.BlockSpec((1, MLA_HEADS, tm, HEAD_PAD), lambda b, i: (b, 0, i, 0)),
        out_shape=jax.ShapeDtypeStruct((bsz, MLA_HEADS, length, HEAD_PAD), BF16),
        compiler_params=_cparams(("parallel", "parallel"), VMEM_LIMIT),
        name="q_proj",
    )(qn, w_uq_p, w_uq_sw, ones2, two(cq), two(sq), two(gq))


def _kv_proj_kernel(lat_ref, krot_ref, wk_ref, wv_ref, g_ref, k_ref, v_ref):
    g = g_ref[...]
    for rs in _sub_tiles(lat_ref.shape[0]):
        lat = lat_ref[rs, :].astype(BF16)
        kf = jnp.dot(lat, wk_ref[...], preferred_element_type=F32)
        vf = jnp.dot(lat, wv_ref[...], preferred_element_type=F32)
        krot = krot_ref[rs, :]
        for h in range(MLA_HEADS):
            k = kf[:, h * HEAD_PAD:(h + 1) * HEAD_PAD] + krot
            k_ref[0, h, rs, :] = _rms(k, g, QK_HEAD).astype(BF16)
        for p in range(MLA_HEADS // 2):
            v_ref[0, p, rs, :] = vf[:, p * LANES:(p + 1) * LANES].astype(BF16)


def _kv_proj(lat, krot, w_k, w_v, gk, bsz, length):
    tm = min(ROW_TILE, length)
    n_l = length // tm
    full = lambda b, i: (0, 0)
    row = lambda b, i: (b * n_l + i, 0)
    return pl.pallas_call(
        _kv_proj_kernel,
        grid=(bsz, n_l),
        in_specs=[pl.BlockSpec((tm, KV_LORA), row),
                  pl.BlockSpec((tm, HEAD_PAD), row),
                  pl.BlockSpec((KV_LORA, MLA_HEADS * HEAD_PAD), full),
                  pl.BlockSpec((KV_LORA, MLA_WIDTH), full),
                  pl.BlockSpec((1, HEAD_PAD), full)],
        out_specs=[pl.BlockSpec((1, MLA_HEADS, tm, HEAD_PAD), lambda b, i: (b, 0, i, 0)),
                   pl.BlockSpec((1, MLA_HEADS // 2, tm, LANES), lambda b, i: (b, 0, i, 0))],
        out_shape=(jax.ShapeDtypeStruct((bsz, MLA_HEADS, length, HEAD_PAD), BF16),
                   jax.ShapeDtypeStruct((bsz, MLA_HEADS // 2, length, LANES), BF16)),
        compiler_params=_cparams(("parallel", "parallel"), VMEM_LIMIT),
        name="kv_proj",
    )(lat, krot, w_k, w_v, gk)


def _scores(q, k):
    return lax.dot_general(q, k, (((1,), (1,)), ((), ())), preferred_element_type=F32)


def _attn_prompt_kernel(q_ref, k_ref, v_ref, o_ref, *, length, tq):
    n_q = length // tq
    row = lax.broadcasted_iota(jnp.int32, (tq, tq), 0)
    col = lax.broadcasted_iota(jnp.int32, (tq, tq), 1)
    visible = (col // CHUNK) <= (row // CHUNK)
    lane = lax.broadcasted_iota(jnp.int32, (tq, LANES), 1)
    for qi in range(n_q):
        q0 = qi * tq
        outs = []
        for hh in range(2):
            q = q_ref[0, hh, q0:q0 + tq, :]
            sd = jnp.where(visible, _scores(q, k_ref[0, hh, q0:q0 + tq, :]), -jnp.inf)
            m = jnp.max(sd, axis=-1, keepdims=True)
            if qi:
                so = _scores(q, k_ref[0, hh, 0:q0, :])
                m = jnp.maximum(m, jnp.max(so, axis=-1, keepdims=True))
            pd = jnp.exp(sd - m)
            l = jnp.sum(pd, axis=-1, keepdims=True)
            acc = jnp.dot(pd.astype(BF16), v_ref[0, 0, q0:q0 + tq, :], preferred_element_type=F32)
            if qi:
                po = jnp.exp(so - m)
                l = l + jnp.sum(po, axis=-1, keepdims=True)
                acc = acc + jnp.dot(po.astype(BF16), v_ref[0, 0, 0:q0, :], preferred_element_type=F32)
            outs.append(acc / l)
        o_ref[0, q0:q0 + tq, :] = jnp.where(lane < V_HEAD, outs[0], outs[1])


def _attn_prompt(q, k, v, bsz, length):
    tq = min(512, length)
    pairs = MLA_HEADS // 2
    return pl.pallas_call(
        functools.partial(_attn_prompt_kernel, length=length, tq=tq),
        grid=(bsz, pairs),
        in_specs=[pl.BlockSpec((1, 2, length, HEAD_PAD), lambda b, p: (b, p, 0, 0)),
                  pl.BlockSpec((1, 2, length, HEAD_PAD), lambda b, p: (b, p, 0, 0)),
                  pl.BlockSpec((1, 1, length, LANES), lambda b, p: (b, p, 0, 0))],
        out_specs=pl.BlockSpec((1, length, LANES), lambda b, p: (b, 0, p)),
        out_shape=jax.ShapeDtypeStruct((bsz, length, MLA_WIDTH), F32),
        compiler_params=_cparams(("parallel", "parallel"), VMEM_LIMIT),
        name="attn_prompt",
    )(q, k, v)


def _attn_sample_kernel(q_ref, kc_ref, vc_ref, kn_ref, vn_ref, o_ref):
    lq = q_ref.shape[2]
    lane = lax.broadcasted_iota(jnp.int32, (lq, LANES), 1)
    outs = []
    for hh in range(2):
        q = q_ref[0, hh]
        sc = _scores(q, kc_ref[0, hh])
        sn = _scores(q, kn_ref[0, hh])
        m = jnp.maximum(jnp.max(sc, axis=-1, keepdims=True), jnp.max(sn, axis=-1, keepdims=True))
        pc = jnp.exp(sc - m)
        pn = jnp.exp(sn - m)
        l = jnp.sum(pc, axis=-1, keepdims=True) + jnp.sum(pn, axis=-1, keepdims=True)
        acc = (jnp.dot(pc.astype(BF16), vc_ref[0, 0], preferred_element_type=F32)
               + jnp.dot(pn.astype(BF16), vn_ref[0, 0], preferred_element_type=F32))
        outs.append(acc / l)
    o_ref[0] = jnp.where(lane < V_HEAD, outs[0], outs[1])


def _attn_sample(q, kc, vc, kn, vn, bsz, lq, past):
    pairs = MLA_HEADS // 2
    hp = lambda b, p: (b, p, 0, 0)
    return pl.pallas_call(
        _attn_sample_kernel,
        grid=(bsz, pairs),
        in_specs=[pl.BlockSpec((1, 2, lq, HEAD_PAD), hp),
                  pl.BlockSpec((1, 2, past, HEAD_PAD), hp),
                  pl.BlockSpec((1, 1, past, LANES), hp),
                  pl.BlockSpec((1, 2, lq, HEAD_PAD), hp),
                  pl.BlockSpec((1, 1, lq, LANES), hp)],
        out_specs=pl.BlockSpec((1, lq, LANES), lambda b, p: (b, 0, p)),
        out_shape=jax.ShapeDtypeStruct((bsz, lq, MLA_WIDTH), F32),
        compiler_params=_cparams(("parallel", "parallel"), VMEM_LIMIT),
        name="attn_sample",
    )(q, kc, vc, kn, vn)


def _out_proj_kernel(x_ref, s5_ref, at_ref, gmla_ref, wout_ref, gffn_ref, wr_ref, br_ref, tri_ref, cin_ref,
                     x1_ref, hn_ref, route_ref, gate_ref, cout_ref, run_ref):
    @pl.when(pl.program_id(0) == 0)
    def _():
        run_ref[...] = cin_ref[...]

    bsz, steps, _ = x_ref.shape
    nb = max(1, min(bsz, SUB_TILE // steps))
    for b0 in range(0, bsz, nb):
        bs = slice(b0, b0 + nb)
        rows = nb * steps
        s5 = jnp.concatenate(
            [jnp.concatenate([s5_ref[c, pl.ds(b0 + bl, steps, stride=bsz), :] for c in range(S5_CHUNKS)], axis=1)
             for bl in range(nb)], axis=0)
        an = _rms(at_ref[bs].reshape(rows, MLA_WIDTH), gmla_ref[...])
        merged = jnp.concatenate([s5.astype(BF16), an.astype(BF16)], axis=1)
        x1 = x_ref[bs].reshape(rows, D_MODEL) + jnp.dot(merged, wout_ref[...], preferred_element_type=F32)
        x1_ref[bs] = x1.reshape(nb, steps, D_MODEL)
        hn = _rms(x1, gffn_ref[...])
        hn_ref[bs] = _pack_rows(hn).reshape(nb, steps, D_PACK)
        hn_hi = hn.astype(BF16)
        hn_lo = (hn - hn_hi.astype(F32)).astype(BF16)
        parts = (jnp.dot(hn_hi, wr_ref[...], preferred_element_type=F32)
                 + jnp.dot(hn_lo, wr_ref[...], preferred_element_type=F32))
        logits = parts[:, :LANES] + parts[:, LANES:] + br_ref[...]
        lane = lax.broadcasted_iota(jnp.int32, logits.shape, 1)
        lane_f = lane.astype(F32)
        work = jnp.where(lane < N_EXPERTS, logits, -jnp.inf)
        vals, idxs = [], []
        for _ in range(TOP_K):
            m = jnp.max(work, axis=-1, keepdims=True)
            i = jnp.min(jnp.where(work == m, lane_f, float(LANES)), axis=-1, keepdims=True)
            vals.append(m)
            idxs.append(i)
            work = jnp.where(lane_f == i, -jnp.inf, work)
        es = [jnp.exp(v - vals[0]) for v in vals]
        den = es[0] + es[1] + es[2] + es[3]
        onehots = [(lane_f == idxs[k]).astype(F32) for k in range(TOP_K)]
        e_all = onehots[0] + onehots[1] + onehots[2] + onehots[3]
        before = jnp.dot(tri_ref[...], e_all.astype(BF16), preferred_element_type=F32) + run_ref[...]
        run_ref[...] = run_ref[...] + jnp.sum(e_all, axis=0, keepdims=True)
        route = jnp.zeros(logits.shape, F32)
        gate_out = jnp.zeros(logits.shape, F32)
        for k in range(TOP_K):
            rank = jnp.sum(onehots[k] * before, axis=-1, keepdims=True)
            route = jnp.where(lane == k, idxs[k], route)
            route = jnp.where(lane == 8 + k, rank, route)
            gate_out = jnp.where(lane == k, es[k] / den, gate_out)
        route_ref[bs] = route.astype(jnp.int32).reshape(nb, steps, LANES)
        gate_ref[bs] = gate_out.reshape(nb, steps, LANES)
    cout_ref[...] = run_ref[...]


def _out_proj(x, s5n_tm, attn, g_mla, w_out, g_ffn, w_r, b_r_p, counts_in):
    bsz, length, _ = x.shape
    steps = min(TIME_TILE, length)
    full = lambda i: (0, 0)
    blk = lambda w: pl.BlockSpec((bsz, steps, w), lambda i: (0, i, 0))
    sub = max(1, min(bsz, SUB_TILE // steps)) * steps
    tri = (jnp.arange(sub)[:, None] > jnp.arange(sub)[None, :]).astype(BF16)
    return pl.pallas_call(
        _out_proj_kernel,
        grid=(length // steps,),
        in_specs=[blk(D_MODEL),
                  pl.BlockSpec((S5_CHUNKS, steps * bsz, LANES), lambda i: (0, i, 0)),
                  blk(MLA_WIDTH),
                  pl.BlockSpec((1, MLA_WIDTH), full),
                  pl.BlockSpec((D_MODEL, D_MODEL), full),
                  pl.BlockSpec((1, D_MODEL), full),
                  pl.BlockSpec((D_MODEL, 2 * LANES), full),
                  pl.BlockSpec((1, LANES), full),
                  pl.BlockSpec((sub, sub), full),
                  pl.BlockSpec((1, LANES), full)],
        out_specs=[blk(D_MODEL), blk(D_PACK), blk(LANES), blk(LANES),
                   pl.BlockSpec((1, LANES), full)],
        out_shape=(jax.ShapeDtypeStruct((bsz, length, D_MODEL), F32),
                   jax.ShapeDtypeStruct((bsz, length, D_PACK), U32),
                   jax.ShapeDtypeStruct((bsz, length, LANES), jnp.int32),
                   jax.ShapeDtypeStruct((bsz, length, LANES), F32),
                   jax.ShapeDtypeStruct((1, LANES), F32)),
        scratch_shapes=[pltpu.VMEM((1, LANES), F32)],
        compiler_params=_cparams(("arbitrary",), VMEM_LIMIT),
        name="out_proj",
    )(x, s5n_tm, attn, g_mla, w_out, g_ffn, w_r, b_r_p, tri, counts_in)


def _sc_mesh():
    return plsc.VectorSubcoreMesh(core_axis_name="c", subcore_axis_name="s",
                                  num_cores=SC_CORES, num_subcores=SC_SUBCORES)


def _sc_for_chunks(n_chunks, fn):
    wid = lax.axis_index("s") * SC_CORES + lax.axis_index("c")
    full, rem = divmod(n_chunks, SC_WORKERS)
    if full:
        @pl.loop(0, full)
        def _(j):
            fn(j * SC_WORKERS + wid)
    if rem:
        @pl.when(wid < rem)
        def _():
            fn(full * SC_WORKERS + wid)


def _chunk_rows(c):
    return pl.ds(pl.multiple_of(c * SC_ROWS, SC_ROWS), SC_ROWS)


def _sc_dispatch_body(dp_hbm, ds_hbm, hp_hbm, hs_hbm, xs_hbm, idx_v, rows_v, sem):
    def chunk(d_hbm, h_hbm, c):
        pltpu.sync_copy(d_hbm.at[c], idx_v)
        pltpu.sync_copy(h_hbm.at[_chunk_rows(c)], rows_v)
        copies = [pltpu.async_copy(rows_v, xs_hbm.at[idx_v.at[k]], sem) for k in range(TOP_K)]
        for cp in copies:
            cp.wait()

    _sc_for_chunks(dp_hbm.shape[0], functools.partial(chunk, dp_hbm, hp_hbm))
    _sc_for_chunks(ds_hbm.shape[0], functools.partial(chunk, ds_hbm, hs_hbm))


def _dispatch(dest_p3, dest_s3, hn_p, hn_s, n_blocks):
    return pl.kernel(
        _sc_dispatch_body,
        out_type=jax.ShapeDtypeStruct((n_blocks * MOE_ROWS, D_PACK), U32),
        mesh=_sc_mesh(),
        scratch_types=[pltpu.VMEM((8, SC_ROWS), jnp.int32), pltpu.VMEM((SC_ROWS, D_PACK), U32),
                       pltpu.SemaphoreType.DMA],
        name="moe_dispatch_sc",
    )(dest_p3, dest_s3, hn_p, hn_s)


def _expert_kernel(be_ref, nb_ref, nv_ref, first_ref, slot_ref, nxt_ref,
                   x_ref, w1_hbm, b1_ref, w2_hbm, b2_ref, y_ref, w1f_ref, w2f_ref, w1b_ref, w2b_ref, sem):
    step = pl.program_id(0)

    def weight_copies(e):
        return (pltpu.make_async_copy(w1_hbm.at[e], w1f_ref, sem.at[0]),
                pltpu.make_async_copy(w2_hbm.at[e], w2f_ref, sem.at[1]))

    @pl.when(jnp.logical_and(step == 0, nb_ref[0] > 0))
    def _():
        for cp in weight_copies(be_ref[0]):
            cp.start()

    for sb in range(MOE_PAIR):
        b = step * MOE_PAIR + sb

        @pl.when(jnp.logical_and(b < nb_ref[0], first_ref[b] == 1))
        def _(b=b):
            s = slot_ref[b]
            for cp in weight_copies(be_ref[b]):
                cp.wait()
            w1b_ref[s] = w1f_ref[...].astype(BF16)
            w2b_ref[s] = w2f_ref[...].astype(BF16)

            @pl.when(nxt_ref[b] >= 0)
            def _():
                for cp in weight_copies(nxt_ref[b]):
                    cp.start()

    @pl.when(step * MOE_PAIR < nb_ref[0])
    def _():
        cw = D_FF // MOE_COL_CHUNKS
        for sb in range(MOE_PAIR):
            b = step * MOE_PAIR + sb
            e, s = be_ref[b], slot_ref[b]
            rs = slice(sb * MOE_ROWS, (sb + 1) * MOE_ROWS)
            live = lax.broadcasted_iota(jnp.int32, (MOE_ROWS, D_PACK), 0) < nv_ref[b]
            lo, hi = _unpack_rows(jnp.where(live, x_ref[rs, :], jnp.uint32(0)))
            lo, hi = lo.astype(BF16), hi.astype(BF16)
            b1 = b1_ref[e]

            def up(c0, lo=lo, hi=hi, s=s, b1=b1):
                return (jnp.dot(lo, w1b_ref[s, :D_PACK, c0:c0 + cw], preferred_element_type=F32)
                        + jnp.dot(hi, w1b_ref[s, D_PACK:, c0:c0 + cw], preferred_element_type=F32)
                        + b1[:, c0:c0 + cw])

            y = b2_ref[e]
            for j in range(MOE_COL_CHUNKS):
                gate = jnp.minimum(up(j * cw), SWIGLU_LIMIT)
                lin = jnp.clip(up(D_FF + j * cw), -SWIGLU_LIMIT, SWIGLU_LIMIT)
                act = gate * jax.nn.sigmoid(SWIGLU_ALPHA * gate) * (lin + 1.0)
                y = y + jnp.dot(act.astype(BF16), w2b_ref[s, j * cw:(j + 1) * cw, :],
                                preferred_element_type=F32)
            y_ref[rs, :] = _pack_rows(y)

    @pl.when(step * MOE_PAIR >= nb_ref[0])
    def _():
        y_ref[...] = jnp.zeros(y_ref.shape, y_ref.dtype)


def _experts(tables, xs, w1, b1, w2, b2, n_blocks):
    rows = MOE_PAIR * MOE_ROWS
    last = lambda p, be, nb, *_: (jnp.maximum(jnp.minimum(p, (nb[0] - 1) // MOE_PAIR), 0), 0)
    whole = lambda p, *_: (0, 0, 0)
    return pl.pallas_call(
        _expert_kernel,
        grid_spec=pltpu.PrefetchScalarGridSpec(
            num_scalar_prefetch=6,
            grid=(n_blocks // MOE_PAIR,),
            in_specs=[pl.BlockSpec((rows, D_PACK), last),
                      pl.BlockSpec(memory_space=pl.ANY),
                      pl.BlockSpec((N_EXPERTS, 1, 2 * D_FF), whole),
                      pl.BlockSpec(memory_space=pl.ANY),
                      pl.BlockSpec((N_EXPERTS, 1, D_MODEL), whole)],
            out_specs=pl.BlockSpec((rows, D_PACK), lambda p, *_: (p, 0)),
            scratch_shapes=[pltpu.VMEM((D_MODEL, 2 * D_FF), F32), pltpu.VMEM((D_FF, D_MODEL), F32),
                            pltpu.VMEM((2, D_MODEL, 2 * D_FF), BF16), pltpu.VMEM((2, D_FF, D_MODEL), BF16),
                            pltpu.SemaphoreType.DMA((2,))]),
        out_shape=jax.ShapeDtypeStruct((n_blocks * MOE_ROWS, D_PACK), U32),
        compiler_params=_cparams(("arbitrary",), VMEM_LIMIT),
        name="moe_experts",
    )(*tables, xs, w1, b1.reshape(N_EXPERTS, 1, 2 * D_FF), w2, b2.reshape(N_EXPERTS, 1, D_MODEL))


def _sc_gather_body(dp_hbm, ds_hbm, yb_hbm, gp_hbm, gs_hbm, idx_v, rows_v, sems):
    n = SC_GATHER_ROWS

    def chunk(d_hbm, g_hbm, c):
        pltpu.sync_copy(d_hbm.at[c], idx_v)
        rows = pl.ds(pl.multiple_of(c * n, n), n)
        gather = lambda k: pltpu.async_copy(yb_hbm.at[idx_v.at[k]], rows_v.at[k % 2], sems.at[k % 2])
        cp = gather(0)
        for k in range(TOP_K):
            cp.wait()
            if k + 1 < TOP_K:
                cp = gather(k + 1)
            pltpu.sync_copy(rows_v.at[k % 2], g_hbm.at[k, rows])

    _sc_for_chunks(dp_hbm.shape[0], functools.partial(chunk, dp_hbm, gp_hbm))
    _sc_for_chunks(ds_hbm.shape[0], functools.partial(chunk, ds_hbm, gs_hbm))


def _gather_expert_rows(dest_p3, dest_s3, yb):
    n_p, n_s = dest_p3.shape[0] * SC_GATHER_ROWS, dest_s3.shape[0] * SC_GATHER_ROWS
    return pl.kernel(
        _sc_gather_body,
        out_type=(jax.ShapeDtypeStruct((TOP_K, n_p, D_PACK), U32),
                  jax.ShapeDtypeStruct((TOP_K, n_s, D_PACK), U32)),
        mesh=_sc_mesh(),
        scratch_types=[pltpu.VMEM((8, SC_GATHER_ROWS), jnp.int32),
                       pltpu.VMEM((2, SC_GATHER_ROWS, D_PACK), U32),
                       pltpu.SemaphoreType.DMA((2,))],
        name="moe_gather_sc",
    )(dest_p3, dest_s3, yb)


def _combine_kernel(g_ref, gate_ref, x1_ref, o_ref):
    gate = gate_ref[...]
    x1 = x1_ref[...]
    acc_lo, acc_hi = x1[:, :D_PACK], x1[:, D_PACK:]
    for k in range(TOP_K):
        lo, hi = _unpack_rows(g_ref[k])
        acc_lo = acc_lo + gate[:, k:k + 1] * lo
        acc_hi = acc_hi + gate[:, k:k + 1] * hi
    o_ref[:, :D_PACK] = acc_lo
    o_ref[:, D_PACK:] = acc_hi


def _combine(g, gates, x1):
    n_tok = x1.shape[0]
    tm = min(512, n_tok)
    return pl.pallas_call(
        _combine_kernel,
        grid=(n_tok // tm,),
        in_specs=[pl.BlockSpec((TOP_K, tm, D_PACK), lambda i: (0, i, 0)),
                  pl.BlockSpec((tm, LANES), lambda i: (i, 0)),
                  pl.BlockSpec((tm, D_MODEL), lambda i: (i, 0))],
        out_specs=pl.BlockSpec((tm, D_MODEL), lambda i: (i, 0)),
        out_shape=jax.ShapeDtypeStruct((n_tok, D_MODEL), F32),
        compiler_params=_cparams(("parallel",), VMEM_LIMIT),
        name="moe_combine",
    )(g, gates, x1)


def _route_tables(counts_f, n_assign):
    counts = counts_f[0, :N_EXPERTS].astype(jnp.int32)
    nblk_e = (counts + MOE_ROWS - 1) // MOE_ROWS
    bend = jnp.cumsum(nblk_e)
    bstart = bend - nblk_e
    pad_start = bstart * MOE_ROWS
    n_blocks = -(-(n_assign + N_EXPERTS * (MOE_ROWS - 1)) // (MOE_ROWS * MOE_PAIR)) * MOE_PAIR
    e_ids = jnp.arange(N_EXPERTS, dtype=jnp.int32)
    nonempty = counts > 0
    slot_e = (jnp.cumsum(nonempty.astype(jnp.int32)) - 1) % 2
    later = jnp.where(nonempty[None, :] & (e_ids[None, :] > e_ids[:, None]), e_ids[None, :], N_EXPERTS)
    next_e = jnp.min(later, axis=1)
    next_e = jnp.where(next_e == N_EXPERTS, -1, next_e)
    bidx = jnp.arange(n_blocks, dtype=jnp.int32)
    mine = ((bidx[:, None] >= bstart[None, :]) & (bidx[:, None] < bend[None, :])).astype(jnp.int32)
    vals = jnp.stack([e_ids, slot_e, next_e, pad_start + counts, bstart, jnp.ones_like(e_ids)], axis=0)
    per_block = jnp.sum(mine[:, None, :] * vals[None, :, :], axis=-1)
    block_e, slot, nxt, row_end, first_blk, used = (per_block[:, j] for j in range(6))
    n_valid = jnp.clip(row_end - bidx * MOE_ROWS, 0, MOE_ROWS)
    first = ((bidx == first_blk) & (used > 0)).astype(jnp.int32)
    tables = (block_e, bend[-1:].astype(jnp.int32), n_valid, first, slot, nxt)
    return pad_start, tables, n_blocks


def _dest_kernel(ps_ref, route_ref, d128_ref, d64_ref):
    rt = route_ref[...].T
    idx, rank = rt[0:8, :], rt[8:16, :]
    base = jnp.zeros_like(idx)
    for e in range(N_EXPERTS):
        base = jnp.where(idx == e, ps_ref[e], base)
    dest = base + rank
    for j in range(dest.shape[1] // SC_ROWS):
        d128_ref[j] = dest[:, j * SC_ROWS:(j + 1) * SC_ROWS]
    for j in range(dest.shape[1] // SC_GATHER_ROWS):
        d64_ref[j] = dest[:, j * SC_GATHER_ROWS:(j + 1) * SC_GATHER_ROWS]


def _dest(pad_start, route):
    n_tok = route.shape[0]
    tm = min(512, n_tok)
    return pl.pallas_call(
        _dest_kernel,
        grid_spec=pltpu.PrefetchScalarGridSpec(
            num_scalar_prefetch=1,
            grid=(n_tok // tm,),
            in_specs=[pl.BlockSpec((tm, LANES), lambda i, ps: (i, 0))],
            out_specs=[pl.BlockSpec((tm // SC_ROWS, 8, SC_ROWS), lambda i, ps: (i, 0, 0)),
                       pl.BlockSpec((tm // SC_GATHER_ROWS, 8, SC_GATHER_ROWS), lambda i, ps: (i, 0, 0))]),
        out_shape=(jax.ShapeDtypeStruct((n_tok // SC_ROWS, 8, SC_ROWS), jnp.int32),
                   jax.ShapeDtypeStruct((n_tok // SC_GATHER_ROWS, 8, SC_GATHER_ROWS), jnp.int32)),
        compiler_params=_cparams(("parallel",), VMEM_LIMIT),
        name="moe_dest",
    )(pad_start, route)


def _rope_tables(offset, length):
    f32 = np.float32
    pos = f32(offset) + np.arange(length, dtype=f32)
    inv = np.power(f32(ROPE_THETA), -np.arange(QK_ROPE // 2, dtype=f32) * f32(2.0 / QK_ROPE))
    ang = pos[:, None] * inv[None, :]
    cos, sin = np.cos(ang), np.sin(ang)
    z32 = np.zeros((length, 32), f32)
    rot_c = np.concatenate([cos, cos], axis=1)
    rot_s = np.concatenate([-sin, sin], axis=1)
    ck = np.concatenate([rot_c, z32, rot_c, z32], axis=1)
    sk = np.concatenate([rot_s, z32, rot_s, z32], axis=1)
    cq = np.concatenate([np.ones((length, QK_NOPE), f32), rot_c, z32], axis=1)
    sq = np.concatenate([np.zeros((length, QK_NOPE), f32), rot_s, z32], axis=1)
    return tuple(jnp.asarray(t, F32) for t in (ck, sk, cq, sq))


def _head_gain(g_nope, g_rope):
    return jnp.concatenate([g_nope, g_rope, g_rope, jnp.zeros((HEAD_PAD - QK_HEAD,), F32)]).reshape(1, HEAD_PAD)


def _mixer_stage(x, past, prm, counts_in):
    bsz, length, _ = x.shape
    offset = 0 if past is None else past[0].shape[1]
    ck, sk, cq, sq = _rope_tables(offset, length)
    u_tm, qn, lat, k_rope, krot = _in_proj(x, prm['g_mix'], prm['w_in_p'], prm['g_q_lat'], prm['g_kv_lat'], ck, sk)
    if past is None:
        h0 = jnp.zeros((bsz, S5_STATE_LANES), F32)
    else:
        h0 = _state_layout(past[2], past[3])
    s5n_tm, h_last = _s5(u_tm, h0, prm['lam'], prm['wb'], prm['wc'],
                         prm['d_skip'], prm['w_glu'], prm['b_glu'], prm['g_s5_out'], bsz, length)
    h_re, h_im = _state_unlayout(h_last)
    t = bsz * length
    q = _q_proj(qn.reshape(t, Q_LORA), prm['w_uq_p'], prm['w_uq_sw'], cq, sq, prm['gq'], bsz, length)
    k, v = _kv_proj(lat.reshape(t, KV_LORA), krot.reshape(t, LANES), prm['w_k'], prm['w_v'], prm['gk'],
                    bsz, length)
    if past is None:
        attn = _attn_prompt(q, k, v, bsz, length)
    else:
        n_past = past[0].shape[1]
        c_lat = past[0].reshape(bsz * n_past, KV_LORA)
        c_rot = jnp.pad(past[1].reshape(bsz * n_past, QK_ROPE), ((0, 0), (QK_NOPE, HEAD_PAD - QK_HEAD)))
        kc, vc = _kv_proj(c_lat, c_rot, prm['w_k'], prm['w_v'], prm['gk'], bsz, n_past)
        attn = _attn_sample(q, kc, vc, k, v, bsz, length, n_past)
    x1, hn, route, gates, counts = _out_proj(x, s5n_tm, attn, prm['g_mla_out'], prm['w_out'], prm['g_ffn'],
                                             prm['w_r'], prm['b_r_p'], counts_in)
    return (x1.reshape(t, D_MODEL), hn.reshape(t, D_PACK), route.reshape(t, LANES), gates.reshape(t, LANES),
            counts, lat, k_rope, h_re, h_im)


def _prepare(g_mix, w_in, lam_re, lam_im, log_dt, b_s5_re, b_s5_im, c_s5_re, c_s5_im, d_s5, w_glu, b_glu,
             g_q_lat, w_uq, g_kv_lat, w_ukv, g_qn_nope, g_qn_rope, g_kn_nope, g_kn_rope, g_s5_out,
             g_mla_out, w_out, g_ffn, w_router, b_router):
    c0 = S5_WIDTH + Q_LORA + KV_LORA
    w_pe = w_in[:, c0:]
    z = jnp.zeros((D_MODEL, 32), F32)
    w_in_p = jnp.concatenate([w_in[:, :c0], w_pe, z, w_pe, z], axis=1).astype(BF16)
    ar, ai, bbr, bbi = _s5_prep(lam_re, lam_im, log_dt, b_s5_re, b_s5_im)
    wb, wc = _s5_weights(bbr, bbi, c_s5_re, c_s5_im)
    w_uq_p = jnp.pad(w_uq.reshape(Q_LORA, MLA_HEADS, QK_HEAD), ((0, 0), (0, 0), (0, HEAD_PAD - QK_HEAD)))
    r0, r1, r2 = QK_NOPE, QK_NOPE + QK_ROPE // 2, QK_HEAD
    w_uq_sw = jnp.zeros_like(w_uq_p).at[:, :, r0:r1].set(w_uq_p[:, :, r1:r2]).at[:, :, r1:r2].set(w_uq_p[:, :, r0:r1])
    w_kv = w_ukv.reshape(KV_LORA, MLA_HEADS, QK_NOPE + V_HEAD)
    w_k = jnp.pad(w_kv[:, :, :QK_NOPE], ((0, 0), (0, 0), (0, HEAD_PAD - QK_NOPE)))
    w_v = w_kv[:, :, QK_NOPE:]
    w_r_p = jnp.pad(w_router, ((0, 0), (0, LANES - N_EXPERTS)))
    w_r_hi = w_r_p.astype(BF16)
    return dict(
        g_mix=g_mix.reshape(1, -1), w_in_p=w_in_p,
        g_q_lat=g_q_lat.reshape(1, -1), g_kv_lat=g_kv_lat.reshape(1, -1),
        lam=_state_layout(ar, ai).reshape(1, S5_STATE_LANES), wb=wb, wc=wc,
        d_skip=d_s5.reshape(1, -1), w_glu=w_glu.astype(BF16), b_glu=b_glu.reshape(1, -1),
        g_s5_out=g_s5_out.reshape(1, -1),
        w_uq_p=w_uq_p.reshape(Q_LORA, MLA_HEADS * HEAD_PAD).astype(BF16),
        w_uq_sw=w_uq_sw.reshape(Q_LORA, MLA_HEADS * HEAD_PAD).astype(BF16),
        w_k=w_k.reshape(KV_LORA, MLA_HEADS * HEAD_PAD).astype(BF16),
        w_v=w_v.reshape(KV_LORA, MLA_WIDTH).astype(BF16),
        gq=_head_gain(g_qn_nope, g_qn_rope), gk=_head_gain(g_kn_nope, g_kn_rope),
        g_mla_out=g_mla_out.reshape(1, -1), w_out=w_out.astype(BF16), g_ffn=g_ffn.reshape(1, -1),
        w_r=jnp.concatenate([w_r_hi, (w_r_p - w_r_hi.astype(F32)).astype(BF16)], axis=1),
        b_r_p=jnp.pad(b_router, (0, LANES - N_EXPERTS)).reshape(1, LANES),
    )


def _layer(xp, xs, cache_lat, cache_kr, st_re, st_im, mixer_w, w1, b1, w2, b2):
    prm = _prepare(*mixer_w)
    bp, lp, _ = xp.shape
    bs, ls, _ = xs.shape
    zero_counts = jnp.zeros((1, LANES), F32)
    x1p, hnp, routep, gatep, counts_p, latp, krp, hrp, hip = _mixer_stage(xp, None, prm, zero_counts)
    x1s, hns, routes, gates, counts, lats, krs, hrs, his = _mixer_stage(
        xs, (cache_lat, cache_kr, st_re, st_im), prm, counts_p)
    n_tok = bp * lp + bs * ls
    pad_start, tables, n_blocks = _route_tables(counts, n_tok * TOP_K)
    dp128, dp64 = _dest(pad_start, routep)
    ds128, ds64 = _dest(pad_start, routes)
    xs_sorted = _dispatch(dp128, ds128, hnp, hns, n_blocks)
    yb = _experts(tables, xs_sorted, w1, b1, w2, b2, n_blocks)
    g_p, g_s = _gather_expert_rows(dp64, ds64, yb)
    yp = _combine(g_p, gatep, x1p).reshape(bp, lp, D_MODEL)
    ys = _combine(g_s, gates, x1s).reshape(bs, ls, D_MODEL)
    return yp, ys, latp, krp, hrp, hip, lats, krs, hrs, his


def kernel(x_prompt, x_sample, cache_kv_latent, cache_k_rope, state_s5_re, state_s5_im, g_mix, w_in, lam_re,
           lam_im, log_dt, b_s5_re, b_s5_im, c_s5_re, c_s5_im, d_s5, w_glu, b_glu, g_q_lat, w_uq, g_kv_lat,
           w_ukv, g_qn_nope, g_qn_rope, g_kn_nope, g_kn_rope, g_s5_out, g_mla_out, w_out, g_ffn, w_router,
           b_router, w_mlp1, b_mlp1, w_mlp2, b_mlp2):
    depth = g_mix.shape[0]
    yp, ys = x_prompt, x_sample
    outs = [[] for _ in range(8)]
    for l in range(depth):
        mixer_w = (g_mix[l], w_in[l], lam_re[l], lam_im[l], log_dt[l], b_s5_re[l], b_s5_im[l], c_s5_re[l],
                   c_s5_im[l], d_s5[l], w_glu[l], b_glu[l], g_q_lat[l], w_uq[l], g_kv_lat[l], w_ukv[l],
                   g_qn_nope[l], g_qn_rope[l], g_kn_nope[l], g_kn_rope[l], g_s5_out[l], g_mla_out[l],
                   w_out[l], g_ffn[l], w_router[l], b_router[l])
        res = _layer(yp, ys, cache_kv_latent[l], cache_k_rope[l], state_s5_re[l], state_s5_im[l], mixer_w,
                     w_mlp1[l], b_mlp1[l], w_mlp2[l], b_mlp2[l])
        yp, ys = res[0], res[1]
        for o, r in zip(outs, res[2:]):
            o.append(r)
    return (yp, ys) + tuple(jnp.stack(o) for o in outs)
```

```python
import functools
import math

import jax
import jax.numpy as jnp
import numpy as np
from jax import lax
from jax.experimental import pallas as pl
from jax.experimental.pallas import tpu as pltpu
from jax.experimental.pallas import tpu_sc as plsc

F32 = jnp.float32
BF16 = jnp.bfloat16
U32 = jnp.uint32

D_MODEL = 1024
S5_WIDTH = 512
S5_GROUP = 16
S5_GROUPS = 32
S5_STATE = 64
MLA_HEADS = 8
QK_NOPE = 64
QK_ROPE = 32
QK_HEAD = QK_NOPE + QK_ROPE
V_HEAD = 64
MLA_WIDTH = MLA_HEADS * V_HEAD
Q_LORA = 384
KV_LORA = 256
ROPE_THETA = 10000.0
CHUNK = 64
N_EXPERTS = 32
TOP_K = 4
D_FF = D_MODEL
SWIGLU_LIMIT = 7.0
SWIGLU_ALPHA = 1.702
EPS = 1e-6

LANES = 128
HEAD_PAD = 128
D_IN_PAD = 1280
S5_CHUNKS = 4
S5_CHUNK_LANES = 1024
S5_STATE_LANES = S5_CHUNKS * S5_CHUNK_LANES
ROW_TILE = 1024
SUB_TILE = 512
TIME_TILE = 64
S5_STEPS = 32
MOE_ROWS = 512
MOE_COL_CHUNKS = 2
MOE_PAIR = 2
VMEM_LIMIT = 56 * 1024 * 1024
SC_CORES = 2
SC_SUBCORES = 16
SC_WORKERS = SC_CORES * SC_SUBCORES
SC_GATHER_ROWS = 64
D_PACK = D_MODEL // 2


def _cparams(sem, vmem=None):
    return pltpu.CompilerParams(dimension_semantics=sem, vmem_limit_bytes=vmem)


def _rms(x, g, n=None):
    n = x.shape[-1] if n is None else n
    ms = jnp.sum(x * x, axis=-1, keepdims=True) * (1.0 / n)
    return x * lax.rsqrt(ms + EPS) * g


def _sub_tiles(rows):
    sub = min(SUB_TILE, rows)
    return [slice(r, r + sub) for r in range(0, rows, sub)]


def _pack_rows(x):
    lo = lax.bitcast_convert_type(x[:, :D_PACK].astype(BF16).astype(F32), U32)
    hi = lax.bitcast_convert_type(x[:, D_PACK:].astype(BF16).astype(F32), U32)
    return (lo >> 16) | (hi & jnp.uint32(0xFFFF0000))


def _unpack_rows(w):
    lo = lax.bitcast_convert_type(w << 16, F32)
    hi = lax.bitcast_convert_type(w & jnp.uint32(0xFFFF0000), F32)
    return lo, hi


def _prep_kernel(lr_ref, li_ref, ldt_ref, br_ref, bi_ref, ar_ref, ai_ref, bbr_ref, bbi_ref):
    lr = lr_ref[...]
    li = li_ref[...]
    dt = jnp.exp(ldt_ref[...])
    mag = jnp.exp(lr * dt)
    ar = mag * jnp.cos(li * dt)
    ai = mag * jnp.sin(li * dt)
    ar_ref[...] = ar
    ai_ref[...] = ai
    den = lr * lr + li * li
    cr = ((ar - 1.0) * lr + ai * li) / den
    ci = (ai * lr - (ar - 1.0) * li) / den
    br = br_ref[...]
    bi = bi_ref[...]
    bbr_ref[...] = cr[:, None, :] * br - ci[:, None, :] * bi
    bbi_ref[...] = cr[:, None, :] * bi + ci[:, None, :] * br


def _s5_prep(lam_re, lam_im, log_dt, b_re, b_im):
    g, n = lam_re.shape
    p = b_re.shape[-1]
    out = pl.pallas_call(
        _prep_kernel,
        out_shape=(jax.ShapeDtypeStruct((g, n), F32), jax.ShapeDtypeStruct((g, n), F32),
                   jax.ShapeDtypeStruct((g, p, n), F32), jax.ShapeDtypeStruct((g, p, n), F32)),
        name="s5_prep",
    )(lam_re, lam_im, log_dt.reshape(g, 1), jnp.swapaxes(b_re, 1, 2), jnp.swapaxes(b_im, 1, 2))
    return out


def _state_layout(re, im):
    lead = re.shape[:-2]
    re = re.reshape(lead + (S5_CHUNKS, 512))
    im = im.reshape(lead + (S5_CHUNKS, 512))
    return jnp.stack([re, im], axis=-2).reshape(lead + (S5_STATE_LANES,))


def _state_unlayout(h):
    lead = h.shape[:-1]
    h = h.reshape(lead + (S5_CHUNKS, 2, 512))
    re = h[..., 0, :].reshape(lead + (S5_GROUPS, S5_STATE))
    im = h[..., 1, :].reshape(lead + (S5_GROUPS, S5_STATE))
    return re, im


def _s5_weights(bbr, bbi, c_re, c_im):
    eye8 = jnp.eye(8, dtype=F32)

    def blockdiag(m):
        a, b = m.shape[1], m.shape[2]
        return (eye8[:, None, :, None] * m[:, :, None, :]).reshape(8 * a, 8 * b)

    wb, wc = [], []
    for c in range(S5_CHUNKS):
        sl = slice(8 * c, 8 * c + 8)
        wb.append(jnp.concatenate([blockdiag(bbr[sl]), blockdiag(bbi[sl])], axis=1))
        cr_t = jnp.swapaxes(c_re[sl], 1, 2)
        ci_t = jnp.swapaxes(c_im[sl], 1, 2)
        wc.append(jnp.concatenate([blockdiag(cr_t), -blockdiag(ci_t)], axis=0))
    return jnp.stack(wb).astype(BF16), jnp.stack(wc).astype(BF16)


def _in_proj_kernel(x_ref, gmix_ref, w_ref, gq_ref, gkv_ref, ck_ref, sk_ref,
                    u_ref, qn_ref, lat_ref, krope_ref, krot_ref):
    bsz, steps, _ = x_ref.shape
    nb = max(1, min(bsz, SUB_TILE // steps))
    ck, sk = ck_ref[...][None], sk_ref[...][None]
    for b0 in range(0, bsz, nb):
        bs = slice(b0, b0 + nb)
        rows = nb * steps
        xn = _rms(x_ref[bs].reshape(rows, D_MODEL), gmix_ref[...]).astype(BF16)
        z = jnp.dot(xn, w_ref[...], preferred_element_type=F32)
        for bl in range(nb):
            for c in range(S5_CHUNKS):
                u_ref[c, pl.ds(b0 + bl, steps, stride=bsz), :] = z[bl * steps:(bl + 1) * steps,
                                                                   c * LANES:(c + 1) * LANES]
        qn = _rms(z[:, S5_WIDTH:S5_WIDTH + Q_LORA], gq_ref[...]).astype(BF16)
        qn_ref[bs] = qn.reshape(nb, steps, Q_LORA)
        c0 = S5_WIDTH + Q_LORA
        lat_ref[bs] = _rms(z[:, c0:c0 + KV_LORA], gkv_ref[...]).reshape(nb, steps, KV_LORA)
        kp = z[:, c0 + KV_LORA:]
        lane = lax.broadcasted_iota(jnp.int32, kp.shape, 1)
        first_half = (lane % 64) < 16
        sw = jnp.where(first_half, pltpu.roll(kp, LANES - 16, axis=1), pltpu.roll(kp, 16, axis=1))
        kr = kp.reshape(nb, steps, LANES) * ck + sw.reshape(nb, steps, LANES) * sk
        krope_ref[bs] = kr[:, :, :QK_ROPE]
        krot_ref[bs] = jnp.where(lane.reshape(nb, steps, LANES) >= 64, kr, 0.0)


def _in_proj(x, g_mix, w_in_p, g_q_lat, g_kv_lat, ck, sk):
    bsz, length, _ = x.shape
    steps = min(TIME_TILE, length)
    full = lambda i: (0, 0)
    blk = lambda w: pl.BlockSpec((bsz, steps, w), lambda i: (0, i, 0))
    return pl.pallas_call(
        _in_proj_kernel,
        grid=(length // steps,),
        in_specs=[blk(D_MODEL),
                  pl.BlockSpec((1, D_MODEL), full),
                  pl.BlockSpec((D_MODEL, D_IN_PAD), full),
                  pl.BlockSpec((1, Q_LORA), full),
                  pl.BlockSpec((1, KV_LORA), full),
                  pl.BlockSpec((steps, LANES), lambda i: (i, 0)),
                  pl.BlockSpec((steps, LANES), lambda i: (i, 0))],
        out_specs=[pl.BlockSpec((S5_CHUNKS, steps * bsz, LANES), lambda i: (0, i, 0)),
                   blk(Q_LORA), blk(KV_LORA), blk(QK_ROPE), blk(LANES)],
        out_shape=(jax.ShapeDtypeStruct((S5_CHUNKS, length * bsz, LANES), F32),
                   jax.ShapeDtypeStruct((bsz, length, Q_LORA), BF16),
                   jax.ShapeDtypeStruct((bsz, length, KV_LORA), F32),
                   jax.ShapeDtypeStruct((bsz, length, QK_ROPE), F32),
                   jax.ShapeDtypeStruct((bsz, length, LANES), F32)),
        compiler_params=_cparams(("parallel",), VMEM_LIMIT),
        name="in_proj",
    )(x, g_mix, w_in_p, g_q_lat, g_kv_lat, ck, sk)


def _s5_kernel(u_ref, h0_ref, lam_ref, wb_ref, wc_ref, dskip_ref, wglu_ref, bglu_ref, gout_ref,
               y_ref, hlast_ref, bu_ref, h_ref, *, bsz, steps):
    c_id = pl.program_id(0)

    @pl.when(c_id == 0)
    def _():
        h_ref[...] = h0_ref[...]

    for c in range(S5_CHUNKS):
        bu_ref[:, c * S5_CHUNK_LANES:(c + 1) * S5_CHUNK_LANES] = jnp.dot(
            u_ref[c].astype(BF16), wb_ref[c], preferred_element_type=F32)

    for c in range(S5_CHUNKS):
        re = slice(c * S5_CHUNK_LANES, c * S5_CHUNK_LANES + 512)
        im = slice(c * S5_CHUNK_LANES + 512, (c + 1) * S5_CHUNK_LANES)
        lam_r = jnp.broadcast_to(lam_ref[:, re], (bsz, 512))
        lam_i = jnp.broadcast_to(lam_ref[:, im], (bsz, 512))

        hr, hi = h_ref[:, re], h_ref[:, im]
        for t in range(steps):
            rows = slice(t * bsz, (t + 1) * bsz)
            hr, hi = (lam_r * hr - lam_i * hi + bu_ref[rows, re],
                      lam_r * hi + lam_i * hr + bu_ref[rows, im])
            bu_ref[rows, re] = hr
            bu_ref[rows, im] = hi
        h_ref[:, re] = hr
        h_ref[:, im] = hi

    ys = []
    for c in range(S5_CHUNKS):
        hs = bu_ref[:, c * S5_CHUNK_LANES:(c + 1) * S5_CHUNK_LANES].astype(BF16)
        ys.append(jnp.dot(hs, wc_ref[c], preferred_element_type=F32))
    u = jnp.concatenate([u_ref[c] for c in range(S5_CHUNKS)], axis=1)
    y = jnp.concatenate(ys, axis=1) + dskip_ref[...] * u
    y = jax.nn.gelu(y)
    gate = jnp.dot(y.astype(BF16), wglu_ref[...], preferred_element_type=F32) + bglu_ref[...]
    y = _rms(y * jax.nn.sigmoid(gate), gout_ref[...])
    for c in range(S5_CHUNKS):
        y_ref[c] = y[:, c * LANES:(c + 1) * LANES]

    @pl.when(c_id == pl.num_programs(0) - 1)
    def _():
        hlast_ref[...] = h_ref[...]


def _s5(u_tm, h0, lam, wb, wc, d_skip, w_glu, b_glu, g_out, bsz, length):
    steps = min(S5_STEPS, length)
    rows = steps * bsz
    full2 = lambda c: (0, 0)
    full3 = lambda c: (0, 0, 0)
    return pl.pallas_call(
        functools.partial(_s5_kernel, bsz=bsz, steps=steps),
        grid=(length // steps,),
        in_specs=[pl.BlockSpec((S5_CHUNKS, rows, LANES), lambda c: (0, c, 0)),
                  pl.BlockSpec((bsz, S5_STATE_LANES), full2),
                  pl.BlockSpec((1, S5_STATE_LANES), full2),
                  pl.BlockSpec((S5_CHUNKS, LANES, S5_CHUNK_LANES), full3),
                  pl.BlockSpec((S5_CHUNKS, S5_CHUNK_LANES, LANES), full3),
                  pl.BlockSpec((1, S5_WIDTH), full2),
                  pl.BlockSpec((S5_WIDTH, S5_WIDTH), full2),
                  pl.BlockSpec((1, S5_WIDTH), full2),
                  pl.BlockSpec((1, S5_WIDTH), full2)],
        out_specs=[pl.BlockSpec((S5_CHUNKS, rows, LANES), lambda c: (0, c, 0)),
                   pl.BlockSpec((bsz, S5_STATE_LANES), full2)],
        out_shape=(jax.ShapeDtypeStruct((S5_CHUNKS, length * bsz, LANES), F32),
                   jax.ShapeDtypeStruct((bsz, S5_STATE_LANES), F32)),
        scratch_shapes=[pltpu.VMEM((rows, S5_STATE_LANES), F32),
                        pltpu.VMEM((bsz, S5_STATE_LANES), F32)],
        compiler_params=_cparams(("arbitrary",), VMEM_LIMIT),
        name="s5_mixer",
    )(u_tm, h0, lam, wb, wc, d_skip, w_glu, b_glu, g_out)


def _q_proj_kernel(qn_ref, w_ref, wsw_ref, ones_ref, cq_ref, sq_ref, g_ref, q_ref):
    g = g_ref[...] * (QK_HEAD ** -0.5)
    for rs in _sub_tiles(qn_ref.shape[0]):
        qn = qn_ref[rs, :]
        qf = jnp.dot(qn, w_ref[...], preferred_element_type=F32)
        qs = jnp.dot(qn, wsw_ref[...], preferred_element_type=F32)
        cq, sq = cq_ref[rs, :], sq_ref[rs, :]
        for p in range(MLA_HEADS // 2):
            cols = slice(2 * p * HEAD_PAD, 2 * (p + 1) * HEAD_PAD)
            xr = qf[:, cols] * cq + qs[:, cols] * sq
            sqr = xr * xr
            hi = sqr.astype(BF16)
            lo = (sqr - hi.astype(F32)).astype(BF16)
            ss = (jnp.dot(hi, ones_ref[...], preferred_element_type=F32)
                  + jnp.dot(lo, ones_ref[...], preferred_element_type=F32))
            q = (xr * lax.rsqrt(ss * (1.0 / QK_HEAD) + EPS) * g).astype(BF16)
            q_ref[0, 2 * p, rs, :] = q[:, :HEAD_PAD]
            q_ref[0, 2 * p + 1, rs, :] = q[:, HEAD_PAD:]


def _q_proj(qn, w_uq_p, w_uq_sw, cq, sq, gq, bsz, length):
    tm = min(ROW_TILE, length)
    n_l = length // tm
    full = lambda b, i: (0, 0)
    pair = 2 * HEAD_PAD
    two = lambda a: jnp.concatenate([a, a], axis=1)
    ones2 = (jnp.arange(pair)[:, None] // HEAD_PAD == jnp.arange(pair)[None, :] // HEAD_PAD).astype(BF16)
    return pl.pallas_call(
        _q_proj_kernel,
        grid=(bsz, n_l),
        in_specs=[pl.BlockSpec((tm, Q_LORA), lambda b, i: (b * n_l + i, 0)),
                  pl.BlockSpec((Q_LORA, MLA_HEADS * HEAD_PAD), full),
                  pl.BlockSpec((Q_LORA, MLA_HEADS * HEAD_PAD), full),
                  pl.BlockSpec((pair, pair), full),
                  pl.BlockSpec((tm, pair), lambda b, i: (i, 0)),
                  pl.BlockSpec((tm, pair), lambda b, i: (i, 0)),
                  pl.BlockSpec((1, pair), full)],
        out_specs=pl.BlockSpec((1, MLA_HEADS, tm, HEAD_PAD), lambda b, i: (b, 0, i, 0)),
        out_shape=jax.ShapeDtypeStruct((bsz, MLA_HEADS, length, HEAD_PAD), BF16),
        compiler_params=_cparams(("parallel", "parallel"), VMEM_LIMIT),
        name="q_proj",
    )(qn, w_uq_p, w_uq_sw, ones2, two(cq), two(sq), two(gq))


def _kv_proj_kernel(lat_ref, krot_ref, wk_ref, wv_ref, g_ref, k_ref, v_ref):
    g = g_ref[...]
    for rs in _sub_tiles(lat_ref.shape[0]):
        lat = lat_ref[rs, :].astype(BF16)
        kf = jnp.dot(lat, wk_ref[...], preferred_element_type=F32)
        vf = jnp.dot(lat, wv_ref[...], preferred_element_type=F32)
        krot = krot_ref[rs, :]
        for h in range(MLA_HEADS):
            k = kf[:, h * HEAD_PAD:(h + 1) * HEAD_PAD] + krot
            k_ref[0, h, rs, :] = _rms(k, g, QK_HEAD).astype(BF16)
        for p in range(MLA_HEADS // 2):
            v_ref[0, p, rs, :] = vf[:, p * LANES:(p + 1) * LANES].astype(BF16)


def _kv_proj(lat, krot, w_k, w_v, gk, bsz, length):
    tm = min(ROW_TILE, length)
    n_l = length // tm
    full = lambda b, i: (0, 0)
    row = lambda b, i: (b * n_l + i, 0)
    return pl.pallas_call(
        _kv_proj_kernel,
        grid=(bsz, n_l),
        in_specs=[pl.BlockSpec((tm, KV_LORA), row),
                  pl.BlockSpec((tm, HEAD_PAD), row),
                  pl.BlockSpec((KV_LORA, MLA_HEADS * HEAD_PAD), full),
                  pl.BlockSpec((KV_LORA, MLA_WIDTH), full),
                  pl.BlockSpec((1, HEAD_PAD), full)],
        out_specs=[pl.BlockSpec((1, MLA_HEADS, tm, HEAD_PAD), lambda b, i: (b, 0, i, 0)),
                   pl.BlockSpec((1, MLA_HEADS // 2, tm, LANES), lambda b, i: (b, 0, i, 0))],
        out_shape=(jax.ShapeDtypeStruct((bsz, MLA_HEADS, length, HEAD_PAD), BF16),
                   jax.ShapeDtypeStruct((bsz, MLA_HEADS // 2, length, LANES), BF16)),
        compiler_params=_cparams(("parallel", "parallel"), VMEM_LIMIT),
        name="kv_proj",
    )(lat, krot, w_k, w_v, gk)


def _scores(q, k):
    return lax.dot_general(q, k, (((1,), (1,)), ((), ())), preferred_element_type=F32)


def _attn_prompt_kernel(q_ref, k_ref, v_ref, o_ref, *, length, tq):
    n_q = length // tq
    row = lax.broadcasted_iota(jnp.int32, (tq, tq), 0)
    col = lax.broadcasted_iota(jnp.int32, (tq, tq), 1)
    visible = (col // CHUNK) <= (row // CHUNK)
    lane = lax.broadcasted_iota(jnp.int32, (tq, LANES), 1)
    for qi in range(n_q):
        q0 = qi * tq
        outs = []
        for hh in range(2):
            q = q_ref[0, hh, q0:q0 + tq, :]
            sd = jnp.where(visible, _scores(q, k_ref[0, hh, q0:q0 + tq, :]), -jnp.inf)
            m = jnp.max(sd, axis=-1, keepdims=True)
            if qi:
                so = _scores(q, k_ref[0, hh, 0:q0, :])
                m = jnp.maximum(m, jnp.max(so, axis=-1, keepdims=True))
            pd = jnp.exp(sd - m)
            l = jnp.sum(pd, axis=-1, keepdims=True)
            acc = jnp.dot(pd.astype(BF16), v_ref[0, 0, q0:q0 + tq, :], preferred_element_type=F32)
            if qi:
                po = jnp.exp(so - m)
                l = l + jnp.sum(po, axis=-1, keepdims=True)
                acc = acc + jnp.dot(po.astype(BF16), v_ref[0, 0, 0:q0, :], preferred_element_type=F32)
            outs.append(acc / l)
        o_ref[0, q0:q0 + tq, :] = jnp.where(lane < V_HEAD, outs[0], outs[1])


def _attn_prompt(q, k, v, bsz, length):
    tq = min(512, length)
    pairs = MLA_HEADS // 2
    return pl.pallas_call(
        functools.partial(_attn_prompt_kernel, length=length, tq=tq),
        grid=(bsz, pairs),
        in_specs=[pl.BlockSpec((1, 2, length, HEAD_PAD), lambda b, p: (b, p, 0, 0)),
                  pl.BlockSpec((1, 2, length, HEAD_PAD), lambda b, p: (b, p, 0, 0)),
                  pl.BlockSpec((1, 1, length, LANES), lambda b, p: (b, p, 0, 0))],
        out_specs=pl.BlockSpec((1, length, LANES), lambda b, p: (b, 0, p)),
        out_shape=jax.ShapeDtypeStruct((bsz, length, MLA_WIDTH), F32),
        compiler_params=_cparams(("parallel", "parallel"), VMEM_LIMIT),
        name="attn_prompt",
    )(q, k, v)


def _attn_sample_kernel(q_ref, kc_ref, vc_ref, kn_ref, vn_ref, o_ref):
    lq = q_ref.shape[2]
    lane = lax.broadcasted_iota(jnp.int32, (lq, LANES), 1)
    outs = []
    for hh in range(2):
        q = q_ref[0, hh]
        sc = _scores(q, kc_ref[0, hh])
        sn = _scores(q, kn_ref[0, hh])
        m = jnp.maximum(jnp.max(sc, axis=-1, keepdims=True), jnp.max(sn, axis=-1, keepdims=True))
        pc = jnp.exp(sc - m)
        pn = jnp.exp(sn - m)
        l = jnp.sum(pc, axis=-1, keepdims=True) + jnp.sum(pn, axis=-1, keepdims=True)
        acc = (jnp.dot(pc.astype(BF16), vc_ref[0, 0], preferred_element_type=F32)
               + jnp.dot(pn.astype(BF16), vn_ref[0, 0], preferred_element_type=F32))
        outs.append(acc / l)
    o_ref[0] = jnp.where(lane < V_HEAD, outs[0], outs[1])


def _attn_sample(q, kc, vc, kn, vn, bsz, lq, past):
    pairs = MLA_HEADS // 2
    hp = lambda b, p: (b, p, 0, 0)
    return pl.pallas_call(
        _attn_sample_kernel,
        grid=(bsz, pairs),
        in_specs=[pl.BlockSpec((1, 2, lq, HEAD_PAD), hp),
                  pl.BlockSpec((1, 2, past, HEAD_PAD), hp),
                  pl.BlockSpec((1, 1, past, LANES), hp),
                  pl.BlockSpec((1, 2, lq, HEAD_PAD), hp),
                  pl.BlockSpec((1, 1, lq, LANES), hp)],
        out_specs=pl.BlockSpec((1, lq, LANES), lambda b, p: (b, 0, p)),
        out_shape=jax.ShapeDtypeStruct((bsz, lq, MLA_WIDTH), F32),
        compiler_params=_cparams(("parallel", "parallel"), VMEM_LIMIT),
        name="attn_sample",
    )(q, kc, vc, kn, vn)


def _out_proj_kernel(x_ref, s5_ref, at_ref, gmla_ref, wout_ref, gffn_ref, wr_ref, br_ref, tri_ref, cin_ref,
                     x1_ref, hn_ref, route_ref, gate_ref, cout_ref, run_ref):
    @pl.when(pl.program_id(0) == 0)
    def _():
        run_ref[...] = cin_ref[...]

    bsz, steps, _ = x_ref.shape
    nb = max(1, min(bsz, SUB_TILE // steps))
    for b0 in range(0, bsz, nb):
        bs = slice(b0, b0 + nb)
        rows = nb * steps
        s5 = jnp.concatenate(
            [jnp.concatenate([s5_ref[c, pl.ds(b0 + bl, steps, stride=bsz), :] for c in range(S5_CHUNKS)], axis=1)
             for bl in range(nb)], axis=0)
        an = _rms(at_ref[bs].reshape(rows, MLA_WIDTH), gmla_ref[...])
        merged = jnp.concatenate([s5.astype(BF16), an.astype(BF16)], axis=1)
        x1 = x_ref[bs].reshape(rows, D_MODEL) + jnp.dot(merged, wout_ref[...], preferred_element_type=F32)
        x1_ref[bs] = x1.reshape(nb, steps, D_MODEL)
        hn = _rms(x1, gffn_ref[...])
        hn_ref[bs] = _pack_rows(hn).reshape(nb, steps, D_PACK)
        hn_hi = hn.astype(BF16)
        hn_lo = (hn - hn_hi.astype(F32)).astype(BF16)
        parts = (jnp.dot(hn_hi, wr_ref[...], preferred_element_type=F32)
                 + jnp.dot(hn_lo, wr_ref[...], preferred_element_type=F32))
        logits = parts[:, :LANES] + parts[:, LANES:] + br_ref[...]
        lane = lax.broadcasted_iota(jnp.int32, logits.shape, 1)
        lane_f = lane.astype(F32)
        work = jnp.where(lane < N_EXPERTS, logits, -jnp.inf)
        vals, idxs = [], []
        for _ in range(TOP_K):
            m = jnp.max(work, axis=-1, keepdims=True)
            i = jnp.min(jnp.where(work == m, lane_f, float(LANES)), axis=-1, keepdims=True)
            vals.append(m)
            idxs.append(i)
            work = jnp.where(lane_f == i, -jnp.inf, work)
        es = [jnp.exp(v - vals[0]) for v in vals]
        den = es[0] + es[1] + es[2] + es[3]
        onehots = [(lane_f == idxs[k]).astype(F32) for k in range(TOP_K)]
        e_all = onehots[0] + onehots[1] + onehots[2] + onehots[3]
        before = jnp.dot(tri_ref[...], e_all.astype(BF16), preferred_element_type=F32) + run_ref[...]
        run_ref[...] = run_ref[...] + jnp.sum(e_all, axis=0, keepdims=True)
        route = jnp.zeros(logits.shape, F32)
        gate_out = jnp.zeros(logits.shape, F32)
        for k in range(TOP_K):
            rank = jnp.sum(onehots[k] * before, axis=-1, keepdims=True)
            route = jnp.where(lane == k, idxs[k], route)
            route = jnp.where(lane == 8 + k, rank, route)
            gate_out = jnp.where(lane == k, es[k] / den, gate_out)
        route_ref[bs] = route.astype(jnp.int32).reshape(nb, steps, LANES)
        gate_ref[bs] = gate_out.reshape(nb, steps, LANES)
    cout_ref[...] = run_ref[...]


def _out_proj(x, s5n_tm, attn, g_mla, w_out, g_ffn, w_r, b_r_p, counts_in):
    bsz, length, _ = x.shape
    steps = min(TIME_TILE, length)
    full = lambda i: (0, 0)
    blk = lambda w: pl.BlockSpec((bsz, steps, w), lambda i: (0, i, 0))
    sub = max(1, min(bsz, SUB_TILE // steps)) * steps
    tri = (jnp.arange(sub)[:, None] > jnp.arange(sub)[None, :]).astype(BF16)
    return pl.pallas_call(
        _out_proj_kernel,
        grid=(length // steps,),
        in_specs=[blk(D_MODEL),
                  pl.BlockSpec((S5_CHUNKS, steps * bsz, LANES), lambda i: (0, i, 0)),
                  blk(MLA_WIDTH),
                  pl.BlockSpec((1, MLA_WIDTH), full),
                  pl.BlockSpec((D_MODEL, D_MODEL), full),
                  pl.BlockSpec((1, D_MODEL), full),
                  pl.BlockSpec((D_MODEL, 2 * LANES), full),
                  pl.BlockSpec((1, LANES), full),
                  pl.BlockSpec((sub, sub), full),
                  pl.BlockSpec((1, LANES), full)],
        out_specs=[blk(D_MODEL), blk(D_PACK), blk(LANES), blk(LANES),
                   pl.BlockSpec((1, LANES), full)],
        out_shape=(jax.ShapeDtypeStruct((bsz, length, D_MODEL), F32),
                   jax.ShapeDtypeStruct((bsz, length, D_PACK), U32),
                   jax.ShapeDtypeStruct((bsz, length, LANES), jnp.int32),
                   jax.ShapeDtypeStruct((bsz, length, LANES), F32),
                   jax.ShapeDtypeStruct((1, LANES), F32)),
        scratch_shapes=[pltpu.VMEM((1, LANES), F32)],
        compiler_params=_cparams(("arbitrary",), VMEM_LIMIT),
        name="out_proj",
    )(x, s5n_tm, attn, g_mla, w_out, g_ffn, w_r, b_r_p, tri, counts_in)


def _sc_mesh():
    return plsc.VectorSubcoreMesh(core_axis_name="c", subcore_axis_name="s",
                                  num_cores=SC_CORES, num_subcores=SC_SUBCORES)


def _sc_for_chunks(n_chunks, fn):
    wid = lax.axis_index("s") * SC_CORES + lax.axis_index("c")
    full, rem = divmod(n_chunks, SC_WORKERS)
    if full:
        @pl.loop(0, full)
        def _(j):
            fn(j * SC_WORKERS + wid)
    if rem:
        @pl.when(wid < rem)
        def _():
            fn(full * SC_WORKERS + wid)


def _sc_dispatch_body(dp_hbm, ds_hbm, hp_hbm, hs_hbm, xs_hbm, idx_v, rows_v, sems):
    n = SC_GATHER_ROWS
    wid = lax.axis_index("s") * SC_CORES + lax.axis_index("c")

    def scatter(d_hbm, h_hbm, c, slot):
        pltpu.sync_copy(d_hbm.at[c], idx_v.at[slot])
        pltpu.sync_copy(h_hbm.at[pl.ds(pl.multiple_of(c * n, n), n)], rows_v.at[slot])
        return [pltpu.async_copy(rows_v.at[slot], xs_hbm.at[idx_v.at[slot, k]], sems.at[slot])
                for k in range(TOP_K)]

    def run(d_hbm, h_hbm):
        full, rem = divmod(d_hbm.shape[0], SC_WORKERS)
        if full // 2:
            @pl.loop(0, full // 2)
            def _(j):
                first = scatter(d_hbm, h_hbm, (2 * j) * SC_WORKERS + wid, 0)
                second = scatter(d_hbm, h_hbm, (2 * j + 1) * SC_WORKERS + wid, 1)
                for cp in first + second:
                    cp.wait()
        tail = [(full - 1, None)] if full % 2 else []
        if rem:
            tail.append((full, rem))
        for row, limit in tail:
            def one(row=row):
                for cp in scatter(d_hbm, h_hbm, row * SC_WORKERS + wid, 0):
                    cp.wait()
            if limit is None:
                one()
            else:
                pl.when(wid < limit)(one)

    run(dp_hbm, hp_hbm)
    run(ds_hbm, hs_hbm)


def _dispatch(dest_p3, dest_s3, hn_p, hn_s, n_blocks):
    return pl.kernel(
        _sc_dispatch_body,
        out_type=jax.ShapeDtypeStruct((n_blocks * MOE_ROWS, D_PACK), U32),
        mesh=_sc_mesh(),
        scratch_types=[pltpu.VMEM((2, 8, SC_GATHER_ROWS), jnp.int32),
                       pltpu.VMEM((2, SC_GATHER_ROWS, D_PACK), U32),
                       pltpu.SemaphoreType.DMA((2,))],
        name="moe_dispatch_sc",
    )(dest_p3, dest_s3, hn_p, hn_s)


def _expert_kernel(be_ref, nb_ref, nv_ref, first_ref, slot_ref, nxt_ref,
                   x_ref, w1_hbm, b1_ref, w2_hbm, b2_ref, y_ref, w1f_ref, w2f_ref, w1b_ref, w2b_ref, sem):
    step = pl.program_id(0)

    def weight_copies(e):
        return (pltpu.make_async_copy(w1_hbm.at[e], w1f_ref, sem.at[0]),
                pltpu.make_async_copy(w2_hbm.at[e], w2f_ref, sem.at[1]))

    @pl.when(jnp.logical_and(step == 0, nb_ref[0] > 0))
    def _():
        for cp in weight_copies(be_ref[0]):
            cp.start()

    for sb in range(MOE_PAIR):
        b = step * MOE_PAIR + sb

        @pl.when(jnp.logical_and(b < nb_ref[0], first_ref[b] == 1))
        def _(b=b):
            s = slot_ref[b]
            for cp in weight_copies(be_ref[b]):
                cp.wait()
            w1b_ref[s] = w1f_ref[...].astype(BF16)
            w2b_ref[s] = w2f_ref[...].astype(BF16)

            @pl.when(nxt_ref[b] >= 0)
            def _():
                for cp in weight_copies(nxt_ref[b]):
                    cp.start()

    @pl.when(step * MOE_PAIR < nb_ref[0])
    def _():
        cw = D_FF // MOE_COL_CHUNKS
        for sb in range(MOE_PAIR):
            b = step * MOE_PAIR + sb
            e, s = be_ref[b], slot_ref[b]
            rs = slice(sb * MOE_ROWS, (sb + 1) * MOE_ROWS)
            live = lax.broadcasted_iota(jnp.int32, (MOE_ROWS, D_PACK), 0) < nv_ref[b]
            lo, hi = _unpack_rows(jnp.where(live, x_ref[rs, :], jnp.uint32(0)))
            lo, hi = lo.astype(BF16), hi.astype(BF16)
            b1 = b1_ref[e]

            def up(c0, lo=lo, hi=hi, s=s, b1=b1):
                return (jnp.dot(lo, w1b_ref[s, :D_PACK, c0:c0 + cw], preferred_element_type=F32)
                        + jnp.dot(hi, w1b_ref[s, D_PACK:, c0:c0 + cw], preferred_element_type=F32)
                        + b1[:, c0:c0 + cw])

            y = b2_ref[e]
            for j in range(MOE_COL_CHUNKS):
                gate = jnp.minimum(up(j * cw), SWIGLU_LIMIT)
                lin = jnp.clip(up(D_FF + j * cw), -SWIGLU_LIMIT, SWIGLU_LIMIT)
                act = gate * jax.nn.sigmoid(SWIGLU_ALPHA * gate) * (lin + 1.0)
                y = y + jnp.dot(act.astype(BF16), w2b_ref[s, j * cw:(j + 1) * cw, :],
                                preferred_element_type=F32)
            y_ref[rs, :] = _pack_rows(y)

    @pl.when(step * MOE_PAIR >= nb_ref[0])
    def _():
        y_ref[...] = jnp.zeros(y_ref.shape, y_ref.dtype)


def _experts(tables, xs, w1, b1, w2, b2, n_blocks):
    rows = MOE_PAIR * MOE_ROWS
    last = lambda p, be, nb, *_: (jnp.maximum(jnp.minimum(p, (nb[0] - 1) // MOE_PAIR), 0), 0)
    whole = lambda p, *_: (0, 0, 0)
    return pl.pallas_call(
        _expert_kernel,
        grid_spec=pltpu.PrefetchScalarGridSpec(
            num_scalar_prefetch=6,
            grid=(n_blocks // MOE_PAIR,),
            in_specs=[pl.BlockSpec((rows, D_PACK), last),
                      pl.BlockSpec(memory_space=pl.ANY),
                      pl.BlockSpec((N_EXPERTS, 1, 2 * D_FF), whole),
                      pl.BlockSpec(memory_space=pl.ANY),
                      pl.BlockSpec((N_EXPERTS, 1, D_MODEL), whole)],
            out_specs=pl.BlockSpec((rows, D_PACK), lambda p, *_: (p, 0)),
            scratch_shapes=[pltpu.VMEM((D_MODEL, 2 * D_FF), F32), pltpu.VMEM((D_FF, D_MODEL), F32),
                            pltpu.VMEM((MOE_PAIR, D_MODEL, 2 * D_FF), BF16),
                            pltpu.VMEM((MOE_PAIR, D_FF, D_MODEL), BF16),
                            pltpu.SemaphoreType.DMA((2,))]),
        out_shape=jax.ShapeDtypeStruct((n_blocks * MOE_ROWS, D_PACK), U32),
        compiler_params=_cparams(("arbitrary",), VMEM_LIMIT),
        name="moe_experts",
    )(*tables, xs, w1, b1.reshape(N_EXPERTS, 1, 2 * D_FF), w2, b2.reshape(N_EXPERTS, 1, D_MODEL))


def _sc_gather_body(dp_hbm, ds_hbm, yb_hbm, gp_hbm, gs_hbm, idx_v, rows_v, sems):
    n = SC_GATHER_ROWS

    def chunk(d_hbm, g_hbm, c):
        pltpu.sync_copy(d_hbm.at[c], idx_v)
        rows = pl.ds(pl.multiple_of(c * n, n), n)
        gather = lambda k: pltpu.async_copy(yb_hbm.at[idx_v.at[k]], rows_v.at[k % 2], sems.at[k % 2])
        cp = gather(0)
        for k in range(TOP_K):
            cp.wait()
            if k + 1 < TOP_K:
                cp = gather(k + 1)
            pltpu.sync_copy(rows_v.at[k % 2], g_hbm.at[k, rows])

    _sc_for_chunks(dp_hbm.shape[0], functools.partial(chunk, dp_hbm, gp_hbm))
    _sc_for_chunks(ds_hbm.shape[0], functools.partial(chunk, ds_hbm, gs_hbm))


def _gather_expert_rows(dest_p3, dest_s3, yb):
    n_p, n_s = dest_p3.shape[0] * SC_GATHER_ROWS, dest_s3.shape[0] * SC_GATHER_ROWS
    return pl.kernel(
        _sc_gather_body,
        out_type=(jax.ShapeDtypeStruct((TOP_K, n_p, D_PACK), U32),
                  jax.ShapeDtypeStruct((TOP_K, n_s, D_PACK), U32)),
        mesh=_sc_mesh(),
        scratch_types=[pltpu.VMEM((8, SC_GATHER_ROWS), jnp.int32),
                       pltpu.VMEM((2, SC_GATHER_ROWS, D_PACK), U32),
                       pltpu.SemaphoreType.DMA((2,))],
        name="moe_gather_sc",
    )(dest_p3, dest_s3, yb)


def _combine_kernel(g_ref, gate_ref, x1_ref, o_ref):
    gate = gate_ref[...]
    x1 = x1_ref[...]
    acc_lo, acc_hi = x1[:, :D_PACK], x1[:, D_PACK:]
    for k in range(TOP_K):
        lo, hi = _unpack_rows(g_ref[k])
        acc_lo = acc_lo + gate[:, k:k + 1] * lo
        acc_hi = acc_hi + gate[:, k:k + 1] * hi
    o_ref[:, :D_PACK] = acc_lo
    o_ref[:, D_PACK:] = acc_hi


def _combine(g, gates, x1):
    n_tok = x1.shape[0]
    tm = min(512, n_tok)
    return pl.pallas_call(
        _combine_kernel,
        grid=(n_tok // tm,),
        in_specs=[pl.BlockSpec((TOP_K, tm, D_PACK), lambda i: (0, i, 0)),
                  pl.BlockSpec((tm, LANES), lambda i: (i, 0)),
                  pl.BlockSpec((tm, D_MODEL), lambda i: (i, 0))],
        out_specs=pl.BlockSpec((tm, D_MODEL), lambda i: (i, 0)),
        out_shape=jax.ShapeDtypeStruct((n_tok, D_MODEL), F32),
        compiler_params=_cparams(("parallel",), VMEM_LIMIT),
        name="moe_combine",
    )(g, gates, x1)


def _route_tables(counts_f, n_assign):
    counts = counts_f[0, :N_EXPERTS].astype(jnp.int32)
    nblk_e = (counts + MOE_ROWS - 1) // MOE_ROWS
    bend = jnp.cumsum(nblk_e)
    bstart = bend - nblk_e
    pad_start = bstart * MOE_ROWS
    n_blocks = -(-(n_assign + N_EXPERTS * (MOE_ROWS - 1)) // (MOE_ROWS * MOE_PAIR)) * MOE_PAIR
    e_ids = jnp.arange(N_EXPERTS, dtype=jnp.int32)
    nonempty = counts > 0
    slot_e = (jnp.cumsum(nonempty.astype(jnp.int32)) - 1) % MOE_PAIR
    later = jnp.where(nonempty[None, :] & (e_ids[None, :] > e_ids[:, None]), e_ids[None, :], N_EXPERTS)
    next_e = jnp.min(later, axis=1)
    next_e = jnp.where(next_e == N_EXPERTS, -1, next_e)
    bidx = jnp.arange(n_blocks, dtype=jnp.int32)
    mine = ((bidx[:, None] >= bstart[None, :]) & (bidx[:, None] < bend[None, :])).astype(jnp.int32)
    vals = jnp.stack([e_ids, slot_e, next_e, pad_start + counts, bstart, jnp.ones_like(e_ids)], axis=0)
    per_block = jnp.sum(mine[:, None, :] * vals[None, :, :], axis=-1)
    block_e, slot, nxt, row_end, first_blk, used = (per_block[:, j] for j in range(6))
    n_valid = jnp.clip(row_end - bidx * MOE_ROWS, 0, MOE_ROWS)
    first = ((bidx == first_blk) & (used > 0)).astype(jnp.int32)
    tables = (block_e, bend[-1:].astype(jnp.int32), n_valid, first, slot, nxt)
    return pad_start, tables, n_blocks


def _dest_kernel(ps_ref, route_ref, d_ref):
    rt = route_ref[...].T
    idx, rank = rt[0:8, :], rt[8:16, :]
    base = jnp.zeros_like(idx)
    for e in range(N_EXPERTS):
        base = jnp.where(idx == e, ps_ref[e], base)
    dest = base + rank
    for j in range(dest.shape[1] // SC_GATHER_ROWS):
        d_ref[j] = dest[:, j * SC_GATHER_ROWS:(j + 1) * SC_GATHER_ROWS]


def _dest(pad_start, route):
    n_tok = route.shape[0]
    tm = min(2048, n_tok)
    n = SC_GATHER_ROWS
    return pl.pallas_call(
        _dest_kernel,
        grid_spec=pltpu.PrefetchScalarGridSpec(
            num_scalar_prefetch=1,
            grid=(n_tok // tm,),
            in_specs=[pl.BlockSpec((tm, LANES), lambda i, ps: (i, 0))],
            out_specs=pl.BlockSpec((tm // n, 8, n), lambda i, ps: (i, 0, 0))),
        out_shape=jax.ShapeDtypeStruct((n_tok // n, 8, n), jnp.int32),
        compiler_params=_cparams(("parallel",), VMEM_LIMIT),
        name="moe_dest",
    )(pad_start, route)


def _rope_tables(offset, length):
    f32 = np.float32
    pos = f32(offset) + np.arange(length, dtype=f32)
    inv = np.power(f32(ROPE_THETA), -np.arange(QK_ROPE // 2, dtype=f32) * f32(2.0 / QK_ROPE))
    ang = pos[:, None] * inv[None, :]
    cos, sin = np.cos(ang), np.sin(ang)
    z32 = np.zeros((length, 32), f32)
    rot_c = np.concatenate([cos, cos], axis=1)
    rot_s = np.concatenate([-sin, sin], axis=1)
    ck = np.concatenate([rot_c, z32, rot_c, z32], axis=1)
    sk = np.concatenate([rot_s, z32, rot_s, z32], axis=1)
    cq = np.concatenate([np.ones((length, QK_NOPE), f32), rot_c, z32], axis=1)
    sq = np.concatenate([np.zeros((length, QK_NOPE), f32), rot_s, z32], axis=1)
    return tuple(jnp.asarray(t, F32) for t in (ck, sk, cq, sq))


def _head_gain(g_nope, g_rope):
    return jnp.concatenate([g_nope, g_rope, g_rope, jnp.zeros((HEAD_PAD - QK_HEAD,), F32)]).reshape(1, HEAD_PAD)


def _mixer_stage(x, past, prm, counts_in):
    bsz, length, _ = x.shape
    offset = 0 if past is None else past[0].shape[1]
    ck, sk, cq, sq = _rope_tables(offset, length)
    u_tm, qn, lat, k_rope, krot = _in_proj(x, prm['g_mix'], prm['w_in_p'], prm['g_q_lat'], prm['g_kv_lat'], ck, sk)
    if past is None:
        h0 = jnp.zeros((bsz, S5_STATE_LANES), F32)
    else:
        h0 = _state_layout(past[2], past[3])
    s5n_tm, h_last = _s5(u_tm, h0, prm['lam'], prm['wb'], prm['wc'],
                         prm['d_skip'], prm['w_glu'], prm['b_glu'], prm['g_s5_out'], bsz, length)
    h_re, h_im = _state_unlayout(h_last)
    t = bsz * length
    q = _q_proj(qn.reshape(t, Q_LORA), prm['w_uq_p'], prm['w_uq_sw'], cq, sq, prm['gq'], bsz, length)
    k, v = _kv_proj(lat.reshape(t, KV_LORA), krot.reshape(t, LANES), prm['w_k'], prm['w_v'], prm['gk'],
                    bsz, length)
    if past is None:
        attn = _attn_prompt(q, k, v, bsz, length)
    else:
        n_past = past[0].shape[1]
        c_lat = past[0].reshape(bsz * n_past, KV_LORA)
        c_rot = jnp.pad(past[1].reshape(bsz * n_past, QK_ROPE), ((0, 0), (QK_NOPE, HEAD_PAD - QK_HEAD)))
        kc, vc = _kv_proj(c_lat, c_rot, prm['w_k'], prm['w_v'], prm['gk'], bsz, n_past)
        attn = _attn_sample(q, kc, vc, k, v, bsz, length, n_past)
    x1, hn, route, gates, counts = _out_proj(x, s5n_tm, attn, prm['g_mla_out'], prm['w_out'], prm['g_ffn'],
                                             prm['w_r'], prm['b_r_p'], counts_in)
    return (x1.reshape(t, D_MODEL), hn.reshape(t, D_PACK), route.reshape(t, LANES), gates.reshape(t, LANES),
            counts, lat, k_rope, h_re, h_im)


def _prepare(g_mix, w_in, lam_re, lam_im, log_dt, b_s5_re, b_s5_im, c_s5_re, c_s5_im, d_s5, w_glu, b_glu,
             g_q_lat, w_uq, g_kv_lat, w_ukv, g_qn_nope, g_qn_rope, g_kn_nope, g_kn_rope, g_s5_out,
             g_mla_out, w_out, g_ffn, w_router, b_router):
    c0 = S5_WIDTH + Q_LORA + KV_LORA
    w_pe = w_in[:, c0:]
    z = jnp.zeros((D_MODEL, 32), F32)
    w_in_p = jnp.concatenate([w_in[:, :c0], w_pe, z, w_pe, z], axis=1).astype(BF16)
    ar, ai, bbr, bbi = _s5_prep(lam_re, lam_im, log_dt, b_s5_re, b_s5_im)
    wb, wc = _s5_weights(bbr, bbi, c_s5_re, c_s5_im)
    w_uq_p = jnp.pad(w_uq.reshape(Q_LORA, MLA_HEADS, QK_HEAD), ((0, 0), (0, 0), (0, HEAD_PAD - QK_HEAD)))
    r0, r1, r2 = QK_NOPE, QK_NOPE + QK_ROPE // 2, QK_HEAD
    w_uq_sw = jnp.zeros_like(w_uq_p).at[:, :, r0:r1].set(w_uq_p[:, :, r1:r2]).at[:, :, r1:r2].set(w_uq_p[:, :, r0:r1])
    w_kv = w_ukv.reshape(KV_LORA, MLA_HEADS, QK_NOPE + V_HEAD)
    w_k = jnp.pad(w_kv[:, :, :QK_NOPE], ((0, 0), (0, 0), (0, HEAD_PAD - QK_NOPE)))
    w_v = w_kv[:, :, QK_NOPE:]
    w_r_p = jnp.pad(w_router, ((0, 0), (0, LANES - N_EXPERTS)))
    w_r_hi = w_r_p.astype(BF16)
    return dict(
        g_mix=g_mix.reshape(1, -1), w_in_p=w_in_p,
        g_q_lat=g_q_lat.reshape(1, -1), g_kv_lat=g_kv_lat.reshape(1, -1),
        lam=_state_layout(ar, ai).reshape(1, S5_STATE_LANES), wb=wb, wc=wc,
        d_skip=d_s5.reshape(1, -1), w_glu=w_glu.astype(BF16), b_glu=b_glu.reshape(1, -1),
        g_s5_out=g_s5_out.reshape(1, -1),
        w_uq_p=w_uq_p.reshape(Q_LORA, MLA_HEADS * HEAD_PAD).astype(BF16),
        w_uq_sw=w_uq_sw.reshape(Q_LORA, MLA_HEADS * HEAD_PAD).astype(BF16),
        w_k=w_k.reshape(KV_LORA, MLA_HEADS * HEAD_PAD).astype(BF16),
        w_v=w_v.reshape(KV_LORA, MLA_WIDTH).astype(BF16),
        gq=_head_gain(g_qn_nope, g_qn_rope), gk=_head_gain(g_kn_nope, g_kn_rope),
        g_mla_out=g_mla_out.reshape(1, -1), w_out=w_out.astype(BF16), g_ffn=g_ffn.reshape(1, -1),
        w_r=jnp.concatenate([w_r_hi, (w_r_p - w_r_hi.astype(F32)).astype(BF16)], axis=1),
        b_r_p=jnp.pad(b_router, (0, LANES - N_EXPERTS)).reshape(1, LANES),
    )


def _layer(xp, xs, cache_lat, cache_kr, st_re, st_im, mixer_w, w1, b1, w2, b2):
    prm = _prepare(*mixer_w)
    bp, lp, _ = xp.shape
    bs, ls, _ = xs.shape
    zero_counts = jnp.zeros((1, LANES), F32)
    x1p, hnp, routep, gatep, counts_p, latp, krp, hrp, hip = _mixer_stage(xp, None, prm, zero_counts)
    x1s, hns, routes, gates, counts, lats, krs, hrs, his = _mixer_stage(
        xs, (cache_lat, cache_kr, st_re, st_im), prm, counts_p)
    n_tok = bp * lp + bs * ls
    pad_start, tables, n_blocks = _route_tables(counts, n_tok * TOP_K)
    dest_p = _dest(pad_start, routep)
    dest_s = _dest(pad_start, routes)
    xs_sorted = _dispatch(dest_p, dest_s, hnp, hns, n_blocks)
    yb = _experts(tables, xs_sorted, w1, b1, w2, b2, n_blocks)
    g_p, g_s = _gather_expert_rows(dest_p, dest_s, yb)
    yp = _combine(g_p, gatep, x1p).reshape(bp, lp, D_MODEL)
    ys = _combine(g_s, gates, x1s).reshape(bs, ls, D_MODEL)
    return yp, ys, latp, krp, hrp, hip, lats, krs, hrs, his


def kernel(x_prompt, x_sample, cache_kv_latent, cache_k_rope, state_s5_re, state_s5_im, g_mix, w_in, lam_re,
           lam_im, log_dt, b_s5_re, b_s5_im, c_s5_re, c_s5_im, d_s5, w_glu, b_glu, g_q_lat, w_uq, g_kv_lat,
           w_ukv, g_qn_nope, g_qn_rope, g_kn_nope, g_kn_rope, g_s5_out, g_mla_out, w_out, g_ffn, w_router,
           b_router, w_mlp1, b_mlp1, w_mlp2, b_mlp2):
    depth = g_mix.shape[0]
    yp, ys = x_prompt, x_sample
    outs = [[] for _ in range(8)]
    for l in range(depth):
        mixer_w = (g_mix[l], w_in[l], lam_re[l], lam_im[l], log_dt[l], b_s5_re[l], b_s5_im[l], c_s5_re[l],
                   c_s5_im[l], d_s5[l], w_glu[l], b_glu[l], g_q_lat[l], w_uq[l], g_kv_lat[l], w_ukv[l],
                   g_qn_nope[l], g_qn_rope[l], g_kn_nope[l], g_kn_rope[l], g_s5_out[l], g_mla_out[l],
                   w_out[l], g_ffn[l], w_router[l], b_router[l])
        res = _layer(yp, ys, cache_kv_latent[l], cache_k_rope[l], state_s5_re[l], state_s5_im[l], mixer_w,
                     w_mlp1[l], b_mlp1[l], w_mlp2[l], b_mlp2[l])
        yp, ys = res[0], res[1]
        for o, r in zip(outs, res[2:]):
            o.append(r)
    return (yp, ys) + tuple(jnp.stack(o) for o in outs)
```

```python
import functools
import math

import jax
import jax.numpy as jnp
import numpy as np
from jax import lax
from jax.experimental import pallas as pl
from jax.experimental.pallas import tpu as pltpu
from jax.experimental.pallas import tpu_sc as plsc

F32 = jnp.float32
BF16 = jnp.bfloat16
U32 = jnp.uint32

D_MODEL = 1024
S5_WIDTH = 512
S5_GROUP = 16
S5_GROUPS = 32
S5_STATE = 64
MLA_HEADS = 8
QK_NOPE = 64
QK_ROPE = 32
QK_HEAD = QK_NOPE + QK_ROPE
V_HEAD = 64
MLA_WIDTH = MLA_HEADS * V_HEAD
Q_LORA = 384
KV_LORA = 256
ROPE_THETA = 10000.0
CHUNK = 64
N_EXPERTS = 32
TOP_K = 4
D_FF = D_MODEL
SWIGLU_LIMIT = 7.0
SWIGLU_ALPHA = 1.702
EPS = 1e-6

LANES = 128
HEAD_PAD = 128
D_IN_PAD = 1280
S5_CHUNKS = 4
S5_CHUNK_LANES = 1024
S5_STATE_LANES = S5_CHUNKS * S5_CHUNK_LANES
ROW_TILE = 1024
SUB_TILE = 512
S5_ROWS = 512
PROMPT_GROUPS = 2
MOE_ROWS = 256
MOE_COL_CHUNKS = 2
MOE_PAIR = 4
VMEM_LIMIT = 56 * 1024 * 1024
SC_CORES = 2
SC_SUBCORES = 16
SC_WORKERS = SC_CORES * SC_SUBCORES
SC_GATHER_ROWS = 64
D_PACK = D_MODEL // 2


def _cparams(sem, vmem=None):
    return pltpu.CompilerParams(dimension_semantics=sem, vmem_limit_bytes=vmem)


def _rms(x, g, n=None):
    n = x.shape[-1] if n is None else n
    ms = jnp.sum(x * x, axis=-1, keepdims=True) * (1.0 / n)
    return x * lax.rsqrt(ms + EPS) * g


def _sub_tiles(rows):
    sub = min(SUB_TILE, rows)
    return [slice(r, r + sub) for r in range(0, rows, sub)]


def _pack_rows(x):
    lo = lax.bitcast_convert_type(x[:, :D_PACK].astype(BF16).astype(F32), U32)
    hi = lax.bitcast_convert_type(x[:, D_PACK:].astype(BF16).astype(F32), U32)
    return (lo >> 16) | (hi & jnp.uint32(0xFFFF0000))


def _unpack_rows(w):
    lo = lax.bitcast_convert_type(w << 16, F32)
    hi = lax.bitcast_convert_type(w & jnp.uint32(0xFFFF0000), F32)
    return lo, hi


def _prep_kernel(lr_ref, li_ref, ldt_ref, br_ref, bi_ref, ar_ref, ai_ref, bbr_ref, bbi_ref):
    lr = lr_ref[...]
    li = li_ref[...]
    dt = jnp.exp(ldt_ref[...])
    mag = jnp.exp(lr * dt)
    ar = mag * jnp.cos(li * dt)
    ai = mag * jnp.sin(li * dt)
    ar_ref[...] = ar
    ai_ref[...] = ai
    den = lr * lr + li * li
    cr = ((ar - 1.0) * lr + ai * li) / den
    ci = (ai * lr - (ar - 1.0) * li) / den
    br = br_ref[...]
    bi = bi_ref[...]
    bbr_ref[...] = cr[:, None, :] * br - ci[:, None, :] * bi
    bbi_ref[...] = cr[:, None, :] * bi + ci[:, None, :] * br


def _s5_prep(lam_re, lam_im, log_dt, b_re, b_im):
    g, n = lam_re.shape
    p = b_re.shape[-1]
    out = pl.pallas_call(
        _prep_kernel,
        out_shape=(jax.ShapeDtypeStruct((g, n), F32), jax.ShapeDtypeStruct((g, n), F32),
                   jax.ShapeDtypeStruct((g, p, n), F32), jax.ShapeDtypeStruct((g, p, n), F32)),
        name="s5_prep",
    )(lam_re, lam_im, log_dt.reshape(g, 1), jnp.swapaxes(b_re, 1, 2), jnp.swapaxes(b_im, 1, 2))
    return out


def _state_layout(re, im):
    lead = re.shape[:-2]
    re = re.reshape(lead + (S5_CHUNKS, 512))
    im = im.reshape(lead + (S5_CHUNKS, 512))
    return jnp.stack([re, im], axis=-2).reshape(lead + (S5_STATE_LANES,))


def _state_unlayout(h):
    lead = h.shape[:-1]
    h = h.reshape(lead + (S5_CHUNKS, 2, 512))
    re = h[..., 0, :].reshape(lead + (S5_GROUPS, S5_STATE))
    im = h[..., 1, :].reshape(lead + (S5_GROUPS, S5_STATE))
    return re, im


def _s5_weights(bbr, bbi, c_re, c_im):
    eye8 = jnp.eye(8, dtype=F32)

    def blockdiag(m):
        a, b = m.shape[1], m.shape[2]
        return (eye8[:, None, :, None] * m[:, :, None, :]).reshape(8 * a, 8 * b)

    wb, wc = [], []
    for c in range(S5_CHUNKS):
        sl = slice(8 * c, 8 * c + 8)
        wb.append(jnp.concatenate([blockdiag(bbr[sl]), blockdiag(bbi[sl])], axis=1))
        cr_t = jnp.swapaxes(c_re[sl], 1, 2)
        ci_t = jnp.swapaxes(c_im[sl], 1, 2)
        wc.append(jnp.concatenate([blockdiag(cr_t), -blockdiag(ci_t)], axis=0))
    return jnp.stack(wb).astype(BF16), jnp.stack(wc).astype(BF16)


def _in_proj_kernel(x_ref, gmix_ref, w_ref, gq_ref, gkv_ref, ck_ref, sk_ref, *rest, grp):
    u_ref, qn_ref, lat_ref, krot_ref, latall_ref, krope_ref = rest[-6:]
    bsz = x_ref.shape[0]
    if latall_ref.shape[0] != bsz:
        latall_ref[...] = jnp.zeros(latall_ref.shape, F32)
        krope_ref[...] = jnp.zeros(krope_ref.shape, F32)
        latall_ref, krope_ref = latall_ref.at[grp * bsz:(grp + 1) * bsz], krope_ref.at[grp * bsz:(grp + 1) * bsz]
    steps = x_ref.shape[1]
    nb = max(1, min(bsz, SUB_TILE // steps))
    ck, sk = ck_ref[...][None], sk_ref[...][None]
    for b0 in range(0, bsz, nb):
        bs = slice(b0, b0 + nb)
        rows = nb * steps
        xn = _rms(x_ref[bs].reshape(rows, D_MODEL), gmix_ref[...]).astype(BF16)
        z = jnp.dot(xn, w_ref[...], preferred_element_type=F32)
        for bl in range(nb):
            for c in range(S5_CHUNKS):
                u_ref[c, pl.ds(b0 + bl, steps, stride=bsz), :] = z[bl * steps:(bl + 1) * steps,
                                                                   c * LANES:(c + 1) * LANES]
        qn = _rms(z[:, S5_WIDTH:S5_WIDTH + Q_LORA], gq_ref[...]).astype(BF16)
        qn_ref[bs] = qn.reshape(nb, steps, Q_LORA)
        c0 = S5_WIDTH + Q_LORA
        lat = _rms(z[:, c0:c0 + KV_LORA], gkv_ref[...]).reshape(nb, steps, KV_LORA)
        lat_ref[bs] = lat
        latall_ref[bs] = lat
        kp = z[:, c0 + KV_LORA:]
        lane = lax.broadcasted_iota(jnp.int32, kp.shape, 1)
        first_half = (lane % 64) < 16
        sw = jnp.where(first_half, pltpu.roll(kp, LANES - 16, axis=1), pltpu.roll(kp, 16, axis=1))
        kr = kp.reshape(nb, steps, LANES) * ck + sw.reshape(nb, steps, LANES) * sk
        krope_ref[bs] = kr[:, :, :QK_ROPE]
        krot_ref[bs] = jnp.where(lane.reshape(nb, steps, LANES) >= 64, kr, 0.0)


def _in_proj(x, grp, bsz, g_mix, w_in_p, g_q_lat, g_kv_lat, ck, sk, prev):
    n_all, length, _ = x.shape
    steps = min(ROW_TILE // bsz, length)
    full = lambda i: (0, 0)
    blk = lambda w: pl.BlockSpec((bsz, steps, w), lambda i: (0, i, 0))
    if prev:
        shared = lambda w: pl.BlockSpec((bsz, steps, w), lambda i: (grp, i, 0))
    else:
        shared = lambda w: pl.BlockSpec((n_all, steps, w), lambda i: (0, i, 0))
    return pl.pallas_call(
        functools.partial(_in_proj_kernel, grp=grp),
        grid=(length // steps,),
        in_specs=[pl.BlockSpec((bsz, steps, D_MODEL), lambda i: (grp, i, 0)),
                  pl.BlockSpec((1, D_MODEL), full),
                  pl.BlockSpec((D_MODEL, D_IN_PAD), full),
                  pl.BlockSpec((1, Q_LORA), full),
                  pl.BlockSpec((1, KV_LORA), full),
                  pl.BlockSpec((steps, LANES), lambda i: (i, 0)),
                  pl.BlockSpec((steps, LANES), lambda i: (i, 0))]
                 + [pl.BlockSpec(memory_space=pl.ANY)] * len(prev),
        out_specs=[pl.BlockSpec((S5_CHUNKS, steps * bsz, LANES), lambda i: (0, i, 0)),
                   blk(Q_LORA), blk(KV_LORA), blk(LANES),
                   shared(KV_LORA), shared(QK_ROPE)],
        out_shape=(jax.ShapeDtypeStruct((S5_CHUNKS, length * bsz, LANES), F32),
                   jax.ShapeDtypeStruct((bsz, length, Q_LORA), BF16),
                   jax.ShapeDtypeStruct((bsz, length, KV_LORA), F32),
                   jax.ShapeDtypeStruct((bsz, length, LANES), F32),
                   jax.ShapeDtypeStruct((n_all, length, KV_LORA), F32),
                   jax.ShapeDtypeStruct((n_all, length, QK_ROPE), F32)),
        input_output_aliases={7 + j: 4 + j for j in range(len(prev))},
        compiler_params=_cparams(("parallel",), VMEM_LIMIT),
        name="in_proj",
    )(x, g_mix, w_in_p, g_q_lat, g_kv_lat, ck, sk, *prev)


def _s5_kernel(u_ref, h0_ref, lam_ref, wb_ref, wc_ref, dskip_ref, wglu_ref, bglu_ref, gout_ref,
               y_ref, hlast_ref, bu_ref, h_ref, *, bsz, steps):
    c_id = pl.program_id(0)

    @pl.when(c_id == 0)
    def _():
        h_ref[...] = h0_ref[...]

    for c in range(S5_CHUNKS):
        bu_ref[:, c * S5_CHUNK_LANES:(c + 1) * S5_CHUNK_LANES] = jnp.dot(
            u_ref[c].astype(BF16), wb_ref[c], preferred_element_type=F32)

    for c in range(S5_CHUNKS):
        re = slice(c * S5_CHUNK_LANES, c * S5_CHUNK_LANES + 512)
        im = slice(c * S5_CHUNK_LANES + 512, (c + 1) * S5_CHUNK_LANES)
        lam_r = jnp.broadcast_to(lam_ref[:, re], (bsz, 512))
        lam_i = jnp.broadcast_to(lam_ref[:, im], (bsz, 512))

        hr, hi = h_ref[:, re], h_ref[:, im]
        for t in range(steps):
            rows = slice(t * bsz, (t + 1) * bsz)
            hr, hi = (lam_r * hr - lam_i * hi + bu_ref[rows, re],
                      lam_r * hi + lam_i * hr + bu_ref[rows, im])
            bu_ref[rows, re] = hr
            bu_ref[rows, im] = hi
        h_ref[:, re] = hr
        h_ref[:, im] = hi

    ys = []
    for c in range(S5_CHUNKS):
        hs = bu_ref[:, c * S5_CHUNK_LANES:(c + 1) * S5_CHUNK_LANES].astype(BF16)
        ys.append(jnp.dot(hs, wc_ref[c], preferred_element_type=F32))
    u = jnp.concatenate([u_ref[c] for c in range(S5_CHUNKS)], axis=1)
    y = jnp.concatenate(ys, axis=1) + dskip_ref[...] * u
    y = jax.nn.gelu(y)
    gate = jnp.dot(y.astype(BF16), wglu_ref[...], preferred_element_type=F32) + bglu_ref[...]
    y = _rms(y * jax.nn.sigmoid(gate), gout_ref[...])
    for c in range(S5_CHUNKS):
        y_ref[c] = y[:, c * LANES:(c + 1) * LANES]

    @pl.when(c_id == pl.num_programs(0) - 1)
    def _():
        hlast_ref[...] = h_ref[...]


def _s5(u_tm, h0, lam, wb, wc, d_skip, w_glu, b_glu, g_out, bsz, length):
    steps = min(S5_ROWS // bsz, length)
    rows = steps * bsz
    full2 = lambda c: (0, 0)
    full3 = lambda c: (0, 0, 0)
    return pl.pallas_call(
        functools.partial(_s5_kernel, bsz=bsz, steps=steps),
        grid=(length // steps,),
        in_specs=[pl.BlockSpec((S5_CHUNKS, rows, LANES), lambda c: (0, c, 0)),
                  pl.BlockSpec((bsz, S5_STATE_LANES), full2),
                  pl.BlockSpec((1, S5_STATE_LANES), full2),
                  pl.BlockSpec((S5_CHUNKS, LANES, S5_CHUNK_LANES), full3),
                  pl.BlockSpec((S5_CHUNKS, S5_CHUNK_LANES, LANES), full3),
                  pl.BlockSpec((1, S5_WIDTH), full2),
                  pl.BlockSpec((S5_WIDTH, S5_WIDTH), full2),
                  pl.BlockSpec((1, S5_WIDTH), full2),
                  pl.BlockSpec((1, S5_WIDTH), full2)],
        out_specs=[pl.BlockSpec((S5_CHUNKS, rows, LANES), lambda c: (0, c, 0)),
                   pl.BlockSpec((bsz, S5_STATE_LANES), full2)],
        out_shape=(jax.ShapeDtypeStruct((S5_CHUNKS, length * bsz, LANES), F32),
                   jax.ShapeDtypeStruct((bsz, S5_STATE_LANES), F32)),
        scratch_shapes=[pltpu.VMEM((rows, S5_STATE_LANES), F32),
                        pltpu.VMEM((bsz, S5_STATE_LANES), F32)],
        compiler_params=_cparams(("arbitrary",), VMEM_LIMIT),
        name="s5_mixer",
    )(u_tm, h0, lam, wb, wc, d_skip, w_glu, b_glu, g_out)


def _q_proj_kernel(qn_ref, w_ref, wsw_ref, ones_ref, cq_ref, sq_ref, g_ref, q_ref):
    g = g_ref[...] * (QK_HEAD ** -0.5)
    for rs in _sub_tiles(qn_ref.shape[0]):
        qn = qn_ref[rs, :]
        qf = jnp.dot(qn, w_ref[...], preferred_element_type=F32)
        qs = jnp.dot(qn, wsw_ref[...], preferred_element_type=F32)
        cq, sq = cq_ref[rs, :], sq_ref[rs, :]
        for p in range(MLA_HEADS // 2):
            cols = slice(2 * p * HEAD_PAD, 2 * (p + 1) * HEAD_PAD)
            xr = qf[:, cols] * cq + qs[:, cols] * sq
            sqr = xr * xr
            hi = sqr.astype(BF16)
            lo = (sqr - hi.astype(F32)).astype(BF16)
            ss = (jnp.dot(hi, ones_ref[...], preferred_element_type=F32)
                  + jnp.dot(lo, ones_ref[...], preferred_element_type=F32))
            q = (xr * lax.rsqrt(ss * (1.0 / QK_HEAD) + EPS) * g).astype(BF16)
            q_ref[0, 2 * p, rs, :] = q[:, :HEAD_PAD]
            q_ref[0, 2 * p + 1, rs, :] = q[:, HEAD_PAD:]


def _q_proj(qn, w_uq_p, w_uq_sw, cq, sq, gq, bsz, length):
    tm = min(ROW_TILE, length)
    n_l = length // tm
    full = lambda b, i: (0, 0)
    pair = 2 * HEAD_PAD
    two = lambda a: jnp.concatenate([a, a], axis=1)
    ones2 = (jnp.arange(pair)[:, None] // HEAD_PAD == jnp.arange(pair)[None, :] // HEAD_PAD).astype(BF16)
    return pl.pallas_call(
        _q_proj_kernel,
        grid=(bsz, n_l),
        in_specs=[pl.BlockSpec((tm, Q_LORA), lambda b, i: (b * n_l + i, 0)),
                  pl.BlockSpec((Q_LORA, MLA_HEADS * HEAD_PAD), full),
                  pl.BlockSpec((Q_LORA, MLA_HEADS * HEAD_PAD), full),
                  pl.BlockSpec((pair, pair), full),
                  pl.BlockSpec((tm, pair), lambda b, i: (i, 0)),
                  pl.BlockSpec((tm, pair), lambda b, i: (i, 0)),
                  pl.BlockSpec((1, pair), full)],
        out_specs=pl.BlockSpec((1, MLA_HEADS, tm, HEAD_PAD), lambda b, i: (b, 0, i, 0)),
        out_shape=jax.ShapeDtypeStruct((bsz, MLA_HEADS, length, HEAD_PAD), BF16),
        compiler_params=_cparams(("parallel", "parallel"), VMEM_LIMIT),
        name="q_proj",
    )(qn, w_uq_p, w_uq_sw, ones2, two(cq), two(sq), two(gq))


def _kv_proj_kernel(lat_ref, krot_ref, wk_ref, wv_ref, g_ref, k_ref, v_ref):
    g = g_ref[...]
    for rs in _sub_tiles(lat_ref.shape[0]):
        lat = lat_ref[rs, :].astype(BF16)
        kf = jnp.dot(lat, wk_ref[...], preferred_element_type=F32)
        vf = jnp.dot(lat, wv_ref[...], preferred_element_type=F32)
        krot = krot_ref[rs, :]
        for h in range(MLA_HEADS):
            k = kf[:, h * HEAD_PAD:(h + 1) * HEAD_PAD] + krot
            k_ref[0, h, rs, :] = _rms(k, g, QK_HEAD).astype(BF16)
        for p in range(MLA_HEADS // 2):
            v_ref[0, p, rs, :] = vf[:, p * LANES:(p + 1) * LANES].astype(BF16)


def _kv_proj(lat, krot, w_k, w_v, gk, bsz, length):
    tm = min(ROW_TILE, length)
    n_l = length // tm
    full = lambda b, i: (0, 0)
    row = lambda b, i: (b * n_l + i, 0)
    return pl.pallas_call(
        _kv_proj_kernel,
        grid=(bsz, n_l),
        in_specs=[pl.BlockSpec((tm, KV_LORA), row),
                  pl.BlockSpec((tm, HEAD_PAD), row),
                  pl.BlockSpec((KV_LORA, MLA_HEADS * HEAD_PAD), full),
                  pl.BlockSpec((KV_LORA, MLA_WIDTH), full),
                  pl.BlockSpec((1, HEAD_PAD), full)],
        out_specs=[pl.BlockSpec((1, MLA_HEADS, tm, HEAD_PAD), lambda b, i: (b, 0, i, 0)),
                   pl.BlockSpec((1, MLA_HEADS // 2, tm, LANES), lambda b, i: (b, 0, i, 0))],
        out_shape=(jax.ShapeDtypeStruct((bsz, MLA_HEADS, length, HEAD_PAD), BF16),
                   jax.ShapeDtypeStruct((bsz, MLA_HEADS // 2, length, LANES), BF16)),
        compiler_params=_cparams(("parallel", "parallel"), VMEM_LIMIT),
        name="kv_proj",
    )(lat, krot, w_k, w_v, gk)


def _scores(q, k):
    return lax.dot_general(q, k, (((1,), (1,)), ((), ())), preferred_element_type=F32)


def _attn_prompt_kernel(q_ref, k_ref, v_ref, o_ref, *, length, tq):
    n_q = length // tq
    row = lax.broadcasted_iota(jnp.int32, (tq, tq), 0)
    col = lax.broadcasted_iota(jnp.int32, (tq, tq), 1)
    visible = (col // CHUNK) <= (row // CHUNK)
    lane = lax.broadcasted_iota(jnp.int32, (tq, LANES), 1)
    for qi in range(n_q):
        q0 = qi * tq
        outs = []
        for hh in range(2):
            q = q_ref[0, hh, q0:q0 + tq, :]
            sd = jnp.where(visible, _scores(q, k_ref[0, hh, q0:q0 + tq, :]), -jnp.inf)
            m = jnp.max(sd, axis=-1, keepdims=True)
            if qi:
                so = _scores(q, k_ref[0, hh, 0:q0, :])
                m = jnp.maximum(m, jnp.max(so, axis=-1, keepdims=True))
            pd = jnp.exp(sd - m)
            l = jnp.sum(pd, axis=-1, keepdims=True)
            acc = jnp.dot(pd.astype(BF16), v_ref[0, 0, q0:q0 + tq, :], preferred_element_type=F32)
            if qi:
                po = jnp.exp(so - m)
                l = l + jnp.sum(po, axis=-1, keepdims=True)
                acc = acc + jnp.dot(po.astype(BF16), v_ref[0, 0, 0:q0, :], preferred_element_type=F32)
            outs.append(acc / l)
        o_ref[0, q0:q0 + tq, :] = jnp.where(lane < V_HEAD, outs[0], outs[1])


def _attn_prompt(q, k, v, bsz, length):
    tq = min(512, length)
    pairs = MLA_HEADS // 2
    return pl.pallas_call(
        functools.partial(_attn_prompt_kernel, length=length, tq=tq),
        grid=(bsz, pairs),
        in_specs=[pl.BlockSpec((1, 2, length, HEAD_PAD), lambda b, p: (b, p, 0, 0)),
                  pl.BlockSpec((1, 2, length, HEAD_PAD), lambda b, p: (b, p, 0, 0)),
                  pl.BlockSpec((1, 1, length, LANES), lambda b, p: (b, p, 0, 0))],
        out_specs=pl.BlockSpec((1, length, LANES), lambda b, p: (b, 0, p)),
        out_shape=jax.ShapeDtypeStruct((bsz, length, MLA_WIDTH), F32),
        compiler_params=_cparams(("parallel", "parallel"), VMEM_LIMIT),
        name="attn_prompt",
    )(q, k, v)


def _attn_sample_kernel(q_ref, kc_ref, vc_ref, kn_ref, vn_ref, o_ref):
    lq = q_ref.shape[2]
    lane = lax.broadcasted_iota(jnp.int32, (lq, LANES), 1)
    outs = []
    for hh in range(2):
        q = q_ref[0, hh]
        sc = _scores(q, kc_ref[0, hh])
        sn = _scores(q, kn_ref[0, hh])
        m = jnp.maximum(jnp.max(sc, axis=-1, keepdims=True), jnp.max(sn, axis=-1, keepdims=True))
        pc = jnp.exp(sc - m)
        pn = jnp.exp(sn - m)
        l = jnp.sum(pc, axis=-1, keepdims=True) + jnp.sum(pn, axis=-1, keepdims=True)
        acc = (jnp.dot(pc.astype(BF16), vc_ref[0, 0], preferred_element_type=F32)
               + jnp.dot(pn.astype(BF16), vn_ref[0, 0], preferred_element_type=F32))
        outs.append(acc / l)
    o_ref[0] = jnp.where(lane < V_HEAD, outs[0], outs[1])


def _attn_sample(q, kc, vc, kn, vn, bsz, lq, past):
    pairs = MLA_HEADS // 2
    hp = lambda b, p: (b, p, 0, 0)
    return pl.pallas_call(
        _attn_sample_kernel,
        grid=(bsz, pairs),
        in_specs=[pl.BlockSpec((1, 2, lq, HEAD_PAD), hp),
                  pl.BlockSpec((1, 2, past, HEAD_PAD), hp),
                  pl.BlockSpec((1, 1, past, LANES), hp),
                  pl.BlockSpec((1, 2, lq, HEAD_PAD), hp),
                  pl.BlockSpec((1, 1, lq, LANES), hp)],
        out_specs=pl.BlockSpec((1, lq, LANES), lambda b, p: (b, 0, p)),
        out_shape=jax.ShapeDtypeStruct((bsz, lq, MLA_WIDTH), F32),
        compiler_params=_cparams(("parallel", "parallel"), VMEM_LIMIT),
        name="attn_sample",
    )(q, kc, vc, kn, vn)


def _out_proj_kernel(x_ref, s5_ref, at_ref, gmla_ref, wout_ref, gffn_ref, wr_ref, br_ref, tri_ref, cin_ref,
                     x1_ref, hn_ref, route_ref, gate_ref, cout_ref, run_ref):
    @pl.when(pl.program_id(0) == 0)
    def _():
        run_ref[...] = cin_ref[...]

    bsz, steps, _ = x_ref.shape
    nb = max(1, min(bsz, SUB_TILE // steps))
    for b0 in range(0, bsz, nb):
        bs = slice(b0, b0 + nb)
        rows = nb * steps
        s5 = jnp.concatenate(
            [jnp.concatenate([s5_ref[c, pl.ds(b0 + bl, steps, stride=bsz), :] for c in range(S5_CHUNKS)], axis=1)
             for bl in range(nb)], axis=0)
        an = _rms(at_ref[bs].reshape(rows, MLA_WIDTH), gmla_ref[...])
        merged = jnp.concatenate([s5.astype(BF16), an.astype(BF16)], axis=1)
        x1 = x_ref[bs].reshape(rows, D_MODEL) + jnp.dot(merged, wout_ref[...], preferred_element_type=F32)
        x1_ref[bs] = x1.reshape(nb, steps, D_MODEL)
        hn = _rms(x1, gffn_ref[...])
        hn_ref[bs] = _pack_rows(hn).reshape(nb, steps, D_PACK)
        hn_hi = hn.astype(BF16)
        hn_lo = (hn - hn_hi.astype(F32)).astype(BF16)
        parts = (jnp.dot(hn_hi, wr_ref[...], preferred_element_type=F32)
                 + jnp.dot(hn_lo, wr_ref[...], preferred_element_type=F32))
        logits = parts[:, :LANES] + parts[:, LANES:] + br_ref[...]
        lane = lax.broadcasted_iota(jnp.int32, logits.shape, 1)
        lane_f = lane.astype(F32)
        work = jnp.where(lane < N_EXPERTS, logits, -jnp.inf)
        vals, idxs = [], []
        for _ in range(TOP_K):
            m = jnp.max(work, axis=-1, keepdims=True)
            i = jnp.min(jnp.where(work == m, lane_f, float(LANES)), axis=-1, keepdims=True)
            vals.append(m)
            idxs.append(i)
            work = jnp.where(lane_f == i, -jnp.inf, work)
        es = [jnp.exp(v - vals[0]) for v in vals]
        den = es[0] + es[1] + es[2] + es[3]
        onehots = [(lane_f == idxs[k]).astype(F32) for k in range(TOP_K)]
        e_all = onehots[0] + onehots[1] + onehots[2] + onehots[3]
        before = jnp.dot(tri_ref[...], e_all.astype(BF16), preferred_element_type=F32) + run_ref[...]
        run_ref[...] = run_ref[...] + jnp.sum(e_all, axis=0, keepdims=True)
        route = jnp.zeros(logits.shape, F32)
        gate_out = jnp.zeros(logits.shape, F32)
        for k in range(TOP_K):
            rank = jnp.sum(onehots[k] * before, axis=-1, keepdims=True)
            route = jnp.where(lane == k, idxs[k], route)
            route = jnp.where(lane == 8 + k, rank, route)
            gate_out = jnp.where(lane == k, es[k] / den, gate_out)
        route_ref[bs] = route.astype(jnp.int32).reshape(nb, steps, LANES)
        gate_ref[bs] = gate_out.reshape(nb, steps, LANES)
    cout_ref[...] = run_ref[...]


def _out_proj(x, grp, s5n_tm, attn, g_mla, w_out, g_ffn, w_r, b_r_p, counts_in):
    bsz, length, _ = attn.shape
    steps = min(ROW_TILE // bsz, length)
    full = lambda i: (0, 0)
    blk = lambda w: pl.BlockSpec((bsz, steps, w), lambda i: (0, i, 0))
    sub = max(1, min(bsz, SUB_TILE // steps)) * steps
    tri = (jnp.arange(sub)[:, None] > jnp.arange(sub)[None, :]).astype(BF16)
    return pl.pallas_call(
        _out_proj_kernel,
        grid=(length // steps,),
        in_specs=[pl.BlockSpec((bsz, steps, D_MODEL), lambda i: (grp, i, 0)),
                  pl.BlockSpec((S5_CHUNKS, steps * bsz, LANES), lambda i: (0, i, 0)),
                  blk(MLA_WIDTH),
                  pl.BlockSpec((1, MLA_WIDTH), full),
                  pl.BlockSpec((D_MODEL, D_MODEL), full),
                  pl.BlockSpec((1, D_MODEL), full),
                  pl.BlockSpec((D_MODEL, 2 * LANES), full),
                  pl.BlockSpec((1, LANES), full),
                  pl.BlockSpec((sub, sub), full),
                  pl.BlockSpec((1, LANES), full)],
        out_specs=[blk(D_MODEL), blk(D_PACK), blk(LANES), blk(LANES),
                   pl.BlockSpec((1, LANES), full)],
        out_shape=(jax.ShapeDtypeStruct((bsz, length, D_MODEL), F32),
                   jax.ShapeDtypeStruct((bsz, length, D_PACK), U32),
                   jax.ShapeDtypeStruct((bsz, length, LANES), jnp.int32),
                   jax.ShapeDtypeStruct((bsz, length, LANES), F32),
                   jax.ShapeDtypeStruct((1, LANES), F32)),
        scratch_shapes=[pltpu.VMEM((1, LANES), F32)],
        compiler_params=_cparams(("arbitrary",), VMEM_LIMIT),
        name="out_proj",
    )(x, s5n_tm, attn, g_mla, w_out, g_ffn, w_r, b_r_p, tri, counts_in)


def _sc_mesh():
    return plsc.VectorSubcoreMesh(core_axis_name="c", subcore_axis_name="s",
                                  num_cores=SC_CORES, num_subcores=SC_SUBCORES)


def _sc_for_chunks(n_chunks, fn):
    wid = lax.axis_index("s") * SC_CORES + lax.axis_index("c")
    full, rem = divmod(n_chunks, SC_WORKERS)
    if full:
        @pl.loop(0, full)
        def _(j):
            fn(j * SC_WORKERS + wid)
    if rem:
        @pl.when(wid < rem)
        def _():
            fn(full * SC_WORKERS + wid)


def _sc_dispatch_body(*refs, n_seg):
    d_hbms, h_hbms = refs[:n_seg], refs[n_seg:2 * n_seg]
    xs_hbm, idx_v, rows_v, sems = refs[2 * n_seg:]
    n = SC_GATHER_ROWS
    wid = lax.axis_index("s") * SC_CORES + lax.axis_index("c")

    def scatter(d_hbm, h_hbm, c, slot):
        pltpu.sync_copy(d_hbm.at[c], idx_v.at[slot])
        pltpu.sync_copy(h_hbm.at[pl.ds(pl.multiple_of(c * n, n), n)], rows_v.at[slot])
        return [pltpu.async_copy(rows_v.at[slot], xs_hbm.at[idx_v.at[slot, k]], sems.at[slot])
                for k in range(TOP_K)]

    def run(d_hbm, h_hbm):
        full, rem = divmod(d_hbm.shape[0], SC_WORKERS)
        if full // 2:
            @pl.loop(0, full // 2)
            def _(j):
                first = scatter(d_hbm, h_hbm, (2 * j) * SC_WORKERS + wid, 0)
                second = scatter(d_hbm, h_hbm, (2 * j + 1) * SC_WORKERS + wid, 1)
                for cp in first + second:
                    cp.wait()
        tail = [(full - 1, None)] if full % 2 else []
        if rem:
            tail.append((full, rem))
        for row, limit in tail:
            def one(row=row):
                for cp in scatter(d_hbm, h_hbm, row * SC_WORKERS + wid, 0):
                    cp.wait()
            if limit is None:
                one()
            else:
                pl.when(wid < limit)(one)

    for d_hbm, h_hbm in zip(d_hbms, h_hbms):
        run(d_hbm, h_hbm)


def _dispatch(dests, hns, n_blocks):
    return pl.kernel(
        functools.partial(_sc_dispatch_body, n_seg=len(dests)),
        out_type=jax.ShapeDtypeStruct((n_blocks * MOE_ROWS, D_PACK), U32),
        mesh=_sc_mesh(),
        scratch_types=[pltpu.VMEM((2, 8, SC_GATHER_ROWS), jnp.int32),
                       pltpu.VMEM((2, SC_GATHER_ROWS, D_PACK), U32),
                       pltpu.SemaphoreType.DMA((2,))],
        name="moe_dispatch_sc",
    )(*dests, *hns)


def _expert_kernel(be_ref, nb_ref, nv_ref, first_ref, slot_ref, nxt_ref,
                   x_ref, w1_hbm, b1_ref, w2_hbm, b2_ref, y_ref, w1f_ref, w2f_ref, w1b_ref, w2b_ref, sem):
    step = pl.program_id(0)

    def weight_copies(e):
        return (pltpu.make_async_copy(w1_hbm.at[e], w1f_ref, sem.at[0]),
                pltpu.make_async_copy(w2_hbm.at[e], w2f_ref, sem.at[1]))

    @pl.when(jnp.logical_and(step == 0, nb_ref[0] > 0))
    def _():
        for cp in weight_copies(be_ref[0]):
            cp.start()

    for sb in range(MOE_PAIR):
        b = step * MOE_PAIR + sb

        @pl.when(jnp.logical_and(b < nb_ref[0], first_ref[b] == 1))
        def _(b=b):
            s = slot_ref[b]
            for cp in weight_copies(be_ref[b]):
                cp.wait()
            w1b_ref[s] = w1f_ref[...].astype(BF16)
            w2b_ref[s] = w2f_ref[...].astype(BF16)

            @pl.when(nxt_ref[b] >= 0)
            def _():
                for cp in weight_copies(nxt_ref[b]):
                    cp.start()

    @pl.when(step * MOE_PAIR < nb_ref[0])
    def _():
        cw = D_FF // MOE_COL_CHUNKS
        for sb in range(MOE_PAIR):
            b = step * MOE_PAIR + sb
            e, s = be_ref[b], slot_ref[b]
            rs = slice(sb * MOE_ROWS, (sb + 1) * MOE_ROWS)
            live = lax.broadcasted_iota(jnp.int32, (MOE_ROWS, D_PACK), 0) < nv_ref[b]
            lo, hi = _unpack_rows(jnp.where(live, x_ref[rs, :], jnp.uint32(0)))
            lo, hi = lo.astype(BF16), hi.astype(BF16)
            b1 = b1_ref[e]

            def up(c0, lo=lo, hi=hi, s=s, b1=b1):
                return (jnp.dot(lo, w1b_ref[s, :D_PACK, c0:c0 + cw], preferred_element_type=F32)
                        + jnp.dot(hi, w1b_ref[s, D_PACK:, c0:c0 + cw], preferred_element_type=F32)
                        + b1[:, c0:c0 + cw])

            y = b2_ref[e]
            for j in range(MOE_COL_CHUNKS):
                gate = jnp.minimum(up(j * cw), SWIGLU_LIMIT)
                lin = jnp.clip(up(D_FF + j * cw), -SWIGLU_LIMIT, SWIGLU_LIMIT)
                act = gate * jax.nn.sigmoid(SWIGLU_ALPHA * gate) * (lin + 1.0)
                y = y + jnp.dot(act.astype(BF16), w2b_ref[s, j * cw:(j + 1) * cw, :],
                                preferred_element_type=F32)
            y_ref[rs, :] = _pack_rows(y)

    @pl.when(step * MOE_PAIR >= nb_ref[0])
    def _():
        y_ref[...] = jnp.zeros(y_ref.shape, y_ref.dtype)


def _experts(tables, xs, w1, b1, w2, b2, n_blocks):
    rows = MOE_PAIR * MOE_ROWS
    last = lambda p, be, nb, *_: (jnp.maximum(jnp.minimum(p, (nb[0] - 1) // MOE_PAIR), 0), 0)
    whole = lambda p, *_: (0, 0, 0)
    return pl.pallas_call(
        _expert_kernel,
        grid_spec=pltpu.PrefetchScalarGridSpec(
            num_scalar_prefetch=6,
            grid=(n_blocks // MOE_PAIR,),
            in_specs=[pl.BlockSpec((rows, D_PACK), last),
                      pl.BlockSpec(memory_space=pl.ANY),
                      pl.BlockSpec((N_EXPERTS, 1, 2 * D_FF), whole),
                      pl.BlockSpec(memory_space=pl.ANY),
                      pl.BlockSpec((N_EXPERTS, 1, D_MODEL), whole)],
            out_specs=pl.BlockSpec((rows, D_PACK), lambda p, *_: (p, 0)),
            scratch_shapes=[pltpu.VMEM((D_MODEL, 2 * D_FF), F32), pltpu.VMEM((D_FF, D_MODEL), F32),
                            pltpu.VMEM((MOE_PAIR, D_MODEL, 2 * D_FF), BF16),
                            pltpu.VMEM((MOE_PAIR, D_FF, D_MODEL), BF16),
                            pltpu.SemaphoreType.DMA((2,))]),
        out_shape=jax.ShapeDtypeStruct((n_blocks * MOE_ROWS, D_PACK), U32),
        compiler_params=_cparams(("arbitrary",), VMEM_LIMIT),
        name="moe_experts",
    )(*tables, xs, w1, b1.reshape(N_EXPERTS, 1, 2 * D_FF), w2, b2.reshape(N_EXPERTS, 1, D_MODEL))


def _sc_gather_body(*refs, n_seg):
    d_hbms, yb_hbm = refs[:n_seg], refs[n_seg]
    g_hbms = refs[n_seg + 1:2 * n_seg + 1]
    idx_v, rows_v, sems = refs[2 * n_seg + 1:]
    n = SC_GATHER_ROWS

    def chunk(d_hbm, g_hbm, c):
        pltpu.sync_copy(d_hbm.at[c], idx_v)
        rows = pl.ds(pl.multiple_of(c * n, n), n)
        gather = lambda k: pltpu.async_copy(yb_hbm.at[idx_v.at[k]], rows_v.at[k % 2], sems.at[k % 2])
        cp = gather(0)
        for k in range(TOP_K):
            cp.wait()
            if k + 1 < TOP_K:
                cp = gather(k + 1)
            pltpu.sync_copy(rows_v.at[k % 2], g_hbm.at[k, rows])

    for d_hbm, g_hbm in zip(d_hbms, g_hbms):
        _sc_for_chunks(d_hbm.shape[0], functools.partial(chunk, d_hbm, g_hbm))


def _gather_expert_rows(dests, yb):
    return pl.kernel(
        functools.partial(_sc_gather_body, n_seg=len(dests)),
        out_type=tuple(jax.ShapeDtypeStruct((TOP_K, d.shape[0] * SC_GATHER_ROWS, D_PACK), U32) for d in dests),
        mesh=_sc_mesh(),
        scratch_types=[pltpu.VMEM((8, SC_GATHER_ROWS), jnp.int32),
                       pltpu.VMEM((2, SC_GATHER_ROWS, D_PACK), U32),
                       pltpu.SemaphoreType.DMA((2,))],
        name="moe_gather_sc",
    )(*dests, yb)


def _combine_kernel(g_ref, gate_ref, x1_ref, *rest, first_blk, n_blk):
    o_ref = rest[-1]
    i = pl.program_id(0)

    @pl.when(jnp.logical_and(i >= first_blk, i < first_blk + n_blk))
    def _():
        gate = gate_ref[...]
        x1 = x1_ref[...]
        acc_lo, acc_hi = x1[:, :D_PACK], x1[:, D_PACK:]
        for k in range(TOP_K):
            lo, hi = _unpack_rows(g_ref[k])
            acc_lo = acc_lo + gate[:, k:k + 1] * lo
            acc_hi = acc_hi + gate[:, k:k + 1] * hi
        o_ref[:, :D_PACK] = acc_lo
        o_ref[:, D_PACK:] = acc_hi

    @pl.when(jnp.logical_or(i < first_blk, i >= first_blk + n_blk))
    def _():
        o_ref[...] = jnp.zeros(o_ref.shape, o_ref.dtype)


def _combine(g, gates, x1, out_rows=None, row0=0, prev=None):
    n_tok = x1.shape[0]
    out_rows = n_tok if out_rows is None else out_rows
    tm = min(512, n_tok)
    n_blk = n_tok // tm
    first_blk, grid = (0, n_blk) if prev is not None else (row0 // tm, out_rows // tm)
    own = lambda i: jnp.clip(i - first_blk, 0, n_blk - 1)
    in_specs = [pl.BlockSpec((TOP_K, tm, D_PACK), lambda i: (0, own(i), 0)),
                pl.BlockSpec((tm, LANES), lambda i: (own(i), 0)),
                pl.BlockSpec((tm, D_MODEL), lambda i: (own(i), 0))]
    args = [g, gates, x1]
    out_blk0, aliases = 0, {}
    if prev is not None:
        in_specs.append(pl.BlockSpec(memory_space=pl.ANY))
        args.append(prev)
        out_blk0, aliases = row0 // tm, {3: 0}
    return pl.pallas_call(
        functools.partial(_combine_kernel, first_blk=first_blk, n_blk=n_blk),
        grid=(grid,),
        in_specs=in_specs,
        out_specs=pl.BlockSpec((tm, D_MODEL), lambda i: (out_blk0 + i, 0)),
        out_shape=jax.ShapeDtypeStruct((out_rows, D_MODEL), F32),
        input_output_aliases=aliases,
        compiler_params=_cparams(("parallel",), VMEM_LIMIT),
        name="moe_combine",
    )(*args)


def _route_tables(counts_f, n_assign):
    counts = counts_f[0, :N_EXPERTS].astype(jnp.int32)
    nblk_e = (counts + MOE_ROWS - 1) // MOE_ROWS
    bend = jnp.cumsum(nblk_e)
    bstart = bend - nblk_e
    pad_start = bstart * MOE_ROWS
    n_blocks = -(-(n_assign + N_EXPERTS * (MOE_ROWS - 1)) // (MOE_ROWS * MOE_PAIR)) * MOE_PAIR
    e_ids = jnp.arange(N_EXPERTS, dtype=jnp.int32)
    nonempty = counts > 0
    slot_e = (jnp.cumsum(nonempty.astype(jnp.int32)) - 1) % MOE_PAIR
    later = jnp.where(nonempty[None, :] & (e_ids[None, :] > e_ids[:, None]), e_ids[None, :], N_EXPERTS)
    next_e = jnp.min(later, axis=1)
    next_e = jnp.where(next_e == N_EXPERTS, -1, next_e)
    bidx = jnp.arange(n_blocks, dtype=jnp.int32)
    mine = ((bidx[:, None] >= bstart[None, :]) & (bidx[:, None] < bend[None, :])).astype(jnp.int32)
    vals = jnp.stack([e_ids, slot_e, next_e, pad_start + counts, bstart, jnp.ones_like(e_ids)], axis=0)
    per_block = jnp.sum(mine[:, None, :] * vals[None, :, :], axis=-1)
    block_e, slot, nxt, row_end, first_blk, used = (per_block[:, j] for j in range(6))
    n_valid = jnp.clip(row_end - bidx * MOE_ROWS, 0, MOE_ROWS)
    first = ((bidx == first_blk) & (used > 0)).astype(jnp.int32)
    tables = (block_e, bend[-1:].astype(jnp.int32), n_valid, first, slot, nxt)
    return pad_start, tables, n_blocks


def _dest_kernel(ps_ref, route_ref, d_ref):
    rt = route_ref[...].T
    idx, rank = rt[0:8, :], rt[8:16, :]
    base = jnp.zeros_like(idx)
    for e in range(N_EXPERTS):
        base = jnp.where(idx == e, ps_ref[e], base)
    dest = base + rank
    for j in range(dest.shape[1] // SC_GATHER_ROWS):
        d_ref[j] = dest[:, j * SC_GATHER_ROWS:(j + 1) * SC_GATHER_ROWS]


def _dest(pad_start, route):
    n_tok = route.shape[0]
    tm = min(2048, n_tok)
    n = SC_GATHER_ROWS
    return pl.pallas_call(
        _dest_kernel,
        grid_spec=pltpu.PrefetchScalarGridSpec(
            num_scalar_prefetch=1,
            grid=(n_tok // tm,),
            in_specs=[pl.BlockSpec((tm, LANES), lambda i, ps: (i, 0))],
            out_specs=pl.BlockSpec((tm // n, 8, n), lambda i, ps: (i, 0, 0))),
        out_shape=jax.ShapeDtypeStruct((n_tok // n, 8, n), jnp.int32),
        compiler_params=_cparams(("parallel",), VMEM_LIMIT),
        name="moe_dest",
    )(pad_start, route)


def _rope_tables(offset, length):
    f32 = np.float32
    pos = f32(offset) + np.arange(length, dtype=f32)
    inv = np.power(f32(ROPE_THETA), -np.arange(QK_ROPE // 2, dtype=f32) * f32(2.0 / QK_ROPE))
    ang = pos[:, None] * inv[None, :]
    cos, sin = np.cos(ang), np.sin(ang)
    z32 = np.zeros((length, 32), f32)
    rot_c = np.concatenate([cos, cos], axis=1)
    rot_s = np.concatenate([-sin, sin], axis=1)
    ck = np.concatenate([rot_c, z32, rot_c, z32], axis=1)
    sk = np.concatenate([rot_s, z32, rot_s, z32], axis=1)
    cq = np.concatenate([np.ones((length, QK_NOPE), f32), rot_c, z32], axis=1)
    sq = np.concatenate([np.zeros((length, QK_NOPE), f32), rot_s, z32], axis=1)
    return tuple(jnp.asarray(t, F32) for t in (ck, sk, cq, sq))


def _head_gain(g_nope, g_rope):
    return jnp.concatenate([g_nope, g_rope, g_rope, jnp.zeros((HEAD_PAD - QK_HEAD,), F32)]).reshape(1, HEAD_PAD)


def _mixer_stage(x, grp, bsz, past, prm, counts_in, prev=()):
    length = x.shape[1]
    offset = 0 if past is None else past[0].shape[1]
    ck, sk, cq, sq = _rope_tables(offset, length)
    u_tm, qn, lat, krot, lat_all, k_rope = _in_proj(x, grp, bsz, prm['g_mix'], prm['w_in_p'], prm['g_q_lat'],
                                                    prm['g_kv_lat'], ck, sk, prev)
    if past is None:
        h0 = jnp.zeros((bsz, S5_STATE_LANES), F32)
    else:
        h0 = _state_layout(past[2], past[3])
    s5n_tm, h_last = _s5(u_tm, h0, prm['lam'], prm['wb'], prm['wc'],
                         prm['d_skip'], prm['w_glu'], prm['b_glu'], prm['g_s5_out'], bsz, length)
    h_re, h_im = _state_unlayout(h_last)
    t = bsz * length
    q = _q_proj(qn.reshape(t, Q_LORA), prm['w_uq_p'], prm['w_uq_sw'], cq, sq, prm['gq'], bsz, length)
    k, v = _kv_proj(lat.reshape(t, KV_LORA), krot.reshape(t, LANES), prm['w_k'], prm['w_v'], prm['gk'],
                    bsz, length)
    if past is None:
        attn = _attn_prompt(q, k, v, bsz, length)
    else:
        n_past = past[0].shape[1]
        c_lat = past[0].reshape(bsz * n_past, KV_LORA)
        c_rot = jnp.pad(past[1].reshape(bsz * n_past, QK_ROPE), ((0, 0), (QK_NOPE, HEAD_PAD - QK_HEAD)))
        kc, vc = _kv_proj(c_lat, c_rot, prm['w_k'], prm['w_v'], prm['gk'], bsz, n_past)
        attn = _attn_sample(q, kc, vc, k, v, bsz, length, n_past)
    x1, hn, route, gates, counts = _out_proj(x, grp, s5n_tm, attn, prm['g_mla_out'], prm['w_out'], prm['g_ffn'],
                                             prm['w_r'], prm['b_r_p'], counts_in)
    tok = dict(x1=x1.reshape(t, D_MODEL), hn=hn.reshape(t, D_PACK), route=route.reshape(t, LANES),
               gates=gates.reshape(t, LANES))
    return tok, counts, (lat_all, k_rope, h_re, h_im)


def _prepare(g_mix, w_in, lam_re, lam_im, log_dt, b_s5_re, b_s5_im, c_s5_re, c_s5_im, d_s5, w_glu, b_glu,
             g_q_lat, w_uq, g_kv_lat, w_ukv, g_qn_nope, g_qn_rope, g_kn_nope, g_kn_rope, g_s5_out,
             g_mla_out, w_out, g_ffn, w_router, b_router):
    c0 = S5_WIDTH + Q_LORA + KV_LORA
    w_pe = w_in[:, c0:]
    z = jnp.zeros((D_MODEL, 32), F32)
    w_in_p = jnp.concatenate([w_in[:, :c0], w_pe, z, w_pe, z], axis=1).astype(BF16)
    ar, ai, bbr, bbi = _s5_prep(lam_re, lam_im, log_dt, b_s5_re, b_s5_im)
    wb, wc = _s5_weights(bbr, bbi, c_s5_re, c_s5_im)
    w_uq_p = jnp.pad(w_uq.reshape(Q_LORA, MLA_HEADS, QK_HEAD), ((0, 0), (0, 0), (0, HEAD_PAD - QK_HEAD)))
    r0, r1, r2 = QK_NOPE, QK_NOPE + QK_ROPE // 2, QK_HEAD
    w_uq_sw = jnp.zeros_like(w_uq_p).at[:, :, r0:r1].set(w_uq_p[:, :, r1:r2]).at[:, :, r1:r2].set(w_uq_p[:, :, r0:r1])
    w_kv = w_ukv.reshape(KV_LORA, MLA_HEADS, QK_NOPE + V_HEAD)
    w_k = jnp.pad(w_kv[:, :, :QK_NOPE], ((0, 0), (0, 0), (0, HEAD_PAD - QK_NOPE)))
    w_v = w_kv[:, :, QK_NOPE:]
    w_r_p = jnp.pad(w_router, ((0, 0), (0, LANES - N_EXPERTS)))
    w_r_hi = w_r_p.astype(BF16)
    return dict(
        g_mix=g_mix.reshape(1, -1), w_in_p=w_in_p,
        g_q_lat=g_q_lat.reshape(1, -1), g_kv_lat=g_kv_lat.reshape(1, -1),
        lam=_state_layout(ar, ai).reshape(1, S5_STATE_LANES), wb=wb, wc=wc,
        d_skip=d_s5.reshape(1, -1), w_glu=w_glu.astype(BF16), b_glu=b_glu.reshape(1, -1),
        g_s5_out=g_s5_out.reshape(1, -1),
        w_uq_p=w_uq_p.reshape(Q_LORA, MLA_HEADS * HEAD_PAD).astype(BF16),
        w_uq_sw=w_uq_sw.reshape(Q_LORA, MLA_HEADS * HEAD_PAD).astype(BF16),
        w_k=w_k.reshape(KV_LORA, MLA_HEADS * HEAD_PAD).astype(BF16),
        w_v=w_v.reshape(KV_LORA, MLA_WIDTH).astype(BF16),
        gq=_head_gain(g_qn_nope, g_qn_rope), gk=_head_gain(g_kn_nope, g_kn_rope),
        g_mla_out=g_mla_out.reshape(1, -1), w_out=w_out.astype(BF16), g_ffn=g_ffn.reshape(1, -1),
        w_r=jnp.concatenate([w_r_hi, (w_r_p - w_r_hi.astype(F32)).astype(BF16)], axis=1),
        b_r_p=jnp.pad(b_router, (0, LANES - N_EXPERTS)).reshape(1, LANES),
    )


def _moe_dispatch(segs, counts, w1, b1, w2, b2):
    n_tok = sum(seg['x1'].shape[0] for seg in segs)
    pad_start, tables, n_blocks = _route_tables(counts, n_tok * TOP_K)
    dests = [_dest(pad_start, seg['route']) for seg in segs]
    xs_sorted = _dispatch(dests, [seg['hn'] for seg in segs], n_blocks)
    return dests, _experts(tables, xs_sorted, w1, b1, w2, b2, n_blocks)


def _layer(xp, xs, cache_lat, cache_kr, st_re, st_im, mixer_w, w1, b1, w2, b2):
    prm = _prepare(*mixer_w)
    bp, lp, _ = xp.shape
    bs, ls, _ = xs.shape
    bg = bp // PROMPT_GROUPS
    zero_counts = jnp.zeros((1, LANES), F32)
    passes, prompt_outs = [], []
    for grp in range(PROMPT_GROUPS):
        tok, counts, outs = _mixer_stage(xp, grp, bg, None, prm, zero_counts,
                                         prompt_outs[-1][:2] if prompt_outs else ())
        prompt_outs.append(outs)
        segs = [tok]
        if grp == PROMPT_GROUPS - 1:
            tok_s, counts, sample_outs = _mixer_stage(xs, 0, bs, (cache_lat, cache_kr, st_re, st_im), prm, counts)
            segs.append(tok_s)
        passes.append((segs,) + _moe_dispatch(segs, counts, w1, b1, w2, b2))
    yp = None
    for grp, (segs, dests, yb) in enumerate(passes):
        gs = _gather_expert_rows(dests, yb)
        yp = _combine(gs[0], segs[0]['gates'], segs[0]['x1'], bp * lp, grp * bg * lp, yp)
        if len(segs) > 1:
            ys = _combine(gs[1], segs[1]['gates'], segs[1]['x1'])
    cat = lambda j: jnp.concatenate([o[j] for o in prompt_outs], axis=0)
    lat_p, krope_p = prompt_outs[-1][:2]
    return (yp.reshape(bp, lp, D_MODEL), ys.reshape(bs, ls, D_MODEL), lat_p, krope_p, cat(2), cat(3)) + sample_outs


def kernel(x_prompt, x_sample, cache_kv_latent, cache_k_rope, state_s5_re, state_s5_im, g_mix, w_in, lam_re,
           lam_im, log_dt, b_s5_re, b_s5_im, c_s5_re, c_s5_im, d_s5, w_glu, b_glu, g_q_lat, w_uq, g_kv_lat,
           w_ukv, g_qn_nope, g_qn_rope, g_kn_nope, g_kn_rope, g_s5_out, g_mla_out, w_out, g_ffn, w_router,
           b_router, w_mlp1, b_mlp1, w_mlp2, b_mlp2):
    depth = g_mix.shape[0]
    yp, ys = x_prompt, x_sample
    outs = [[] for _ in range(8)]
    for l in range(depth):
        mixer_w = (g_mix[l], w_in[l], lam_re[l], lam_im[l], log_dt[l], b_s5_re[l], b_s5_im[l], c_s5_re[l],
                   c_s5_im[l], d_s5[l], w_glu[l], b_glu[l], g_q_lat[l], w_uq[l], g_kv_lat[l], w_ukv[l],
                   g_qn_nope[l], g_qn_rope[l], g_kn_nope[l], g_kn_rope[l], g_s5_out[l], g_mla_out[l],
                   w_out[l], g_ffn[l], w_router[l], b_router[l])
        res = _layer(yp, ys, cache_kv_latent[l], cache_k_rope[l], state_s5_re[l], state_s5_im[l], mixer_w,
                     w_mlp1[l], b_mlp1[l], w_mlp2[l], b_mlp2[l])
        yp, ys = res[0], res[1]
        for o, r in zip(outs, res[2:]):
            o.append(r)
    return (yp, ys) + tuple(jnp.stack(o) for o in outs)
```

```python
import functools
import math

import jax
import jax.numpy as jnp
import numpy as np
from jax import lax
from jax.experimental import pallas as pl
from jax.experimental.pallas import tpu as pltpu
from jax.experimental.pallas import tpu_sc as plsc

F32 = jnp.float32
BF16 = jnp.bfloat16
U32 = jnp.uint32

D_MODEL = 1024
S5_WIDTH = 512
S5_GROUP = 16
S5_GROUPS = 32
S5_STATE = 64
MLA_HEADS = 8
QK_NOPE = 64
QK_ROPE = 32
QK_HEAD = QK_NOPE + QK_ROPE
V_HEAD = 64
MLA_WIDTH = MLA_HEADS * V_HEAD
Q_LORA = 384
KV_LORA = 256
ROPE_THETA = 10000.0
CHUNK = 64
N_EXPERTS = 32
TOP_K = 4
D_FF = D_MODEL
SWIGLU_LIMIT = 7.0
SWIGLU_ALPHA = 1.702
EPS = 1e-6

LANES = 128
HEAD_PAD = 128
D_IN_PAD = 1280
S5_CHUNKS = 4
S5_CHUNK_LANES = 1024
S5_STATE_LANES = S5_CHUNKS * S5_CHUNK_LANES
ROW_TILE = 1024
SUB_TILE = 512
S5_ROWS = 512
PROMPT_GROUPS = 2
MOE_ROWS = 512
MOE_COL_CHUNKS = 2
MOE_PAIR = 2
VMEM_LIMIT = 56 * 1024 * 1024
SC_CORES = 2
SC_SUBCORES = 16
SC_WORKERS = SC_CORES * SC_SUBCORES
SC_GATHER_ROWS = 64
D_PACK = D_MODEL // 2


def _cparams(sem, vmem=None):
    return pltpu.CompilerParams(dimension_semantics=sem, vmem_limit_bytes=vmem)


def _rms(x, g, n=None):
    n = x.shape[-1] if n is None else n
    ms = jnp.sum(x * x, axis=-1, keepdims=True) * (1.0 / n)
    return x * lax.rsqrt(ms + EPS) * g


def _sub_tiles(rows):
    sub = min(SUB_TILE, rows)
    return [slice(r, r + sub) for r in range(0, rows, sub)]


def _pack_rows(x):
    lo = lax.bitcast_convert_type(x[:, :D_PACK].astype(BF16).astype(F32), U32)
    hi = lax.bitcast_convert_type(x[:, D_PACK:].astype(BF16).astype(F32), U32)
    return (lo >> 16) | (hi & jnp.uint32(0xFFFF0000))


def _unpack_rows(w):
    lo = lax.bitcast_convert_type(w << 16, F32)
    hi = lax.bitcast_convert_type(w & jnp.uint32(0xFFFF0000), F32)
    return lo, hi


def _prep_kernel(lr_ref, li_ref, ldt_ref, br_ref, bi_ref, ar_ref, ai_ref, bbr_ref, bbi_ref):
    lr = lr_ref[...]
    li = li_ref[...]
    dt = jnp.exp(ldt_ref[...])
    mag = jnp.exp(lr * dt)
    ar = mag * jnp.cos(li * dt)
    ai = mag * jnp.sin(li * dt)
    ar_ref[...] = ar
    ai_ref[...] = ai
    den = lr * lr + li * li
    cr = ((ar - 1.0) * lr + ai * li) / den
    ci = (ai * lr - (ar - 1.0) * li) / den
    br = br_ref[...]
    bi = bi_ref[...]
    bbr_ref[...] = cr[:, None, :] * br - ci[:, None, :] * bi
    bbi_ref[...] = cr[:, None, :] * bi + ci[:, None, :] * br


def _s5_prep(lam_re, lam_im, log_dt, b_re, b_im):
    g, n = lam_re.shape
    p = b_re.shape[-1]
    out = pl.pallas_call(
        _prep_kernel,
        out_shape=(jax.ShapeDtypeStruct((g, n), F32), jax.ShapeDtypeStruct((g, n), F32),
                   jax.ShapeDtypeStruct((g, p, n), F32), jax.ShapeDtypeStruct((g, p, n), F32)),
        name="s5_prep",
    )(lam_re, lam_im, log_dt.reshape(g, 1), jnp.swapaxes(b_re, 1, 2), jnp.swapaxes(b_im, 1, 2))
    return out


def _state_layout(re, im):
    lead = re.shape[:-2]
    re = re.reshape(lead + (S5_CHUNKS, 512))
    im = im.reshape(lead + (S5_CHUNKS, 512))
    return jnp.stack([re, im], axis=-2).reshape(lead + (S5_STATE_LANES,))


def _state_unlayout(h):
    lead = h.shape[:-1]
    h = h.reshape(lead + (S5_CHUNKS, 2, 512))
    re = h[..., 0, :].reshape(lead + (S5_GROUPS, S5_STATE))
    im = h[..., 1, :].reshape(lead + (S5_GROUPS, S5_STATE))
    return re, im


def _s5_weights(bbr, bbi, c_re, c_im):
    eye8 = jnp.eye(8, dtype=F32)

    def blockdiag(m):
        a, b = m.shape[1], m.shape[2]
        return (eye8[:, None, :, None] * m[:, :, None, :]).reshape(8 * a, 8 * b)

    wb, wc = [], []
    for c in range(S5_CHUNKS):
        sl = slice(8 * c, 8 * c + 8)
        wb.append(jnp.concatenate([blockdiag(bbr[sl]), blockdiag(bbi[sl])], axis=1))
        cr_t = jnp.swapaxes(c_re[sl], 1, 2)
        ci_t = jnp.swapaxes(c_im[sl], 1, 2)
        wc.append(jnp.concatenate([blockdiag(cr_t), -blockdiag(ci_t)], axis=0))
    return jnp.stack(wb).astype(BF16), jnp.stack(wc).astype(BF16)


def _in_proj_kernel(x_ref, gmix_ref, w_ref, gq_ref, gkv_ref, ck_ref, sk_ref, *rest, grp):
    u_ref, qn_ref, lat_ref, krot_ref, latall_ref, krope_ref = rest[-6:]
    bsz = x_ref.shape[0]
    if latall_ref.shape[0] != bsz:
        latall_ref[...] = jnp.zeros(latall_ref.shape, F32)
        krope_ref[...] = jnp.zeros(krope_ref.shape, F32)
        latall_ref, krope_ref = latall_ref.at[grp * bsz:(grp + 1) * bsz], krope_ref.at[grp * bsz:(grp + 1) * bsz]
    steps = x_ref.shape[1]
    nb = max(1, min(bsz, SUB_TILE // steps))
    ck, sk = ck_ref[...][None], sk_ref[...][None]
    for b0 in range(0, bsz, nb):
        bs = slice(b0, b0 + nb)
        rows = nb * steps
        xn = _rms(x_ref[bs].reshape(rows, D_MODEL), gmix_ref[...]).astype(BF16)
        z = jnp.dot(xn, w_ref[...], preferred_element_type=F32)
        for bl in range(nb):
            for c in range(S5_CHUNKS):
                u_ref[c, pl.ds(b0 + bl, steps, stride=bsz), :] = z[bl * steps:(bl + 1) * steps,
                                                                   c * LANES:(c + 1) * LANES]
        qn = _rms(z[:, S5_WIDTH:S5_WIDTH + Q_LORA], gq_ref[...]).astype(BF16)
        qn_ref[bs] = qn.reshape(nb, steps, Q_LORA)
        c0 = S5_WIDTH + Q_LORA
        lat = _rms(z[:, c0:c0 + KV_LORA], gkv_ref[...]).reshape(nb, steps, KV_LORA)
        lat_ref[bs] = lat
        latall_ref[bs] = lat
        kp = z[:, c0 + KV_LORA:]
        lane = lax.broadcasted_iota(jnp.int32, kp.shape, 1)
        first_half = (lane % 64) < 16
        sw = jnp.where(first_half, pltpu.roll(kp, LANES - 16, axis=1), pltpu.roll(kp, 16, axis=1))
        kr = kp.reshape(nb, steps, LANES) * ck + sw.reshape(nb, steps, LANES) * sk
        krope_ref[bs] = kr[:, :, :QK_ROPE]
        krot_ref[bs] = jnp.where(lane.reshape(nb, steps, LANES) >= 64, kr, 0.0)


def _in_proj(x, grp, bsz, g_mix, w_in_p, g_q_lat, g_kv_lat, ck, sk, prev):
    n_all, length, _ = x.shape
    steps = min(ROW_TILE // bsz, length)
    full = lambda i: (0, 0)
    blk = lambda w: pl.BlockSpec((bsz, steps, w), lambda i: (0, i, 0))
    if prev:
        shared = lambda w: pl.BlockSpec((bsz, steps, w), lambda i: (grp, i, 0))
    else:
        shared = lambda w: pl.BlockSpec((n_all, steps, w), lambda i: (0, i, 0))
    return pl.pallas_call(
        functools.partial(_in_proj_kernel, grp=grp),
        grid=(length // steps,),
        in_specs=[pl.BlockSpec((bsz, steps, D_MODEL), lambda i: (grp, i, 0)),
                  pl.BlockSpec((1, D_MODEL), full),
                  pl.BlockSpec((D_MODEL, D_IN_PAD), full),
                  pl.BlockSpec((1, Q_LORA), full),
                  pl.BlockSpec((1, KV_LORA), full),
                  pl.BlockSpec((steps, LANES), lambda i: (i, 0)),
                  pl.BlockSpec((steps, LANES), lambda i: (i, 0))]
                 + [pl.BlockSpec(memory_space=pl.ANY)] * len(prev),
        out_specs=[pl.BlockSpec((S5_CHUNKS, steps * bsz, LANES), lambda i: (0, i, 0)),
                   blk(Q_LORA), blk(KV_LORA), blk(LANES),
                   shared(KV_LORA), shared(QK_ROPE)],
        out_shape=(jax.ShapeDtypeStruct((S5_CHUNKS, length * bsz, LANES), F32),
                   jax.ShapeDtypeStruct((bsz, length, Q_LORA), BF16),
                   jax.ShapeDtypeStruct((bsz, length, KV_LORA), F32),
                   jax.ShapeDtypeStruct((bsz, length, LANES), F32),
                   jax.ShapeDtypeStruct((n_all, length, KV_LORA), F32),
                   jax.ShapeDtypeStruct((n_all, length, QK_ROPE), F32)),
        input_output_aliases={7 + j: 4 + j for j in range(len(prev))},
        compiler_params=_cparams(("parallel",), VMEM_LIMIT),
        name="in_proj",
    )(x, g_mix, w_in_p, g_q_lat, g_kv_lat, ck, sk, *prev)


def _s5_kernel(u_ref, h0_ref, lam_ref, wb_ref, wc_ref, dskip_ref, wglu_ref, bglu_ref, gout_ref,
               y_ref, hlast_ref, bu_ref, h_ref, *, bsz, steps):
    c_id = pl.program_id(0)

    @pl.when(c_id == 0)
    def _():
        h_ref[...] = h0_ref[...]

    for c in range(S5_CHUNKS):
        bu_ref[:, c * S5_CHUNK_LANES:(c + 1) * S5_CHUNK_LANES] = jnp.dot(
            u_ref[c].astype(BF16), wb_ref[c], preferred_element_type=F32)

    for c in range(S5_CHUNKS):
        re = slice(c * S5_CHUNK_LANES, c * S5_CHUNK_LANES + 512)
        im = slice(c * S5_CHUNK_LANES + 512, (c + 1) * S5_CHUNK_LANES)
        lam_r = jnp.broadcast_to(lam_ref[:, re], (bsz, 512))
        lam_i = jnp.broadcast_to(lam_ref[:, im], (bsz, 512))

        hr, hi = h_ref[:, re], h_ref[:, im]
        for t in range(steps):
            rows = slice(t * bsz, (t + 1) * bsz)
            hr, hi = (lam_r * hr - lam_i * hi + bu_ref[rows, re],
                      lam_r * hi + lam_i * hr + bu_ref[rows, im])
            bu_ref[rows, re] = hr
            bu_ref[rows, im] = hi
        h_ref[:, re] = hr
        h_ref[:, im] = hi

    ys = []
    for c in range(S5_CHUNKS):
        hs = bu_ref[:, c * S5_CHUNK_LANES:(c + 1) * S5_CHUNK_LANES].astype(BF16)
        ys.append(jnp.dot(hs, wc_ref[c], preferred_element_type=F32))
    u = jnp.concatenate([u_ref[c] for c in range(S5_CHUNKS)], axis=1)
    y = jnp.concatenate(ys, axis=1) + dskip_ref[...] * u
    y = jax.nn.gelu(y)
    gate = jnp.dot(y.astype(BF16), wglu_ref[...], preferred_element_type=F32) + bglu_ref[...]
    y = _rms(y * jax.nn.sigmoid(gate), gout_ref[...])
    for c in range(S5_CHUNKS):
        y_ref[c] = y[:, c * LANES:(c + 1) * LANES]

    @pl.when(c_id == pl.num_programs(0) - 1)
    def _():
        hlast_ref[...] = h_ref[...]


def _s5(u_tm, h0, lam, wb, wc, d_skip, w_glu, b_glu, g_out, bsz, length):
    steps = min(S5_ROWS // bsz, length)
    rows = steps * bsz
    full2 = lambda c: (0, 0)
    full3 = lambda c: (0, 0, 0)
    return pl.pallas_call(
        functools.partial(_s5_kernel, bsz=bsz, steps=steps),
        grid=(length // steps,),
        in_specs=[pl.BlockSpec((S5_CHUNKS, rows, LANES), lambda c: (0, c, 0)),
                  pl.BlockSpec((bsz, S5_STATE_LANES), full2),
                  pl.BlockSpec((1, S5_STATE_LANES), full2),
                  pl.BlockSpec((S5_CHUNKS, LANES, S5_CHUNK_LANES), full3),
                  pl.BlockSpec((S5_CHUNKS, S5_CHUNK_LANES, LANES), full3),
                  pl.BlockSpec((1, S5_WIDTH), full2),
                  pl.BlockSpec((S5_WIDTH, S5_WIDTH), full2),
                  pl.BlockSpec((1, S5_WIDTH), full2),
                  pl.BlockSpec((1, S5_WIDTH), full2)],
        out_specs=[pl.BlockSpec((S5_CHUNKS, rows, LANES), lambda c: (0, c, 0)),
                   pl.BlockSpec((bsz, S5_STATE_LANES), full2)],
        out_shape=(jax.ShapeDtypeStruct((S5_CHUNKS, length * bsz, LANES), F32),
                   jax.ShapeDtypeStruct((bsz, S5_STATE_LANES), F32)),
        scratch_shapes=[pltpu.VMEM((rows, S5_STATE_LANES), F32),
                        pltpu.VMEM((bsz, S5_STATE_LANES), F32)],
        compiler_params=_cparams(("arbitrary",), VMEM_LIMIT),
        name="s5_mixer",
    )(u_tm, h0, lam, wb, wc, d_skip, w_glu, b_glu, g_out)


def _q_proj_kernel(qn_ref, w_ref, wsw_ref, ones_ref, cq_ref, sq_ref, g_ref, q_ref):
    g = g_ref[...] * (QK_HEAD ** -0.5)
    for rs in _sub_tiles(qn_ref.shape[0]):
        qn = qn_ref[rs, :]
        qf = jnp.dot(qn, w_ref[...], preferred_element_type=F32)
        qs = jnp.dot(qn, wsw_ref[...], preferred_element_type=F32)
        cq, sq = cq_ref[rs, :], sq_ref[rs, :]
        for p in range(MLA_HEADS // 2):
            cols = slice(2 * p * HEAD_PAD, 2 * (p + 1) * HEAD_PAD)
            xr = qf[:, cols] * cq + qs[:, cols] * sq
            sqr = xr * xr
            hi = sqr.astype(BF16)
            lo = (sqr - hi.astype(F32)).astype(BF16)
            ss = (jnp.dot(hi, ones_ref[...], preferred_element_type=F32)
                  + jnp.dot(lo, ones_ref[...], preferred_element_type=F32))
            q = (xr * lax.rsqrt(ss * (1.0 / QK_HEAD) + EPS) * g).astype(BF16)
            q_ref[0, 2 * p, rs, :] = q[:, :HEAD_PAD]
            q_ref[0, 2 * p + 1, rs, :] = q[:, HEAD_PAD:]


def _q_proj(qn, w_uq_p, w_uq_sw, cq, sq, gq, bsz, length):
    tm = min(ROW_TILE, length)
    n_l = length // tm
    full = lambda b, i: (0, 0)
    pair = 2 * HEAD_PAD
    two = lambda a: jnp.concatenate([a, a], axis=1)
    ones2 = (jnp.arange(pair)[:, None] // HEAD_PAD == jnp.arange(pair)[None, :] // HEAD_PAD).astype(BF16)
    return pl.pallas_call(
        _q_proj_kernel,
        grid=(bsz, n_l),
        in_specs=[pl.BlockSpec((tm, Q_LORA), lambda b, i: (b * n_l + i, 0)),
                  pl.BlockSpec((Q_LORA, MLA_HEADS * HEAD_PAD), full),
                  pl.BlockSpec((Q_LORA, MLA_HEADS * HEAD_PAD), full),
                  pl.BlockSpec((pair, pair), full),
                  pl.BlockSpec((tm, pair), lambda b, i: (i, 0)),
                  pl.BlockSpec((tm, pair), lambda b, i: (i, 0)),
                  pl.BlockSpec((1, pair), full)],
        out_specs=pl.BlockSpec((1, MLA_HEADS, tm, HEAD_PAD), lambda b, i: (b, 0, i, 0)),
        out_shape=jax.ShapeDtypeStruct((bsz, MLA_HEADS, length, HEAD_PAD), BF16),
        compiler_params=_cparams(("parallel", "parallel"), VMEM_LIMIT),
        name="q_proj",
    )(qn, w_uq_p, w_uq_sw, ones2, two(cq), two(sq), two(gq))


def _kv_proj_kernel(lat_ref, krot_ref, wk_ref, wv_ref, g_ref, k_ref, v_ref):
    g = g_ref[...]
    for rs in _sub_tiles(lat_ref.shape[0]):
        lat = lat_ref[rs, :].astype(BF16)
        kf = jnp.dot(lat, wk_ref[...], preferred_element_type=F32)
        vf = jnp.dot(lat, wv_ref[...], preferred_element_type=F32)
        krot = krot_ref[rs, :]
        for h in range(MLA_HEADS):
            k = kf[:, h * HEAD_PAD:(h + 1) * HEAD_PAD] + krot
            k_ref[0, h, rs, :] = _rms(k, g, QK_HEAD).astype(BF16)
        for p in range(MLA_HEADS // 2):
            v_ref[0, p, rs, :] = vf[:, p * LANES:(p + 1) * LANES].astype(BF16)


def _kv_proj(lat, krot, w_k, w_v, gk, bsz, length):
    tm = min(ROW_TILE, length)
    n_l = length // tm
    full = lambda b, i: (0, 0)
    row = lambda b, i: (b * n_l + i, 0)
    return pl.pallas_call(
        _kv_proj_kernel,
        grid=(bsz, n_l),
        in_specs=[pl.BlockSpec((tm, KV_LORA), row),
                  pl.BlockSpec((tm, HEAD_PAD), row),
                  pl.BlockSpec((KV_LORA, MLA_HEADS * HEAD_PAD), full),
                  pl.BlockSpec((KV_LORA, MLA_WIDTH), full),
                  pl.BlockSpec((1, HEAD_PAD), full)],
        out_specs=[pl.BlockSpec((1, MLA_HEADS, tm, HEAD_PAD), lambda b, i: (b, 0, i, 0)),
                   pl.BlockSpec((1, MLA_HEADS // 2, tm, LANES), lambda b, i: (b, 0, i, 0))],
        out_shape=(jax.ShapeDtypeStruct((bsz, MLA_HEADS, length, HEAD_PAD), BF16),
                   jax.ShapeDtypeStruct((bsz, MLA_HEADS // 2, length, LANES), BF16)),
        compiler_params=_cparams(("parallel", "parallel"), VMEM_LIMIT),
        name="kv_proj",
    )(lat, krot, w_k, w_v, gk)


def _scores(q, k):
    return lax.dot_general(q, k, (((1,), (1,)), ((), ())), preferred_element_type=F32)


def _attn_prompt_kernel(q_ref, k_ref, v_ref, o_ref, *, length, tq):
    n_q = length // tq
    row = lax.broadcasted_iota(jnp.int32, (tq, tq), 0)
    col = lax.broadcasted_iota(jnp.int32, (tq, tq), 1)
    visible = (col // CHUNK) <= (row // CHUNK)
    lane = lax.broadcasted_iota(jnp.int32, (tq, LANES), 1)
    for qi in range(n_q):
        q0 = qi * tq
        outs = []
        for hh in range(2):
            q = q_ref[0, hh, q0:q0 + tq, :]
            sd = jnp.where(visible, _scores(q, k_ref[0, hh, q0:q0 + tq, :]), -jnp.inf)
            m = jnp.max(sd, axis=-1, keepdims=True)
            if qi:
                so = _scores(q, k_ref[0, hh, 0:q0, :])
                m = jnp.maximum(m, jnp.max(so, axis=-1, keepdims=True))
            pd = jnp.exp(sd - m)
            l = jnp.sum(pd, axis=-1, keepdims=True)
            acc = jnp.dot(pd.astype(BF16), v_ref[0, 0, q0:q0 + tq, :], preferred_element_type=F32)
            if qi:
                po = jnp.exp(so - m)
                l = l + jnp.sum(po, axis=-1, keepdims=True)
                acc = acc + jnp.dot(po.astype(BF16), v_ref[0, 0, 0:q0, :], preferred_element_type=F32)
            outs.append(acc / l)
        o_ref[0, q0:q0 + tq, :] = jnp.where(lane < V_HEAD, outs[0], outs[1])


def _attn_prompt(q, k, v, bsz, length):
    tq = min(512, length)
    pairs = MLA_HEADS // 2
    return pl.pallas_call(
        functools.partial(_attn_prompt_kernel, length=length, tq=tq),
        grid=(bsz, pairs),
        in_specs=[pl.BlockSpec((1, 2, length, HEAD_PAD), lambda b, p: (b, p, 0, 0)),
                  pl.BlockSpec((1, 2, length, HEAD_PAD), lambda b, p: (b, p, 0, 0)),
                  pl.BlockSpec((1, 1, length, LANES), lambda b, p: (b, p, 0, 0))],
        out_specs=pl.BlockSpec((1, length, LANES), lambda b, p: (b, 0, p)),
        out_shape=jax.ShapeDtypeStruct((bsz, length, MLA_WIDTH), F32),
        compiler_params=_cparams(("parallel", "parallel"), VMEM_LIMIT),
        name="attn_prompt",
    )(q, k, v)


def _attn_sample_kernel(q_ref, kc_ref, vc_ref, kn_ref, vn_ref, o_ref):
    lq = q_ref.shape[2]
    lane = lax.broadcasted_iota(jnp.int32, (lq, LANES), 1)
    outs = []
    for hh in range(2):
        q = q_ref[0, hh]
        sc = _scores(q, kc_ref[0, hh])
        sn = _scores(q, kn_ref[0, hh])
        m = jnp.maximum(jnp.max(sc, axis=-1, keepdims=True), jnp.max(sn, axis=-1, keepdims=True))
        pc = jnp.exp(sc - m)
        pn = jnp.exp(sn - m)
        l = jnp.sum(pc, axis=-1, keepdims=True) + jnp.sum(pn, axis=-1, keepdims=True)
        acc = (jnp.dot(pc.astype(BF16), vc_ref[0, 0], preferred_element_type=F32)
               + jnp.dot(pn.astype(BF16), vn_ref[0, 0], preferred_element_type=F32))
        outs.append(acc / l)
    o_ref[0] = jnp.where(lane < V_HEAD, outs[0], outs[1])


def _attn_sample(q, kc, vc, kn, vn, bsz, lq, past):
    pairs = MLA_HEADS // 2
    hp = lambda b, p: (b, p, 0, 0)
    return pl.pallas_call(
        _attn_sample_kernel,
        grid=(bsz, pairs),
        in_specs=[pl.BlockSpec((1, 2, lq, HEAD_PAD), hp),
                  pl.BlockSpec((1, 2, past, HEAD_PAD), hp),
                  pl.BlockSpec((1, 1, past, LANES), hp),
                  pl.BlockSpec((1, 2, lq, HEAD_PAD), hp),
                  pl.BlockSpec((1, 1, lq, LANES), hp)],
        out_specs=pl.BlockSpec((1, lq, LANES), lambda b, p: (b, 0, p)),
        out_shape=jax.ShapeDtypeStruct((bsz, lq, MLA_WIDTH), F32),
        compiler_params=_cparams(("parallel", "parallel"), VMEM_LIMIT),
        name="attn_sample",
    )(q, kc, vc, kn, vn)


def _out_proj_kernel(x_ref, s5_ref, at_ref, gmla_ref, wout_ref, gffn_ref, wr_ref, br_ref, tri_ref, cin_ref,
                     x1_ref, hn_ref, route_ref, gate_ref, cout_ref, run_ref):
    @pl.when(pl.program_id(0) == 0)
    def _():
        run_ref[...] = cin_ref[...]

    bsz, steps, _ = x_ref.shape
    nb = max(1, min(bsz, SUB_TILE // steps))
    for b0 in range(0, bsz, nb):
        bs = slice(b0, b0 + nb)
        rows = nb * steps
        s5 = jnp.concatenate(
            [jnp.concatenate([s5_ref[c, pl.ds(b0 + bl, steps, stride=bsz), :] for c in range(S5_CHUNKS)], axis=1)
             for bl in range(nb)], axis=0)
        an = _rms(at_ref[bs].reshape(rows, MLA_WIDTH), gmla_ref[...])
        merged = jnp.concatenate([s5.astype(BF16), an.astype(BF16)], axis=1)
        x1 = x_ref[bs].reshape(rows, D_MODEL) + jnp.dot(merged, wout_ref[...], preferred_element_type=F32)
        x1_ref[bs] = x1.reshape(nb, steps, D_MODEL)
        hn = _rms(x1, gffn_ref[...])
        hn_ref[bs] = _pack_rows(hn).reshape(nb, steps, D_PACK)
        hn_hi = hn.astype(BF16)
        hn_lo = (hn - hn_hi.astype(F32)).astype(BF16)
        parts = (jnp.dot(hn_hi, wr_ref[...], preferred_element_type=F32)
                 + jnp.dot(hn_lo, wr_ref[...], preferred_element_type=F32))
        logits = parts[:, :LANES] + parts[:, LANES:] + br_ref[...]
        lane = lax.broadcasted_iota(jnp.int32, logits.shape, 1)
        lane_f = lane.astype(F32)
        work = jnp.where(lane < N_EXPERTS, logits, -jnp.inf)
        vals, idxs = [], []
        for _ in range(TOP_K):
            m = jnp.max(work, axis=-1, keepdims=True)
            i = jnp.min(jnp.where(work == m, lane_f, float(LANES)), axis=-1, keepdims=True)
            vals.append(m)
            idxs.append(i)
            work = jnp.where(lane_f == i, -jnp.inf, work)
        es = [jnp.exp(v - vals[0]) for v in vals]
        den = es[0] + es[1] + es[2] + es[3]
        onehots = [(lane_f == idxs[k]).astype(F32) for k in range(TOP_K)]
        e_all = onehots[0] + onehots[1] + onehots[2] + onehots[3]
        before = jnp.dot(tri_ref[...], e_all.astype(BF16), preferred_element_type=F32) + run_ref[...]
        run_ref[...] = run_ref[...] + jnp.sum(e_all, axis=0, keepdims=True)
        route = jnp.zeros(logits.shape, F32)
        gate_out = jnp.zeros(logits.shape, F32)
        for k in range(TOP_K):
            rank = jnp.sum(onehots[k] * before, axis=-1, keepdims=True)
            route = jnp.where(lane == k, idxs[k], route)
            route = jnp.where(lane == 8 + k, rank, route)
            gate_out = jnp.where(lane == k, es[k] / den, gate_out)
        route_ref[bs] = route.astype(jnp.int32).reshape(nb, steps, LANES)
        gate_ref[bs] = gate_out.reshape(nb, steps, LANES)
    cout_ref[...] = run_ref[...]


def _out_proj(x, grp, s5n_tm, attn, g_mla, w_out, g_ffn, w_r, b_r_p, counts_in):
    bsz, length, _ = attn.shape
    steps = min(ROW_TILE // bsz, length)
    full = lambda i: (0, 0)
    blk = lambda w: pl.BlockSpec((bsz, steps, w), lambda i: (0, i, 0))
    sub = max(1, min(bsz, SUB_TILE // steps)) * steps
    tri = (jnp.arange(sub)[:, None] > jnp.arange(sub)[None, :]).astype(BF16)
    return pl.pallas_call(
        _out_proj_kernel,
        grid=(length // steps,),
        in_specs=[pl.BlockSpec((bsz, steps, D_MODEL), lambda i: (grp, i, 0)),
                  pl.BlockSpec((S5_CHUNKS, steps * bsz, LANES), lambda i: (0, i, 0)),
                  blk(MLA_WIDTH),
                  pl.BlockSpec((1, MLA_WIDTH), full),
                  pl.BlockSpec((D_MODEL, D_MODEL), full),
                  pl.BlockSpec((1, D_MODEL), full),
                  pl.BlockSpec((D_MODEL, 2 * LANES), full),
                  pl.BlockSpec((1, LANES), full),
                  pl.BlockSpec((sub, sub), full),
                  pl.BlockSpec((1, LANES), full)],
        out_specs=[blk(D_MODEL), blk(D_PACK), blk(LANES), blk(LANES),
                   pl.BlockSpec((1, LANES), full)],
        out_shape=(jax.ShapeDtypeStruct((bsz, length, D_MODEL), F32),
                   jax.ShapeDtypeStruct((bsz, length, D_PACK), U32),
                   jax.ShapeDtypeStruct((bsz, length, LANES), jnp.int32),
                   jax.ShapeDtypeStruct((bsz, length, LANES), F32),
                   jax.ShapeDtypeStruct((1, LANES), F32)),
        scratch_shapes=[pltpu.VMEM((1, LANES), F32)],
        compiler_params=_cparams(("arbitrary",), VMEM_LIMIT),
        name="out_proj",
    )(x, s5n_tm, attn, g_mla, w_out, g_ffn, w_r, b_r_p, tri, counts_in)


def _sc_mesh():
    return plsc.VectorSubcoreMesh(core_axis_name="c", subcore_axis_name="s",
                                  num_cores=SC_CORES, num_subcores=SC_SUBCORES)


def _sc_for_chunks(n_chunks, fn):
    wid = lax.axis_index("s") * SC_CORES + lax.axis_index("c")
    full, rem = divmod(n_chunks, SC_WORKERS)
    if full:
        @pl.loop(0, full)
        def _(j):
            fn(j * SC_WORKERS + wid)
    if rem:
        @pl.when(wid < rem)
        def _():
            fn(full * SC_WORKERS + wid)


def _sc_dispatch_body(*refs, n_seg):
    d_hbms, h_hbms = refs[:n_seg], refs[n_seg:2 * n_seg]
    xs_hbm, idx_v, rows_v, sems = refs[2 * n_seg:]
    n = SC_GATHER_ROWS
    wid = lax.axis_index("s") * SC_CORES + lax.axis_index("c")

    def scatter(d_hbm, h_hbm, c, slot):
        pltpu.sync_copy(d_hbm.at[c], idx_v.at[slot])
        pltpu.sync_copy(h_hbm.at[pl.ds(pl.multiple_of(c * n, n), n)], rows_v.at[slot])
        return [pltpu.async_copy(rows_v.at[slot], xs_hbm.at[idx_v.at[slot, k]], sems.at[slot])
                for k in range(TOP_K)]

    def run(d_hbm, h_hbm):
        full, rem = divmod(d_hbm.shape[0], SC_WORKERS)
        if full // 2:
            @pl.loop(0, full // 2)
            def _(j):
                first = scatter(d_hbm, h_hbm, (2 * j) * SC_WORKERS + wid, 0)
                second = scatter(d_hbm, h_hbm, (2 * j + 1) * SC_WORKERS + wid, 1)
                for cp in first + second:
                    cp.wait()
        tail = [(full - 1, None)] if full % 2 else []
        if rem:
            tail.append((full, rem))
        for row, limit in tail:
            def one(row=row):
                for cp in scatter(d_hbm, h_hbm, row * SC_WORKERS + wid, 0):
                    cp.wait()
            if limit is None:
                one()
            else:
                pl.when(wid < limit)(one)

    for d_hbm, h_hbm in zip(d_hbms, h_hbms):
        run(d_hbm, h_hbm)


def _dispatch(dests, hns, n_blocks):
    return pl.kernel(
        functools.partial(_sc_dispatch_body, n_seg=len(dests)),
        out_type=jax.ShapeDtypeStruct((n_blocks * MOE_ROWS, D_PACK), U32),
        mesh=_sc_mesh(),
        scratch_types=[pltpu.VMEM((2, 8, SC_GATHER_ROWS), jnp.int32),
                       pltpu.VMEM((2, SC_GATHER_ROWS, D_PACK), U32),
                       pltpu.SemaphoreType.DMA((2,))],
        name="moe_dispatch_sc",
    )(*dests, *hns)


def _expert_kernel(be_ref, nb_ref, nv_ref, first_ref, slot_ref, nxt_ref,
                   x_ref, w1_hbm, b1_ref, w2_hbm, b2_ref, y_ref, w1f_ref, w2f_ref, w1b_ref, w2b_ref, sem):
    step = pl.program_id(0)

    def weight_copies(e):
        return (pltpu.make_async_copy(w1_hbm.at[e], w1f_ref, sem.at[0]),
                pltpu.make_async_copy(w2_hbm.at[e], w2f_ref, sem.at[1]))

    @pl.when(jnp.logical_and(step == 0, nb_ref[0] > 0))
    def _():
        for cp in weight_copies(be_ref[0]):
            cp.start()

    for sb in range(MOE_PAIR):
        b = step * MOE_PAIR + sb

        @pl.when(jnp.logical_and(b < nb_ref[0], first_ref[b] == 1))
        def _(b=b):
            s = slot_ref[b]
            for cp in weight_copies(be_ref[b]):
                cp.wait()
            w1b_ref[s] = w1f_ref[...].astype(BF16)
            w2b_ref[s] = w2f_ref[...].astype(BF16)

            @pl.when(nxt_ref[b] >= 0)
            def _():
                for cp in weight_copies(nxt_ref[b]):
                    cp.start()

    @pl.when(step * MOE_PAIR < nb_ref[0])
    def _():
        cw = D_FF // MOE_COL_CHUNKS
        for sb in range(MOE_PAIR):
            b = step * MOE_PAIR + sb
            e, s = be_ref[b], slot_ref[b]
            rs = slice(sb * MOE_ROWS, (sb + 1) * MOE_ROWS)
            live = lax.broadcasted_iota(jnp.int32, (MOE_ROWS, D_PACK), 0) < nv_ref[b]
            lo, hi = _unpack_rows(jnp.where(live, x_ref[rs, :], jnp.uint32(0)))
            lo, hi = lo.astype(BF16), hi.astype(BF16)
            b1 = b1_ref[e]

            def up(c0, lo=lo, hi=hi, s=s, b1=b1):
                return (jnp.dot(lo, w1b_ref[s, :D_PACK, c0:c0 + cw], preferred_element_type=F32)
                        + jnp.dot(hi, w1b_ref[s, D_PACK:, c0:c0 + cw], preferred_element_type=F32)
                        + b1[:, c0:c0 + cw])

            y = b2_ref[e]
            for j in range(MOE_COL_CHUNKS):
                gate = jnp.minimum(up(j * cw), SWIGLU_LIMIT)
                lin = jnp.clip(up(D_FF + j * cw), -SWIGLU_LIMIT, SWIGLU_LIMIT)
                act = gate * jax.nn.sigmoid(SWIGLU_ALPHA * gate) * (lin + 1.0)
                y = y + jnp.dot(act.astype(BF16), w2b_ref[s, j * cw:(j + 1) * cw, :],
                                preferred_element_type=F32)
            y_ref[rs, :] = _pack_rows(y)

    @pl.when(step * MOE_PAIR >= nb_ref[0])
    def _():
        y_ref[...] = jnp.zeros(y_ref.shape, y_ref.dtype)


def _experts(tables, xs, w1, b1, w2, b2, n_blocks):
    rows = MOE_PAIR * MOE_ROWS
    last = lambda p, be, nb, *_: (jnp.maximum(jnp.minimum(p, (nb[0] - 1) // MOE_PAIR), 0), 0)
    whole = lambda p, *_: (0, 0, 0)
    return pl.pallas_call(
        _expert_kernel,
        grid_spec=pltpu.PrefetchScalarGridSpec(
            num_scalar_prefetch=6,
            grid=(n_blocks // MOE_PAIR,),
            in_specs=[pl.BlockSpec((rows, D_PACK), last),
                      pl.BlockSpec(memory_space=pl.ANY),
                      pl.BlockSpec((N_EXPERTS, 1, 2 * D_FF), whole),
                      pl.BlockSpec(memory_space=pl.ANY),
                      pl.BlockSpec((N_EXPERTS, 1, D_MODEL), whole)],
            out_specs=pl.BlockSpec((rows, D_PACK), lambda p, *_: (p, 0)),
            scratch_shapes=[pltpu.VMEM((D_MODEL, 2 * D_FF), F32), pltpu.VMEM((D_FF, D_MODEL), F32),
                            pltpu.VMEM((MOE_PAIR, D_MODEL, 2 * D_FF), BF16),
                            pltpu.VMEM((MOE_PAIR, D_FF, D_MODEL), BF16),
                            pltpu.SemaphoreType.DMA((2,))]),
        out_shape=jax.ShapeDtypeStruct((n_blocks * MOE_ROWS, D_PACK), U32),
        compiler_params=_cparams(("arbitrary",), VMEM_LIMIT),
        name="moe_experts",
    )(*tables, xs, w1, b1.reshape(N_EXPERTS, 1, 2 * D_FF), w2, b2.reshape(N_EXPERTS, 1, D_MODEL))


def _sc_gather_body(*refs, n_seg):
    d_hbms, yb_hbm = refs[:n_seg], refs[n_seg]
    g_hbms = refs[n_seg + 1:2 * n_seg + 1]
    idx_v, rows_v, sems = refs[2 * n_seg + 1:]
    n = SC_GATHER_ROWS

    def chunk(d_hbm, g_hbm, c):
        pltpu.sync_copy(d_hbm.at[c], idx_v)
        rows = pl.ds(pl.multiple_of(c * n, n), n)
        gather = lambda k: pltpu.async_copy(yb_hbm.at[idx_v.at[k]], rows_v.at[k % 2], sems.at[k % 2])
        cp = gather(0)
        for k in range(TOP_K):
            cp.wait()
            if k + 1 < TOP_K:
                cp = gather(k + 1)
            pltpu.sync_copy(rows_v.at[k % 2], g_hbm.at[k, rows])

    for d_hbm, g_hbm in zip(d_hbms, g_hbms):
        _sc_for_chunks(d_hbm.shape[0], functools.partial(chunk, d_hbm, g_hbm))


def _gather_expert_rows(dests, yb):
    return pl.kernel(
        functools.partial(_sc_gather_body, n_seg=len(dests)),
        out_type=tuple(jax.ShapeDtypeStruct((TOP_K, d.shape[0] * SC_GATHER_ROWS, D_PACK), U32) for d in dests),
        mesh=_sc_mesh(),
        scratch_types=[pltpu.VMEM((8, SC_GATHER_ROWS), jnp.int32),
                       pltpu.VMEM((2, SC_GATHER_ROWS, D_PACK), U32),
                       pltpu.SemaphoreType.DMA((2,))],
        name="moe_gather_sc",
    )(*dests, yb)


def _combine_kernel(g_ref, gate_ref, x1_ref, *rest, first_blk, n_blk):
    o_ref = rest[-1]
    i = pl.program_id(0)

    @pl.when(jnp.logical_and(i >= first_blk, i < first_blk + n_blk))
    def _():
        gate = gate_ref[...]
        x1 = x1_ref[...]
        acc_lo, acc_hi = x1[:, :D_PACK], x1[:, D_PACK:]
        for k in range(TOP_K):
            lo, hi = _unpack_rows(g_ref[k])
            acc_lo = acc_lo + gate[:, k:k + 1] * lo
            acc_hi = acc_hi + gate[:, k:k + 1] * hi
        o_ref[:, :D_PACK] = acc_lo
        o_ref[:, D_PACK:] = acc_hi

    @pl.when(jnp.logical_or(i < first_blk, i >= first_blk + n_blk))
    def _():
        o_ref[...] = jnp.zeros(o_ref.shape, o_ref.dtype)


def _combine(g, gates, x1, out_rows=None, row0=0, prev=None):
    n_tok = x1.shape[0]
    out_rows = n_tok if out_rows is None else out_rows
    tm = min(512, n_tok)
    n_blk = n_tok // tm
    first_blk, grid = (0, n_blk) if prev is not None else (row0 // tm, out_rows // tm)
    own = lambda i: jnp.clip(i - first_blk, 0, n_blk - 1)
    in_specs = [pl.BlockSpec((TOP_K, tm, D_PACK), lambda i: (0, own(i), 0)),
                pl.BlockSpec((tm, LANES), lambda i: (own(i), 0)),
                pl.BlockSpec((tm, D_MODEL), lambda i: (own(i), 0))]
    args = [g, gates, x1]
    out_blk0, aliases = 0, {}
    if prev is not None:
        in_specs.append(pl.BlockSpec(memory_space=pl.ANY))
        args.append(prev)
        out_blk0, aliases = row0 // tm, {3: 0}
    return pl.pallas_call(
        functools.partial(_combine_kernel, first_blk=first_blk, n_blk=n_blk),
        grid=(grid,),
        in_specs=in_specs,
        out_specs=pl.BlockSpec((tm, D_MODEL), lambda i: (out_blk0 + i, 0)),
        out_shape=jax.ShapeDtypeStruct((out_rows, D_MODEL), F32),
        input_output_aliases=aliases,
        compiler_params=_cparams(("parallel",), VMEM_LIMIT),
        name="moe_combine",
    )(*args)


def _route_tables(counts_f, n_assign):
    counts = counts_f[0, :N_EXPERTS].astype(jnp.int32)
    nblk_e = (counts + MOE_ROWS - 1) // MOE_ROWS
    bend = jnp.cumsum(nblk_e)
    bstart = bend - nblk_e
    pad_start = bstart * MOE_ROWS
    n_blocks = -(-(n_assign + N_EXPERTS * (MOE_ROWS - 1)) // (MOE_ROWS * MOE_PAIR)) * MOE_PAIR
    e_ids = jnp.arange(N_EXPERTS, dtype=jnp.int32)
    nonempty = counts > 0
    slot_e = (jnp.cumsum(nonempty.astype(jnp.int32)) - 1) % MOE_PAIR
    later = jnp.where(nonempty[None, :] & (e_ids[None, :] > e_ids[:, None]), e_ids[None, :], N_EXPERTS)
    next_e = jnp.min(later, axis=1)
    next_e = jnp.where(next_e == N_EXPERTS, -1, next_e)
    bidx = jnp.arange(n_blocks, dtype=jnp.int32)
    mine = ((bidx[:, None] >= bstart[None, :]) & (bidx[:, None] < bend[None, :])).astype(jnp.int32)
    vals = jnp.stack([e_ids, slot_e, next_e, pad_start + counts, bstart, jnp.ones_like(e_ids)], axis=0)
    per_block = jnp.sum(mine[:, None, :] * vals[None, :, :], axis=-1)
    block_e, slot, nxt, row_end, first_blk, used = (per_block[:, j] for j in range(6))
    n_valid = jnp.clip(row_end - bidx * MOE_ROWS, 0, MOE_ROWS)
    first = ((bidx == first_blk) & (used > 0)).astype(jnp.int32)
    tables = (block_e, bend[-1:].astype(jnp.int32), n_valid, first, slot, nxt)
    return pad_start, tables, n_blocks


def _dest_kernel(ps_ref, route_ref, d_ref):
    rt = route_ref[...].T
    idx, rank = rt[0:8, :], rt[8:16, :]
    base = jnp.zeros_like(idx)
    for e in range(N_EXPERTS):
        base = jnp.where(idx == e, ps_ref[e], base)
    dest = base + rank
    for j in range(dest.shape[1] // SC_GATHER_ROWS):
        d_ref[j] = dest[:, j * SC_GATHER_ROWS:(j + 1) * SC_GATHER_ROWS]


def _dest(pad_start, route):
    n_tok = route.shape[0]
    tm = min(2048, n_tok)
    n = SC_GATHER_ROWS
    return pl.pallas_call(
        _dest_kernel,
        grid_spec=pltpu.PrefetchScalarGridSpec(
            num_scalar_prefetch=1,
            grid=(n_tok // tm,),
            in_specs=[pl.BlockSpec((tm, LANES), lambda i, ps: (i, 0))],
            out_specs=pl.BlockSpec((tm // n, 8, n), lambda i, ps: (i, 0, 0))),
        out_shape=jax.ShapeDtypeStruct((n_tok // n, 8, n), jnp.int32),
        compiler_params=_cparams(("parallel",), VMEM_LIMIT),
        name="moe_dest",
    )(pad_start, route)


def _rope_tables(offset, length):
    f32 = np.float32
    pos = f32(offset) + np.arange(length, dtype=f32)
    inv = np.power(f32(ROPE_THETA), -np.arange(QK_ROPE // 2, dtype=f32) * f32(2.0 / QK_ROPE))
    ang = pos[:, None] * inv[None, :]
    cos, sin = np.cos(ang), np.sin(ang)
    z32 = np.zeros((length, 32), f32)
    rot_c = np.concatenate([cos, cos], axis=1)
    rot_s = np.concatenate([-sin, sin], axis=1)
    ck = np.concatenate([rot_c, z32, rot_c, z32], axis=1)
    sk = np.concatenate([rot_s, z32, rot_s, z32], axis=1)
    cq = np.concatenate([np.ones((length, QK_NOPE), f32), rot_c, z32], axis=1)
    sq = np.concatenate([np.zeros((length, QK_NOPE), f32), rot_s, z32], axis=1)
    return tuple(jnp.asarray(t, F32) for t in (ck, sk, cq, sq))


def _head_gain(g_nope, g_rope):
    return jnp.concatenate([g_nope, g_rope, g_rope, jnp.zeros((HEAD_PAD - QK_HEAD,), F32)]).reshape(1, HEAD_PAD)


def _mixer_stage(x, grp, bsz, past, prm, counts_in, prev=()):
    length = x.shape[1]
    offset = 0 if past is None else past[0].shape[1]
    ck, sk, cq, sq = _rope_tables(offset, length)
    u_tm, qn, lat, krot, lat_all, k_rope = _in_proj(x, grp, bsz, prm['g_mix'], prm['w_in_p'], prm['g_q_lat'],
                                                    prm['g_kv_lat'], ck, sk, prev)
    if past is None:
        h0 = jnp.zeros((bsz, S5_STATE_LANES), F32)
    else:
        h0 = _state_layout(past[2], past[3])
    s5n_tm, h_last = _s5(u_tm, h0, prm['lam'], prm['wb'], prm['wc'],
                         prm['d_skip'], prm['w_glu'], prm['b_glu'], prm['g_s5_out'], bsz, length)
    h_re, h_im = _state_unlayout(h_last)
    t = bsz * length
    q = _q_proj(qn.reshape(t, Q_LORA), prm['w_uq_p'], prm['w_uq_sw'], cq, sq, prm['gq'], bsz, length)
    k, v = _kv_proj(lat.reshape(t, KV_LORA), krot.reshape(t, LANES), prm['w_k'], prm['w_v'], prm['gk'],
                    bsz, length)
    if past is None:
        attn = _attn_prompt(q, k, v, bsz, length)
    else:
        n_past = past[0].shape[1]
        c_lat = past[0].reshape(bsz * n_past, KV_LORA)
        c_rot = jnp.pad(past[1].reshape(bsz * n_past, QK_ROPE), ((0, 0), (QK_NOPE, HEAD_PAD - QK_HEAD)))
        kc, vc = _kv_proj(c_lat, c_rot, prm['w_k'], prm['w_v'], prm['gk'], bsz, n_past)
        attn = _attn_sample(q, kc, vc, k, v, bsz, length, n_past)
    x1, hn, route, gates, counts = _out_proj(x, grp, s5n_tm, attn, prm['g_mla_out'], prm['w_out'], prm['g_ffn'],
                                             prm['w_r'], prm['b_r_p'], counts_in)
    tok = dict(x1=x1.reshape(t, D_MODEL), hn=hn.reshape(t, D_PACK), route=route.reshape(t, LANES),
               gates=gates.reshape(t, LANES))
    return tok, counts, (lat_all, k_rope, h_re, h_im)


def _prepare(g_mix, w_in, lam_re, lam_im, log_dt, b_s5_re, b_s5_im, c_s5_re, c_s5_im, d_s5, w_glu, b_glu,
             g_q_lat, w_uq, g_kv_lat, w_ukv, g_qn_nope, g_qn_rope, g_kn_nope, g_kn_rope, g_s5_out,
             g_mla_out, w_out, g_ffn, w_router, b_router):
    c0 = S5_WIDTH + Q_LORA + KV_LORA
    w_pe = w_in[:, c0:]
    z = jnp.zeros((D_MODEL, 32), F32)
    w_in_p = jnp.concatenate([w_in[:, :c0], w_pe, z, w_pe, z], axis=1).astype(BF16)
    ar, ai, bbr, bbi = _s5_prep(lam_re, lam_im, log_dt, b_s5_re, b_s5_im)
    wb, wc = _s5_weights(bbr, bbi, c_s5_re, c_s5_im)
    w_uq_p = jnp.pad(w_uq.reshape(Q_LORA, MLA_HEADS, QK_HEAD), ((0, 0), (0, 0), (0, HEAD_PAD - QK_HEAD)))
    r0, r1, r2 = QK_NOPE, QK_NOPE + QK_ROPE // 2, QK_HEAD
    w_uq_sw = jnp.zeros_like(w_uq_p).at[:, :, r0:r1].set(w_uq_p[:, :, r1:r2]).at[:, :, r1:r2].set(w_uq_p[:, :, r0:r1])
    w_kv = w_ukv.reshape(KV_LORA, MLA_HEADS, QK_NOPE + V_HEAD)
    w_k = jnp.pad(w_kv[:, :, :QK_NOPE], ((0, 0), (0, 0), (0, HEAD_PAD - QK_NOPE)))
    w_v = w_kv[:, :, QK_NOPE:]
    w_r_p = jnp.pad(w_router, ((0, 0), (0, LANES - N_EXPERTS)))
    w_r_hi = w_r_p.astype(BF16)
    return dict(
        g_mix=g_mix.reshape(1, -1), w_in_p=w_in_p,
        g_q_lat=g_q_lat.reshape(1, -1), g_kv_lat=g_kv_lat.reshape(1, -1),
        lam=_state_layout(ar, ai).reshape(1, S5_STATE_LANES), wb=wb, wc=wc,
        d_skip=d_s5.reshape(1, -1), w_glu=w_glu.astype(BF16), b_glu=b_glu.reshape(1, -1),
        g_s5_out=g_s5_out.reshape(1, -1),
        w_uq_p=w_uq_p.reshape(Q_LORA, MLA_HEADS * HEAD_PAD).astype(BF16),
        w_uq_sw=w_uq_sw.reshape(Q_LORA, MLA_HEADS * HEAD_PAD).astype(BF16),
        w_k=w_k.reshape(KV_LORA, MLA_HEADS * HEAD_PAD).astype(BF16),
        w_v=w_v.reshape(KV_LORA, MLA_WIDTH).astype(BF16),
        gq=_head_gain(g_qn_nope, g_qn_rope), gk=_head_gain(g_kn_nope, g_kn_rope),
        g_mla_out=g_mla_out.reshape(1, -1), w_out=w_out.astype(BF16), g_ffn=g_ffn.reshape(1, -1),
        w_r=jnp.concatenate([w_r_hi, (w_r_p - w_r_hi.astype(F32)).astype(BF16)], axis=1),
        b_r_p=jnp.pad(b_router, (0, LANES - N_EXPERTS)).reshape(1, LANES),
    )


def _moe_dispatch(segs, counts, w1, b1, w2, b2):
    n_tok = sum(seg['x1'].shape[0] for seg in segs)
    pad_start, tables, n_blocks = _route_tables(counts, n_tok * TOP_K)
    dests = [_dest(pad_start, seg['route']) for seg in segs]
    xs_sorted = _dispatch(dests, [seg['hn'] for seg in segs], n_blocks)
    return dests, _experts(tables, xs_sorted, w1, b1, w2, b2, n_blocks)


def _layer(xp, xs, cache_lat, cache_kr, st_re, st_im, mixer_w, w1, b1, w2, b2):
    prm = _prepare(*mixer_w)
    bp, lp, _ = xp.shape
    bs, ls, _ = xs.shape
    bg = bp // PROMPT_GROUPS
    zero_counts = jnp.zeros((1, LANES), F32)
    passes, prompt_outs = [], []
    for grp in range(PROMPT_GROUPS):
        tok, counts, outs = _mixer_stage(xp, grp, bg, None, prm, zero_counts,
                                         prompt_outs[-1][:2] if prompt_outs else ())
        prompt_outs.append(outs)
        segs = [tok]
        if grp == PROMPT_GROUPS - 1:
            tok_s, counts, sample_outs = _mixer_stage(xs, 0, bs, (cache_lat, cache_kr, st_re, st_im), prm, counts)
            segs.append(tok_s)
        passes.append((segs,) + _moe_dispatch(segs, counts, w1, b1, w2, b2))
    yp = None
    for grp, (segs, dests, yb) in reversed(list(enumerate(passes))):
        gs = _gather_expert_rows(dests, yb)
        yp = _combine(gs[0], segs[0]['gates'], segs[0]['x1'], bp * lp, grp * bg * lp, yp)
        if len(segs) > 1:
            ys = _combine(gs[1], segs[1]['gates'], segs[1]['x1'])
    cat = lambda j: jnp.concatenate([o[j] for o in prompt_outs], axis=0)
    lat_p, krope_p = prompt_outs[-1][:2]
    return (yp.reshape(bp, lp, D_MODEL), ys.reshape(bs, ls, D_MODEL), lat_p, krope_p, cat(2), cat(3)) + sample_outs


def kernel(x_prompt, x_sample, cache_kv_latent, cache_k_rope, state_s5_re, state_s5_im, g_mix, w_in, lam_re,
           lam_im, log_dt, b_s5_re, b_s5_im, c_s5_re, c_s5_im, d_s5, w_glu, b_glu, g_q_lat, w_uq, g_kv_lat,
           w_ukv, g_qn_nope, g_qn_rope, g_kn_nope, g_kn_rope, g_s5_out, g_mla_out, w_out, g_ffn, w_router,
           b_router, w_mlp1, b_mlp1, w_mlp2, b_mlp2):
    depth = g_mix.shape[0]
    yp, ys = x_prompt, x_sample
    outs = [[] for _ in range(8)]
    for l in range(depth):
        mixer_w = (g_mix[l], w_in[l], lam_re[l], lam_im[l], log_dt[l], b_s5_re[l], b_s5_im[l], c_s5_re[l],
                   c_s5_im[l], d_s5[l], w_glu[l], b_glu[l], g_q_lat[l], w_uq[l], g_kv_lat[l], w_ukv[l],
                   g_qn_nope[l], g_qn_rope[l], g_kn_nope[l], g_kn_rope[l], g_s5_out[l], g_mla_out[l],
                   w_out[l], g_ffn[l], w_router[l], b_router[l])
        res = _layer(yp, ys, cache_kv_latent[l], cache_k_rope[l], state_s5_re[l], state_s5_im[l], mixer_w,
                     w_mlp1[l], b_mlp1[l], w_mlp2[l], b_mlp2[l])
        yp, ys = res[0], res[1]
        for o, r in zip(outs, res[2:]):
            o.append(r)
    return (yp, ys) + tuple(jnp.stack(o) for o in outs)
```

```python
import functools
import math

import jax
import jax.numpy as jnp
import numpy as np
from jax import lax
from jax.experimental import pallas as pl
from jax.experimental.pallas import tpu as pltpu
from jax.experimental.pallas import tpu_sc as plsc

F32 = jnp.float32
BF16 = jnp.bfloat16
U32 = jnp.uint32

D_MODEL = 1024
S5_WIDTH = 512
S5_GROUP = 16
S5_GROUPS = 32
S5_STATE = 64
MLA_HEADS = 8
QK_NOPE = 64
QK_ROPE = 32
QK_HEAD = QK_NOPE + QK_ROPE
V_HEAD = 64
MLA_WIDTH = MLA_HEADS * V_HEAD
Q_LORA = 384
KV_LORA = 256
ROPE_THETA = 10000.0
CHUNK = 64
N_EXPERTS = 32
TOP_K = 4
D_FF = D_MODEL
SWIGLU_LIMIT = 7.0
SWIGLU_ALPHA = 1.702
EPS = 1e-6

LANES = 128
HEAD_PAD = 128
D_IN_PAD = 1280
S5_CHUNKS = 4
S5_CHUNK_LANES = 1024
S5_STATE_LANES = S5_CHUNKS * S5_CHUNK_LANES
ROW_TILE = 1024
SUB_TILE = 512
S5_ROWS = 512
PROMPT_GROUPS = 2
MOE_ROWS = 512
MOE_COL_CHUNKS = 2
MOE_PAIR = 2
VMEM_LIMIT = 56 * 1024 * 1024
SC_CORES = 2
SC_SUBCORES = 16
SC_WORKERS = SC_CORES * SC_SUBCORES
SC_GATHER_ROWS = 64
D_PACK = D_MODEL // 2


def _cparams(sem, vmem=None):
    return pltpu.CompilerParams(dimension_semantics=sem, vmem_limit_bytes=vmem)


def _rms(x, g, n=None):
    n = x.shape[-1] if n is None else n
    ms = jnp.sum(x * x, axis=-1, keepdims=True) * (1.0 / n)
    return x * lax.rsqrt(ms + EPS) * g


def _sub_tiles(rows):
    sub = min(SUB_TILE, rows)
    return [slice(r, r + sub) for r in range(0, rows, sub)]


def _pack_rows(x):
    lo = lax.bitcast_convert_type(x[:, :D_PACK].astype(BF16).astype(F32), U32)
    hi = lax.bitcast_convert_type(x[:, D_PACK:].astype(BF16).astype(F32), U32)
    return (lo >> 16) | (hi & jnp.uint32(0xFFFF0000))


def _unpack_rows(w):
    lo = lax.bitcast_convert_type(w << 16, F32)
    hi = lax.bitcast_convert_type(w & jnp.uint32(0xFFFF0000), F32)
    return lo, hi


def _prep_kernel(lr_ref, li_ref, ldt_ref, br_ref, bi_ref, ar_ref, ai_ref, bbr_ref, bbi_ref):
    lr = lr_ref[...]
    li = li_ref[...]
    dt = jnp.exp(ldt_ref[...])
    mag = jnp.exp(lr * dt)
    ar = mag * jnp.cos(li * dt)
    ai = mag * jnp.sin(li * dt)
    ar_ref[...] = ar
    ai_ref[...] = ai
    den = lr * lr + li * li
    cr = ((ar - 1.0) * lr + ai * li) / den
    ci = (ai * lr - (ar - 1.0) * li) / den
    br = br_ref[...]
    bi = bi_ref[...]
    bbr_ref[...] = cr[:, None, :] * br - ci[:, None, :] * bi
    bbi_ref[...] = cr[:, None, :] * bi + ci[:, None, :] * br


def _s5_prep(lam_re, lam_im, log_dt, b_re, b_im):
    g, n = lam_re.shape
    p = b_re.shape[-1]
    out = pl.pallas_call(
        _prep_kernel,
        out_shape=(jax.ShapeDtypeStruct((g, n), F32), jax.ShapeDtypeStruct((g, n), F32),
                   jax.ShapeDtypeStruct((g, p, n), F32), jax.ShapeDtypeStruct((g, p, n), F32)),
        name="s5_prep",
    )(lam_re, lam_im, log_dt.reshape(g, 1), jnp.swapaxes(b_re, 1, 2), jnp.swapaxes(b_im, 1, 2))
    return out


def _state_layout(re, im):
    lead = re.shape[:-2]
    re = re.reshape(lead + (S5_CHUNKS, 512))
    im = im.reshape(lead + (S5_CHUNKS, 512))
    return jnp.stack([re, im], axis=-2).reshape(lead + (S5_STATE_LANES,))


def _state_unlayout(h):
    lead = h.shape[:-1]
    h = h.reshape(lead + (S5_CHUNKS, 2, 512))
    re = h[..., 0, :].reshape(lead + (S5_GROUPS, S5_STATE))
    im = h[..., 1, :].reshape(lead + (S5_GROUPS, S5_STATE))
    return re, im


def _s5_weights(bbr, bbi, c_re, c_im):
    eye8 = jnp.eye(8, dtype=F32)

    def blockdiag(m):
        a, b = m.shape[1], m.shape[2]
        m = m.reshape(S5_CHUNKS, 8, a, b)
        return (eye8[None, :, None, :, None] * m[:, :, :, None, :]).reshape(S5_CHUNKS, 8 * a, 8 * b)

    wb = jnp.concatenate([blockdiag(bbr), blockdiag(bbi)], axis=2)
    cr_t = jnp.swapaxes(c_re, 1, 2)
    ci_t = jnp.swapaxes(c_im, 1, 2)
    wc = jnp.concatenate([blockdiag(cr_t), -blockdiag(ci_t)], axis=1)
    return wb.astype(BF16), wc.astype(BF16)


def _in_proj_kernel(x_ref, gmix_ref, w_ref, gq_ref, gkv_ref, ck_ref, sk_ref, *rest, grp):
    u_ref, qn_ref, lat_ref, krot_ref, latall_ref, krope_ref = rest[-6:]
    bsz = x_ref.shape[0]
    if latall_ref.shape[0] != bsz:
        latall_ref[...] = jnp.zeros(latall_ref.shape, F32)
        krope_ref[...] = jnp.zeros(krope_ref.shape, F32)
        latall_ref, krope_ref = latall_ref.at[grp * bsz:(grp + 1) * bsz], krope_ref.at[grp * bsz:(grp + 1) * bsz]
    steps = x_ref.shape[1]
    nb = max(1, min(bsz, SUB_TILE // steps))
    ck, sk = ck_ref[...][None], sk_ref[...][None]
    for b0 in range(0, bsz, nb):
        bs = slice(b0, b0 + nb)
        rows = nb * steps
        xn = _rms(x_ref[bs].reshape(rows, D_MODEL), gmix_ref[...]).astype(BF16)
        z = jnp.dot(xn, w_ref[...], preferred_element_type=F32)
        for bl in range(nb):
            for c in range(S5_CHUNKS):
                u_ref[c, pl.ds(b0 + bl, steps, stride=bsz), :] = z[bl * steps:(bl + 1) * steps,
                                                                   c * LANES:(c + 1) * LANES]
        qn = _rms(z[:, S5_WIDTH:S5_WIDTH + Q_LORA], gq_ref[...]).astype(BF16)
        qn_ref[bs] = qn.reshape(nb, steps, Q_LORA)
        c0 = S5_WIDTH + Q_LORA
        lat = _rms(z[:, c0:c0 + KV_LORA], gkv_ref[...]).reshape(nb, steps, KV_LORA)
        lat_ref[bs] = lat
        latall_ref[bs] = lat
        kp = z[:, c0 + KV_LORA:]
        lane = lax.broadcasted_iota(jnp.int32, kp.shape, 1)
        first_half = (lane % 64) < 16
        sw = jnp.where(first_half, pltpu.roll(kp, LANES - 16, axis=1), pltpu.roll(kp, 16, axis=1))
        kr = kp.reshape(nb, steps, LANES) * ck + sw.reshape(nb, steps, LANES) * sk
        krope_ref[bs] = kr[:, :, :QK_ROPE]
        krot_ref[bs] = jnp.where(lane.reshape(nb, steps, LANES) >= 64, kr, 0.0)


def _in_proj(x, grp, bsz, g_mix, w_in_p, g_q_lat, g_kv_lat, ck, sk, prev):
    n_all, length, _ = x.shape
    steps = min(ROW_TILE // bsz, length)
    full = lambda i: (0, 0)
    blk = lambda w: pl.BlockSpec((bsz, steps, w), lambda i: (0, i, 0))
    if prev:
        shared = lambda w: pl.BlockSpec((bsz, steps, w), lambda i: (grp, i, 0))
    else:
        shared = lambda w: pl.BlockSpec((n_all, steps, w), lambda i: (0, i, 0))
    return pl.pallas_call(
        functools.partial(_in_proj_kernel, grp=grp),
        grid=(length // steps,),
        in_specs=[pl.BlockSpec((bsz, steps, D_MODEL), lambda i: (grp, i, 0)),
                  pl.BlockSpec((1, D_MODEL), full),
                  pl.BlockSpec((D_MODEL, D_IN_PAD), full),
                  pl.BlockSpec((1, Q_LORA), full),
                  pl.BlockSpec((1, KV_LORA), full),
                  pl.BlockSpec((steps, LANES), lambda i: (i, 0)),
                  pl.BlockSpec((steps, LANES), lambda i: (i, 0))]
                 + [pl.BlockSpec(memory_space=pl.ANY)] * len(prev),
        out_specs=[pl.BlockSpec((S5_CHUNKS, steps * bsz, LANES), lambda i: (0, i, 0)),
                   blk(Q_LORA), blk(KV_LORA), blk(LANES),
                   shared(KV_LORA), shared(QK_ROPE)],
        out_shape=(jax.ShapeDtypeStruct((S5_CHUNKS, length * bsz, LANES), F32),
                   jax.ShapeDtypeStruct((bsz, length, Q_LORA), BF16),
                   jax.ShapeDtypeStruct((bsz, length, KV_LORA), F32),
                   jax.ShapeDtypeStruct((bsz, length, LANES), F32),
                   jax.ShapeDtypeStruct((n_all, length, KV_LORA), F32),
                   jax.ShapeDtypeStruct((n_all, length, QK_ROPE), F32)),
        input_output_aliases={7 + j: 4 + j for j in range(len(prev))},
        compiler_params=_cparams(("parallel",), VMEM_LIMIT),
        name="in_proj",
    )(x, g_mix, w_in_p, g_q_lat, g_kv_lat, ck, sk, *prev)


def _s5_kernel(u_ref, h0_ref, lam_ref, wb_ref, wc_ref, dskip_ref, wglu_ref, bglu_ref, gout_ref,
               y_ref, hlast_ref, bu_ref, h_ref, *, bsz, steps):
    c_id = pl.program_id(0)

    @pl.when(c_id == 0)
    def _():
        h_ref[...] = h0_ref[...]

    for c in range(S5_CHUNKS):
        bu_ref[:, c * S5_CHUNK_LANES:(c + 1) * S5_CHUNK_LANES] = jnp.dot(
            u_ref[c].astype(BF16), wb_ref[c], preferred_element_type=F32)

    for c in range(S5_CHUNKS):
        re = slice(c * S5_CHUNK_LANES, c * S5_CHUNK_LANES + 512)
        im = slice(c * S5_CHUNK_LANES + 512, (c + 1) * S5_CHUNK_LANES)
        lam_r = jnp.broadcast_to(lam_ref[:, re], (bsz, 512))
        lam_i = jnp.broadcast_to(lam_ref[:, im], (bsz, 512))

        hr, hi = h_ref[:, re], h_ref[:, im]
        for t in range(steps):
            rows = slice(t * bsz, (t + 1) * bsz)
            hr, hi = (lam_r * hr - lam_i * hi + bu_ref[rows, re],
                      lam_r * hi + lam_i * hr + bu_ref[rows, im])
            bu_ref[rows, re] = hr
            bu_ref[rows, im] = hi
        h_ref[:, re] = hr
        h_ref[:, im] = hi

    ys = []
    for c in range(S5_CHUNKS):
        hs = bu_ref[:, c * S5_CHUNK_LANES:(c + 1) * S5_CHUNK_LANES].astype(BF16)
        ys.append(jnp.dot(hs, wc_ref[c], preferred_element_type=F32))
    u = jnp.concatenate([u_ref[c] for c in range(S5_CHUNKS)], axis=1)
    y = jnp.concatenate(ys, axis=1) + dskip_ref[...] * u
    y = jax.nn.gelu(y)
    gate = jnp.dot(y.astype(BF16), wglu_ref[...], preferred_element_type=F32) + bglu_ref[...]
    y = _rms(y * jax.nn.sigmoid(gate), gout_ref[...])
    for c in range(S5_CHUNKS):
        y_ref[c] = y[:, c * LANES:(c + 1) * LANES]

    @pl.when(c_id == pl.num_programs(0) - 1)
    def _():
        hlast_ref[...] = h_ref[...]


def _s5(u_tm, h0, lam, wb, wc, d_skip, w_glu, b_glu, g_out, bsz, length):
    steps = min(S5_ROWS // bsz, length)
    rows = steps * bsz
    full2 = lambda c: (0, 0)
    full3 = lambda c: (0, 0, 0)
    return pl.pallas_call(
        functools.partial(_s5_kernel, bsz=bsz, steps=steps),
        grid=(length // steps,),
        in_specs=[pl.BlockSpec((S5_CHUNKS, rows, LANES), lambda c: (0, c, 0)),
                  pl.BlockSpec((bsz, S5_STATE_LANES), full2),
                  pl.BlockSpec((1, S5_STATE_LANES), full2),
                  pl.BlockSpec((S5_CHUNKS, LANES, S5_CHUNK_LANES), full3),
                  pl.BlockSpec((S5_CHUNKS, S5_CHUNK_LANES, LANES), full3),
                  pl.BlockSpec((1, S5_WIDTH), full2),
                  pl.BlockSpec((S5_WIDTH, S5_WIDTH), full2),
                  pl.BlockSpec((1, S5_WIDTH), full2),
                  pl.BlockSpec((1, S5_WIDTH), full2)],
        out_specs=[pl.BlockSpec((S5_CHUNKS, rows, LANES), lambda c: (0, c, 0)),
                   pl.BlockSpec((bsz, S5_STATE_LANES), full2)],
        out_shape=(jax.ShapeDtypeStruct((S5_CHUNKS, length * bsz, LANES), F32),
                   jax.ShapeDtypeStruct((bsz, S5_STATE_LANES), F32)),
        scratch_shapes=[pltpu.VMEM((rows, S5_STATE_LANES), F32),
                        pltpu.VMEM((bsz, S5_STATE_LANES), F32)],
        compiler_params=_cparams(("arbitrary",), VMEM_LIMIT),
        name="s5_mixer",
    )(u_tm, h0, lam, wb, wc, d_skip, w_glu, b_glu, g_out)


def _q_proj_kernel(qn_ref, w_ref, wsw_ref, ones_ref, cq_ref, sq_ref, g_ref, q_ref):
    g = g_ref[...] * (QK_HEAD ** -0.5)
    for rs in _sub_tiles(qn_ref.shape[0]):
        qn = qn_ref[rs, :]
        qf = jnp.dot(qn, w_ref[...], preferred_element_type=F32)
        qs = jnp.dot(qn, wsw_ref[...], preferred_element_type=F32)
        cq, sq = cq_ref[rs, :], sq_ref[rs, :]
        for p in range(MLA_HEADS // 2):
            cols = slice(2 * p * HEAD_PAD, 2 * (p + 1) * HEAD_PAD)
            xr = qf[:, cols] * cq + qs[:, cols] * sq
            sqr = xr * xr
            hi = sqr.astype(BF16)
            lo = (sqr - hi.astype(F32)).astype(BF16)
            ss = (jnp.dot(hi, ones_ref[...], preferred_element_type=F32)
                  + jnp.dot(lo, ones_ref[...], preferred_element_type=F32))
            q = (xr * lax.rsqrt(ss * (1.0 / QK_HEAD) + EPS) * g).astype(BF16)
            q_ref[0, 2 * p, rs, :] = q[:, :HEAD_PAD]
            q_ref[0, 2 * p + 1, rs, :] = q[:, HEAD_PAD:]


def _q_proj(qn, w_uq_p, w_uq_sw, cq, sq, gq, bsz, length):
    tm = min(ROW_TILE, length)
    n_l = length // tm
    full = lambda b, i: (0, 0)
    pair = 2 * HEAD_PAD
    two = lambda a: jnp.concatenate([a, a], axis=1)
    ones2 = (jnp.arange(pair)[:, None] // HEAD_PAD == jnp.arange(pair)[None, :] // HEAD_PAD).astype(BF16)
    return pl.pallas_call(
        _q_proj_kernel,
        grid=(bsz, n_l),
        in_specs=[pl.BlockSpec((tm, Q_LORA), lambda b, i: (b * n_l + i, 0)),
                  pl.BlockSpec((Q_LORA, MLA_HEADS * HEAD_PAD), full),
                  pl.BlockSpec((Q_LORA, MLA_HEADS * HEAD_PAD), full),
                  pl.BlockSpec((pair, pair), full),
                  pl.BlockSpec((tm, pair), lambda b, i: (i, 0)),
                  pl.BlockSpec((tm, pair), lambda b, i: (i, 0)),
                  pl.BlockSpec((1, pair), full)],
        out_specs=pl.BlockSpec((1, MLA_HEADS, tm, HEAD_PAD), lambda b, i: (b, 0, i, 0)),
        out_shape=jax.ShapeDtypeStruct((bsz, MLA_HEADS, length, HEAD_PAD), BF16),
        compiler_params=_cparams(("parallel", "parallel"), VMEM_LIMIT),
        name="q_proj",
    )(qn, w_uq_p, w_uq_sw, ones2, two(cq), two(sq), two(gq))


def _kv_proj_kernel(lat_ref, krot_ref, wk_ref, wv_ref, g_ref, k_ref, v_ref):
    g = g_ref[...]
    for rs in _sub_tiles(lat_ref.shape[0]):
        lat = lat_ref[rs, :].astype(BF16)
        kf = jnp.dot(lat, wk_ref[...], preferred_element_type=F32)
        vf = jnp.dot(lat, wv_ref[...], preferred_element_type=F32)
        krot = krot_ref[rs, :]
        for h in range(MLA_HEADS):
            k = kf[:, h * HEAD_PAD:(h + 1) * HEAD_PAD] + krot
            k_ref[0, h, rs, :] = _rms(k, g, QK_HEAD).astype(BF16)
        for p in range(MLA_HEADS // 2):
            v_ref[0, p, rs, :] = vf[:, p * LANES:(p + 1) * LANES].astype(BF16)


def _kv_proj(lat, krot, w_k, w_v, gk, bsz, length):
    tm = min(ROW_TILE, length)
    n_l = length // tm
    full = lambda b, i: (0, 0)
    row = lambda b, i: (b * n_l + i, 0)
    return pl.pallas_call(
        _kv_proj_kernel,
        grid=(bsz, n_l),
        in_specs=[pl.BlockSpec((tm, KV_LORA), row),
                  pl.BlockSpec((tm, HEAD_PAD), row),
                  pl.BlockSpec((KV_LORA, MLA_HEADS * HEAD_PAD), full),
                  pl.BlockSpec((KV_LORA, MLA_WIDTH), full),
                  pl.BlockSpec((1, HEAD_PAD), full)],
        out_specs=[pl.BlockSpec((1, MLA_HEADS, tm, HEAD_PAD), lambda b, i: (b, 0, i, 0)),
                   pl.BlockSpec((1, MLA_HEADS // 2, tm, LANES), lambda b, i: (b, 0, i, 0))],
        out_shape=(jax.ShapeDtypeStruct((bsz, MLA_HEADS, length, HEAD_PAD), BF16),
                   jax.ShapeDtypeStruct((bsz, MLA_HEADS // 2, length, LANES), BF16)),
        compiler_params=_cparams(("parallel", "parallel"), VMEM_LIMIT),
        name="kv_proj",
    )(lat, krot, w_k, w_v, gk)


def _scores(q, k):
    return lax.dot_general(q, k, (((1,), (1,)), ((), ())), preferred_element_type=F32)


def _attn_prompt_kernel(q_ref, k_ref, v_ref, o_ref, *, length, tq):
    n_q = length // tq
    row = lax.broadcasted_iota(jnp.int32, (tq, tq), 0)
    col = lax.broadcasted_iota(jnp.int32, (tq, tq), 1)
    visible = (col // CHUNK) <= (row // CHUNK)
    lane = lax.broadcasted_iota(jnp.int32, (tq, LANES), 1)
    for qi in range(n_q):
        q0 = qi * tq
        outs = []
        for hh in range(2):
            q = q_ref[0, hh, q0:q0 + tq, :]
            sd = jnp.where(visible, _scores(q, k_ref[0, hh, q0:q0 + tq, :]), -jnp.inf)
            m = jnp.max(sd, axis=-1, keepdims=True)
            if qi:
                so = _scores(q, k_ref[0, hh, 0:q0, :])
                m = jnp.maximum(m, jnp.max(so, axis=-1, keepdims=True))
            pd = jnp.exp(sd - m)
            l = jnp.sum(pd, axis=-1, keepdims=True)
            acc = jnp.dot(pd.astype(BF16), v_ref[0, 0, q0:q0 + tq, :], preferred_element_type=F32)
            if qi:
                po = jnp.exp(so - m)
                l = l + jnp.sum(po, axis=-1, keepdims=True)
                acc = acc + jnp.dot(po.astype(BF16), v_ref[0, 0, 0:q0, :], preferred_element_type=F32)
            outs.append(acc / l)
        o_ref[0, q0:q0 + tq, :] = jnp.where(lane < V_HEAD, outs[0], outs[1])


def _attn_prompt(q, k, v, bsz, length):
    tq = min(512, length)
    pairs = MLA_HEADS // 2
    return pl.pallas_call(
        functools.partial(_attn_prompt_kernel, length=length, tq=tq),
        grid=(bsz, pairs),
        in_specs=[pl.BlockSpec((1, 2, length, HEAD_PAD), lambda b, p: (b, p, 0, 0)),
                  pl.BlockSpec((1, 2, length, HEAD_PAD), lambda b, p: (b, p, 0, 0)),
                  pl.BlockSpec((1, 1, length, LANES), lambda b, p: (b, p, 0, 0))],
        out_specs=pl.BlockSpec((1, length, LANES), lambda b, p: (b, 0, p)),
        out_shape=jax.ShapeDtypeStruct((bsz, length, MLA_WIDTH), F32),
        compiler_params=_cparams(("parallel", "parallel"), VMEM_LIMIT),
        name="attn_prompt",
    )(q, k, v)


def _attn_sample_kernel(q_ref, kc_ref, vc_ref, kn_ref, vn_ref, o_ref):
    lq = q_ref.shape[2]
    lane = lax.broadcasted_iota(jnp.int32, (lq, LANES), 1)
    outs = []
    for hh in range(2):
        q = q_ref[0, hh]
        sc = _scores(q, kc_ref[0, hh])
        sn = _scores(q, kn_ref[0, hh])
        m = jnp.maximum(jnp.max(sc, axis=-1, keepdims=True), jnp.max(sn, axis=-1, keepdims=True))
        pc = jnp.exp(sc - m)
        pn = jnp.exp(sn - m)
        l = jnp.sum(pc, axis=-1, keepdims=True) + jnp.sum(pn, axis=-1, keepdims=True)
        acc = (jnp.dot(pc.astype(BF16), vc_ref[0, 0], preferred_element_type=F32)
               + jnp.dot(pn.astype(BF16), vn_ref[0, 0], preferred_element_type=F32))
        outs.append(acc / l)
    o_ref[0] = jnp.where(lane < V_HEAD, outs[0], outs[1])


def _attn_sample(q, kc, vc, kn, vn, bsz, lq, past):
    pairs = MLA_HEADS // 2
    hp = lambda b, p: (b, p, 0, 0)
    return pl.pallas_call(
        _attn_sample_kernel,
        grid=(bsz, pairs),
        in_specs=[pl.BlockSpec((1, 2, lq, HEAD_PAD), hp),
                  pl.BlockSpec((1, 2, past, HEAD_PAD), hp),
                  pl.BlockSpec((1, 1, past, LANES), hp),
                  pl.BlockSpec((1, 2, lq, HEAD_PAD), hp),
                  pl.BlockSpec((1, 1, lq, LANES), hp)],
        out_specs=pl.BlockSpec((1, lq, LANES), lambda b, p: (b, 0, p)),
        out_shape=jax.ShapeDtypeStruct((bsz, lq, MLA_WIDTH), F32),
        compiler_params=_cparams(("parallel", "parallel"), VMEM_LIMIT),
        name="attn_sample",
    )(q, kc, vc, kn, vn)


def _out_proj_kernel(x_ref, s5_ref, at_ref, gmla_ref, wout_ref, gffn_ref, wr_ref, br_ref, tri_ref, cin_ref,
                     x1_ref, hn_ref, route_ref, gate_ref, cout_ref, run_ref):
    @pl.when(pl.program_id(0) == 0)
    def _():
        run_ref[...] = cin_ref[...]

    bsz, steps, _ = x_ref.shape
    nb = max(1, min(bsz, SUB_TILE // steps))
    for b0 in range(0, bsz, nb):
        bs = slice(b0, b0 + nb)
        rows = nb * steps
        s5 = jnp.concatenate(
            [jnp.concatenate([s5_ref[c, pl.ds(b0 + bl, steps, stride=bsz), :] for c in range(S5_CHUNKS)], axis=1)
             for bl in range(nb)], axis=0)
        an = _rms(at_ref[bs].reshape(rows, MLA_WIDTH), gmla_ref[...])
        merged = jnp.concatenate([s5.astype(BF16), an.astype(BF16)], axis=1)
        x1 = x_ref[bs].reshape(rows, D_MODEL) + jnp.dot(merged, wout_ref[...], preferred_element_type=F32)
        x1_ref[bs] = x1.reshape(nb, steps, D_MODEL)
        hn = _rms(x1, gffn_ref[...])
        hn_ref[bs] = _pack_rows(hn).reshape(nb, steps, D_PACK)
        hn_hi = hn.astype(BF16)
        hn_lo = (hn - hn_hi.astype(F32)).astype(BF16)
        parts = (jnp.dot(hn_hi, wr_ref[...], preferred_element_type=F32)
                 + jnp.dot(hn_lo, wr_ref[...], preferred_element_type=F32))
        logits = parts[:, :LANES] + parts[:, LANES:] + br_ref[...]
        lane = lax.broadcasted_iota(jnp.int32, logits.shape, 1)
        lane_f = lane.astype(F32)
        work = jnp.where(lane < N_EXPERTS, logits, -jnp.inf)
        vals, idxs = [], []
        for _ in range(TOP_K):
            m = jnp.max(work, axis=-1, keepdims=True)
            i = jnp.min(jnp.where(work == m, lane_f, float(LANES)), axis=-1, keepdims=True)
            vals.append(m)
            idxs.append(i)
            work = jnp.where(lane_f == i, -jnp.inf, work)
        es = [jnp.exp(v - vals[0]) for v in vals]
        den = es[0] + es[1] + es[2] + es[3]
        onehots = [(lane_f == idxs[k]).astype(F32) for k in range(TOP_K)]
        e_all = onehots[0] + onehots[1] + onehots[2] + onehots[3]
        before = jnp.dot(tri_ref[...], e_all.astype(BF16), preferred_element_type=F32) + run_ref[...]
        run_ref[...] = run_ref[...] + jnp.sum(e_all, axis=0, keepdims=True)
        route = jnp.zeros(logits.shape, F32)
        gate_out = jnp.zeros(logits.shape, F32)
        for k in range(TOP_K):
            rank = jnp.sum(onehots[k] * before, axis=-1, keepdims=True)
            route = jnp.where(lane == k, idxs[k], route)
            route = jnp.where(lane == 8 + k, rank, route)
            gate_out = jnp.where(lane == k, es[k] / den, gate_out)
        route_ref[bs] = route.astype(jnp.int32).reshape(nb, steps, LANES)
        gate_ref[bs] = gate_out.reshape(nb, steps, LANES)
    cout_ref[...] = run_ref[...]


def _out_proj(x, grp, s5n_tm, attn, g_mla, w_out, g_ffn, w_r, b_r_p, counts_in):
    bsz, length, _ = attn.shape
    steps = min(ROW_TILE // bsz, length)
    full = lambda i: (0, 0)
    blk = lambda w: pl.BlockSpec((bsz, steps, w), lambda i: (0, i, 0))
    sub = max(1, min(bsz, SUB_TILE // steps)) * steps
    tri = (jnp.arange(sub)[:, None] > jnp.arange(sub)[None, :]).astype(BF16)
    return pl.pallas_call(
        _out_proj_kernel,
        grid=(length // steps,),
        in_specs=[pl.BlockSpec((bsz, steps, D_MODEL), lambda i: (grp, i, 0)),
                  pl.BlockSpec((S5_CHUNKS, steps * bsz, LANES), lambda i: (0, i, 0)),
                  blk(MLA_WIDTH),
                  pl.BlockSpec((1, MLA_WIDTH), full),
                  pl.BlockSpec((D_MODEL, D_MODEL), full),
                  pl.BlockSpec((1, D_MODEL), full),
                  pl.BlockSpec((D_MODEL, 2 * LANES), full),
                  pl.BlockSpec((1, LANES), full),
                  pl.BlockSpec((sub, sub), full),
                  pl.BlockSpec((1, LANES), full)],
        out_specs=[blk(D_MODEL), blk(D_PACK), blk(LANES), blk(LANES),
                   pl.BlockSpec((1, LANES), full)],
        out_shape=(jax.ShapeDtypeStruct((bsz, length, D_MODEL), F32),
                   jax.ShapeDtypeStruct((bsz, length, D_PACK), U32),
                   jax.ShapeDtypeStruct((bsz, length, LANES), jnp.int32),
                   jax.ShapeDtypeStruct((bsz, length, LANES), F32),
                   jax.ShapeDtypeStruct((1, LANES), F32)),
        scratch_shapes=[pltpu.VMEM((1, LANES), F32)],
        compiler_params=_cparams(("arbitrary",), VMEM_LIMIT),
        name="out_proj",
    )(x, s5n_tm, attn, g_mla, w_out, g_ffn, w_r, b_r_p, tri, counts_in)


def _sc_mesh():
    return plsc.VectorSubcoreMesh(core_axis_name="c", subcore_axis_name="s",
                                  num_cores=SC_CORES, num_subcores=SC_SUBCORES)


def _sc_for_chunks(n_chunks, fn):
    wid = lax.axis_index("s") * SC_CORES + lax.axis_index("c")
    full, rem = divmod(n_chunks, SC_WORKERS)
    if full:
        @pl.loop(0, full)
        def _(j):
            fn(j * SC_WORKERS + wid)
    if rem:
        @pl.when(wid < rem)
        def _():
            fn(full * SC_WORKERS + wid)


def _sc_dispatch_body(*refs, n_seg):
    d_hbms, h_hbms = refs[:n_seg], refs[n_seg:2 * n_seg]
    xs_hbm, idx_v, rows_v, sems = refs[2 * n_seg:]
    n = SC_GATHER_ROWS
    wid = lax.axis_index("s") * SC_CORES + lax.axis_index("c")

    def scatter(d_hbm, h_hbm, c, slot):
        pltpu.sync_copy(d_hbm.at[c], idx_v.at[slot])
        pltpu.sync_copy(h_hbm.at[pl.ds(pl.multiple_of(c * n, n), n)], rows_v.at[slot])
        return [pltpu.async_copy(rows_v.at[slot], xs_hbm.at[idx_v.at[slot, k]], sems.at[slot])
                for k in range(TOP_K)]

    def run(d_hbm, h_hbm):
        full, rem = divmod(d_hbm.shape[0], SC_WORKERS)
        if full // 2:
            @pl.loop(0, full // 2)
            def _(j):
                first = scatter(d_hbm, h_hbm, (2 * j) * SC_WORKERS + wid, 0)
                second = scatter(d_hbm, h_hbm, (2 * j + 1) * SC_WORKERS + wid, 1)
                for cp in first + second:
                    cp.wait()
        tail = [(full - 1, None)] if full % 2 else []
        if rem:
            tail.append((full, rem))
        for row, limit in tail:
            def one(row=row):
                for cp in scatter(d_hbm, h_hbm, row * SC_WORKERS + wid, 0):
                    cp.wait()
            if limit is None:
                one()
            else:
                pl.when(wid < limit)(one)

    for d_hbm, h_hbm in zip(d_hbms, h_hbms):
        run(d_hbm, h_hbm)


def _dispatch(dests, hns, n_blocks):
    return pl.kernel(
        functools.partial(_sc_dispatch_body, n_seg=len(dests)),
        out_type=jax.ShapeDtypeStruct((n_blocks * MOE_ROWS, D_PACK), U32),
        mesh=_sc_mesh(),
        scratch_types=[pltpu.VMEM((2, 8, SC_GATHER_ROWS), jnp.int32),
                       pltpu.VMEM((2, SC_GATHER_ROWS, D_PACK), U32),
                       pltpu.SemaphoreType.DMA((2,))],
        name="moe_dispatch_sc",
    )(*dests, *hns)


def _expert_kernel(be_ref, nb_ref, nv_ref, first_ref, slot_ref, nxt_ref,
                   x_ref, w1_hbm, b1_ref, w2_hbm, b2_ref, y_ref, w1f_ref, w2f_ref, w1b_ref, w2b_ref, sem):
    step = pl.program_id(0)

    def weight_copies(e):
        return (pltpu.make_async_copy(w1_hbm.at[e], w1f_ref, sem.at[0]),
                pltpu.make_async_copy(w2_hbm.at[e], w2f_ref, sem.at[1]))

    @pl.when(jnp.logical_and(step == 0, nb_ref[0] > 0))
    def _():
        for cp in weight_copies(be_ref[0]):
            cp.start()

    for sb in range(MOE_PAIR):
        b = step * MOE_PAIR + sb

        @pl.when(jnp.logical_and(b < nb_ref[0], first_ref[b] == 1))
        def _(b=b):
            s = slot_ref[b]
            for cp in weight_copies(be_ref[b]):
                cp.wait()
            w1b_ref[s] = w1f_ref[...].astype(BF16)
            w2b_ref[s] = w2f_ref[...].astype(BF16)

            @pl.when(nxt_ref[b] >= 0)
            def _():
                for cp in weight_copies(nxt_ref[b]):
                    cp.start()

    @pl.when(step * MOE_PAIR < nb_ref[0])
    def _():
        cw = D_FF // MOE_COL_CHUNKS
        for sb in range(MOE_PAIR):
            b = step * MOE_PAIR + sb
            e, s = be_ref[b], slot_ref[b]
            rs = slice(sb * MOE_ROWS, (sb + 1) * MOE_ROWS)
            live = lax.broadcasted_iota(jnp.int32, (MOE_ROWS, D_PACK), 0) < nv_ref[b]
            lo, hi = _unpack_rows(jnp.where(live, x_ref[rs, :], jnp.uint32(0)))
            lo, hi = lo.astype(BF16), hi.astype(BF16)
            b1 = b1_ref[e]

            def up(c0, lo=lo, hi=hi, s=s, b1=b1):
                return (jnp.dot(lo, w1b_ref[s, :D_PACK, c0:c0 + cw], preferred_element_type=F32)
                        + jnp.dot(hi, w1b_ref[s, D_PACK:, c0:c0 + cw], preferred_element_type=F32)
                        + b1[:, c0:c0 + cw])

            y = b2_ref[e]
            for j in range(MOE_COL_CHUNKS):
                gate = jnp.minimum(up(j * cw), SWIGLU_LIMIT)
                lin = jnp.clip(up(D_FF + j * cw), -SWIGLU_LIMIT, SWIGLU_LIMIT)
                act = gate * jax.nn.sigmoid(SWIGLU_ALPHA * gate) * (lin + 1.0)
                y = y + jnp.dot(act.astype(BF16), w2b_ref[s, j * cw:(j + 1) * cw, :],
                                preferred_element_type=F32)
            y_ref[rs, :] = _pack_rows(y)

    @pl.when(step * MOE_PAIR >= nb_ref[0])
    def _():
        y_ref[...] = jnp.zeros(y_ref.shape, y_ref.dtype)


def _experts(tables, xs, w1, b1, w2, b2, n_blocks):
    rows = MOE_PAIR * MOE_ROWS
    last = lambda p, be, nb, *_: (jnp.maximum(jnp.minimum(p, (nb[0] - 1) // MOE_PAIR), 0), 0)
    whole = lambda p, *_: (0, 0, 0)
    return pl.pallas_call(
        _expert_kernel,
        grid_spec=pltpu.PrefetchScalarGridSpec(
            num_scalar_prefetch=6,
            grid=(n_blocks // MOE_PAIR,),
            in_specs=[pl.BlockSpec((rows, D_PACK), last),
                      pl.BlockSpec(memory_space=pl.ANY),
                      pl.BlockSpec((N_EXPERTS, 1, 2 * D_FF), whole),
                      pl.BlockSpec(memory_space=pl.ANY),
                      pl.BlockSpec((N_EXPERTS, 1, D_MODEL), whole)],
            out_specs=pl.BlockSpec((rows, D_PACK), lambda p, *_: (p, 0)),
            scratch_shapes=[pltpu.VMEM((D_MODEL, 2 * D_FF), F32), pltpu.VMEM((D_FF, D_MODEL), F32),
                            pltpu.VMEM((MOE_PAIR, D_MODEL, 2 * D_FF), BF16),
                            pltpu.VMEM((MOE_PAIR, D_FF, D_MODEL), BF16),
                            pltpu.SemaphoreType.DMA((2,))]),
        out_shape=jax.ShapeDtypeStruct((n_blocks * MOE_ROWS, D_PACK), U32),
        compiler_params=_cparams(("arbitrary",), VMEM_LIMIT),
        name="moe_experts",
    )(*tables, xs, w1, b1.reshape(N_EXPERTS, 1, 2 * D_FF), w2, b2.reshape(N_EXPERTS, 1, D_MODEL))


def _sc_gather_body(*refs, n_seg):
    d_hbms, yb_hbm = refs[:n_seg], refs[n_seg]
    g_hbms = refs[n_seg + 1:2 * n_seg + 1]
    idx_v, rows_v, sems = refs[2 * n_seg + 1:]
    n = SC_GATHER_ROWS

    def chunk(d_hbm, g_hbm, c):
        pltpu.sync_copy(d_hbm.at[c], idx_v)
        rows = pl.ds(pl.multiple_of(c * n, n), n)
        gather = lambda k: pltpu.async_copy(yb_hbm.at[idx_v.at[k]], rows_v.at[k % 2], sems.at[k % 2])
        cp = gather(0)
        for k in range(TOP_K):
            cp.wait()
            if k + 1 < TOP_K:
                cp = gather(k + 1)
            pltpu.sync_copy(rows_v.at[k % 2], g_hbm.at[k, rows])

    for d_hbm, g_hbm in zip(d_hbms, g_hbms):
        _sc_for_chunks(d_hbm.shape[0], functools.partial(chunk, d_hbm, g_hbm))


def _gather_expert_rows(dests, yb):
    return pl.kernel(
        functools.partial(_sc_gather_body, n_seg=len(dests)),
        out_type=tuple(jax.ShapeDtypeStruct((TOP_K, d.shape[0] * SC_GATHER_ROWS, D_PACK), U32) for d in dests),
        mesh=_sc_mesh(),
        scratch_types=[pltpu.VMEM((8, SC_GATHER_ROWS), jnp.int32),
                       pltpu.VMEM((2, SC_GATHER_ROWS, D_PACK), U32),
                       pltpu.SemaphoreType.DMA((2,))],
        name="moe_gather_sc",
    )(*dests, yb)


def _combine_kernel(g_ref, gate_ref, x1_ref, *rest, first_blk, n_blk):
    o_ref = rest[-1]
    i = pl.program_id(0)

    @pl.when(jnp.logical_and(i >= first_blk, i < first_blk + n_blk))
    def _():
        gate = gate_ref[...]
        x1 = x1_ref[...]
        acc_lo, acc_hi = x1[:, :D_PACK], x1[:, D_PACK:]
        for k in range(TOP_K):
            lo, hi = _unpack_rows(g_ref[k])
            acc_lo = acc_lo + gate[:, k:k + 1] * lo
            acc_hi = acc_hi + gate[:, k:k + 1] * hi
        o_ref[:, :D_PACK] = acc_lo
        o_ref[:, D_PACK:] = acc_hi

    @pl.when(jnp.logical_or(i < first_blk, i >= first_blk + n_blk))
    def _():
        o_ref[...] = jnp.zeros(o_ref.shape, o_ref.dtype)


def _combine(g, gates, x1, out_rows=None, row0=0, prev=None):
    n_tok = x1.shape[0]
    out_rows = n_tok if out_rows is None else out_rows
    tm = min(512, n_tok)
    n_blk = n_tok // tm
    first_blk, grid = (0, n_blk) if prev is not None else (row0 // tm, out_rows // tm)
    own = lambda i: jnp.clip(i - first_blk, 0, n_blk - 1)
    in_specs = [pl.BlockSpec((TOP_K, tm, D_PACK), lambda i: (0, own(i), 0)),
                pl.BlockSpec((tm, LANES), lambda i: (own(i), 0)),
                pl.BlockSpec((tm, D_MODEL), lambda i: (own(i), 0))]
    args = [g, gates, x1]
    out_blk0, aliases = 0, {}
    if prev is not None:
        in_specs.append(pl.BlockSpec(memory_space=pl.ANY))
        args.append(prev)
        out_blk0, aliases = row0 // tm, {3: 0}
    return pl.pallas_call(
        functools.partial(_combine_kernel, first_blk=first_blk, n_blk=n_blk),
        grid=(grid,),
        in_specs=in_specs,
        out_specs=pl.BlockSpec((tm, D_MODEL), lambda i: (out_blk0 + i, 0)),
        out_shape=jax.ShapeDtypeStruct((out_rows, D_MODEL), F32),
        input_output_aliases=aliases,
        compiler_params=_cparams(("parallel",), VMEM_LIMIT),
        name="moe_combine",
    )(*args)


def _route_tables(counts_f, n_assign):
    counts = counts_f[0, :N_EXPERTS].astype(jnp.int32)
    nblk_e = (counts + MOE_ROWS - 1) // MOE_ROWS
    bend = jnp.cumsum(nblk_e)
    bstart = bend - nblk_e
    pad_start = bstart * MOE_ROWS
    n_blocks = -(-(n_assign + N_EXPERTS * (MOE_ROWS - 1)) // (MOE_ROWS * MOE_PAIR)) * MOE_PAIR
    e_ids = jnp.arange(N_EXPERTS, dtype=jnp.int32)
    nonempty = counts > 0
    slot_e = (jnp.cumsum(nonempty.astype(jnp.int32)) - 1) % MOE_PAIR
    later = jnp.where(nonempty[None, :] & (e_ids[None, :] > e_ids[:, None]), e_ids[None, :], N_EXPERTS)
    next_e = jnp.min(later, axis=1)
    next_e = jnp.where(next_e == N_EXPERTS, -1, next_e)
    bidx = jnp.arange(n_blocks, dtype=jnp.int32)
    mine = ((bidx[:, None] >= bstart[None, :]) & (bidx[:, None] < bend[None, :])).astype(jnp.int32)
    vals = jnp.stack([e_ids, slot_e, next_e, pad_start + counts, bstart, jnp.ones_like(e_ids)], axis=0)
    per_block = jnp.sum(mine[:, None, :] * vals[None, :, :], axis=-1)
    block_e, slot, nxt, row_end, first_blk, used = (per_block[:, j] for j in range(6))
    n_valid = jnp.clip(row_end - bidx * MOE_ROWS, 0, MOE_ROWS)
    first = ((bidx == first_blk) & (used > 0)).astype(jnp.int32)
    tables = (block_e, bend[-1:].astype(jnp.int32), n_valid, first, slot, nxt)
    return pad_start, tables, n_blocks


def _dest_kernel(ps_ref, route_ref, d_ref):
    rt = route_ref[...].T
    idx, rank = rt[0:8, :], rt[8:16, :]
    base = jnp.zeros_like(idx)
    for e in range(N_EXPERTS):
        base = jnp.where(idx == e, ps_ref[e], base)
    dest = base + rank
    for j in range(dest.shape[1] // SC_GATHER_ROWS):
        d_ref[j] = dest[:, j * SC_GATHER_ROWS:(j + 1) * SC_GATHER_ROWS]


def _dest(pad_start, route):
    n_tok = route.shape[0]
    tm = min(2048, n_tok)
    n = SC_GATHER_ROWS
    return pl.pallas_call(
        _dest_kernel,
        grid_spec=pltpu.PrefetchScalarGridSpec(
            num_scalar_prefetch=1,
            grid=(n_tok // tm,),
            in_specs=[pl.BlockSpec((tm, LANES), lambda i, ps: (i, 0))],
            out_specs=pl.BlockSpec((tm // n, 8, n), lambda i, ps: (i, 0, 0))),
        out_shape=jax.ShapeDtypeStruct((n_tok // n, 8, n), jnp.int32),
        compiler_params=_cparams(("parallel",), VMEM_LIMIT),
        name="moe_dest",
    )(pad_start, route)


def _rope_tables(offset, length):
    f32 = np.float32
    pos = f32(offset) + np.arange(length, dtype=f32)
    inv = np.power(f32(ROPE_THETA), -np.arange(QK_ROPE // 2, dtype=f32) * f32(2.0 / QK_ROPE))
    ang = pos[:, None] * inv[None, :]
    cos, sin = np.cos(ang), np.sin(ang)
    z32 = np.zeros((length, 32), f32)
    rot_c = np.concatenate([cos, cos], axis=1)
    rot_s = np.concatenate([-sin, sin], axis=1)
    ck = np.concatenate([rot_c, z32, rot_c, z32], axis=1)
    sk = np.concatenate([rot_s, z32, rot_s, z32], axis=1)
    cq = np.concatenate([np.ones((length, QK_NOPE), f32), rot_c, z32], axis=1)
    sq = np.concatenate([np.zeros((length, QK_NOPE), f32), rot_s, z32], axis=1)
    return tuple(jnp.asarray(t, F32) for t in (ck, sk, cq, sq))


def _head_gain(g_nope, g_rope):
    return jnp.concatenate([g_nope, g_rope, g_rope, jnp.zeros((HEAD_PAD - QK_HEAD,), F32)]).reshape(1, HEAD_PAD)


def _mixer_stage(x, grp, bsz, past, prm, counts_in, prev=()):
    length = x.shape[1]
    offset = 0 if past is None else past[0].shape[1]
    ck, sk, cq, sq = _rope_tables(offset, length)
    u_tm, qn, lat, krot, lat_all, k_rope = _in_proj(x, grp, bsz, prm['g_mix'], prm['w_in_p'], prm['g_q_lat'],
                                                    prm['g_kv_lat'], ck, sk, prev)
    if past is None:
        h0 = jnp.zeros((bsz, S5_STATE_LANES), F32)
    else:
        h0 = _state_layout(past[2], past[3])
    s5n_tm, h_last = _s5(u_tm, h0, prm['lam'], prm['wb'], prm['wc'],
                         prm['d_skip'], prm['w_glu'], prm['b_glu'], prm['g_s5_out'], bsz, length)
    h_re, h_im = _state_unlayout(h_last)
    t = bsz * length
    q = _q_proj(qn.reshape(t, Q_LORA), prm['w_uq_p'], prm['w_uq_sw'], cq, sq, prm['gq'], bsz, length)
    k, v = _kv_proj(lat.reshape(t, KV_LORA), krot.reshape(t, LANES), prm['w_k'], prm['w_v'], prm['gk'],
                    bsz, length)
    if past is None:
        attn = _attn_prompt(q, k, v, bsz, length)
    else:
        n_past = past[0].shape[1]
        c_lat = past[0].reshape(bsz * n_past, KV_LORA)
        c_rot = jnp.pad(past[1].reshape(bsz * n_past, QK_ROPE), ((0, 0), (QK_NOPE, HEAD_PAD - QK_HEAD)))
        kc, vc = _kv_proj(c_lat, c_rot, prm['w_k'], prm['w_v'], prm['gk'], bsz, n_past)
        attn = _attn_sample(q, kc, vc, k, v, bsz, length, n_past)
    x1, hn, route, gates, counts = _out_proj(x, grp, s5n_tm, attn, prm['g_mla_out'], prm['w_out'], prm['g_ffn'],
                                             prm['w_r'], prm['b_r_p'], counts_in)
    tok = dict(x1=x1.reshape(t, D_MODEL), hn=hn.reshape(t, D_PACK), route=route.reshape(t, LANES),
               gates=gates.reshape(t, LANES))
    return tok, counts, (lat_all, k_rope, h_re, h_im)


def _prepare(g_mix, w_in, lam_re, lam_im, log_dt, b_s5_re, b_s5_im, c_s5_re, c_s5_im, d_s5, w_glu, b_glu,
             g_q_lat, w_uq, g_kv_lat, w_ukv, g_qn_nope, g_qn_rope, g_kn_nope, g_kn_rope, g_s5_out,
             g_mla_out, w_out, g_ffn, w_router, b_router):
    c0 = S5_WIDTH + Q_LORA + KV_LORA
    w_pe = w_in[:, c0:]
    z = jnp.zeros((D_MODEL, 32), F32)
    w_in_p = jnp.concatenate([w_in[:, :c0], w_pe, z, w_pe, z], axis=1).astype(BF16)
    ar, ai, bbr, bbi = _s5_prep(lam_re, lam_im, log_dt, b_s5_re, b_s5_im)
    wb, wc = _s5_weights(bbr, bbi, c_s5_re, c_s5_im)
    w_uq_p = jnp.pad(w_uq.reshape(Q_LORA, MLA_HEADS, QK_HEAD), ((0, 0), (0, 0), (0, HEAD_PAD - QK_HEAD)))
    r0, r1, r2 = QK_NOPE, QK_NOPE + QK_ROPE // 2, QK_HEAD
    w_uq_sw = jnp.zeros_like(w_uq_p).at[:, :, r0:r1].set(w_uq_p[:, :, r1:r2]).at[:, :, r1:r2].set(w_uq_p[:, :, r0:r1])
    w_kv = w_ukv.reshape(KV_LORA, MLA_HEADS, QK_NOPE + V_HEAD)
    w_k = jnp.pad(w_kv[:, :, :QK_NOPE], ((0, 0), (0, 0), (0, HEAD_PAD - QK_NOPE)))
    w_v = w_kv[:, :, QK_NOPE:]
    w_r_p = jnp.pad(w_router, ((0, 0), (0, LANES - N_EXPERTS)))
    w_r_hi = w_r_p.astype(BF16)
    return dict(
        g_mix=g_mix.reshape(1, -1), w_in_p=w_in_p,
        g_q_lat=g_q_lat.reshape(1, -1), g_kv_lat=g_kv_lat.reshape(1, -1),
        lam=_state_layout(ar, ai).reshape(1, S5_STATE_LANES), wb=wb, wc=wc,
        d_skip=d_s5.reshape(1, -1), w_glu=w_glu.astype(BF16), b_glu=b_glu.reshape(1, -1),
        g_s5_out=g_s5_out.reshape(1, -1),
        w_uq_p=w_uq_p.reshape(Q_LORA, MLA_HEADS * HEAD_PAD).astype(BF16),
        w_uq_sw=w_uq_sw.reshape(Q_LORA, MLA_HEADS * HEAD_PAD).astype(BF16),
        w_k=w_k.reshape(KV_LORA, MLA_HEADS * HEAD_PAD).astype(BF16),
        w_v=w_v.reshape(KV_LORA, MLA_WIDTH).astype(BF16),
        gq=_head_gain(g_qn_nope, g_qn_rope), gk=_head_gain(g_kn_nope, g_kn_rope),
        g_mla_out=g_mla_out.reshape(1, -1), w_out=w_out.astype(BF16), g_ffn=g_ffn.reshape(1, -1),
        w_r=jnp.concatenate([w_r_hi, (w_r_p - w_r_hi.astype(F32)).astype(BF16)], axis=1),
        b_r_p=jnp.pad(b_router, (0, LANES - N_EXPERTS)).reshape(1, LANES),
    )


def _moe_dispatch(segs, counts, w1, b1, w2, b2):
    n_tok = sum(seg['x1'].shape[0] for seg in segs)
    pad_start, tables, n_blocks = _route_tables(counts, n_tok * TOP_K)
    dests = [_dest(pad_start, seg['route']) for seg in segs]
    xs_sorted = _dispatch(dests, [seg['hn'] for seg in segs], n_blocks)
    return dests, _experts(tables, xs_sorted, w1, b1, w2, b2, n_blocks)


def _layer(xp, xs, cache_lat, cache_kr, st_re, st_im, mixer_w, w1, b1, w2, b2):
    prm = _prepare(*mixer_w)
    bp, lp, _ = xp.shape
    bs, ls, _ = xs.shape
    bg = bp // PROMPT_GROUPS
    zero_counts = jnp.zeros((1, LANES), F32)
    passes, prompt_outs = [], []
    for grp in range(PROMPT_GROUPS):
        tok, counts, outs = _mixer_stage(xp, grp, bg, None, prm, zero_counts,
                                         prompt_outs[-1][:2] if prompt_outs else ())
        prompt_outs.append(outs)
        segs = [tok]
        if grp == PROMPT_GROUPS - 1:
            tok_s, counts, sample_outs = _mixer_stage(xs, 0, bs, (cache_lat, cache_kr, st_re, st_im), prm, counts)
            segs.append(tok_s)
        passes.append((segs,) + _moe_dispatch(segs, counts, w1, b1, w2, b2))
    yp = None
    for grp, (segs, dests, yb) in reversed(list(enumerate(passes))):
        gs = _gather_expert_rows(dests, yb)
        yp = _combine(gs[0], segs[0]['gates'], segs[0]['x1'], bp * lp, grp * bg * lp, yp)
        if len(segs) > 1:
            ys = _combine(gs[1], segs[1]['gates'], segs[1]['x1'])
    cat = lambda j: jnp.concatenate([o[j] for o in prompt_outs], axis=0)
    lat_p, krope_p = prompt_outs[-1][:2]
    return (yp.reshape(bp, lp, D_MODEL), ys.reshape(bs, ls, D_MODEL), lat_p, krope_p, cat(2), cat(3)) + sample_outs


def kernel(x_prompt, x_sample, cache_kv_latent, cache_k_rope, state_s5_re, state_s5_im, g_mix, w_in, lam_re,
           lam_im, log_dt, b_s5_re, b_s5_im, c_s5_re, c_s5_im, d_s5, w_glu, b_glu, g_q_lat, w_uq, g_kv_lat,
           w_ukv, g_qn_nope, g_qn_rope, g_kn_nope, g_kn_rope, g_s5_out, g_mla_out, w_out, g_ffn, w_router,
           b_router, w_mlp1, b_mlp1, w_mlp2, b_mlp2):
    depth = g_mix.shape[0]
    yp, ys = x_prompt, x_sample
    outs = [[] for _ in range(8)]
    for l in range(depth):
        mixer_w = (g_mix[l], w_in[l], lam_re[l], lam_im[l], log_dt[l], b_s5_re[l], b_s5_im[l], c_s5_re[l],
                   c_s5_im[l], d_s5[l], w_glu[l], b_glu[l], g_q_lat[l], w_uq[l], g_kv_lat[l], w_ukv[l],
                   g_qn_nope[l], g_qn_rope[l], g_kn_nope[l], g_kn_rope[l], g_s5_out[l], g_mla_out[l],
                   w_out[l], g_ffn[l], w_router[l], b_router[l])
        res = _layer(yp, ys, cache_kv_latent[l], cache_k_rope[l], state_s5_re[l], state_s5_im[l], mixer_w,
                     w_mlp1[l], b_mlp1[l], w_mlp2[l], b_mlp2[l])
        yp, ys = res[0], res[1]
        for o, r in zip(outs, res[2:]):
            o.append(r)
    return (yp, ys) + tuple(jnp.stack(o) for o in outs)
```

```python
import functools
import math

import jax
import jax.numpy as jnp
import numpy as np
from jax import lax
from jax.experimental import pallas as pl
from jax.experimental.pallas import tpu as pltpu
from jax.experimental.pallas import tpu_sc as plsc

F32 = jnp.float32
BF16 = jnp.bfloat16
U32 = jnp.uint32

D_MODEL = 1024
S5_WIDTH = 512
S5_GROUP = 16
S5_GROUPS = 32
S5_STATE = 64
MLA_HEADS = 8
QK_NOPE = 64
QK_ROPE = 32
QK_HEAD = QK_NOPE + QK_ROPE
V_HEAD = 64
MLA_WIDTH = MLA_HEADS * V_HEAD
Q_LORA = 384
KV_LORA = 256
ROPE_THETA = 10000.0
CHUNK = 64
N_EXPERTS = 32
TOP_K = 4
D_FF = D_MODEL
SWIGLU_LIMIT = 7.0
SWIGLU_ALPHA = 1.702
EPS = 1e-6

LANES = 128
HEAD_PAD = 128
D_IN_PAD = 1280
S5_CHUNKS = 4
S5_CHUNK_LANES = 1024
S5_STATE_LANES = S5_CHUNKS * S5_CHUNK_LANES
ROW_TILE = 1024
SUB_TILE = 512
S5_ROWS = 512
PROMPT_GROUPS = 2
MOE_ROWS = 512
MOE_COL_CHUNKS = 2
MOE_PAIR = 2
VMEM_LIMIT = 56 * 1024 * 1024
SC_CORES = 2
SC_SUBCORES = 16
SC_WORKERS = SC_CORES * SC_SUBCORES
SC_GATHER_ROWS = 64
D_PACK = D_MODEL // 2


def _cparams(sem, vmem=None):
    return pltpu.CompilerParams(dimension_semantics=sem, vmem_limit_bytes=vmem)


def _rms(x, g, n=None):
    n = x.shape[-1] if n is None else n
    ms = jnp.sum(x * x, axis=-1, keepdims=True) * (1.0 / n)
    return x * lax.rsqrt(ms + EPS) * g


def _sub_tiles(rows):
    sub = min(SUB_TILE, rows)
    return [slice(r, r + sub) for r in range(0, rows, sub)]


def _pack_rows(x):
    lo = lax.bitcast_convert_type(x[:, :D_PACK].astype(BF16).astype(F32), U32)
    hi = lax.bitcast_convert_type(x[:, D_PACK:].astype(BF16).astype(F32), U32)
    return (lo >> 16) | (hi & jnp.uint32(0xFFFF0000))


def _unpack_rows(w):
    lo = lax.bitcast_convert_type(w << 16, F32)
    hi = lax.bitcast_convert_type(w & jnp.uint32(0xFFFF0000), F32)
    return lo, hi


def _prep_kernel(lr_ref, li_ref, ldt_ref, br_ref, bi_ref, ar_ref, ai_ref, bbr_ref, bbi_ref):
    lr = lr_ref[...]
    li = li_ref[...]
    dt = jnp.exp(ldt_ref[...])
    mag = jnp.exp(lr * dt)
    ar = mag * jnp.cos(li * dt)
    ai = mag * jnp.sin(li * dt)
    ar_ref[...] = ar
    ai_ref[...] = ai
    den = lr * lr + li * li
    cr = ((ar - 1.0) * lr + ai * li) / den
    ci = (ai * lr - (ar - 1.0) * li) / den
    br = br_ref[...]
    bi = bi_ref[...]
    bbr_ref[...] = cr[:, None, :] * br - ci[:, None, :] * bi
    bbi_ref[...] = cr[:, None, :] * bi + ci[:, None, :] * br


def _s5_prep(lam_re, lam_im, log_dt, b_re, b_im):
    g, n = lam_re.shape
    p = b_re.shape[-1]
    out = pl.pallas_call(
        _prep_kernel,
        out_shape=(jax.ShapeDtypeStruct((g, n), F32), jax.ShapeDtypeStruct((g, n), F32),
                   jax.ShapeDtypeStruct((g, p, n), F32), jax.ShapeDtypeStruct((g, p, n), F32)),
        name="s5_prep",
    )(lam_re, lam_im, log_dt.reshape(g, 1), jnp.swapaxes(b_re, 1, 2), jnp.swapaxes(b_im, 1, 2))
    return out


def _state_layout(re, im):
    lead = re.shape[:-2]
    re = re.reshape(lead + (S5_CHUNKS, 512))
    im = im.reshape(lead + (S5_CHUNKS, 512))
    return jnp.stack([re, im], axis=-2).reshape(lead + (S5_STATE_LANES,))


def _state_unlayout(h):
    lead = h.shape[:-1]
    h = h.reshape(lead + (S5_CHUNKS, 2, 512))
    re = h[..., 0, :].reshape(lead + (S5_GROUPS, S5_STATE))
    im = h[..., 1, :].reshape(lead + (S5_GROUPS, S5_STATE))
    return re, im


def _s5_weights(bbr, bbi, c_re, c_im):
    eye8 = jnp.eye(8, dtype=F32)

    def blockdiag(m):
        a, b = m.shape[1], m.shape[2]
        return (eye8[:, None, :, None] * m[:, :, None, :]).reshape(8 * a, 8 * b)

    wb, wc = [], []
    for c in range(S5_CHUNKS):
        sl = slice(8 * c, 8 * c + 8)
        wb.append(jnp.concatenate([blockdiag(bbr[sl]), blockdiag(bbi[sl])], axis=1))
        cr_t = jnp.swapaxes(c_re[sl], 1, 2)
        ci_t = jnp.swapaxes(c_im[sl], 1, 2)
        wc.append(jnp.concatenate([blockdiag(cr_t), -blockdiag(ci_t)], axis=0))
    return jnp.stack(wb).astype(BF16), jnp.stack(wc).astype(BF16)


def _in_proj_kernel(x_ref, gmix_ref, w_ref, gq_ref, gkv_ref, ck_ref, sk_ref, *rest, grp):
    u_ref, qn_ref, lat_ref, krot_ref, latall_ref, krope_ref = rest[-6:]
    bsz = x_ref.shape[0]
    if latall_ref.shape[0] != bsz:
        latall_ref[...] = jnp.zeros(latall_ref.shape, F32)
        krope_ref[...] = jnp.zeros(krope_ref.shape, F32)
        latall_ref, krope_ref = latall_ref.at[grp * bsz:(grp + 1) * bsz], krope_ref.at[grp * bsz:(grp + 1) * bsz]
    steps = x_ref.shape[1]
    nb = max(1, min(bsz, SUB_TILE // steps))
    ck, sk = ck_ref[...][None], sk_ref[...][None]
    for b0 in range(0, bsz, nb):
        bs = slice(b0, b0 + nb)
        rows = nb * steps
        xn = _rms(x_ref[bs].reshape(rows, D_MODEL), gmix_ref[...]).astype(BF16)
        z = jnp.dot(xn, w_ref[...], preferred_element_type=F32)
        for bl in range(nb):
            for c in range(S5_CHUNKS):
                u_ref[c, pl.ds(b0 + bl, steps, stride=bsz), :] = z[bl * steps:(bl + 1) * steps,
                                                                   c * LANES:(c + 1) * LANES]
        qn = _rms(z[:, S5_WIDTH:S5_WIDTH + Q_LORA], gq_ref[...]).astype(BF16)
        qn_ref[bs] = qn.reshape(nb, steps, Q_LORA)
        c0 = S5_WIDTH + Q_LORA
        lat = _rms(z[:, c0:c0 + KV_LORA], gkv_ref[...]).reshape(nb, steps, KV_LORA)
        lat_ref[bs] = lat
        latall_ref[bs] = lat
        kp = z[:, c0 + KV_LORA:]
        lane = lax.broadcasted_iota(jnp.int32, kp.shape, 1)
        first_half = (lane % 64) < 16
        sw = jnp.where(first_half, pltpu.roll(kp, LANES - 16, axis=1), pltpu.roll(kp, 16, axis=1))
        kr = kp.reshape(nb, steps, LANES) * ck + sw.reshape(nb, steps, LANES) * sk
        for bl in range(nb):
            krope_ref[b0 + bl] = kr[bl].T[:QK_ROPE]
        krot_ref[bs] = jnp.where(lane.reshape(nb, steps, LANES) >= 64, kr, 0.0)


def _in_proj(x, grp, bsz, g_mix, w_in_p, g_q_lat, g_kv_lat, ck, sk, prev):
    n_all, length, _ = x.shape
    steps = min(ROW_TILE // bsz, length)
    full = lambda i: (0, 0)
    blk = lambda w: pl.BlockSpec((bsz, steps, w), lambda i: (0, i, 0))
    if prev:
        shared = lambda w: pl.BlockSpec((bsz, steps, w), lambda i: (grp, i, 0))
        shared_t = pl.BlockSpec((bsz, QK_ROPE, steps), lambda i: (grp, 0, i))
    else:
        shared = lambda w: pl.BlockSpec((n_all, steps, w), lambda i: (0, i, 0))
        shared_t = pl.BlockSpec((n_all, QK_ROPE, steps), lambda i: (0, 0, i))
    return pl.pallas_call(
        functools.partial(_in_proj_kernel, grp=grp),
        grid=(length // steps,),
        in_specs=[pl.BlockSpec((bsz, steps, D_MODEL), lambda i: (grp, i, 0)),
                  pl.BlockSpec((1, D_MODEL), full),
                  pl.BlockSpec((D_MODEL, D_IN_PAD), full),
                  pl.BlockSpec((1, Q_LORA), full),
                  pl.BlockSpec((1, KV_LORA), full),
                  pl.BlockSpec((steps, LANES), lambda i: (i, 0)),
                  pl.BlockSpec((steps, LANES), lambda i: (i, 0))]
                 + [pl.BlockSpec(memory_space=pl.ANY)] * len(prev),
        out_specs=[pl.BlockSpec((S5_CHUNKS, steps * bsz, LANES), lambda i: (0, i, 0)),
                   blk(Q_LORA), blk(KV_LORA), blk(LANES),
                   shared(KV_LORA), shared_t],
        out_shape=(jax.ShapeDtypeStruct((S5_CHUNKS, length * bsz, LANES), F32),
                   jax.ShapeDtypeStruct((bsz, length, Q_LORA), BF16),
                   jax.ShapeDtypeStruct((bsz, length, KV_LORA), F32),
                   jax.ShapeDtypeStruct((bsz, length, LANES), F32),
                   jax.ShapeDtypeStruct((n_all, length, KV_LORA), F32),
                   jax.ShapeDtypeStruct((n_all, QK_ROPE, length), F32)),
        input_output_aliases={7 + j: 4 + j for j in range(len(prev))},
        compiler_params=_cparams(("parallel",), VMEM_LIMIT),
        name="in_proj",
    )(x, g_mix, w_in_p, g_q_lat, g_kv_lat, ck, sk, *prev)


def _s5_kernel(u_ref, h0_ref, lam_ref, wb_ref, wc_ref, dskip_ref, wglu_ref, bglu_ref, gout_ref,
               y_ref, hlast_ref, bu_ref, h_ref, *, bsz, steps):
    c_id = pl.program_id(0)

    @pl.when(c_id == 0)
    def _():
        h_ref[...] = h0_ref[...]

    for c in range(S5_CHUNKS):
        bu_ref[:, c * S5_CHUNK_LANES:(c + 1) * S5_CHUNK_LANES] = jnp.dot(
            u_ref[c].astype(BF16), wb_ref[c], preferred_element_type=F32)

    for c in range(S5_CHUNKS):
        re = slice(c * S5_CHUNK_LANES, c * S5_CHUNK_LANES + 512)
        im = slice(c * S5_CHUNK_LANES + 512, (c + 1) * S5_CHUNK_LANES)
        lam_r = jnp.broadcast_to(lam_ref[:, re], (bsz, 512))
        lam_i = jnp.broadcast_to(lam_ref[:, im], (bsz, 512))

        hr, hi = h_ref[:, re], h_ref[:, im]
        for t in range(steps):
            rows = slice(t * bsz, (t + 1) * bsz)
            hr, hi = (lam_r * hr - lam_i * hi + bu_ref[rows, re],
                      lam_r * hi + lam_i * hr + bu_ref[rows, im])
            bu_ref[rows, re] = hr
            bu_ref[rows, im] = hi
        h_ref[:, re] = hr
        h_ref[:, im] = hi

    ys = []
    for c in range(S5_CHUNKS):
        hs = bu_ref[:, c * S5_CHUNK_LANES:(c + 1) * S5_CHUNK_LANES].astype(BF16)
        ys.append(jnp.dot(hs, wc_ref[c], preferred_element_type=F32))
    u = jnp.concatenate([u_ref[c] for c in range(S5_CHUNKS)], axis=1)
    y = jnp.concatenate(ys, axis=1) + dskip_ref[...] * u
    y = jax.nn.gelu(y)
    gate = jnp.dot(y.astype(BF16), wglu_ref[...], preferred_element_type=F32) + bglu_ref[...]
    y = _rms(y * jax.nn.sigmoid(gate), gout_ref[...])
    for c in range(S5_CHUNKS):
        y_ref[c] = y[:, c * LANES:(c + 1) * LANES]

    @pl.when(c_id == pl.num_programs(0) - 1)
    def _():
        hlast_ref[...] = h_ref[...]


def _s5(u_tm, h0, lam, wb, wc, d_skip, w_glu, b_glu, g_out, bsz, length):
    steps = min(S5_ROWS // bsz, length)
    rows = steps * bsz
    full2 = lambda c: (0, 0)
    full3 = lambda c: (0, 0, 0)
    return pl.pallas_call(
        functools.partial(_s5_kernel, bsz=bsz, steps=steps),
        grid=(length // steps,),
        in_specs=[pl.BlockSpec((S5_CHUNKS, rows, LANES), lambda c: (0, c, 0)),
                  pl.BlockSpec((bsz, S5_STATE_LANES), full2),
                  pl.BlockSpec((1, S5_STATE_LANES), full2),
                  pl.BlockSpec((S5_CHUNKS, LANES, S5_CHUNK_LANES), full3),
                  pl.BlockSpec((S5_CHUNKS, S5_CHUNK_LANES, LANES), full3),
                  pl.BlockSpec((1, S5_WIDTH), full2),
                  pl.BlockSpec((S5_WIDTH, S5_WIDTH), full2),
                  pl.BlockSpec((1, S5_WIDTH), full2),
                  pl.BlockSpec((1, S5_WIDTH), full2)],
        out_specs=[pl.BlockSpec((S5_CHUNKS, rows, LANES), lambda c: (0, c, 0)),
                   pl.BlockSpec((bsz, S5_STATE_LANES), full2)],
        out_shape=(jax.ShapeDtypeStruct((S5_CHUNKS, length * bsz, LANES), F32),
                   jax.ShapeDtypeStruct((bsz, S5_STATE_LANES), F32)),
        scratch_shapes=[pltpu.VMEM((rows, S5_STATE_LANES), F32),
                        pltpu.VMEM((bsz, S5_STATE_LANES), F32)],
        compiler_params=_cparams(("arbitrary",), VMEM_LIMIT),
        name="s5_mixer",
    )(u_tm, h0, lam, wb, wc, d_skip, w_glu, b_glu, g_out)


def _q_proj_kernel(qn_ref, w_ref, wsw_ref, ones_ref, cq_ref, sq_ref, g_ref, q_ref):
    g = g_ref[...] * (QK_HEAD ** -0.5)
    for rs in _sub_tiles(qn_ref.shape[0]):
        qn = qn_ref[rs, :]
        qf = jnp.dot(qn, w_ref[...], preferred_element_type=F32)
        qs = jnp.dot(qn, wsw_ref[...], preferred_element_type=F32)
        cq, sq = cq_ref[rs, :], sq_ref[rs, :]
        for p in range(MLA_HEADS // 2):
            cols = slice(2 * p * HEAD_PAD, 2 * (p + 1) * HEAD_PAD)
            xr = qf[:, cols] * cq + qs[:, cols] * sq
            sqr = xr * xr
            hi = sqr.astype(BF16)
            lo = (sqr - hi.astype(F32)).astype(BF16)
            ss = (jnp.dot(hi, ones_ref[...], preferred_element_type=F32)
                  + jnp.dot(lo, ones_ref[...], preferred_element_type=F32))
            q = (xr * lax.rsqrt(ss * (1.0 / QK_HEAD) + EPS) * g).astype(BF16)
            q_ref[0, 2 * p, rs, :] = q[:, :HEAD_PAD]
            q_ref[0, 2 * p + 1, rs, :] = q[:, HEAD_PAD:]


def _q_proj(qn, w_uq_p, w_uq_sw, cq, sq, gq, bsz, length):
    tm = min(ROW_TILE, length)
    n_l = length // tm
    full = lambda b, i: (0, 0)
    pair = 2 * HEAD_PAD
    two = lambda a: jnp.concatenate([a, a], axis=1)
    ones2 = (jnp.arange(pair)[:, None] // HEAD_PAD == jnp.arange(pair)[None, :] // HEAD_PAD).astype(BF16)
    return pl.pallas_call(
        _q_proj_kernel,
        grid=(bsz, n_l),
        in_specs=[pl.BlockSpec((tm, Q_LORA), lambda b, i: (b * n_l + i, 0)),
                  pl.BlockSpec((Q_LORA, MLA_HEADS * HEAD_PAD), full),
                  pl.BlockSpec((Q_LORA, MLA_HEADS * HEAD_PAD), full),
                  pl.BlockSpec((pair, pair), full),
                  pl.BlockSpec((tm, pair), lambda b, i: (i, 0)),
                  pl.BlockSpec((tm, pair), lambda b, i: (i, 0)),
                  pl.BlockSpec((1, pair), full)],
        out_specs=pl.BlockSpec((1, MLA_HEADS, tm, HEAD_PAD), lambda b, i: (b, 0, i, 0)),
        out_shape=jax.ShapeDtypeStruct((bsz, MLA_HEADS, length, HEAD_PAD), BF16),
        compiler_params=_cparams(("parallel", "parallel"), VMEM_LIMIT),
        name="q_proj",
    )(qn, w_uq_p, w_uq_sw, ones2, two(cq), two(sq), two(gq))


def _kv_proj_kernel(lat_ref, krot_ref, wk_ref, wv_ref, g_ref, k_ref, v_ref):
    g = g_ref[...]
    for rs in _sub_tiles(lat_ref.shape[0]):
        lat = lat_ref[rs, :].astype(BF16)
        kf = jnp.dot(lat, wk_ref[...], preferred_element_type=F32)
        vf = jnp.dot(lat, wv_ref[...], preferred_element_type=F32)
        krot = krot_ref[rs, :]
        for h in range(MLA_HEADS):
            k = kf[:, h * HEAD_PAD:(h + 1) * HEAD_PAD] + krot
            k_ref[0, h, rs, :] = _rms(k, g, QK_HEAD).astype(BF16)
        for p in range(MLA_HEADS // 2):
            v_ref[0, p, rs, :] = vf[:, p * LANES:(p + 1) * LANES].astype(BF16)


def _kv_proj(lat, krot, w_k, w_v, gk, bsz, length):
    tm = min(ROW_TILE, length)
    n_l = length // tm
    full = lambda b, i: (0, 0)
    row = lambda b, i: (b * n_l + i, 0)
    return pl.pallas_call(
        _kv_proj_kernel,
        grid=(bsz, n_l),
        in_specs=[pl.BlockSpec((tm, KV_LORA), row),
                  pl.BlockSpec((tm, HEAD_PAD), row),
                  pl.BlockSpec((KV_LORA, MLA_HEADS * HEAD_PAD), full),
                  pl.BlockSpec((KV_LORA, MLA_WIDTH), full),
                  pl.BlockSpec((1, HEAD_PAD), full)],
        out_specs=[pl.BlockSpec((1, MLA_HEADS, tm, HEAD_PAD), lambda b, i: (b, 0, i, 0)),
                   pl.BlockSpec((1, MLA_HEADS // 2, tm, LANES), lambda b, i: (b, 0, i, 0))],
        out_shape=(jax.ShapeDtypeStruct((bsz, MLA_HEADS, length, HEAD_PAD), BF16),
                   jax.ShapeDtypeStruct((bsz, MLA_HEADS // 2, length, LANES), BF16)),
        compiler_params=_cparams(("parallel", "parallel"), VMEM_LIMIT),
        name="kv_proj",
    )(lat, krot, w_k, w_v, gk)


def _scores(q, k):
    return lax.dot_general(q, k, (((1,), (1,)), ((), ())), preferred_element_type=F32)


def _attn_prompt_kernel(q_ref, k_ref, v_ref, o_ref, *, length, tq):
    n_q = length // tq
    row = lax.broadcasted_iota(jnp.int32, (tq, tq), 0)
    col = lax.broadcasted_iota(jnp.int32, (tq, tq), 1)
    visible = (col // CHUNK) <= (row // CHUNK)
    lane = lax.broadcasted_iota(jnp.int32, (tq, LANES), 1)
    for qi in range(n_q):
        q0 = qi * tq
        outs = []
        for hh in range(2):
            q = q_ref[0, hh, q0:q0 + tq, :]
            sd = jnp.where(visible, _scores(q, k_ref[0, hh, q0:q0 + tq, :]), -jnp.inf)
            m = jnp.max(sd, axis=-1, keepdims=True)
            if qi:
                so = _scores(q, k_ref[0, hh, 0:q0, :])
                m = jnp.maximum(m, jnp.max(so, axis=-1, keepdims=True))
            pd = jnp.exp(sd - m)
            l = jnp.sum(pd, axis=-1, keepdims=True)
            acc = jnp.dot(pd.astype(BF16), v_ref[0, 0, q0:q0 + tq, :], preferred_element_type=F32)
            if qi:
                po = jnp.exp(so - m)
                l = l + jnp.sum(po, axis=-1, keepdims=True)
                acc = acc + jnp.dot(po.astype(BF16), v_ref[0, 0, 0:q0, :], preferred_element_type=F32)
            outs.append(acc / l)
        o_ref[0, q0:q0 + tq, :] = jnp.where(lane < V_HEAD, outs[0], outs[1])


def _attn_prompt(q, k, v, bsz, length):
    tq = min(512, length)
    pairs = MLA_HEADS // 2
    return pl.pallas_call(
        functools.partial(_attn_prompt_kernel, length=length, tq=tq),
        grid=(bsz, pairs),
        in_specs=[pl.BlockSpec((1, 2, length, HEAD_PAD), lambda b, p: (b, p, 0, 0)),
                  pl.BlockSpec((1, 2, length, HEAD_PAD), lambda b, p: (b, p, 0, 0)),
                  pl.BlockSpec((1, 1, length, LANES), lambda b, p: (b, p, 0, 0))],
        out_specs=pl.BlockSpec((1, length, LANES), lambda b, p: (b, 0, p)),
        out_shape=jax.ShapeDtypeStruct((bsz, length, MLA_WIDTH), F32),
        compiler_params=_cparams(("parallel", "parallel"), VMEM_LIMIT),
        name="attn_prompt",
    )(q, k, v)


def _attn_sample_kernel(q_ref, kc_ref, vc_ref, kn_ref, vn_ref, o_ref):
    lq = q_ref.shape[2]
    lane = lax.broadcasted_iota(jnp.int32, (lq, LANES), 1)
    outs = []
    for hh in range(2):
        q = q_ref[0, hh]
        sc = _scores(q, kc_ref[0, hh])
        sn = _scores(q, kn_ref[0, hh])
        m = jnp.maximum(jnp.max(sc, axis=-1, keepdims=True), jnp.max(sn, axis=-1, keepdims=True))
        pc = jnp.exp(sc - m)
        pn = jnp.exp(sn - m)
        l = jnp.sum(pc, axis=-1, keepdims=True) + jnp.sum(pn, axis=-1, keepdims=True)
        acc = (jnp.dot(pc.astype(BF16), vc_ref[0, 0], preferred_element_type=F32)
               + jnp.dot(pn.astype(BF16), vn_ref[0, 0], preferred_element_type=F32))
        outs.append(acc / l)
    o_ref[0] = jnp.where(lane < V_HEAD, outs[0], outs[1])


def _attn_sample(q, kc, vc, kn, vn, bsz, lq, past):
    pairs = MLA_HEADS // 2
    hp = lambda b, p: (b, p, 0, 0)
    return pl.pallas_call(
        _attn_sample_kernel,
        grid=(bsz, pairs),
        in_specs=[pl.BlockSpec((1, 2, lq, HEAD_PAD), hp),
                  pl.BlockSpec((1, 2, past, HEAD_PAD), hp),
                  pl.BlockSpec((1, 1, past, LANES), hp),
                  pl.BlockSpec((1, 2, lq, HEAD_PAD), hp),
                  pl.BlockSpec((1, 1, lq, LANES), hp)],
        out_specs=pl.BlockSpec((1, lq, LANES), lambda b, p: (b, 0, p)),
        out_shape=jax.ShapeDtypeStruct((bsz, lq, MLA_WIDTH), F32),
        compiler_params=_cparams(("parallel", "parallel"), VMEM_LIMIT),
        name="attn_sample",
    )(q, kc, vc, kn, vn)


def _out_proj_kernel(x_ref, s5_ref, at_ref, gmla_ref, wout_ref, gffn_ref, wr_ref, br_ref, tri_ref, cin_ref,
                     x1_ref, hn_ref, route_ref, gate_ref, cout_ref, run_ref):
    @pl.when(pl.program_id(0) == 0)
    def _():
        run_ref[...] = cin_ref[...]

    bsz, steps, _ = x_ref.shape
    nb = max(1, min(bsz, SUB_TILE // steps))
    for b0 in range(0, bsz, nb):
        bs = slice(b0, b0 + nb)
        rows = nb * steps
        s5 = jnp.concatenate(
            [jnp.concatenate([s5_ref[c, pl.ds(b0 + bl, steps, stride=bsz), :] for c in range(S5_CHUNKS)], axis=1)
             for bl in range(nb)], axis=0)
        an = _rms(at_ref[bs].reshape(rows, MLA_WIDTH), gmla_ref[...])
        merged = jnp.concatenate([s5.astype(BF16), an.astype(BF16)], axis=1)
        x1 = x_ref[bs].reshape(rows, D_MODEL) + jnp.dot(merged, wout_ref[...], preferred_element_type=F32)
        x1_ref[bs] = x1.reshape(nb, steps, D_MODEL)
        hn = _rms(x1, gffn_ref[...])
        hn_ref[bs] = _pack_rows(hn).reshape(nb, steps, D_PACK)
        hn_hi = hn.astype(BF16)
        hn_lo = (hn - hn_hi.astype(F32)).astype(BF16)
        parts = (jnp.dot(hn_hi, wr_ref[...], preferred_element_type=F32)
                 + jnp.dot(hn_lo, wr_ref[...], preferred_element_type=F32))
        logits = parts[:, :LANES] + parts[:, LANES:] + br_ref[...]
        lane = lax.broadcasted_iota(jnp.int32, logits.shape, 1)
        lane_f = lane.astype(F32)
        work = jnp.where(lane < N_EXPERTS, logits, -jnp.inf)
        vals, idxs = [], []
        for _ in range(TOP_K):
            m = jnp.max(work, axis=-1, keepdims=True)
            i = jnp.min(jnp.where(work == m, lane_f, float(LANES)), axis=-1, keepdims=True)
            vals.append(m)
            idxs.append(i)
            work = jnp.where(lane_f == i, -jnp.inf, work)
        es = [jnp.exp(v - vals[0]) for v in vals]
        den = es[0] + es[1] + es[2] + es[3]
        onehots = [(lane_f == idxs[k]).astype(F32) for k in range(TOP_K)]
        e_all = onehots[0] + onehots[1] + onehots[2] + onehots[3]
        before = jnp.dot(tri_ref[...], e_all.astype(BF16), preferred_element_type=F32) + run_ref[...]
        run_ref[...] = run_ref[...] + jnp.sum(e_all, axis=0, keepdims=True)
        route = jnp.zeros(logits.shape, F32)
        gate_out = jnp.zeros(logits.shape, F32)
        for k in range(TOP_K):
            rank = jnp.sum(onehots[k] * before, axis=-1, keepdims=True)
            route = jnp.where(lane == k, idxs[k], route)
            route = jnp.where(lane == 8 + k, rank, route)
            gate_out = jnp.where(lane == k, es[k] / den, gate_out)
        route_ref[bs] = route.astype(jnp.int32).reshape(nb, steps, LANES)
        gate_ref[bs] = gate_out.reshape(nb, steps, LANES)
    cout_ref[...] = run_ref[...]


def _out_proj(x, grp, s5n_tm, attn, g_mla, w_out, g_ffn, w_r, b_r_p, counts_in):
    bsz, length, _ = attn.shape
    steps = min(ROW_TILE // bsz, length)
    full = lambda i: (0, 0)
    blk = lambda w: pl.BlockSpec((bsz, steps, w), lambda i: (0, i, 0))
    sub = max(1, min(bsz, SUB_TILE // steps)) * steps
    tri = (jnp.arange(sub)[:, None] > jnp.arange(sub)[None, :]).astype(BF16)
    return pl.pallas_call(
        _out_proj_kernel,
        grid=(length // steps,),
        in_specs=[pl.BlockSpec((bsz, steps, D_MODEL), lambda i: (grp, i, 0)),
                  pl.BlockSpec((S5_CHUNKS, steps * bsz, LANES), lambda i: (0, i, 0)),
                  blk(MLA_WIDTH),
                  pl.BlockSpec((1, MLA_WIDTH), full),
                  pl.BlockSpec((D_MODEL, D_MODEL), full),
                  pl.BlockSpec((1, D_MODEL), full),
                  pl.BlockSpec((D_MODEL, 2 * LANES), full),
                  pl.BlockSpec((1, LANES), full),
                  pl.BlockSpec((sub, sub), full),
                  pl.BlockSpec((1, LANES), full)],
        out_specs=[blk(D_MODEL), blk(D_PACK), blk(LANES), blk(LANES),
                   pl.BlockSpec((1, LANES), full)],
        out_shape=(jax.ShapeDtypeStruct((bsz, length, D_MODEL), F32),
                   jax.ShapeDtypeStruct((bsz, length, D_PACK), U32),
                   jax.ShapeDtypeStruct((bsz, length, LANES), jnp.int32),
                   jax.ShapeDtypeStruct((bsz, length, LANES), F32),
                   jax.ShapeDtypeStruct((1, LANES), F32)),
        scratch_shapes=[pltpu.VMEM((1, LANES), F32)],
        compiler_params=_cparams(("arbitrary",), VMEM_LIMIT),
        name="out_proj",
    )(x, s5n_tm, attn, g_mla, w_out, g_ffn, w_r, b_r_p, tri, counts_in)


def _sc_mesh():
    return plsc.VectorSubcoreMesh(core_axis_name="c", subcore_axis_name="s",
                                  num_cores=SC_CORES, num_subcores=SC_SUBCORES)


def _sc_for_chunks(n_chunks, fn):
    wid = lax.axis_index("s") * SC_CORES + lax.axis_index("c")
    full, rem = divmod(n_chunks, SC_WORKERS)
    if full:
        @pl.loop(0, full)
        def _(j):
            fn(j * SC_WORKERS + wid)
    if rem:
        @pl.when(wid < rem)
        def _():
            fn(full * SC_WORKERS + wid)


def _sc_dispatch_body(*refs, n_seg):
    d_hbms, h_hbms = refs[:n_seg], refs[n_seg:2 * n_seg]
    xs_hbm, idx_v, rows_v, sems = refs[2 * n_seg:]
    n = SC_GATHER_ROWS
    wid = lax.axis_index("s") * SC_CORES + lax.axis_index("c")

    def scatter(d_hbm, h_hbm, c, slot):
        pltpu.sync_copy(d_hbm.at[c], idx_v.at[slot])
        pltpu.sync_copy(h_hbm.at[pl.ds(pl.multiple_of(c * n, n), n)], rows_v.at[slot])
        return [pltpu.async_copy(rows_v.at[slot], xs_hbm.at[idx_v.at[slot, k]], sems.at[slot])
                for k in range(TOP_K)]

    def run(d_hbm, h_hbm):
        full, rem = divmod(d_hbm.shape[0], SC_WORKERS)
        if full // 2:
            @pl.loop(0, full // 2)
            def _(j):
                first = scatter(d_hbm, h_hbm, (2 * j) * SC_WORKERS + wid, 0)
                second = scatter(d_hbm, h_hbm, (2 * j + 1) * SC_WORKERS + wid, 1)
                for cp in first + second:
                    cp.wait()
        tail = [(full - 1, None)] if full % 2 else []
        if rem:
            tail.append((full, rem))
        for row, limit in tail:
            def one(row=row):
                for cp in scatter(d_hbm, h_hbm, row * SC_WORKERS + wid, 0):
                    cp.wait()
            if limit is None:
                one()
            else:
                pl.when(wid < limit)(one)

    for d_hbm, h_hbm in zip(d_hbms, h_hbms):
        run(d_hbm, h_hbm)


def _dispatch(dests, hns, n_blocks):
    return pl.kernel(
        functools.partial(_sc_dispatch_body, n_seg=len(dests)),
        out_type=jax.ShapeDtypeStruct((n_blocks * MOE_ROWS, D_PACK), U32),
        mesh=_sc_mesh(),
        scratch_types=[pltpu.VMEM((2, 8, SC_GATHER_ROWS), jnp.int32),
                       pltpu.VMEM((2, SC_GATHER_ROWS, D_PACK), U32),
                       pltpu.SemaphoreType.DMA((2,))],
        name="moe_dispatch_sc",
    )(*dests, *hns)


def _expert_kernel(be_ref, nb_ref, nv_ref, first_ref, slot_ref, nxt_ref,
                   x_ref, w1_hbm, b1_ref, w2_hbm, b2_ref, y_ref, w1f_ref, w2f_ref, w1b_ref, w2b_ref, sem):
    step = pl.program_id(0)

    def weight_copies(e):
        return (pltpu.make_async_copy(w1_hbm.at[e], w1f_ref, sem.at[0]),
                pltpu.make_async_copy(w2_hbm.at[e], w2f_ref, sem.at[1]))

    @pl.when(jnp.logical_and(step == 0, nb_ref[0] > 0))
    def _():
        for cp in weight_copies(be_ref[0]):
            cp.start()

    for sb in range(MOE_PAIR):
        b = step * MOE_PAIR + sb

        @pl.when(jnp.logical_and(b < nb_ref[0], first_ref[b] == 1))
        def _(b=b):
            s = slot_ref[b]
            for cp in weight_copies(be_ref[b]):
                cp.wait()
            w1b_ref[s] = w1f_ref[...].astype(BF16)
            w2b_ref[s] = w2f_ref[...].astype(BF16)

            @pl.when(nxt_ref[b] >= 0)
            def _():
                for cp in weight_copies(nxt_ref[b]):
                    cp.start()

    @pl.when(step * MOE_PAIR < nb_ref[0])
    def _():
        cw = D_FF // MOE_COL_CHUNKS
        for sb in range(MOE_PAIR):
            b = step * MOE_PAIR + sb
            e, s = be_ref[b], slot_ref[b]
            rs = slice(sb * MOE_ROWS, (sb + 1) * MOE_ROWS)
            live = lax.broadcasted_iota(jnp.int32, (MOE_ROWS, D_PACK), 0) < nv_ref[b]
            lo, hi = _unpack_rows(jnp.where(live, x_ref[rs, :], jnp.uint32(0)))
            lo, hi = lo.astype(BF16), hi.astype(BF16)
            b1 = b1_ref[e]

            def up(c0, lo=lo, hi=hi, s=s, b1=b1):
                return (jnp.dot(lo, w1b_ref[s, :D_PACK, c0:c0 + cw], preferred_element_type=F32)
                        + jnp.dot(hi, w1b_ref[s, D_PACK:, c0:c0 + cw], preferred_element_type=F32)
                        + b1[:, c0:c0 + cw])

            y = b2_ref[e]
            for j in range(MOE_COL_CHUNKS):
                gate = jnp.minimum(up(j * cw), SWIGLU_LIMIT)
                lin = jnp.clip(up(D_FF + j * cw), -SWIGLU_LIMIT, SWIGLU_LIMIT)
                act = gate * jax.nn.sigmoid(SWIGLU_ALPHA * gate) * (lin + 1.0)
                y = y + jnp.dot(act.astype(BF16), w2b_ref[s, j * cw:(j + 1) * cw, :],
                                preferred_element_type=F32)
            y_ref[rs, :] = _pack_rows(y)

    @pl.when(step * MOE_PAIR >= nb_ref[0])
    def _():
        y_ref[...] = jnp.zeros(y_ref.shape, y_ref.dtype)


def _experts(tables, xs, w1, b1, w2, b2, n_blocks):
    rows = MOE_PAIR * MOE_ROWS
    last = lambda p, be, nb, *_: (jnp.maximum(jnp.minimum(p, (nb[0] - 1) // MOE_PAIR), 0), 0)
    whole = lambda p, *_: (0, 0, 0)
    return pl.pallas_call(
        _expert_kernel,
        grid_spec=pltpu.PrefetchScalarGridSpec(
            num_scalar_prefetch=6,
            grid=(n_blocks // MOE_PAIR,),
            in_specs=[pl.BlockSpec((rows, D_PACK), last),
                      pl.BlockSpec(memory_space=pl.ANY),
                      pl.BlockSpec((N_EXPERTS, 1, 2 * D_FF), whole),
                      pl.BlockSpec(memory_space=pl.ANY),
                      pl.BlockSpec((N_EXPERTS, 1, D_MODEL), whole)],
            out_specs=pl.BlockSpec((rows, D_PACK), lambda p, *_: (p, 0)),
            scratch_shapes=[pltpu.VMEM((D_MODEL, 2 * D_FF), F32), pltpu.VMEM((D_FF, D_MODEL), F32),
                            pltpu.VMEM((MOE_PAIR, D_MODEL, 2 * D_FF), BF16),
                            pltpu.VMEM((MOE_PAIR, D_FF, D_MODEL), BF16),
                            pltpu.SemaphoreType.DMA((2,))]),
        out_shape=jax.ShapeDtypeStruct((n_blocks * MOE_ROWS, D_PACK), U32),
        compiler_params=_cparams(("arbitrary",), VMEM_LIMIT),
        name="moe_experts",
    )(*tables, xs, w1, b1.reshape(N_EXPERTS, 1, 2 * D_FF), w2, b2.reshape(N_EXPERTS, 1, D_MODEL))


def _sc_gather_body(*refs, n_seg):
    d_hbms, yb_hbm = refs[:n_seg], refs[n_seg]
    g_hbms = refs[n_seg + 1:2 * n_seg + 1]
    idx_v, rows_v, sems = refs[2 * n_seg + 1:]
    n = SC_GATHER_ROWS

    def chunk(d_hbm, g_hbm, c):
        pltpu.sync_copy(d_hbm.at[c], idx_v)
        rows = pl.ds(pl.multiple_of(c * n, n), n)
        gather = lambda k: pltpu.async_copy(yb_hbm.at[idx_v.at[k]], rows_v.at[k % 2], sems.at[k % 2])
        cp = gather(0)
        for k in range(TOP_K):
            cp.wait()
            if k + 1 < TOP_K:
                cp = gather(k + 1)
            pltpu.sync_copy(rows_v.at[k % 2], g_hbm.at[k, rows])

    for d_hbm, g_hbm in zip(d_hbms, g_hbms):
        _sc_for_chunks(d_hbm.shape[0], functools.partial(chunk, d_hbm, g_hbm))


def _gather_expert_rows(dests, yb):
    return pl.kernel(
        functools.partial(_sc_gather_body, n_seg=len(dests)),
        out_type=tuple(jax.ShapeDtypeStruct((TOP_K, d.shape[0] * SC_GATHER_ROWS, D_PACK), U32) for d in dests),
        mesh=_sc_mesh(),
        scratch_types=[pltpu.VMEM((8, SC_GATHER_ROWS), jnp.int32),
                       pltpu.VMEM((2, SC_GATHER_ROWS, D_PACK), U32),
                       pltpu.SemaphoreType.DMA((2,))],
        name="moe_gather_sc",
    )(*dests, yb)


def _combine_kernel(g_ref, gate_ref, x1_ref, *rest, first_blk, n_blk):
    o_ref = rest[-1]
    i = pl.program_id(0)

    @pl.when(jnp.logical_and(i >= first_blk, i < first_blk + n_blk))
    def _():
        gate = gate_ref[...]
        x1 = x1_ref[...]
        acc_lo, acc_hi = x1[:, :D_PACK], x1[:, D_PACK:]
        for k in range(TOP_K):
            lo, hi = _unpack_rows(g_ref[k])
            acc_lo = acc_lo + gate[:, k:k + 1] * lo
            acc_hi = acc_hi + gate[:, k:k + 1] * hi
        o_ref[:, :D_PACK] = acc_lo
        o_ref[:, D_PACK:] = acc_hi

    @pl.when(jnp.logical_or(i < first_blk, i >= first_blk + n_blk))
    def _():
        o_ref[...] = jnp.zeros(o_ref.shape, o_ref.dtype)


def _combine(g, gates, x1, out_rows=None, row0=0, prev=None):
    n_tok = x1.shape[0]
    out_rows = n_tok if out_rows is None else out_rows
    tm = min(512, n_tok)
    n_blk = n_tok // tm
    first_blk, grid = (0, n_blk) if prev is not None else (row0 // tm, out_rows // tm)
    own = lambda i: jnp.clip(i - first_blk, 0, n_blk - 1)
    in_specs = [pl.BlockSpec((TOP_K, tm, D_PACK), lambda i: (0, own(i), 0)),
                pl.BlockSpec((tm, LANES), lambda i: (own(i), 0)),
                pl.BlockSpec((tm, D_MODEL), lambda i: (own(i), 0))]
    args = [g, gates, x1]
    out_blk0, aliases = 0, {}
    if prev is not None:
        in_specs.append(pl.BlockSpec(memory_space=pl.ANY))
        args.append(prev)
        out_blk0, aliases = row0 // tm, {3: 0}
    return pl.pallas_call(
        functools.partial(_combine_kernel, first_blk=first_blk, n_blk=n_blk),
        grid=(grid,),
        in_specs=in_specs,
        out_specs=pl.BlockSpec((tm, D_MODEL), lambda i: (out_blk0 + i, 0)),
        out_shape=jax.ShapeDtypeStruct((out_rows, D_MODEL), F32),
        input_output_aliases=aliases,
        compiler_params=_cparams(("parallel",), VMEM_LIMIT),
        name="moe_combine",
    )(*args)


def _route_tables(counts_f, n_assign):
    counts = counts_f[0, :N_EXPERTS].astype(jnp.int32)
    nblk_e = (counts + MOE_ROWS - 1) // MOE_ROWS
    bend = jnp.cumsum(nblk_e)
    bstart = bend - nblk_e
    pad_start = bstart * MOE_ROWS
    n_blocks = -(-(n_assign + N_EXPERTS * (MOE_ROWS - 1)) // (MOE_ROWS * MOE_PAIR)) * MOE_PAIR
    e_ids = jnp.arange(N_EXPERTS, dtype=jnp.int32)
    nonempty = counts > 0
    slot_e = (jnp.cumsum(nonempty.astype(jnp.int32)) - 1) % MOE_PAIR
    later = jnp.where(nonempty[None, :] & (e_ids[None, :] > e_ids[:, None]), e_ids[None, :], N_EXPERTS)
    next_e = jnp.min(later, axis=1)
    next_e = jnp.where(next_e == N_EXPERTS, -1, next_e)
    bidx = jnp.arange(n_blocks, dtype=jnp.int32)
    mine = ((bidx[:, None] >= bstart[None, :]) & (bidx[:, None] < bend[None, :])).astype(jnp.int32)
    vals = jnp.stack([e_ids, slot_e, next_e, pad_start + counts, bstart, jnp.ones_like(e_ids)], axis=0)
    per_block = jnp.sum(mine[:, None, :] * vals[None, :, :], axis=-1)
    block_e, slot, nxt, row_end, first_blk, used = (per_block[:, j] for j in range(6))
    n_valid = jnp.clip(row_end - bidx * MOE_ROWS, 0, MOE_ROWS)
    first = ((bidx == first_blk) & (used > 0)).astype(jnp.int32)
    tables = (block_e, bend[-1:].astype(jnp.int32), n_valid, first, slot, nxt)
    return pad_start, tables, n_blocks


def _dest_kernel(ps_ref, route_ref, d_ref):
    rt = route_ref[...].T
    idx, rank = rt[0:8, :], rt[8:16, :]
    base = jnp.zeros_like(idx)
    for e in range(N_EXPERTS):
        base = jnp.where(idx == e, ps_ref[e], base)
    dest = base + rank
    for j in range(dest.shape[1] // SC_GATHER_ROWS):
        d_ref[j] = dest[:, j * SC_GATHER_ROWS:(j + 1) * SC_GATHER_ROWS]


def _dest(pad_start, route):
    n_tok = route.shape[0]
    tm = min(2048, n_tok)
    n = SC_GATHER_ROWS
    return pl.pallas_call(
        _dest_kernel,
        grid_spec=pltpu.PrefetchScalarGridSpec(
            num_scalar_prefetch=1,
            grid=(n_tok // tm,),
            in_specs=[pl.BlockSpec((tm, LANES), lambda i, ps: (i, 0))],
            out_specs=pl.BlockSpec((tm // n, 8, n), lambda i, ps: (i, 0, 0))),
        out_shape=jax.ShapeDtypeStruct((n_tok // n, 8, n), jnp.int32),
        compiler_params=_cparams(("parallel",), VMEM_LIMIT),
        name="moe_dest",
    )(pad_start, route)


def _rope_tables(offset, length):
    f32 = np.float32
    pos = f32(offset) + np.arange(length, dtype=f32)
    inv = np.power(f32(ROPE_THETA), -np.arange(QK_ROPE // 2, dtype=f32) * f32(2.0 / QK_ROPE))
    ang = pos[:, None] * inv[None, :]
    cos, sin = np.cos(ang), np.sin(ang)
    z32 = np.zeros((length, 32), f32)
    rot_c = np.concatenate([cos, cos], axis=1)
    rot_s = np.concatenate([-sin, sin], axis=1)
    ck = np.concatenate([rot_c, z32, rot_c, z32], axis=1)
    sk = np.concatenate([rot_s, z32, rot_s, z32], axis=1)
    cq = np.concatenate([np.ones((length, QK_NOPE), f32), rot_c, z32], axis=1)
    sq = np.concatenate([np.zeros((length, QK_NOPE), f32), rot_s, z32], axis=1)
    return tuple(jnp.asarray(t, F32) for t in (ck, sk, cq, sq))


def _head_gain(g_nope, g_rope):
    return jnp.concatenate([g_nope, g_rope, g_rope, jnp.zeros((HEAD_PAD - QK_HEAD,), F32)]).reshape(1, HEAD_PAD)


def _mixer_stage(x, grp, bsz, past, prm, counts_in, prev=()):
    length = x.shape[1]
    offset = 0 if past is None else past[0].shape[1]
    ck, sk, cq, sq = _rope_tables(offset, length)
    u_tm, qn, lat, krot, lat_all, k_rope = _in_proj(x, grp, bsz, prm['g_mix'], prm['w_in_p'], prm['g_q_lat'],
                                                    prm['g_kv_lat'], ck, sk, prev)
    if past is None:
        h0 = jnp.zeros((bsz, S5_STATE_LANES), F32)
    else:
        h0 = _state_layout(past[2], past[3])
    s5n_tm, h_last = _s5(u_tm, h0, prm['lam'], prm['wb'], prm['wc'],
                         prm['d_skip'], prm['w_glu'], prm['b_glu'], prm['g_s5_out'], bsz, length)
    h_re, h_im = _state_unlayout(h_last)
    t = bsz * length
    q = _q_proj(qn.reshape(t, Q_LORA), prm['w_uq_p'], prm['w_uq_sw'], cq, sq, prm['gq'], bsz, length)
    k, v = _kv_proj(lat.reshape(t, KV_LORA), krot.reshape(t, LANES), prm['w_k'], prm['w_v'], prm['gk'],
                    bsz, length)
    if past is None:
        attn = _attn_prompt(q, k, v, bsz, length)
    else:
        n_past = past[0].shape[1]
        c_lat = past[0].reshape(bsz * n_past, KV_LORA)
        c_rot = jnp.pad(past[1].reshape(bsz * n_past, QK_ROPE), ((0, 0), (QK_NOPE, HEAD_PAD - QK_HEAD)))
        kc, vc = _kv_proj(c_lat, c_rot, prm['w_k'], prm['w_v'], prm['gk'], bsz, n_past)
        attn = _attn_sample(q, kc, vc, k, v, bsz, length, n_past)
    x1, hn, route, gates, counts = _out_proj(x, grp, s5n_tm, attn, prm['g_mla_out'], prm['w_out'], prm['g_ffn'],
                                             prm['w_r'], prm['b_r_p'], counts_in)
    tok = dict(x1=x1.reshape(t, D_MODEL), hn=hn.reshape(t, D_PACK), route=route.reshape(t, LANES),
               gates=gates.reshape(t, LANES))
    return tok, counts, (lat_all, k_rope, h_re, h_im)


def _prepare(g_mix, w_in, lam_re, lam_im, log_dt, b_s5_re, b_s5_im, c_s5_re, c_s5_im, d_s5, w_glu, b_glu,
             g_q_lat, w_uq, g_kv_lat, w_ukv, g_qn_nope, g_qn_rope, g_kn_nope, g_kn_rope, g_s5_out,
             g_mla_out, w_out, g_ffn, w_router, b_router):
    c0 = S5_WIDTH + Q_LORA + KV_LORA
    w_pe = w_in[:, c0:]
    z = jnp.zeros((D_MODEL, 32), F32)
    w_in_p = jnp.concatenate([w_in[:, :c0], w_pe, z, w_pe, z], axis=1).astype(BF16)
    ar, ai, bbr, bbi = _s5_prep(lam_re, lam_im, log_dt, b_s5_re, b_s5_im)
    wb, wc = _s5_weights(bbr, bbi, c_s5_re, c_s5_im)
    w_uq_p = jnp.pad(w_uq.reshape(Q_LORA, MLA_HEADS, QK_HEAD), ((0, 0), (0, 0), (0, HEAD_PAD - QK_HEAD)))
    r0, r1, r2 = QK_NOPE, QK_NOPE + QK_ROPE // 2, QK_HEAD
    w_uq_sw = jnp.zeros_like(w_uq_p).at[:, :, r0:r1].set(w_uq_p[:, :, r1:r2]).at[:, :, r1:r2].set(w_uq_p[:, :, r0:r1])
    w_kv = w_ukv.reshape(KV_LORA, MLA_HEADS, QK_NOPE + V_HEAD)
    w_k = jnp.pad(w_kv[:, :, :QK_NOPE], ((0, 0), (0, 0), (0, HEAD_PAD - QK_NOPE)))
    w_v = w_kv[:, :, QK_NOPE:]
    w_r_p = jnp.pad(w_router, ((0, 0), (0, LANES - N_EXPERTS)))
    w_r_hi = w_r_p.astype(BF16)
    return dict(
        g_mix=g_mix.reshape(1, -1), w_in_p=w_in_p,
        g_q_lat=g_q_lat.reshape(1, -1), g_kv_lat=g_kv_lat.reshape(1, -1),
        lam=_state_layout(ar, ai).reshape(1, S5_STATE_LANES), wb=wb, wc=wc,
        d_skip=d_s5.reshape(1, -1), w_glu=w_glu.astype(BF16), b_glu=b_glu.reshape(1, -1),
        g_s5_out=g_s5_out.reshape(1, -1),
        w_uq_p=w_uq_p.reshape(Q_LORA, MLA_HEADS * HEAD_PAD).astype(BF16),
        w_uq_sw=w_uq_sw.reshape(Q_LORA, MLA_HEADS * HEAD_PAD).astype(BF16),
        w_k=w_k.reshape(KV_LORA, MLA_HEADS * HEAD_PAD).astype(BF16),
        w_v=w_v.reshape(KV_LORA, MLA_WIDTH).astype(BF16),
        gq=_head_gain(g_qn_nope, g_qn_rope), gk=_head_gain(g_kn_nope, g_kn_rope),
        g_mla_out=g_mla_out.reshape(1, -1), w_out=w_out.astype(BF16), g_ffn=g_ffn.reshape(1, -1),
        w_r=jnp.concatenate([w_r_hi, (w_r_p - w_r_hi.astype(F32)).astype(BF16)], axis=1),
        b_r_p=jnp.pad(b_router, (0, LANES - N_EXPERTS)).reshape(1, LANES),
    )


def _moe_dispatch(segs, counts, w1, b1, w2, b2):
    n_tok = sum(seg['x1'].shape[0] for seg in segs)
    pad_start, tables, n_blocks = _route_tables(counts, n_tok * TOP_K)
    dests = [_dest(pad_start, seg['route']) for seg in segs]
    xs_sorted = _dispatch(dests, [seg['hn'] for seg in segs], n_blocks)
    return dests, _experts(tables, xs_sorted, w1, b1, w2, b2, n_blocks)


def _layer(xp, xs, cache_lat, cache_kr, st_re, st_im, mixer_w, w1, b1, w2, b2):
    prm = _prepare(*mixer_w)
    bp, lp, _ = xp.shape
    bs, ls, _ = xs.shape
    bg = bp // PROMPT_GROUPS
    zero_counts = jnp.zeros((1, LANES), F32)
    passes, prompt_outs = [], []
    for grp in range(PROMPT_GROUPS):
        tok, counts, outs = _mixer_stage(xp, grp, bg, None, prm, zero_counts,
                                         prompt_outs[-1][:2] if prompt_outs else ())
        prompt_outs.append(outs)
        segs = [tok]
        if grp == PROMPT_GROUPS - 1:
            tok_s, counts, sample_outs = _mixer_stage(xs, 0, bs, (cache_lat, cache_kr, st_re, st_im), prm, counts)
            segs.append(tok_s)
        passes.append((segs,) + _moe_dispatch(segs, counts, w1, b1, w2, b2))
    yp = None
    for grp, (segs, dests, yb) in reversed(list(enumerate(passes))):
        gs = _gather_expert_rows(dests, yb)
        yp = _combine(gs[0], segs[0]['gates'], segs[0]['x1'], bp * lp, grp * bg * lp, yp)
        if len(segs) > 1:
            ys = _combine(gs[1], segs[1]['gates'], segs[1]['x1'])
    cat = lambda j: jnp.concatenate([o[j] for o in prompt_outs], axis=0)
    lat_p, krope_t = prompt_outs[-1][:2]
    unt = lambda k: jnp.swapaxes(k, 1, 2)
    return (yp.reshape(bp, lp, D_MODEL), ys.reshape(bs, ls, D_MODEL), lat_p, unt(krope_t), cat(2), cat(3),
            sample_outs[0], unt(sample_outs[1])) + sample_outs[2:]


def kernel(x_prompt, x_sample, cache_kv_latent, cache_k_rope, state_s5_re, state_s5_im, g_mix, w_in, lam_re,
           lam_im, log_dt, b_s5_re, b_s5_im, c_s5_re, c_s5_im, d_s5, w_glu, b_glu, g_q_lat, w_uq, g_kv_lat,
           w_ukv, g_qn_nope, g_qn_rope, g_kn_nope, g_kn_rope, g_s5_out, g_mla_out, w_out, g_ffn, w_router,
           b_router, w_mlp1, b_mlp1, w_mlp2, b_mlp2):
    depth = g_mix.shape[0]
    yp, ys = x_prompt, x_sample
    outs = [[] for _ in range(8)]
    for l in range(depth):
        mixer_w = (g_mix[l], w_in[l], lam_re[l], lam_im[l], log_dt[l], b_s5_re[l], b_s5_im[l], c_s5_re[l],
                   c_s5_im[l], d_s5[l], w_glu[l], b_glu[l], g_q_lat[l], w_uq[l], g_kv_lat[l], w_ukv[l],
                   g_qn_nope[l], g_qn_rope[l], g_kn_nope[l], g_kn_rope[l], g_s5_out[l], g_mla_out[l],
                   w_out[l], g_ffn[l], w_router[l], b_router[l])
        res = _layer(yp, ys, cache_kv_latent[l], cache_k_rope[l], state_s5_re[l], state_s5_im[l], mixer_w,
                     w_mlp1[l], b_mlp1[l], w_mlp2[l], b_mlp2[l])
        yp, ys = res[0], res[1]
        for o, r in zip(outs, res[2:]):
            o.append(r)
    return (yp, ys) + tuple(jnp.stack(o) for o in outs)
```

```python
import functools
import math

import jax
import jax.numpy as jnp
import numpy as np
from jax import lax
from jax.experimental import pallas as pl
from jax.experimental.pallas import tpu as pltpu
from jax.experimental.pallas import tpu_sc as plsc

F32 = jnp.float32
BF16 = jnp.bfloat16
U32 = jnp.uint32

D_MODEL = 1024
S5_WIDTH = 512
S5_GROUP = 16
S5_GROUPS = 32
S5_STATE = 64
MLA_HEADS = 8
QK_NOPE = 64
QK_ROPE = 32
QK_HEAD = QK_NOPE + QK_ROPE
V_HEAD = 64
MLA_WIDTH = MLA_HEADS * V_HEAD
Q_LORA = 384
KV_LORA = 256
ROPE_THETA = 10000.0
CHUNK = 64
N_EXPERTS = 32
TOP_K = 4
D_FF = D_MODEL
SWIGLU_LIMIT = 7.0
SWIGLU_ALPHA = 1.702
EPS = 1e-6

LANES = 128
HEAD_PAD = 128
D_IN_PAD = 1280
S5_CHUNKS = 4
S5_CHUNK_LANES = 1024
S5_STATE_LANES = S5_CHUNKS * S5_CHUNK_LANES
ROW_TILE = 1024
SUB_TILE = 512
S5_ROWS = 512
PROMPT_GROUPS = 2
MOE_ROWS = 512
MOE_COL_CHUNKS = 2
MOE_PAIR = 2
VMEM_LIMIT = 56 * 1024 * 1024
SC_CORES = 2
SC_SUBCORES = 16
SC_WORKERS = SC_CORES * SC_SUBCORES
SC_GATHER_ROWS = 64
D_PACK = D_MODEL // 2


def _cparams(sem, vmem=None):
    return pltpu.CompilerParams(dimension_semantics=sem, vmem_limit_bytes=vmem)


def _rms(x, g, n=None):
    n = x.shape[-1] if n is None else n
    ms = jnp.sum(x * x, axis=-1, keepdims=True) * (1.0 / n)
    return x * lax.rsqrt(ms + EPS) * g


def _sub_tiles(rows):
    sub = min(SUB_TILE, rows)
    return [slice(r, r + sub) for r in range(0, rows, sub)]


def _pack_rows(x):
    lo = lax.bitcast_convert_type(x[:, :D_PACK].astype(BF16).astype(F32), U32)
    hi = lax.bitcast_convert_type(x[:, D_PACK:].astype(BF16).astype(F32), U32)
    return (lo >> 16) | (hi & jnp.uint32(0xFFFF0000))


def _unpack_rows(w):
    lo = lax.bitcast_convert_type(w << 16, F32)
    hi = lax.bitcast_convert_type(w & jnp.uint32(0xFFFF0000), F32)
    return lo, hi


def _prep_kernel(lr_ref, li_ref, ldt_ref, br_ref, bi_ref, ar_ref, ai_ref, bbr_ref, bbi_ref):
    lr = lr_ref[...]
    li = li_ref[...]
    dt = jnp.exp(ldt_ref[...])
    mag = jnp.exp(lr * dt)
    ar = mag * jnp.cos(li * dt)
    ai = mag * jnp.sin(li * dt)
    ar_ref[...] = ar
    ai_ref[...] = ai
    den = lr * lr + li * li
    cr = ((ar - 1.0) * lr + ai * li) / den
    ci = (ai * lr - (ar - 1.0) * li) / den
    br = br_ref[...]
    bi = bi_ref[...]
    bbr_ref[...] = cr[:, None, :] * br - ci[:, None, :] * bi
    bbi_ref[...] = cr[:, None, :] * bi + ci[:, None, :] * br


def _s5_prep(lam_re, lam_im, log_dt, b_re, b_im):
    g, n = lam_re.shape
    p = b_re.shape[-1]
    out = pl.pallas_call(
        _prep_kernel,
        out_shape=(jax.ShapeDtypeStruct((g, n), F32), jax.ShapeDtypeStruct((g, n), F32),
                   jax.ShapeDtypeStruct((g, p, n), F32), jax.ShapeDtypeStruct((g, p, n), F32)),
        name="s5_prep",
    )(lam_re, lam_im, log_dt.reshape(g, 1), jnp.swapaxes(b_re, 1, 2), jnp.swapaxes(b_im, 1, 2))
    return out


def _state_layout(re, im):
    lead = re.shape[:-2]
    re = re.reshape(lead + (S5_CHUNKS, 512))
    im = im.reshape(lead + (S5_CHUNKS, 512))
    return jnp.stack([re, im], axis=-2).reshape(lead + (S5_STATE_LANES,))


def _state_unlayout(h):
    lead = h.shape[:-1]
    h = h.reshape(lead + (S5_CHUNKS, 2, 512))
    re = h[..., 0, :].reshape(lead + (S5_GROUPS, S5_STATE))
    im = h[..., 1, :].reshape(lead + (S5_GROUPS, S5_STATE))
    return re, im


def _s5_weights(bbr, bbi, c_re, c_im):
    eye8 = jnp.eye(8, dtype=F32)

    def blockdiag(m):
        a, b = m.shape[1], m.shape[2]
        return (eye8[:, None, :, None] * m[:, :, None, :]).reshape(8 * a, 8 * b)

    wb, wc = [], []
    for c in range(S5_CHUNKS):
        sl = slice(8 * c, 8 * c + 8)
        wb.append(jnp.concatenate([blockdiag(bbr[sl]), blockdiag(bbi[sl])], axis=1))
        cr_t = jnp.swapaxes(c_re[sl], 1, 2)
        ci_t = jnp.swapaxes(c_im[sl], 1, 2)
        wc.append(jnp.concatenate([blockdiag(cr_t), -blockdiag(ci_t)], axis=0))
    return jnp.stack(wb).astype(BF16), jnp.stack(wc).astype(BF16)


def _in_proj_kernel(x_ref, gmix_ref, w_ref, gq_ref, gkv_ref, ck_ref, sk_ref, *rest, grp):
    u_ref, qn_ref, lat_ref, krot_ref, latall_ref, krope_ref = rest[-6:]
    bsz = x_ref.shape[0]
    if latall_ref.shape[0] != bsz:
        latall_ref[...] = jnp.zeros(latall_ref.shape, F32)
        krope_ref[...] = jnp.zeros(krope_ref.shape, F32)
        latall_ref, krope_ref = latall_ref.at[grp * bsz:(grp + 1) * bsz], krope_ref.at[grp * bsz:(grp + 1) * bsz]
    steps = x_ref.shape[1]
    nb = max(1, min(bsz, SUB_TILE // steps))
    ck, sk = ck_ref[...][None], sk_ref[...][None]
    for b0 in range(0, bsz, nb):
        bs = slice(b0, b0 + nb)
        rows = nb * steps
        xn = _rms(x_ref[bs].reshape(rows, D_MODEL), gmix_ref[...]).astype(BF16)
        z = jnp.dot(xn, w_ref[...], preferred_element_type=F32)
        for bl in range(nb):
            for c in range(S5_CHUNKS):
                u_ref[c, pl.ds(b0 + bl, steps, stride=bsz), :] = z[bl * steps:(bl + 1) * steps,
                                                                   c * LANES:(c + 1) * LANES]
        qn = _rms(z[:, S5_WIDTH:S5_WIDTH + Q_LORA], gq_ref[...]).astype(BF16)
        qn_ref[bs] = qn.reshape(nb, steps, Q_LORA)
        c0 = S5_WIDTH + Q_LORA
        lat = _rms(z[:, c0:c0 + KV_LORA], gkv_ref[...]).reshape(nb, steps, KV_LORA)
        lat_ref[bs] = lat
        latall_ref[bs] = lat
        kp = z[:, c0 + KV_LORA:]
        lane = lax.broadcasted_iota(jnp.int32, kp.shape, 1)
        first_half = (lane % 64) < 16
        sw = jnp.where(first_half, pltpu.roll(kp, LANES - 16, axis=1), pltpu.roll(kp, 16, axis=1))
        kr = kp.reshape(nb, steps, LANES) * ck + sw.reshape(nb, steps, LANES) * sk
        for bl in range(nb):
            krope_ref[b0 + bl] = kr[bl].T[:QK_ROPE]
        krot_ref[bs] = jnp.where(lane.reshape(nb, steps, LANES) >= 64, kr, 0.0)


def _in_proj(x, grp, bsz, g_mix, w_in_p, g_q_lat, g_kv_lat, ck, sk, prev):
    n_all, length, _ = x.shape
    steps = min(ROW_TILE // bsz, length)
    full = lambda i: (0, 0)
    blk = lambda w: pl.BlockSpec((bsz, steps, w), lambda i: (0, i, 0))
    if prev:
        shared = lambda w: pl.BlockSpec((bsz, steps, w), lambda i: (grp, i, 0))
        shared_t = pl.BlockSpec((bsz, QK_ROPE, steps), lambda i: (grp, 0, i))
    else:
        shared = lambda w: pl.BlockSpec((n_all, steps, w), lambda i: (0, i, 0))
        shared_t = pl.BlockSpec((n_all, QK_ROPE, steps), lambda i: (0, 0, i))
    return pl.pallas_call(
        functools.partial(_in_proj_kernel, grp=grp),
        grid=(length // steps,),
        in_specs=[pl.BlockSpec((bsz, steps, D_MODEL), lambda i: (grp, i, 0)),
                  pl.BlockSpec((1, D_MODEL), full),
                  pl.BlockSpec((D_MODEL, D_IN_PAD), full),
                  pl.BlockSpec((1, Q_LORA), full),
                  pl.BlockSpec((1, KV_LORA), full),
                  pl.BlockSpec((steps, LANES), lambda i: (i, 0)),
                  pl.BlockSpec((steps, LANES), lambda i: (i, 0))]
                 + [pl.BlockSpec(memory_space=pl.ANY)] * len(prev),
        out_specs=[pl.BlockSpec((S5_CHUNKS, steps * bsz, LANES), lambda i: (0, i, 0)),
                   blk(Q_LORA), blk(KV_LORA), blk(LANES),
                   shared(KV_LORA), shared_t],
        out_shape=(jax.ShapeDtypeStruct((S5_CHUNKS, length * bsz, LANES), F32),
                   jax.ShapeDtypeStruct((bsz, length, Q_LORA), BF16),
                   jax.ShapeDtypeStruct((bsz, length, KV_LORA), F32),
                   jax.ShapeDtypeStruct((bsz, length, LANES), F32),
                   jax.ShapeDtypeStruct((n_all, length, KV_LORA), F32),
                   jax.ShapeDtypeStruct((n_all, QK_ROPE, length), F32)),
        input_output_aliases={7 + j: 4 + j for j in range(len(prev))},
        compiler_params=_cparams(("parallel",), VMEM_LIMIT),
        name="in_proj",
    )(x, g_mix, w_in_p, g_q_lat, g_kv_lat, ck, sk, *prev)


def _s5_kernel(u_ref, h0_ref, lam_ref, wb_ref, wc_ref, dskip_ref, wglu_ref, bglu_ref, gout_ref,
               y_ref, hlast_ref, bu_ref, h_ref, *, bsz, steps):
    c_id = pl.program_id(0)

    @pl.when(c_id == 0)
    def _():
        h_ref[...] = h0_ref[...]

    for c in range(S5_CHUNKS):
        bu_ref[:, c * S5_CHUNK_LANES:(c + 1) * S5_CHUNK_LANES] = jnp.dot(
            u_ref[c].astype(BF16), wb_ref[c], preferred_element_type=F32)

    for c in range(S5_CHUNKS):
        re = slice(c * S5_CHUNK_LANES, c * S5_CHUNK_LANES + 512)
        im = slice(c * S5_CHUNK_LANES + 512, (c + 1) * S5_CHUNK_LANES)
        lam_r = jnp.broadcast_to(lam_ref[:, re], (bsz, 512))
        lam_i = jnp.broadcast_to(lam_ref[:, im], (bsz, 512))

        hr, hi = h_ref[:, re], h_ref[:, im]
        for t in range(steps):
            rows = slice(t * bsz, (t + 1) * bsz)
            hr, hi = (lam_r * hr - lam_i * hi + bu_ref[rows, re],
                      lam_r * hi + lam_i * hr + bu_ref[rows, im])
            bu_ref[rows, re] = hr
            bu_ref[rows, im] = hi
        h_ref[:, re] = hr
        h_ref[:, im] = hi

    ys = []
    for c in range(S5_CHUNKS):
        hs = bu_ref[:, c * S5_CHUNK_LANES:(c + 1) * S5_CHUNK_LANES].astype(BF16)
        ys.append(jnp.dot(hs, wc_ref[c], preferred_element_type=F32))
    u = jnp.concatenate([u_ref[c] for c in range(S5_CHUNKS)], axis=1)
    y = jnp.concatenate(ys, axis=1) + dskip_ref[...] * u
    y = jax.nn.gelu(y)
    gate = jnp.dot(y.astype(BF16), wglu_ref[...], preferred_element_type=F32) + bglu_ref[...]
    y = _rms(y * jax.nn.sigmoid(gate), gout_ref[...])
    for c in range(S5_CHUNKS):
        y_ref[c] = y[:, c * LANES:(c + 1) * LANES]

    @pl.when(c_id == pl.num_programs(0) - 1)
    def _():
        hlast_ref[...] = h_ref[...]


def _s5(u_tm, h0, lam, wb, wc, d_skip, w_glu, b_glu, g_out, bsz, length):
    steps = min(S5_ROWS // bsz, length)
    rows = steps * bsz
    full2 = lambda c: (0, 0)
    full3 = lambda c: (0, 0, 0)
    return pl.pallas_call(
        functools.partial(_s5_kernel, bsz=bsz, steps=steps),
        grid=(length // steps,),
        in_specs=[pl.BlockSpec((S5_CHUNKS, rows, LANES), lambda c: (0, c, 0)),
                  pl.BlockSpec((bsz, S5_STATE_LANES), full2),
                  pl.BlockSpec((1, S5_STATE_LANES), full2),
                  pl.BlockSpec((S5_CHUNKS, LANES, S5_CHUNK_LANES), full3),
                  pl.BlockSpec((S5_CHUNKS, S5_CHUNK_LANES, LANES), full3),
                  pl.BlockSpec((1, S5_WIDTH), full2),
                  pl.BlockSpec((S5_WIDTH, S5_WIDTH), full2),
                  pl.BlockSpec((1, S5_WIDTH), full2),
                  pl.BlockSpec((1, S5_WIDTH), full2)],
        out_specs=[pl.BlockSpec((S5_CHUNKS, rows, LANES), lambda c: (0, c, 0)),
                   pl.BlockSpec((bsz, S5_STATE_LANES), full2)],
        out_shape=(jax.ShapeDtypeStruct((S5_CHUNKS, length * bsz, LANES), F32),
                   jax.ShapeDtypeStruct((bsz, S5_STATE_LANES), F32)),
        scratch_shapes=[pltpu.VMEM((rows, S5_STATE_LANES), F32),
                        pltpu.VMEM((bsz, S5_STATE_LANES), F32)],
        compiler_params=_cparams(("arbitrary",), VMEM_LIMIT),
        name="s5_mixer",
    )(u_tm, h0, lam, wb, wc, d_skip, w_glu, b_glu, g_out)


def _q_proj_kernel(qn_ref, w_ref, wsw_ref, ones_ref, cq_ref, sq_ref, g_ref, q_ref):
    g = g_ref[...] * (QK_HEAD ** -0.5)
    for rs in _sub_tiles(qn_ref.shape[0]):
        qn = qn_ref[rs, :]
        qf = jnp.dot(qn, w_ref[...], preferred_element_type=F32)
        qs = jnp.dot(qn, wsw_ref[...], preferred_element_type=F32)
        cq, sq = cq_ref[rs, :], sq_ref[rs, :]
        for p in range(MLA_HEADS // 2):
            cols = slice(2 * p * HEAD_PAD, 2 * (p + 1) * HEAD_PAD)
            xr = qf[:, cols] * cq + qs[:, cols] * sq
            sqr = xr * xr
            hi = sqr.astype(BF16)
            lo = (sqr - hi.astype(F32)).astype(BF16)
            ss = (jnp.dot(hi, ones_ref[...], preferred_element_type=F32)
                  + jnp.dot(lo, ones_ref[...], preferred_element_type=F32))
            q = (xr * lax.rsqrt(ss * (1.0 / QK_HEAD) + EPS) * g).astype(BF16)
            q_ref[0, 2 * p, rs, :] = q[:, :HEAD_PAD]
            q_ref[0, 2 * p + 1, rs, :] = q[:, HEAD_PAD:]


def _q_proj(qn, w_uq_p, w_uq_sw, cq, sq, gq, bsz, length):
    tm = min(ROW_TILE, length)
    n_l = length // tm
    full = lambda b, i: (0, 0)
    pair = 2 * HEAD_PAD
    two = lambda a: jnp.concatenate([a, a], axis=1)
    ones2 = (jnp.arange(pair)[:, None] // HEAD_PAD == jnp.arange(pair)[None, :] // HEAD_PAD).astype(BF16)
    return pl.pallas_call(
        _q_proj_kernel,
        grid=(bsz, n_l),
        in_specs=[pl.BlockSpec((tm, Q_LORA), lambda b, i: (b * n_l + i, 0)),
                  pl.BlockSpec((Q_LORA, MLA_HEADS * HEAD_PAD), full),
                  pl.BlockSpec((Q_LORA, MLA_HEADS * HEAD_PAD), full),
                  pl.BlockSpec((pair, pair), full),
                  pl.BlockSpec((tm, pair), lambda b, i: (i, 0)),
                  pl.BlockSpec((tm, pair), lambda b, i: (i, 0)),
                  pl.BlockSpec((1, pair), full)],
        out_specs=pl.BlockSpec((1, MLA_HEADS, tm, HEAD_PAD), lambda b, i: (b, 0, i, 0)),
        out_shape=jax.ShapeDtypeStruct((bsz, MLA_HEADS, length, HEAD_PAD), BF16),
        compiler_params=_cparams(("parallel", "parallel"), VMEM_LIMIT),
        name="q_proj",
    )(qn, w_uq_p, w_uq_sw, ones2, two(cq), two(sq), two(gq))


def _kv_proj_kernel(lat_ref, krot_ref, wk_ref, wv_ref, g_ref, k_ref, v_ref):
    g = g_ref[...]
    for rs in _sub_tiles(lat_ref.shape[0]):
        lat = lat_ref[rs, :].astype(BF16)
        kf = jnp.dot(lat, wk_ref[...], preferred_element_type=F32)
        vf = jnp.dot(lat, wv_ref[...], preferred_element_type=F32)
        if len(krot_ref.shape) == 3:
            krt = krot_ref[0, :, rs]
            n = krt.shape[1]
            krot = jnp.concatenate([jnp.zeros((QK_NOPE, n), F32), krt,
                                    jnp.zeros((HEAD_PAD - QK_HEAD, n), F32)], axis=0).T
        else:
            krot = krot_ref[rs, :]
        for h in range(MLA_HEADS):
            k = kf[:, h * HEAD_PAD:(h + 1) * HEAD_PAD] + krot
            k_ref[0, h, rs, :] = _rms(k, g, QK_HEAD).astype(BF16)
        for p in range(MLA_HEADS // 2):
            v_ref[0, p, rs, :] = vf[:, p * LANES:(p + 1) * LANES].astype(BF16)


def _kv_proj(lat, krot, w_k, w_v, gk, bsz, length):
    tm = min(ROW_TILE, length)
    n_l = length // tm
    full = lambda b, i: (0, 0)
    row = lambda b, i: (b * n_l + i, 0)
    if krot.ndim == 3:
        krot_spec = pl.BlockSpec((1, QK_ROPE, tm), lambda b, i: (b, 0, i))
    else:
        krot_spec = pl.BlockSpec((tm, HEAD_PAD), row)
    return pl.pallas_call(
        _kv_proj_kernel,
        grid=(bsz, n_l),
        in_specs=[pl.BlockSpec((tm, KV_LORA), row),
                  krot_spec,
                  pl.BlockSpec((KV_LORA, MLA_HEADS * HEAD_PAD), full),
                  pl.BlockSpec((KV_LORA, MLA_WIDTH), full),
                  pl.BlockSpec((1, HEAD_PAD), full)],
        out_specs=[pl.BlockSpec((1, MLA_HEADS, tm, HEAD_PAD), lambda b, i: (b, 0, i, 0)),
                   pl.BlockSpec((1, MLA_HEADS // 2, tm, LANES), lambda b, i: (b, 0, i, 0))],
        out_shape=(jax.ShapeDtypeStruct((bsz, MLA_HEADS, length, HEAD_PAD), BF16),
                   jax.ShapeDtypeStruct((bsz, MLA_HEADS // 2, length, LANES), BF16)),
        compiler_params=_cparams(("parallel", "parallel"), VMEM_LIMIT),
        name="kv_proj",
    )(lat, krot, w_k, w_v, gk)


def _scores(q, k):
    return lax.dot_general(q, k, (((1,), (1,)), ((), ())), preferred_element_type=F32)


def _attn_prompt_kernel(q_ref, k_ref, v_ref, o_ref, *, length, tq):
    n_q = length // tq
    row = lax.broadcasted_iota(jnp.int32, (tq, tq), 0)
    col = lax.broadcasted_iota(jnp.int32, (tq, tq), 1)
    visible = (col // CHUNK) <= (row // CHUNK)
    lane = lax.broadcasted_iota(jnp.int32, (tq, LANES), 1)
    for qi in range(n_q):
        q0 = qi * tq
        outs = []
        for hh in range(2):
            q = q_ref[0, hh, q0:q0 + tq, :]
            sd = jnp.where(visible, _scores(q, k_ref[0, hh, q0:q0 + tq, :]), -jnp.inf)
            m = jnp.max(sd, axis=-1, keepdims=True)
            if qi:
                so = _scores(q, k_ref[0, hh, 0:q0, :])
                m = jnp.maximum(m, jnp.max(so, axis=-1, keepdims=True))
            pd = jnp.exp(sd - m)
            l = jnp.sum(pd, axis=-1, keepdims=True)
            acc = jnp.dot(pd.astype(BF16), v_ref[0, 0, q0:q0 + tq, :], preferred_element_type=F32)
            if qi:
                po = jnp.exp(so - m)
                l = l + jnp.sum(po, axis=-1, keepdims=True)
                acc = acc + jnp.dot(po.astype(BF16), v_ref[0, 0, 0:q0, :], preferred_element_type=F32)
            outs.append(acc / l)
        o_ref[0, q0:q0 + tq, :] = jnp.where(lane < V_HEAD, outs[0], outs[1])


def _attn_prompt(q, k, v, bsz, length):
    tq = min(512, length)
    pairs = MLA_HEADS // 2
    return pl.pallas_call(
        functools.partial(_attn_prompt_kernel, length=length, tq=tq),
        grid=(bsz, pairs),
        in_specs=[pl.BlockSpec((1, 2, length, HEAD_PAD), lambda b, p: (b, p, 0, 0)),
                  pl.BlockSpec((1, 2, length, HEAD_PAD), lambda b, p: (b, p, 0, 0)),
                  pl.BlockSpec((1, 1, length, LANES), lambda b, p: (b, p, 0, 0))],
        out_specs=pl.BlockSpec((1, length, LANES), lambda b, p: (b, 0, p)),
        out_shape=jax.ShapeDtypeStruct((bsz, length, MLA_WIDTH), F32),
        compiler_params=_cparams(("parallel", "parallel"), VMEM_LIMIT),
        name="attn_prompt",
    )(q, k, v)


def _attn_sample_kernel(q_ref, kc_ref, vc_ref, kn_ref, vn_ref, o_ref):
    lq = q_ref.shape[2]
    lane = lax.broadcasted_iota(jnp.int32, (lq, LANES), 1)
    outs = []
    for hh in range(2):
        q = q_ref[0, hh]
        sc = _scores(q, kc_ref[0, hh])
        sn = _scores(q, kn_ref[0, hh])
        m = jnp.maximum(jnp.max(sc, axis=-1, keepdims=True), jnp.max(sn, axis=-1, keepdims=True))
        pc = jnp.exp(sc - m)
        pn = jnp.exp(sn - m)
        l = jnp.sum(pc, axis=-1, keepdims=True) + jnp.sum(pn, axis=-1, keepdims=True)
        acc = (jnp.dot(pc.astype(BF16), vc_ref[0, 0], preferred_element_type=F32)
               + jnp.dot(pn.astype(BF16), vn_ref[0, 0], preferred_element_type=F32))
        outs.append(acc / l)
    o_ref[0] = jnp.where(lane < V_HEAD, outs[0], outs[1])


def _attn_sample(q, kc, vc, kn, vn, bsz, lq, past):
    pairs = MLA_HEADS // 2
    hp = lambda b, p: (b, p, 0, 0)
    return pl.pallas_call(
        _attn_sample_kernel,
        grid=(bsz, pairs),
        in_specs=[pl.BlockSpec((1, 2, lq, HEAD_PAD), hp),
                  pl.BlockSpec((1, 2, past, HEAD_PAD), hp),
                  pl.BlockSpec((1, 1, past, LANES), hp),
                  pl.BlockSpec((1, 2, lq, HEAD_PAD), hp),
                  pl.BlockSpec((1, 1, lq, LANES), hp)],
        out_specs=pl.BlockSpec((1, lq, LANES), lambda b, p: (b, 0, p)),
        out_shape=jax.ShapeDtypeStruct((bsz, lq, MLA_WIDTH), F32),
        compiler_params=_cparams(("parallel", "parallel"), VMEM_LIMIT),
        name="attn_sample",
    )(q, kc, vc, kn, vn)


def _out_proj_kernel(x_ref, s5_ref, at_ref, gmla_ref, wout_ref, gffn_ref, wr_ref, br_ref, tri_ref, cin_ref,
                     x1_ref, hn_ref, route_ref, gate_ref, cout_ref, run_ref):
    @pl.when(pl.program_id(0) == 0)
    def _():
        run_ref[...] = cin_ref[...]

    bsz, steps, _ = x_ref.shape
    nb = max(1, min(bsz, SUB_TILE // steps))
    for b0 in range(0, bsz, nb):
        bs = slice(b0, b0 + nb)
        rows = nb * steps
        s5 = jnp.concatenate(
            [jnp.concatenate([s5_ref[c, pl.ds(b0 + bl, steps, stride=bsz), :] for c in range(S5_CHUNKS)], axis=1)
             for bl in range(nb)], axis=0)
        an = _rms(at_ref[bs].reshape(rows, MLA_WIDTH), gmla_ref[...])
        merged = jnp.concatenate([s5.astype(BF16), an.astype(BF16)], axis=1)
        x1 = x_ref[bs].reshape(rows, D_MODEL) + jnp.dot(merged, wout_ref[...], preferred_element_type=F32)
        x1_ref[bs] = x1.reshape(nb, steps, D_MODEL)
        hn = _rms(x1, gffn_ref[...])
        hn_ref[bs] = _pack_rows(hn).reshape(nb, steps, D_PACK)
        hn_hi = hn.astype(BF16)
        hn_lo = (hn - hn_hi.astype(F32)).astype(BF16)
        parts = (jnp.dot(hn_hi, wr_ref[...], preferred_element_type=F32)
                 + jnp.dot(hn_lo, wr_ref[...], preferred_element_type=F32))
        logits = parts[:, :LANES] + parts[:, LANES:] + br_ref[...]
        lane = lax.broadcasted_iota(jnp.int32, logits.shape, 1)
        lane_f = lane.astype(F32)
        work = jnp.where(lane < N_EXPERTS, logits, -jnp.inf)
        vals, idxs = [], []
        for _ in range(TOP_K):
            m = jnp.max(work, axis=-1, keepdims=True)
            i = jnp.min(jnp.where(work == m, lane_f, float(LANES)), axis=-1, keepdims=True)
            vals.append(m)
            idxs.append(i)
            work = jnp.where(lane_f == i, -jnp.inf, work)
        es = [jnp.exp(v - vals[0]) for v in vals]
        den = es[0] + es[1] + es[2] + es[3]
        onehots = [(lane_f == idxs[k]).astype(F32) for k in range(TOP_K)]
        e_all = onehots[0] + onehots[1] + onehots[2] + onehots[3]
        before = jnp.dot(tri_ref[...], e_all.astype(BF16), preferred_element_type=F32) + run_ref[...]
        run_ref[...] = run_ref[...] + jnp.sum(e_all, axis=0, keepdims=True)
        route = jnp.zeros(logits.shape, F32)
        gate_out = jnp.zeros(logits.shape, F32)
        for k in range(TOP_K):
            rank = jnp.sum(onehots[k] * before, axis=-1, keepdims=True)
            route = jnp.where(lane == k, idxs[k], route)
            route = jnp.where(lane == 8 + k, rank, route)
            gate_out = jnp.where(lane == k, es[k] / den, gate_out)
        route_ref[bs] = route.astype(jnp.int32).reshape(nb, steps, LANES)
        gate_ref[bs] = gate_out.reshape(nb, steps, LANES)
    cout_ref[...] = run_ref[...]


def _out_proj(x, grp, s5n_tm, attn, g_mla, w_out, g_ffn, w_r, b_r_p, counts_in):
    bsz, length, _ = attn.shape
    steps = min(ROW_TILE // bsz, length)
    full = lambda i: (0, 0)
    blk = lambda w: pl.BlockSpec((bsz, steps, w), lambda i: (0, i, 0))
    sub = max(1, min(bsz, SUB_TILE // steps)) * steps
    tri = (jnp.arange(sub)[:, None] > jnp.arange(sub)[None, :]).astype(BF16)
    return pl.pallas_call(
        _out_proj_kernel,
        grid=(length // steps,),
        in_specs=[pl.BlockSpec((bsz, steps, D_MODEL), lambda i: (grp, i, 0)),
                  pl.BlockSpec((S5_CHUNKS, steps * bsz, LANES), lambda i: (0, i, 0)),
                  blk(MLA_WIDTH),
                  pl.BlockSpec((1, MLA_WIDTH), full),
                  pl.BlockSpec((D_MODEL, D_MODEL), full),
                  pl.BlockSpec((1, D_MODEL), full),
                  pl.BlockSpec((D_MODEL, 2 * LANES), full),
                  pl.BlockSpec((1, LANES), full),
                  pl.BlockSpec((sub, sub), full),
                  pl.BlockSpec((1, LANES), full)],
        out_specs=[blk(D_MODEL), blk(D_PACK), blk(LANES), blk(LANES),
                   pl.BlockSpec((1, LANES), full)],
        out_shape=(jax.ShapeDtypeStruct((bsz, length, D_MODEL), F32),
                   jax.ShapeDtypeStruct((bsz, length, D_PACK), U32),
                   jax.ShapeDtypeStruct((bsz, length, LANES), jnp.int32),
                   jax.ShapeDtypeStruct((bsz, length, LANES), F32),
                   jax.ShapeDtypeStruct((1, LANES), F32)),
        scratch_shapes=[pltpu.VMEM((1, LANES), F32)],
        compiler_params=_cparams(("arbitrary",), VMEM_LIMIT),
        name="out_proj",
    )(x, s5n_tm, attn, g_mla, w_out, g_ffn, w_r, b_r_p, tri, counts_in)


def _sc_mesh():
    return plsc.VectorSubcoreMesh(core_axis_name="c", subcore_axis_name="s",
                                  num_cores=SC_CORES, num_subcores=SC_SUBCORES)


def _sc_for_chunks(n_chunks, fn):
    wid = lax.axis_index("s") * SC_CORES + lax.axis_index("c")
    full, rem = divmod(n_chunks, SC_WORKERS)
    if full:
        @pl.loop(0, full)
        def _(j):
            fn(j * SC_WORKERS + wid)
    if rem:
        @pl.when(wid < rem)
        def _():
            fn(full * SC_WORKERS + wid)


def _sc_dispatch_body(*refs, n_seg):
    d_hbms, h_hbms = refs[:n_seg], refs[n_seg:2 * n_seg]
    xs_hbm, idx_v, rows_v, sems = refs[2 * n_seg:]
    n = SC_GATHER_ROWS
    wid = lax.axis_index("s") * SC_CORES + lax.axis_index("c")

    def scatter(d_hbm, h_hbm, c, slot):
        pltpu.sync_copy(d_hbm.at[c], idx_v.at[slot])
        pltpu.sync_copy(h_hbm.at[pl.ds(pl.multiple_of(c * n, n), n)], rows_v.at[slot])
        return [pltpu.async_copy(rows_v.at[slot], xs_hbm.at[idx_v.at[slot, k]], sems.at[slot])
                for k in range(TOP_K)]

    def run(d_hbm, h_hbm):
        full, rem = divmod(d_hbm.shape[0], SC_WORKERS)
        if full // 2:
            @pl.loop(0, full // 2)
            def _(j):
                first = scatter(d_hbm, h_hbm, (2 * j) * SC_WORKERS + wid, 0)
                second = scatter(d_hbm, h_hbm, (2 * j + 1) * SC_WORKERS + wid, 1)
                for cp in first + second:
                    cp.wait()
        tail = [(full - 1, None)] if full % 2 else []
        if rem:
            tail.append((full, rem))
        for row, limit in tail:
            def one(row=row):
                for cp in scatter(d_hbm, h_hbm, row * SC_WORKERS + wid, 0):
                    cp.wait()
            if limit is None:
                one()
            else:
                pl.when(wid < limit)(one)

    for d_hbm, h_hbm in zip(d_hbms, h_hbms):
        run(d_hbm, h_hbm)


def _dispatch(dests, hns, n_blocks):
    return pl.kernel(
        functools.partial(_sc_dispatch_body, n_seg=len(dests)),
        out_type=jax.ShapeDtypeStruct((n_blocks * MOE_ROWS, D_PACK), U32),
        mesh=_sc_mesh(),
        scratch_types=[pltpu.VMEM((2, 8, SC_GATHER_ROWS), jnp.int32),
                       pltpu.VMEM((2, SC_GATHER_ROWS, D_PACK), U32),
                       pltpu.SemaphoreType.DMA((2,))],
        name="moe_dispatch_sc",
    )(*dests, *hns)


def _expert_kernel(be_ref, nb_ref, nv_ref, first_ref, slot_ref, nxt_ref,
                   x_ref, w1_hbm, b1_ref, w2_hbm, b2_ref, y_ref, w1f_ref, w2f_ref, w1b_ref, w2b_ref, sem):
    step = pl.program_id(0)

    def weight_copies(e):
        return (pltpu.make_async_copy(w1_hbm.at[e], w1f_ref, sem.at[0]),
                pltpu.make_async_copy(w2_hbm.at[e], w2f_ref, sem.at[1]))

    @pl.when(jnp.logical_and(step == 0, nb_ref[0] > 0))
    def _():
        for cp in weight_copies(be_ref[0]):
            cp.start()

    for sb in range(MOE_PAIR):
        b = step * MOE_PAIR + sb

        @pl.when(jnp.logical_and(b < nb_ref[0], first_ref[b] == 1))
        def _(b=b):
            s = slot_ref[b]
            for cp in weight_copies(be_ref[b]):
                cp.wait()
            w1b_ref[s] = w1f_ref[...].astype(BF16)
            w2b_ref[s] = w2f_ref[...].astype(BF16)

            @pl.when(nxt_ref[b] >= 0)
            def _():
                for cp in weight_copies(nxt_ref[b]):
                    cp.start()

    @pl.when(step * MOE_PAIR < nb_ref[0])
    def _():
        cw = D_FF // MOE_COL_CHUNKS
        for sb in range(MOE_PAIR):
            b = step * MOE_PAIR + sb
            e, s = be_ref[b], slot_ref[b]
            rs = slice(sb * MOE_ROWS, (sb + 1) * MOE_ROWS)
            live = lax.broadcasted_iota(jnp.int32, (MOE_ROWS, D_PACK), 0) < nv_ref[b]
            lo, hi = _unpack_rows(jnp.where(live, x_ref[rs, :], jnp.uint32(0)))
            lo, hi = lo.astype(BF16), hi.astype(BF16)
            b1 = b1_ref[e]

            def up(c0, lo=lo, hi=hi, s=s, b1=b1):
                return (jnp.dot(lo, w1b_ref[s, :D_PACK, c0:c0 + cw], preferred_element_type=F32)
                        + jnp.dot(hi, w1b_ref[s, D_PACK:, c0:c0 + cw], preferred_element_type=F32)
                        + b1[:, c0:c0 + cw])

            y = b2_ref[e]
            for j in range(MOE_COL_CHUNKS):
                gate = jnp.minimum(up(j * cw), SWIGLU_LIMIT)
                lin = jnp.clip(up(D_FF + j * cw), -SWIGLU_LIMIT, SWIGLU_LIMIT)
                act = gate * jax.nn.sigmoid(SWIGLU_ALPHA * gate) * (lin + 1.0)
                y = y + jnp.dot(act.astype(BF16), w2b_ref[s, j * cw:(j + 1) * cw, :],
                                preferred_element_type=F32)
            y_ref[rs, :] = _pack_rows(y)

    @pl.when(step * MOE_PAIR >= nb_ref[0])
    def _():
        y_ref[...] = jnp.zeros(y_ref.shape, y_ref.dtype)


def _experts(tables, xs, w1, b1, w2, b2, n_blocks):
    rows = MOE_PAIR * MOE_ROWS
    last = lambda p, be, nb, *_: (jnp.maximum(jnp.minimum(p, (nb[0] - 1) // MOE_PAIR), 0), 0)
    whole = lambda p, *_: (0, 0, 0)
    return pl.pallas_call(
        _expert_kernel,
        grid_spec=pltpu.PrefetchScalarGridSpec(
            num_scalar_prefetch=6,
            grid=(n_blocks // MOE_PAIR,),
            in_specs=[pl.BlockSpec((rows, D_PACK), last),
                      pl.BlockSpec(memory_space=pl.ANY),
                      pl.BlockSpec((N_EXPERTS, 1, 2 * D_FF), whole),
                      pl.BlockSpec(memory_space=pl.ANY),
                      pl.BlockSpec((N_EXPERTS, 1, D_MODEL), whole)],
            out_specs=pl.BlockSpec((rows, D_PACK), lambda p, *_: (p, 0)),
            scratch_shapes=[pltpu.VMEM((D_MODEL, 2 * D_FF), F32), pltpu.VMEM((D_FF, D_MODEL), F32),
                            pltpu.VMEM((MOE_PAIR, D_MODEL, 2 * D_FF), BF16),
                            pltpu.VMEM((MOE_PAIR, D_FF, D_MODEL), BF16),
                            pltpu.SemaphoreType.DMA((2,))]),
        out_shape=jax.ShapeDtypeStruct((n_blocks * MOE_ROWS, D_PACK), U32),
        compiler_params=_cparams(("arbitrary",), VMEM_LIMIT),
        name="moe_experts",
    )(*tables, xs, w1, b1.reshape(N_EXPERTS, 1, 2 * D_FF), w2, b2.reshape(N_EXPERTS, 1, D_MODEL))


def _sc_gather_body(*refs, n_seg):
    d_hbms, yb_hbm = refs[:n_seg], refs[n_seg]
    g_hbms = refs[n_seg + 1:2 * n_seg + 1]
    idx_v, rows_v, sems = refs[2 * n_seg + 1:]
    n = SC_GATHER_ROWS

    def chunk(d_hbm, g_hbm, c):
        pltpu.sync_copy(d_hbm.at[c], idx_v)
        rows = pl.ds(pl.multiple_of(c * n, n), n)
        gather = lambda k: pltpu.async_copy(yb_hbm.at[idx_v.at[k]], rows_v.at[k % 2], sems.at[k % 2])
        cp = gather(0)
        for k in range(TOP_K):
            cp.wait()
            if k + 1 < TOP_K:
                cp = gather(k + 1)
            pltpu.sync_copy(rows_v.at[k % 2], g_hbm.at[k, rows])

    for d_hbm, g_hbm in zip(d_hbms, g_hbms):
        _sc_for_chunks(d_hbm.shape[0], functools.partial(chunk, d_hbm, g_hbm))


def _gather_expert_rows(dests, yb):
    return pl.kernel(
        functools.partial(_sc_gather_body, n_seg=len(dests)),
        out_type=tuple(jax.ShapeDtypeStruct((TOP_K, d.shape[0] * SC_GATHER_ROWS, D_PACK), U32) for d in dests),
        mesh=_sc_mesh(),
        scratch_types=[pltpu.VMEM((8, SC_GATHER_ROWS), jnp.int32),
                       pltpu.VMEM((2, SC_GATHER_ROWS, D_PACK), U32),
                       pltpu.SemaphoreType.DMA((2,))],
        name="moe_gather_sc",
    )(*dests, yb)


def _combine_kernel(g_ref, gate_ref, x1_ref, *rest, first_blk, n_blk):
    o_ref = rest[-1]
    i = pl.program_id(0)

    @pl.when(jnp.logical_and(i >= first_blk, i < first_blk + n_blk))
    def _():
        gate = gate_ref[...]
        x1 = x1_ref[...]
        acc_lo, acc_hi = x1[:, :D_PACK], x1[:, D_PACK:]
        for k in range(TOP_K):
            lo, hi = _unpack_rows(g_ref[k])
            acc_lo = acc_lo + gate[:, k:k + 1] * lo
            acc_hi = acc_hi + gate[:, k:k + 1] * hi
        o_ref[:, :D_PACK] = acc_lo
        o_ref[:, D_PACK:] = acc_hi

    @pl.when(jnp.logical_or(i < first_blk, i >= first_blk + n_blk))
    def _():
        o_ref[...] = jnp.zeros(o_ref.shape, o_ref.dtype)


def _combine(g, gates, x1, out_rows=None, row0=0, prev=None):
    n_tok = x1.shape[0]
    out_rows = n_tok if out_rows is None else out_rows
    tm = min(512, n_tok)
    n_blk = n_tok // tm
    first_blk, grid = (0, n_blk) if prev is not None else (row0 // tm, out_rows // tm)
    own = lambda i: jnp.clip(i - first_blk, 0, n_blk - 1)
    in_specs = [pl.BlockSpec((TOP_K, tm, D_PACK), lambda i: (0, own(i), 0)),
                pl.BlockSpec((tm, LANES), lambda i: (own(i), 0)),
                pl.BlockSpec((tm, D_MODEL), lambda i: (own(i), 0))]
    args = [g, gates, x1]
    out_blk0, aliases = 0, {}
    if prev is not None:
        in_specs.append(pl.BlockSpec(memory_space=pl.ANY))
        args.append(prev)
        out_blk0, aliases = row0 // tm, {3: 0}
    return pl.pallas_call(
        functools.partial(_combine_kernel, first_blk=first_blk, n_blk=n_blk),
        grid=(grid,),
        in_specs=in_specs,
        out_specs=pl.BlockSpec((tm, D_MODEL), lambda i: (out_blk0 + i, 0)),
        out_shape=jax.ShapeDtypeStruct((out_rows, D_MODEL), F32),
        input_output_aliases=aliases,
        compiler_params=_cparams(("parallel",), VMEM_LIMIT),
        name="moe_combine",
    )(*args)


def _route_tables(counts_f, n_assign):
    counts = counts_f[0, :N_EXPERTS].astype(jnp.int32)
    nblk_e = (counts + MOE_ROWS - 1) // MOE_ROWS
    bend = jnp.cumsum(nblk_e)
    bstart = bend - nblk_e
    pad_start = bstart * MOE_ROWS
    n_blocks = -(-(n_assign + N_EXPERTS * (MOE_ROWS - 1)) // (MOE_ROWS * MOE_PAIR)) * MOE_PAIR
    e_ids = jnp.arange(N_EXPERTS, dtype=jnp.int32)
    nonempty = counts > 0
    slot_e = (jnp.cumsum(nonempty.astype(jnp.int32)) - 1) % MOE_PAIR
    later = jnp.where(nonempty[None, :] & (e_ids[None, :] > e_ids[:, None]), e_ids[None, :], N_EXPERTS)
    next_e = jnp.min(later, axis=1)
    next_e = jnp.where(next_e == N_EXPERTS, -1, next_e)
    bidx = jnp.arange(n_blocks, dtype=jnp.int32)
    mine = ((bidx[:, None] >= bstart[None, :]) & (bidx[:, None] < bend[None, :])).astype(jnp.int32)
    vals = jnp.stack([e_ids, slot_e, next_e, pad_start + counts, bstart, jnp.ones_like(e_ids)], axis=0)
    per_block = jnp.sum(mine[:, None, :] * vals[None, :, :], axis=-1)
    block_e, slot, nxt, row_end, first_blk, used = (per_block[:, j] for j in range(6))
    n_valid = jnp.clip(row_end - bidx * MOE_ROWS, 0, MOE_ROWS)
    first = ((bidx == first_blk) & (used > 0)).astype(jnp.int32)
    tables = (block_e, bend[-1:].astype(jnp.int32), n_valid, first, slot, nxt)
    return pad_start, tables, n_blocks


def _dest_kernel(ps_ref, route_ref, d_ref):
    rt = route_ref[...].T
    idx, rank = rt[0:8, :], rt[8:16, :]
    base = jnp.zeros_like(idx)
    for e in range(N_EXPERTS):
        base = jnp.where(idx == e, ps_ref[e], base)
    dest = base + rank
    for j in range(dest.shape[1] // SC_GATHER_ROWS):
        d_ref[j] = dest[:, j * SC_GATHER_ROWS:(j + 1) * SC_GATHER_ROWS]


def _dest(pad_start, route):
    n_tok = route.shape[0]
    tm = min(2048, n_tok)
    n = SC_GATHER_ROWS
    return pl.pallas_call(
        _dest_kernel,
        grid_spec=pltpu.PrefetchScalarGridSpec(
            num_scalar_prefetch=1,
            grid=(n_tok // tm,),
            in_specs=[pl.BlockSpec((tm, LANES), lambda i, ps: (i, 0))],
            out_specs=pl.BlockSpec((tm // n, 8, n), lambda i, ps: (i, 0, 0))),
        out_shape=jax.ShapeDtypeStruct((n_tok // n, 8, n), jnp.int32),
        compiler_params=_cparams(("parallel",), VMEM_LIMIT),
        name="moe_dest",
    )(pad_start, route)


def _rope_tables(offset, length):
    f32 = np.float32
    pos = f32(offset) + np.arange(length, dtype=f32)
    inv = np.power(f32(ROPE_THETA), -np.arange(QK_ROPE // 2, dtype=f32) * f32(2.0 / QK_ROPE))
    ang = pos[:, None] * inv[None, :]
    cos, sin = np.cos(ang), np.sin(ang)
    z32 = np.zeros((length, 32), f32)
    rot_c = np.concatenate([cos, cos], axis=1)
    rot_s = np.concatenate([-sin, sin], axis=1)
    ck = np.concatenate([rot_c, z32, rot_c, z32], axis=1)
    sk = np.concatenate([rot_s, z32, rot_s, z32], axis=1)
    cq = np.concatenate([np.ones((length, QK_NOPE), f32), rot_c, z32], axis=1)
    sq = np.concatenate([np.zeros((length, QK_NOPE), f32), rot_s, z32], axis=1)
    return tuple(jnp.asarray(t, F32) for t in (ck, sk, cq, sq))


def _head_gain(g_nope, g_rope):
    return jnp.concatenate([g_nope, g_rope, g_rope, jnp.zeros((HEAD_PAD - QK_HEAD,), F32)]).reshape(1, HEAD_PAD)


def _mixer_stage(x, grp, bsz, past, prm, counts_in, prev=()):
    length = x.shape[1]
    offset = 0 if past is None else past[0].shape[1]
    ck, sk, cq, sq = _rope_tables(offset, length)
    u_tm, qn, lat, krot, lat_all, k_rope = _in_proj(x, grp, bsz, prm['g_mix'], prm['w_in_p'], prm['g_q_lat'],
                                                    prm['g_kv_lat'], ck, sk, prev)
    if past is None:
        h0 = jnp.zeros((bsz, S5_STATE_LANES), F32)
    else:
        h0 = _state_layout(past[2], past[3])
    s5n_tm, h_last = _s5(u_tm, h0, prm['lam'], prm['wb'], prm['wc'],
                         prm['d_skip'], prm['w_glu'], prm['b_glu'], prm['g_s5_out'], bsz, length)
    h_re, h_im = _state_unlayout(h_last)
    t = bsz * length
    q = _q_proj(qn.reshape(t, Q_LORA), prm['w_uq_p'], prm['w_uq_sw'], cq, sq, prm['gq'], bsz, length)
    k, v = _kv_proj(lat.reshape(t, KV_LORA), krot.reshape(t, LANES), prm['w_k'], prm['w_v'], prm['gk'],
                    bsz, length)
    if past is None:
        attn = _attn_prompt(q, k, v, bsz, length)
    else:
        n_past = past[0].shape[1]
        c_lat = past[0].reshape(bsz * n_past, KV_LORA)
        c_rot = jnp.swapaxes(past[1], 1, 2)
        kc, vc = _kv_proj(c_lat, c_rot, prm['w_k'], prm['w_v'], prm['gk'], bsz, n_past)
        attn = _attn_sample(q, kc, vc, k, v, bsz, length, n_past)
    x1, hn, route, gates, counts = _out_proj(x, grp, s5n_tm, attn, prm['g_mla_out'], prm['w_out'], prm['g_ffn'],
                                             prm['w_r'], prm['b_r_p'], counts_in)
    tok = dict(x1=x1.reshape(t, D_MODEL), hn=hn.reshape(t, D_PACK), route=route.reshape(t, LANES),
               gates=gates.reshape(t, LANES))
    return tok, counts, (lat_all, k_rope, h_re, h_im)


def _prepare(g_mix, w_in, lam_re, lam_im, log_dt, b_s5_re, b_s5_im, c_s5_re, c_s5_im, d_s5, w_glu, b_glu,
             g_q_lat, w_uq, g_kv_lat, w_ukv, g_qn_nope, g_qn_rope, g_kn_nope, g_kn_rope, g_s5_out,
             g_mla_out, w_out, g_ffn, w_router, b_router):
    c0 = S5_WIDTH + Q_LORA + KV_LORA
    w_pe = w_in[:, c0:]
    z = jnp.zeros((D_MODEL, 32), F32)
    w_in_p = jnp.concatenate([w_in[:, :c0], w_pe, z, w_pe, z], axis=1).astype(BF16)
    ar, ai, bbr, bbi = _s5_prep(lam_re, lam_im, log_dt, b_s5_re, b_s5_im)
    wb, wc = _s5_weights(bbr, bbi, c_s5_re, c_s5_im)
    w_uq_p = jnp.pad(w_uq.reshape(Q_LORA, MLA_HEADS, QK_HEAD), ((0, 0), (0, 0), (0, HEAD_PAD - QK_HEAD)))
    r0, r1, r2 = QK_NOPE, QK_NOPE + QK_ROPE // 2, QK_HEAD
    w_uq_sw = jnp.zeros_like(w_uq_p).at[:, :, r0:r1].set(w_uq_p[:, :, r1:r2]).at[:, :, r1:r2].set(w_uq_p[:, :, r0:r1])
    w_kv = w_ukv.reshape(KV_LORA, MLA_HEADS, QK_NOPE + V_HEAD)
    w_k = jnp.pad(w_kv[:, :, :QK_NOPE], ((0, 0), (0, 0), (0, HEAD_PAD - QK_NOPE)))
    w_v = w_kv[:, :, QK_NOPE:]
    w_r_p = jnp.pad(w_router, ((0, 0), (0, LANES - N_EXPERTS)))
    w_r_hi = w_r_p.astype(BF16)
    return dict(
        g_mix=g_mix.reshape(1, -1), w_in_p=w_in_p,
        g_q_lat=g_q_lat.reshape(1, -1), g_kv_lat=g_kv_lat.reshape(1, -1),
        lam=_state_layout(ar, ai).reshape(1, S5_STATE_LANES), wb=wb, wc=wc,
        d_skip=d_s5.reshape(1, -1), w_glu=w_glu.astype(BF16), b_glu=b_glu.reshape(1, -1),
        g_s5_out=g_s5_out.reshape(1, -1),
        w_uq_p=w_uq_p.reshape(Q_LORA, MLA_HEADS * HEAD_PAD).astype(BF16),
        w_uq_sw=w_uq_sw.reshape(Q_LORA, MLA_HEADS * HEAD_PAD).astype(BF16),
        w_k=w_k.reshape(KV_LORA, MLA_HEADS * HEAD_PAD).astype(BF16),
        w_v=w_v.reshape(KV_LORA, MLA_WIDTH).astype(BF16),
        gq=_head_gain(g_qn_nope, g_qn_rope), gk=_head_gain(g_kn_nope, g_kn_rope),
        g_mla_out=g_mla_out.reshape(1, -1), w_out=w_out.astype(BF16), g_ffn=g_ffn.reshape(1, -1),
        w_r=jnp.concatenate([w_r_hi, (w_r_p - w_r_hi.astype(F32)).astype(BF16)], axis=1),
        b_r_p=jnp.pad(b_router, (0, LANES - N_EXPERTS)).reshape(1, LANES),
    )


def _moe_dispatch(segs, counts, w1, b1, w2, b2):
    n_tok = sum(seg['x1'].shape[0] for seg in segs)
    pad_start, tables, n_blocks = _route_tables(counts, n_tok * TOP_K)
    dests = [_dest(pad_start, seg['route']) for seg in segs]
    xs_sorted = _dispatch(dests, [seg['hn'] for seg in segs], n_blocks)
    return dests, _experts(tables, xs_sorted, w1, b1, w2, b2, n_blocks)


def _layer(xp, xs, cache_lat, cache_kr, st_re, st_im, mixer_w, w1, b1, w2, b2):
    prm = _prepare(*mixer_w)
    bp, lp, _ = xp.shape
    bs, ls, _ = xs.shape
    bg = bp // PROMPT_GROUPS
    zero_counts = jnp.zeros((1, LANES), F32)
    passes, prompt_outs = [], []
    for grp in range(PROMPT_GROUPS):
        tok, counts, outs = _mixer_stage(xp, grp, bg, None, prm, zero_counts,
                                         prompt_outs[-1][:2] if prompt_outs else ())
        prompt_outs.append(outs)
        segs = [tok]
        if grp == PROMPT_GROUPS - 1:
            tok_s, counts, sample_outs = _mixer_stage(xs, 0, bs, (cache_lat, cache_kr, st_re, st_im), prm, counts)
            segs.append(tok_s)
        passes.append((segs,) + _moe_dispatch(segs, counts, w1, b1, w2, b2))
    yp = None
    for grp, (segs, dests, yb) in reversed(list(enumerate(passes))):
        gs = _gather_expert_rows(dests, yb)
        yp = _combine(gs[0], segs[0]['gates'], segs[0]['x1'], bp * lp, grp * bg * lp, yp)
        if len(segs) > 1:
            ys = _combine(gs[1], segs[1]['gates'], segs[1]['x1'])
    cat = lambda j: jnp.concatenate([o[j] for o in prompt_outs], axis=0)
    lat_p, krope_t = prompt_outs[-1][:2]
    unt = lambda k: jnp.swapaxes(k, 1, 2)
    return (yp.reshape(bp, lp, D_MODEL), ys.reshape(bs, ls, D_MODEL), lat_p, unt(krope_t), cat(2), cat(3),
            sample_outs[0], unt(sample_outs[1])) + sample_outs[2:]


def kernel(x_prompt, x_sample, cache_kv_latent, cache_k_rope, state_s5_re, state_s5_im, g_mix, w_in, lam_re,
           lam_im, log_dt, b_s5_re, b_s5_im, c_s5_re, c_s5_im, d_s5, w_glu, b_glu, g_q_lat, w_uq, g_kv_lat,
           w_ukv, g_qn_nope, g_qn_rope, g_kn_nope, g_kn_rope, g_s5_out, g_mla_out, w_out, g_ffn, w_router,
           b_router, w_mlp1, b_mlp1, w_mlp2, b_mlp2):
    depth = g_mix.shape[0]
    yp, ys = x_prompt, x_sample
    outs = [[] for _ in range(8)]
    for l in range(depth):
        mixer_w = (g_mix[l], w_in[l], lam_re[l], lam_im[l], log_dt[l], b_s5_re[l], b_s5_im[l], c_s5_re[l],
                   c_s5_im[l], d_s5[l], w_glu[l], b_glu[l], g_q_lat[l], w_uq[l], g_kv_lat[l], w_ukv[l],
                   g_qn_nope[l], g_qn_rope[l], g_kn_nope[l], g_kn_rope[l], g_s5_out[l], g_mla_out[l],
                   w_out[l], g_ffn[l], w_router[l], b_router[l])
        res = _layer(yp, ys, cache_kv_latent[l], cache_k_rope[l], state_s5_re[l], state_s5_im[l], mixer_w,
                     w_mlp1[l], b_mlp1[l], w_mlp2[l], b_mlp2[l])
        yp, ys = res[0], res[1]
        for o, r in zip(outs, res[2:]):
            o.append(r)
    return (yp, ys) + tuple(jnp.stack(o) for o in outs)
```
